```python
import math
import jax, jax.numpy as jnp
from jax import lax
import numpy as np

D_MODEL = 1024
BATCH = 4
SEQ = 8192
DEPTH = 2

GRID_W = 64
CTX_LEN = 256
EPS = 1e-6
ROPE_THETA = 10000.0
Q_BLOCK = 128

MIX_HALF = D_MODEL // 2
HEAD_DIM = 64
GQA_Q_HEADS = MIX_HALF // HEAD_DIM
GQA_KV_HEADS = GQA_Q_HEADS // 4
GQA_IN = (GQA_Q_HEADS + 2 * GQA_KV_HEADS) * HEAD_DIM

HG_DK = 128
HG_DV = 128
HG_HEADS = MIX_HALF // HG_DV
HG_W = HG_HEADS * HG_DK
HG_IN = 5 * HG_W
GLA_CHUNK = 64

MLA_HEADS = MIX_HALF // HEAD_DIM
MLA_NOPE = 64
MLA_ROPE = 32
MLA_V = 64
MLA_Q_RANK = 384
MLA_KV_RANK = 256
MLA_IN = MLA_Q_RANK + MLA_KV_RANK + MLA_ROPE

S5_WIDTH = MIX_HALF
S5_GROUP = 16
S5_GROUPS = S5_WIDTH // S5_GROUP
S5_STATE = 64

EVEN_IN = GQA_IN + HG_IN
ODD_IN = MLA_IN + S5_WIDTH
MIX_OUT = 2 * MIX_HALF

PEER_HEADS = 8
PEER_KEYS = 128
N_EXPERTS = PEER_KEYS * PEER_KEYS
PEER_TOPK = 16
PEER_QDIM = 256
PEER_HALF = PEER_QDIM // 2
PEER_BLOCK = 128

N_EVEN = (DEPTH + 1) // 2
N_ODD = DEPTH // 2

kernel_name = 'hybrid_gqa_hgrn2_mla_s5_peer_dit'


def _rms_norm(x, g):
    xf = x.astype(jnp.float32)
    y = xf * lax.rsqrt(jnp.mean(xf * xf, axis=-1, keepdims=True) + EPS)
    return (y * g.astype(jnp.float32)).astype(x.dtype)


def _modulate(h, shift, scale):
    return h * (1.0 + scale) + shift


def _axial_angles(rows, rot_dim):
    axis_dim = rot_dim // 2
    inv = ROPE_THETA ** (-jnp.arange(0, axis_dim, 2, dtype=jnp.float32) / axis_dim)
    t = jnp.arange(rows * GRID_W)
    r = (t // GRID_W).astype(jnp.float32)
    col = (t % GRID_W).astype(jnp.float32)
    return r[:, None] * inv, col[:, None] * inv


def _rope_1d(x, ang):
    cos = jnp.cos(ang)[:, None, :]
    sin = jnp.sin(ang)[:, None, :]
    x1, x2 = jnp.split(x.astype(jnp.float32), 2, axis=-1)
    return jnp.concatenate([x1 * cos - x2 * sin, x1 * sin + x2 * cos], axis=-1).astype(x.dtype)


def _rope_2d(x, angs):
    xr, xc = jnp.split(x, 2, axis=-1)
    return jnp.concatenate([_rope_1d(xr, angs[0]), _rope_1d(xc, angs[1])], axis=-1)


def _attend(q, k, v):
    b, hk, g, t, dq = q.shape
    nb = t // Q_BLOCK
    qb = jnp.moveaxis(q.reshape(b, hk, g, nb, Q_BLOCK, dq), 3, 0)
    scale = dq ** -0.5

    def block(qi):
        s = jnp.einsum('bhgqd,bhkd->bhgqk', qi, k).astype(jnp.float32) * scale
        p = jax.nn.softmax(s, axis=-1).astype(v.dtype)
        return jnp.einsum('bhgqk,bhkd->bhgqd', p, v)

    o = lax.map(block, qb)
    return jnp.moveaxis(o, 0, 3).reshape(b, hk, g, t, v.shape[-1])


def _merge_heads(o):
    return o.transpose(0, 3, 1, 2, 4).reshape(o.shape[0], o.shape[3], -1)


def _gqa(pa, pc, qn_g, kn_g, angs, need_ctx):
    nq = GQA_Q_HEADS * HEAD_DIM
    nk = GQA_KV_HEADS * HEAD_DIM
    grp = GQA_Q_HEADS // GQA_KV_HEADS

    def parts(p):
        b_, t_ = p.shape[0], p.shape[1]
        q = _rms_norm(p[..., :nq].reshape(b_, t_, GQA_Q_HEADS, HEAD_DIM), qn_g)
        k = _rms_norm(p[..., nq:nq + nk].reshape(b_, t_, GQA_KV_HEADS, HEAD_DIM), kn_g)
        v = p[..., nq + nk:].reshape(b_, t_, GQA_KV_HEADS, HEAD_DIM)
        return q, k, v

    def qh(a):
        return a.reshape(a.shape[0], a.shape[1], GQA_KV_HEADS, grp, HEAD_DIM).transpose(0, 2, 3, 1, 4)

    def kh(a):
        return a.transpose(0, 2, 1, 3)

    q, k, v = parts(pa)
    qc, kc, vc = parts(pc)
    q = _rope_2d(q, angs)
    k = _rope_2d(k, angs)
    k_all = jnp.concatenate([kh(kc), kh(k)], axis=2)
    v_all = jnp.concatenate([kh(vc), kh(v)], axis=2)
    y = _merge_heads(_attend(qh(q), k_all, v_all))
    yc = _merge_heads(_attend(qh(qc), kh(kc), kh(vc))) if need_ctx else None
    return y, yc


def _gla_scan(q, k, v, logf, s0):
    q, k, v, logf = (a.astype(jnp.float32) for a in (q, k, v, logf))
    b, h, t, dk = q.shape
    n = t // GLA_CHUNK

    def chunks(a):
        return jnp.moveaxis(a.reshape(b, h, n, GLA_CHUNK, a.shape[-1]), 2, 0)

    lower = jnp.tril(jnp.ones((GLA_CHUNK, GLA_CHUNK), dtype=bool))[:, :, None]

    def step(state, blk):
        qc, kc, vc, lc = blk
        cum = jnp.cumsum(lc, axis=2)
        rel = cum[:, :, :, None, :] - cum[:, :, None, :, :]
        dec = jnp.exp(jnp.where(lower, rel, -jnp.inf))
        att = jnp.einsum('bhtc,bhtsc,bhsc->bhts', qc, dec, kc)
        out = (jnp.einsum('bhts,bhsv->bhtv', att, vc)
               + jnp.einsum('bhtc,bhcv->bhtv', qc * jnp.exp(cum), state))
        last = cum[:, :, -1:, :]
        state = (jnp.exp(last[:, :, 0, :, None]) * state
                 + jnp.einsum('bhsc,bhsv->bhcv', kc * jnp.exp(last - cum), vc))
        return state, out

    s_final, o = lax.scan(step, s0, (chunks(q), chunks(k), chunks(v), chunks(logf)))
    return jnp.moveaxis(o, 0, 2).reshape(b, h, t, v.shape[-1]), s_final


def _hgrn2(pa, pc, lb, onorm_g, need_ctx):
    def heads(a):
        return a.reshape(a.shape[0], a.shape[1], HG_HEADS, -1).transpose(0, 2, 1, 3)

    def parts(p):
        q = heads(p[..., :HG_W])
        fs = (heads(p[..., HG_W:2 * HG_W]), heads(p[..., 2 * HG_W:3 * HG_W]))
        i = heads(p[..., 3 * HG_W:4 * HG_W])
        g = p[..., 4 * HG_W:]
        return q, fs, i, g

    def gate(fraw, lbd):
        lbd = lbd[None, :, None, :]
        f = lbd + (1.0 - lbd) * jax.nn.sigmoid(fraw.astype(jnp.float32))
        return 1.0 - f, jnp.log(f)

    q, fs, i, g = parts(pa)
    qc, fcs, ic, gc = parts(pc)
    s_zero = jnp.zeros((pa.shape[0], HG_HEADS, HG_DK, HG_DV), jnp.float32)
    o = 0.0
    oc = 0.0
    for d in range(2):
        fl = (lambda a: jnp.flip(a, axis=2)) if d == 1 else (lambda a: a)
        kc_, lc_ = gate(fcs[d], lb[d])
        k_, l_ = gate(fs[d], lb[d])
        o_ctx, s_ctx = _gla_scan(fl(qc), fl(kc_), fl(ic), fl(lc_), s_zero)
        o_lat, _ = _gla_scan(fl(q), fl(k_), fl(i), fl(l_), s_ctx)
        o = o + fl(o_lat)
        if need_ctx:
            oc = oc + fl(o_ctx)

    def readout(o_, g_):
        o_ = _rms_norm(o_, onorm_g).transpose(0, 2, 1, 3)
        return o_.reshape(o_.shape[0], o_.shape[1], -1) * jax.nn.silu(g_)

    return readout(o, g), (readout(oc, gc) if need_ctx else None)


def _mla(pa, pc, qa_g, w_qup, kva_g, w_kvup, angs, need_ctx):
    r0 = MLA_Q_RANK
    r1 = MLA_Q_RANK + MLA_KV_RANK

    def parts(p, rotate):
        b_, t_ = p.shape[0], p.shape[1]
        q = (_rms_norm(p[..., :r0], qa_g) @ w_qup).reshape(b_, t_, MLA_HEADS, MLA_NOPE + MLA_ROPE)
        kv = (_rms_norm(p[..., r0:r1], kva_g) @ w_kvup).reshape(b_, t_, MLA_HEADS, MLA_NOPE + MLA_V)
        q_nope, q_rope = q[..., :MLA_NOPE], q[..., MLA_NOPE:]
        k_rope = p[..., r1:][:, :, None, :]
        if rotate:
            q_rope = _rope_2d(q_rope, angs)
            k_rope = _rope_2d(k_rope, angs)
        q = jnp.concatenate([q_nope, q_rope], axis=-1)
        k = jnp.concatenate([kv[..., :MLA_NOPE],
                             jnp.broadcast_to(k_rope, (b_, t_, MLA_HEADS, MLA_ROPE))], axis=-1)
        v = kv[..., MLA_NOPE:]
        return q.transpose(0, 2, 1, 3)[:, :, None], k.transpose(0, 2, 1, 3), v.transpose(0, 2, 1, 3)

    q, k, v = parts(pa, True)
    qc, kc, vc = parts(pc, False)
    k_all = jnp.concatenate([kc, k], axis=2)
    v_all = jnp.concatenate([vc, v], axis=2)
    y = _merge_heads(_attend(q, k_all, v_all))
    yc = _merge_heads(_attend(qc, kc, vc)) if need_ctx else None
    return y, yc


def _diag_scan(abar, bu, h0):
    bu = jnp.moveaxis(bu, 1, 0)
    if h0 is not None:
        bu = bu.at[0].add(abar * h0)
    a = jnp.broadcast_to(abar, (bu.shape[0], 1) + abar.shape)

    def combine(l, r):
        return l[0] * r[0], r[0] * l[1] + r[1]

    _, h = lax.associative_scan(combine, (a, bu))
    return jnp.moveaxis(h, 0, 1)


def _s5(ua, uc, a_re, a_im, log_dt, b_re, b_im, c_re, c_im, d_skip, w_glu, b_glu, need_ctx):
    f32 = jnp.float32

    def grp(u):
        return u.astype(f32).reshape(u.shape[0], u.shape[1], S5_GROUPS, S5_GROUP)

    u = grp(ua)
    ucg = grp(uc)
    dsk = d_skip.astype(f32).reshape(S5_GROUPS, S5_GROUP)
    y = dsk * u
    yc = dsk * ucg if need_ctx else None
    for d in range(2):
        fl = (lambda a: jnp.flip(a, axis=1)) if d == 1 else (lambda a: a)
        lam = lax.complex(a_re[d].astype(f32), a_im[d].astype(f32))
        dt = jnp.exp(log_dt[d].astype(f32))[:, None]
        abar = jnp.exp(lam * dt)
        bbar = ((abar - 1.0) / lam)[..., None] * lax.complex(b_re[d].astype(f32), b_im[d].astype(f32))
        cmat = lax.complex(c_re[d].astype(f32), c_im[d].astype(f32))
        h_ctx = _diag_scan(abar, jnp.einsum('btgc,gpc->btgp', fl(ucg), bbar), None)
        h_lat = _diag_scan(abar, jnp.einsum('btgc,gpc->btgp', fl(u), bbar), h_ctx[:, -1])
        y = y + fl(jnp.einsum('gcp,btgp->btgc', cmat, h_lat).real)
        if need_ctx:
            yc = yc + fl(jnp.einsum('gcp,btgp->btgc', cmat, h_ctx).real)

    def glu(y_):
        z = jax.nn.gelu(y_.reshape(y_.shape[0], y_.shape[1], S5_WIDTH), approximate=False)
        return z * jax.nn.sigmoid(z @ w_glu + b_glu)

    return glu(y), (glu(yc) if need_ctx else None)


def _peer(h, wq, keys, u_tab, v_tab):
    shp = h.shape
    blocks = h.reshape(-1, PEER_BLOCK, shp[-1])

    def block(hb):
        n = hb.shape[0]
        q = (hb @ wq).reshape(n, PEER_HEADS, 2, PEER_HALF)
        s = jnp.einsum('thpk,hpnk->thpn', q, keys).astype(jnp.float32)
        s1, i1 = lax.top_k(s[:, :, 0], PEER_TOPK)
        s2, i2 = lax.top_k(s[:, :, 1], PEER_TOPK)
        cand = (s1[..., :, None] + s2[..., None, :]).reshape(n, PEER_HEADS, PEER_TOPK * PEER_TOPK)
        cidx = (i1[..., :, None] * PEER_KEYS + i2[..., None, :]).reshape(n, PEER_HEADS, PEER_TOPK * PEER_TOPK)
        sc, pos = lax.top_k(cand, PEER_TOPK)
        eidx = jnp.take_along_axis(cidx, pos, axis=-1)
        gw = jax.nn.softmax(sc, axis=-1)
        a = jnp.einsum('thkd,td->thk', u_tab[eidx], hb)
        w = (jax.nn.gelu(a.astype(jnp.float32), approximate=False) * gw).astype(hb.dtype)
        return jnp.einsum('thk,thkd->td', w, v_tab[eidx])

    return lax.map(block, blocks).reshape(shp)


def setup_inputs(seed: int = 0) -> dict:
    key = jax.random.key(seed)
    ks = iter(jax.random.split(key, 48))
    f32 = jnp.float32

    def nrm(shape, scale):
        return jax.random.normal(next(ks), shape, f32) * scale

    def gain(shape):
        return 1.0 + 0.02 * jax.random.normal(next(ks), shape, f32)

    d = D_MODEL
    a_im = (jnp.pi * jnp.arange(S5_STATE, dtype=f32)
            + 0.01 * jax.random.normal(next(ks), (N_ODD, 2, S5_GROUPS, S5_STATE), f32))
    return {
        'x': nrm((BATCH, SEQ, d), 1.0),
        'c': nrm((BATCH, d), 1.0),
        'ctx': nrm((BATCH, CTX_LEN, d), 1.0),
        'c_ctx': nrm((d,), 1.0),
        'ada_w': nrm((DEPTH, d, 6 * d), d ** -0.5),
        'ada_b': nrm((DEPTH, 6 * d), 0.02),
        'norm1_g': gain((DEPTH, d)),
        'norm2_g': gain((DEPTH, d)),
        'ev_w_in': nrm((N_EVEN, d, EVEN_IN), d ** -0.5),
        'ev_w_out': nrm((N_EVEN, MIX_OUT, d), MIX_OUT ** -0.5),
        'gqa_qn_g': gain((N_EVEN, HEAD_DIM)),
        'gqa_kn_g': gain((N_EVEN, HEAD_DIM)),
        'hg_lb_logits': nrm((2, N_EVEN + 1, HG_W), 0.5),
        'hg_onorm_g': gain((N_EVEN, HG_DV)),
        'od_w_in': nrm((N_ODD, d, ODD_IN), d ** -0.5),
        'od_w_out': nrm((N_ODD, MIX_OUT, d), MIX_OUT ** -0.5),
        'mla_qa_g': gain((N_ODD, MLA_Q_RANK)),
        'mla_w_qup': nrm((N_ODD, MLA_Q_RANK, MLA_HEADS * (MLA_NOPE + MLA_ROPE)), MLA_Q_RANK ** -0.5),
        'mla_kva_g': gain((N_ODD, MLA_KV_RANK)),
        'mla_w_kvup': nrm((N_ODD, MLA_KV_RANK, MLA_HEADS * (MLA_NOPE + MLA_V)), MLA_KV_RANK ** -0.5),
        's5_a_re': -0.5 + nrm((N_ODD, 2, S5_GROUPS, S5_STATE), 0.01),
        's5_a_im': a_im,
        's5_log_dt': jax.random.uniform(next(ks), (N_ODD, 2, S5_GROUPS), f32,
                                        minval=math.log(1e-3), maxval=math.log(1e-1)),
        's5_b_re': nrm((N_ODD, 2, S5_GROUPS, S5_STATE, S5_GROUP), (2 * S5_GROUP) ** -0.5),
        's5_b_im': nrm((N_ODD, 2, S5_GROUPS, S5_STATE, S5_GROUP), (2 * S5_GROUP) ** -0.5),
        's5_c_re': nrm((N_ODD, 2, S5_GROUPS, S5_GROUP, S5_STATE), S5_STATE ** -0.5),
        's5_c_im': nrm((N_ODD, 2, S5_GROUPS, S5_GROUP, S5_STATE), S5_STATE ** -0.5),
        's5_d': nrm((N_ODD, S5_WIDTH), 1.0),
        's5_w_glu': nrm((N_ODD, S5_WIDTH, S5_WIDTH), S5_WIDTH ** -0.5),
        's5_b_glu': nrm((N_ODD, S5_WIDTH), 0.02),
        'peer_wq': nrm((DEPTH, d, PEER_HEADS * PEER_QDIM), d ** -0.5),
        'peer_keys': nrm((DEPTH, PEER_HEADS, 2, PEER_KEYS, PEER_HALF), PEER_HALF ** -0.5),
        'peer_u': nrm((DEPTH, N_EXPERTS, d), d ** -0.5),
        'peer_v': nrm((DEPTH, N_EXPERTS, d), PEER_HEADS ** -0.5),
        'final_g': gain((d,)),
    }


def reference(x, c, ctx, c_ctx, ada_w, ada_b, norm1_g, norm2_g, ev_w_in, ev_w_out, gqa_qn_g, gqa_kn_g,
              hg_lb_logits, hg_onorm_g, od_w_in, od_w_out, mla_qa_g, mla_w_qup, mla_kva_g, mla_w_kvup,
              s5_a_re, s5_a_im, s5_log_dt, s5_b_re, s5_b_im, s5_c_re, s5_c_im, s5_d, s5_w_glu, s5_b_glu,
              peer_wq, peer_keys, peer_u, peer_v, final_g):
    rows = x.shape[1] // GRID_W
    angs_gqa = _axial_angles(rows, HEAD_DIM)
    angs_mla = _axial_angles(rows, MLA_ROPE)
    lb_all = jnp.cumsum(jax.nn.softmax(hg_lb_logits.astype(jnp.float32), axis=1), axis=1)
    s_c = jax.nn.silu(c)
    s_ctx = jax.nn.silu(c_ctx)
    xc = ctx
    for layer in range(DEPTH):
        need_ctx = layer < DEPTH - 1
        j = layer // 2
        mod = jnp.split((s_c @ ada_w[layer] + ada_b[layer])[:, None, :], 6, axis=-1)
        modc = jnp.split(s_ctx @ ada_w[layer] + ada_b[layer], 6, axis=-1)
        h = _modulate(_rms_norm(x, norm1_g[layer]), mod[0], mod[1])
        hc = _modulate(_rms_norm(xc, norm1_g[layer]), modc[0], modc[1])
        if layer % 2 == 0:
            pa = h @ ev_w_in[j]
            pc = hc @ ev_w_in[j]
            ya, yac = _gqa(pa[..., :GQA_IN], pc[..., :GQA_IN], gqa_qn_g[j], gqa_kn_g[j], angs_gqa, need_ctx)
            yb, ybc = _hgrn2(pa[..., GQA_IN:], pc[..., GQA_IN:],
                             lb_all[:, j].reshape(2, HG_HEADS, HG_DK), hg_onorm_g[j], need_ctx)
            w_out = ev_w_out[j]
        else:
            pa = h @ od_w_in[j]
            pc = hc @ od_w_in[j]
            ya, yac = _mla(pa[..., :MLA_IN], pc[..., :MLA_IN], mla_qa_g[j], mla_w_qup[j],
                           mla_kva_g[j], mla_w_kvup[j], angs_mla, need_ctx)
            yb, ybc = _s5(pa[..., MLA_IN:], pc[..., MLA_IN:], s5_a_re[j], s5_a_im[j], s5_log_dt[j],
                          s5_b_re[j], s5_b_im[j], s5_c_re[j], s5_c_im[j], s5_d[j], s5_w_glu[j],
                          s5_b_glu[j], need_ctx)
            w_out = od_w_out[j]
        x = x + mod[2] * (jnp.concatenate([ya, yb], axis=-1) @ w_out)
        if need_ctx:
            xc = xc + modc[2] * (jnp.concatenate([yac, ybc], axis=-1) @ w_out)
        h = _modulate(_rms_norm(x, norm2_g[layer]), mod[3], mod[4])
        x = x + mod[5] * _peer(h, peer_wq[layer], peer_keys[layer], peer_u[layer], peer_v[layer])
        if need_ctx:
            hc = _modulate(_rms_norm(xc, norm2_g[layer]), modc[3], modc[4])
            xc = xc + modc[5] * _peer(hc, peer_wq[layer], peer_keys[layer], peer_u[layer], peer_v[layer])
    return _rms_norm(x, final_g)
```

```python
import functools
import math

import jax
import jax.numpy as jnp
from jax import lax
from jax.experimental import pallas as pl
from jax.experimental.pallas import tpu as pltpu

F32 = jnp.float32
BF16 = jnp.bfloat16

D_MODEL = 1024
GRID_W = 64
CTX_LEN = 256
EPS = 1e-6
ROPE_THETA = 10000.0

MIX_HALF = D_MODEL // 2
HEAD_DIM = 64
GQA_Q_HEADS = MIX_HALF // HEAD_DIM
GQA_KV_HEADS = GQA_Q_HEADS // 4
GQA_IN = (GQA_Q_HEADS + 2 * GQA_KV_HEADS) * HEAD_DIM

HG_DK = 128
HG_DV = 128
HG_HEADS = MIX_HALF // HG_DV
HG_W = HG_HEADS * HG_DK
GLA_CHUNK = 64

MLA_HEADS = MIX_HALF // HEAD_DIM
MLA_NOPE = 64
MLA_ROPE = 32
MLA_V = 64
MLA_Q_RANK = 384
MLA_KV_RANK = 256
MLA_IN = MLA_Q_RANK + MLA_KV_RANK + MLA_ROPE

S5_WIDTH = MIX_HALF
S5_GROUP = 16
S5_GROUPS = S5_WIDTH // S5_GROUP
S5_STATE = 64

PEER_HEADS = 8
PEER_KEYS = 128
N_EXPERTS = PEER_KEYS * PEER_KEYS
PEER_TOPK = 16
PEER_QDIM = 256
PEER_HALF = PEER_QDIM // 2
PEER_PAIRS = PEER_HEADS * PEER_TOPK

V7X_LANES = 128
V7X_SUBLANES = 8
V7X_VMEM_BYTES = 64 * 1024 * 1024
VMEM_LIMIT = V7X_VMEM_BYTES - 8 * 1024 * 1024

ROW_TILE = 256
PEER_TILE = 128
ATTN_Q_TILE = 256
EXPERT_WORDS = D_MODEL // 2
EXPERT_ROWS = EXPERT_WORDS // V7X_LANES


def _cparams(*sem):
    return pltpu.CompilerParams(dimension_semantics=sem, vmem_limit_bytes=VMEM_LIMIT)


def _norm_mod(x, g, shift, scale):
    ms = jnp.mean(x * x, axis=-1, keepdims=True)
    h = (x * lax.rsqrt(ms + EPS)) * g
    return h * (1.0 + scale) + shift


def _nmm_kernel(x_ref, g_ref, sh_ref, sc_ref, w_ref, o_ref):
    h = _norm_mod(x_ref[...], g_ref[...], sh_ref[0], sc_ref[0])
    o_ref[...] = jnp.dot(h.astype(BF16), w_ref[...], preferred_element_type=F32)


def norm_mod_matmul(x, g, shift, scale, w, mod_index, tm=ROW_TILE):
    r, k = x.shape
    n = w.shape[1]
    assert r % tm == 0 and w.shape[0] == k
    return pl.pallas_call(
        _nmm_kernel,
        grid=(r // tm,),
        in_specs=[
            pl.BlockSpec((tm, k), lambda i: (i, 0)),
            pl.BlockSpec((1, k), lambda i: (0, 0)),
            pl.BlockSpec((1, 1, k), lambda i: (mod_index(i), 0, 0)),
            pl.BlockSpec((1, 1, k), lambda i: (mod_index(i), 0, 0)),
            pl.BlockSpec((k, n), lambda i: (0, 0)),
        ],
        out_specs=pl.BlockSpec((tm, n), lambda i: (i, 0)),
        out_shape=jax.ShapeDtypeStruct((r, n), F32),
        compiler_params=_cparams("parallel"),
        name="norm_mod_matmul",
    )(x, g, shift, scale, w)


def _mm_kernel(a_ref, w_ref, o_ref):
    o_ref[...] = jnp.dot(a_ref[...].astype(BF16), w_ref[...], preferred_element_type=F32)


def matmul(a, w, tm=ROW_TILE):
    r, k = a.shape
    n = w.shape[1]
    assert r % tm == 0
    return pl.pallas_call(
        _mm_kernel,
        grid=(r // tm,),
        in_specs=[pl.BlockSpec((tm, k), lambda i: (i, 0)), pl.BlockSpec((k, n), lambda i: (0, 0))],
        out_specs=pl.BlockSpec((tm, n), lambda i: (i, 0)),
        out_shape=jax.ShapeDtypeStruct((r, n), F32),
        compiler_params=_cparams("parallel"),
        name="matmul",
    )(a, w)


def _mmres_kernel(a_ref, w_ref, x_ref, gate_ref, o_ref):
    y = jnp.dot(a_ref[...].astype(BF16), w_ref[...], preferred_element_type=F32)
    o_ref[...] = x_ref[...] + gate_ref[0] * y


def matmul_residual(a, w, x, gate, mod_index, tm=ROW_TILE):
    r, k = a.shape
    n = w.shape[1]
    assert r % tm == 0
    return pl.pallas_call(
        _mmres_kernel,
        grid=(r // tm,),
        in_specs=[
            pl.BlockSpec((tm, k), lambda i: (i, 0)),
            pl.BlockSpec((k, n), lambda i: (0, 0)),
            pl.BlockSpec((tm, n), lambda i: (i, 0)),
            pl.BlockSpec((1, 1, n), lambda i: (mod_index(i), 0, 0)),
        ],
        out_specs=pl.BlockSpec((tm, n), lambda i: (i, 0)),
        out_shape=jax.ShapeDtypeStruct((r, n), F32),
        compiler_params=_cparams("parallel"),
        name="matmul_residual",
    )(a, w, x, gate)


def _rms_kernel(x_ref, g_ref, o_ref):
    x = x_ref[...]
    ms = jnp.mean(x * x, axis=-1, keepdims=True)
    o_ref[...] = (x * lax.rsqrt(ms + EPS)) * g_ref[...]


def rmsnorm_rows(x, g, tm=ROW_TILE):
    r, k = x.shape
    return pl.pallas_call(
        _rms_kernel,
        grid=(r // tm,),
        in_specs=[pl.BlockSpec((tm, k), lambda i: (i, 0)), pl.BlockSpec((1, k), lambda i: (0, 0))],
        out_specs=pl.BlockSpec((tm, k), lambda i: (i, 0)),
        out_shape=jax.ShapeDtypeStruct((r, k), F32),
        compiler_params=_cparams("parallel"),
        name="rmsnorm_rows",
    )(x, g)


def _attn_kernel(q_ref, k_ref, v_ref, o_ref):
    q = q_ref[0, 0]
    s = lax.dot_general(q, k_ref[0, 0], (((1,), (1,)), ((), ())), preferred_element_type=F32)
    m = jnp.max(s, axis=-1, keepdims=True)
    p = jnp.exp(s - m)
    l = jnp.sum(p, axis=-1, keepdims=True)
    o = jnp.dot(p.astype(BF16), v_ref[0, 0], preferred_element_type=F32)
    o_ref[0, 0] = o / l


def attention(q, k, v, tq=ATTN_Q_TILE):
    b, h, sq, dl = q.shape
    hk, sk = k.shape[1], k.shape[2]
    grp = h // hk
    tq = min(tq, sq)
    assert sq % tq == 0
    return pl.pallas_call(
        _attn_kernel,
        grid=(b, h, sq // tq),
        in_specs=[
            pl.BlockSpec((1, 1, tq, dl), lambda bi, hi, qi: (bi, hi, qi, 0)),
            pl.BlockSpec((1, 1, sk, dl), lambda bi, hi, qi: (bi, hi // grp, 0, 0)),
            pl.BlockSpec((1, 1, sk, dl), lambda bi, hi, qi: (bi, hi // grp, 0, 0)),
        ],
        out_specs=pl.BlockSpec((1, 1, tq, dl), lambda bi, hi, qi: (bi, hi, qi, 0)),
        out_shape=jax.ShapeDtypeStruct((b, h, sq, dl), F32),
        compiler_params=_cparams("parallel", "parallel", "parallel"),
        name="attention",
    )(q, k, v)


def _topk_rows(s, iota, k):
    n = s.shape[0]
    row = lax.broadcasted_iota(jnp.int32, (k, s.shape[1]), 0)
    vals = jnp.zeros((k, s.shape[1]), F32)
    ids = jnp.zeros((k, s.shape[1]), F32)
    for r in range(k):
        m = jnp.max(s, axis=0, keepdims=True)
        am = jnp.min(jnp.where(s == m, iota, float(n)), axis=0, keepdims=True)
        vals = jnp.where(row == r, m, vals)
        ids = jnp.where(row == r, am, ids)
        s = jnp.where(iota == am, -jnp.inf, s)
    return vals, ids


def _peer_ret_kernel(x_ref, g_ref, sh_ref, sc_ref, wq_ref, keys_ref, h_ref, eidx_ref, gw_ref, hb_ref):
    h = _norm_mod(x_ref[...], g_ref[...], sh_ref[0], sc_ref[0])
    h_ref[...] = h
    hb_ref[...] = h.astype(BF16)
    tb = x_ref.shape[0]
    iota_n = lax.broadcasted_iota(jnp.int32, (PEER_KEYS, tb), 0).astype(F32)
    iota_c = lax.broadcasted_iota(jnp.int32, (PEER_TOPK * PEER_TOPK, tb), 0).astype(F32)

    def head(hd, carry):
        q = jnp.dot(hb_ref[...], wq_ref[hd], preferred_element_type=F32)
        tops = []
        for p in range(2):
            qp = q[:, p * PEER_HALF:(p + 1) * PEER_HALF].astype(BF16)
            s = lax.dot_general(keys_ref[hd, p], qp, (((1,), (1,)), ((), ())),
                                preferred_element_type=F32)
            tops.append(_topk_rows(s, iota_n, PEER_TOPK))
        (s1, i1), (s2, i2) = tops
        cand = jnp.concatenate([s1[a:a + 1] + s2 for a in range(PEER_TOPK)], axis=0)
        cidx = jnp.concatenate([i1[a:a + 1] * float(PEER_KEYS) + i2 for a in range(PEER_TOPK)], axis=0)
        row = lax.broadcasted_iota(jnp.int32, (PEER_TOPK, tb), 0)
        sc = jnp.zeros((PEER_TOPK, tb), F32)
        ex = jnp.zeros((PEER_TOPK, tb), F32)
        for r in range(PEER_TOPK):
            m = jnp.max(cand, axis=0, keepdims=True)
            am = jnp.min(jnp.where(cand == m, iota_c, float(PEER_TOPK * PEER_TOPK)), axis=0, keepdims=True)
            hit = iota_c == am
            e = jnp.max(jnp.where(hit, cidx, 0.0), axis=0, keepdims=True)
            sc = jnp.where(row == r, m, sc)
            ex = jnp.where(row == r, e, ex)
            cand = jnp.where(hit, -jnp.inf, cand)
        pexp = jnp.exp(sc - sc[0:1])
        gw_ref[hd] = pexp / jnp.sum(pexp, axis=0, keepdims=True)
        eidx_ref[hd] = ex.astype(jnp.int32)
        return carry

    lax.fori_loop(0, PEER_HEADS, head, 0)


def peer_retrieve(x, g, shift, scale, wq_heads, keys, mod_index, tb=PEER_TILE):
    r, d = x.shape
    nb = r // tb
    return pl.pallas_call(
        _peer_ret_kernel,
        grid=(nb,),
        in_specs=[
            pl.BlockSpec((tb, d), lambda i: (i, 0)),
            pl.BlockSpec((1, d), lambda i: (0, 0)),
            pl.BlockSpec((1, 1, d), lambda i: (mod_index(i), 0, 0)),
            pl.BlockSpec((1, 1, d), lambda i: (mod_index(i), 0, 0)),
            pl.BlockSpec((PEER_HEADS, d, PEER_QDIM), lambda i: (0, 0, 0)),
            pl.BlockSpec((PEER_HEADS, 2, PEER_KEYS, PEER_HALF), lambda i: (0, 0, 0, 0)),
        ],
        out_specs=[
            pl.BlockSpec((tb, d), lambda i: (i, 0)),
            pl.BlockSpec((None, PEER_HEADS, PEER_TOPK, tb), lambda i: (i, 0, 0, 0)),
            pl.BlockSpec((None, PEER_HEADS, PEER_TOPK, tb), lambda i: (i, 0, 0, 0)),
        ],
        out_shape=[
            jax.ShapeDtypeStruct((r, d), F32),
            jax.ShapeDtypeStruct((nb, PEER_HEADS, PEER_TOPK, tb), jnp.int32),
            jax.ShapeDtypeStruct((nb, PEER_HEADS, PEER_TOPK, tb), F32),
        ],
        scratch_shapes=[pltpu.VMEM((tb, d), BF16)],
        compiler_params=_cparams("parallel"),
        name="peer_retrieve",
    )(x, g, shift, scale, wq_heads, keys)


def pack_expert_table(tab):
    e, d = tab.shape
    tb16 = lax.bitcast_convert_type(tab.astype(BF16), jnp.uint16).astype(jnp.uint32)
    word = tb16[:, :d // 2] | (tb16[:, d // 2:] << 16)
    return lax.bitcast_convert_type(word, jnp.int32).reshape(e * EXPERT_ROWS, V7X_LANES)


def _unpack(w):
    lo = lax.bitcast_convert_type(w << 16, F32)
    hi = lax.bitcast_convert_type(w & jnp.int32(-65536), F32)
    return lo, hi


def _expert_slab(tab_ref, e):
    return tab_ref[pl.ds(pl.multiple_of(e * EXPERT_ROWS, EXPERT_ROWS), EXPERT_ROWS), :]


def _peer_u_kernel(idx_ref, tab_ref, h_ref, o_ref, s_ref):
    tb = o_ref.shape[0]
    ones = jnp.ones((V7X_SUBLANES, V7X_LANES), F32)

    def tok(t, carry):
        hq = h_ref[pl.ds(pl.multiple_of(t * V7X_SUBLANES, V7X_SUBLANES), V7X_SUBLANES), :]
        h_lo = hq[0:EXPERT_ROWS]
        h_hi = hq[EXPERT_ROWS:2 * EXPERT_ROWS]
        for j in range(PEER_PAIRS):
            lo, hi = _unpack(_expert_slab(tab_ref, idx_ref[t * PEER_PAIRS + j]))
            s_ref[pl.ds(j, 1), :] = jnp.sum(lo * h_lo + hi * h_hi, axis=0, keepdims=True)
        a = lax.dot_general(ones, s_ref[...], (((1,), (1,)), ((), ())),
                            precision=lax.Precision.HIGHEST, preferred_element_type=F32)
        o_ref[pl.ds(t, 1), :] = a[0:1]
        return carry

    lax.fori_loop(0, tb, tok, 0)


def peer_expert_dots(idx, tab, h8, tb=PEER_TILE):
    r = h8.shape[0] // V7X_SUBLANES
    return pl.pallas_call(
        _peer_u_kernel,
        grid=(r // tb,),
        in_specs=[
            pl.BlockSpec((tb * PEER_PAIRS,), lambda i: (i,), memory_space=pltpu.SMEM),
            pl.BlockSpec(tab.shape, lambda i: (0, 0), pipeline_mode=pl.Buffered(1)),
            pl.BlockSpec((tb * V7X_SUBLANES, V7X_LANES), lambda i: (i, 0)),
        ],
        out_specs=pl.BlockSpec((tb, PEER_PAIRS), lambda i: (i, 0)),
        out_shape=jax.ShapeDtypeStruct((r, PEER_PAIRS), F32),
        scratch_shapes=[pltpu.VMEM((PEER_PAIRS, V7X_LANES), F32)],
        compiler_params=_cparams("arbitrary"),
        name="peer_expert_dots",
    )(idx, tab, h8)


PEER_V_CHAINS = 4


def _peer_v_kernel(idx_ref, w_ref, tab_ref, o_ref):
    tb = o_ref.shape[0] // V7X_SUBLANES

    def tok(t, carry):
        acc = [[jnp.zeros((EXPERT_ROWS, V7X_LANES), F32) for _ in range(2)] for _ in range(PEER_V_CHAINS)]
        for j in range(PEER_PAIRS):
            lo, hi = _unpack(_expert_slab(tab_ref, idx_ref[t * PEER_PAIRS + j]))
            wj = w_ref[t * PEER_PAIRS + j]
            c = j % PEER_V_CHAINS
            acc[c][0] = acc[c][0] + wj * lo
            acc[c][1] = acc[c][1] + wj * hi
        lo = (acc[0][0] + acc[1][0]) + (acc[2][0] + acc[3][0])
        hi = (acc[0][1] + acc[1][1]) + (acc[2][1] + acc[3][1])
        base = t * V7X_SUBLANES
        o_ref[pl.ds(pl.multiple_of(base, EXPERT_ROWS), EXPERT_ROWS), :] = lo
        o_ref[pl.ds(pl.multiple_of(base + EXPERT_ROWS, EXPERT_ROWS), EXPERT_ROWS), :] = hi
        return carry

    lax.fori_loop(0, tb, tok, 0)


def peer_expert_mix(idx, w, tab, tb=PEER_TILE):
    r = idx.shape[0] // PEER_PAIRS
    return pl.pallas_call(
        _peer_v_kernel,
        grid=(r // tb,),
        in_specs=[
            pl.BlockSpec((tb * PEER_PAIRS,), lambda i: (i,), memory_space=pltpu.SMEM),
            pl.BlockSpec((tb * PEER_PAIRS,), lambda i: (i,), memory_space=pltpu.SMEM),
            pl.BlockSpec(tab.shape, lambda i: (0, 0), pipeline_mode=pl.Buffered(1)),
        ],
        out_specs=pl.BlockSpec((tb * V7X_SUBLANES, V7X_LANES), lambda i: (i, 0)),
        out_shape=jax.ShapeDtypeStruct((r * V7X_SUBLANES, V7X_LANES), F32),
        compiler_params=_cparams("arbitrary"),
        name="peer_expert_mix",
    )(idx, w, tab)


def peer(x, g, shift, scale, gate, wq, keys, u_tab, v_tab, mod_index_peer):
    r, d = x.shape
    wq_heads = wq.astype(BF16).reshape(d, PEER_HEADS, PEER_QDIM).transpose(1, 0, 2)
    h, eidx, gw = peer_retrieve(x, g, shift, scale, wq_heads, keys.astype(BF16), mod_index_peer)
    nb = eidx.shape[0]
    idx = eidx.transpose(0, 3, 1, 2).reshape(r * PEER_PAIRS)
    gw = gw.transpose(0, 3, 1, 2).reshape(r, PEER_PAIRS)
    h8 = h.reshape(r * V7X_SUBLANES, V7X_LANES)
    a = peer_expert_dots(idx, pack_expert_table(u_tab), h8)
    w = (jax.nn.gelu(a, approximate=False) * gw).reshape(r * PEER_PAIRS)
    out = peer_expert_mix(idx, w, pack_expert_table(v_tab)).reshape(r, d)
    return out, nb


def _rms_norm(x, g):
    xf = x.astype(F32)
    y = xf * lax.rsqrt(jnp.mean(xf * xf, axis=-1, keepdims=True) + EPS)
    return y * g.astype(F32)


def _gla_scan(q, k, v, logf, s0):
    b, h, t, dk = q.shape
    n = t // GLA_CHUNK

    def chunks(a):
        return jnp.moveaxis(a.reshape(b, h, n, GLA_CHUNK, a.shape[-1]), 2, 0)

    lower = jnp.tril(jnp.ones((GLA_CHUNK, GLA_CHUNK), dtype=bool))[:, :, None]

    def step(state, blk):
        qc, kc, vc, lc = blk
        cum = jnp.cumsum(lc, axis=2)
        rel = cum[:, :, :, None, :] - cum[:, :, None, :, :]
        dec = jnp.exp(jnp.where(lower, rel, -jnp.inf))
        att = jnp.einsum('bhtc,bhtsc,bhsc->bhts', qc, dec, kc)
        out = (jnp.einsum('bhts,bhsv->bhtv', att, vc)
               + jnp.einsum('bhtc,bhcv->bhtv', qc * jnp.exp(cum), state))
        last = cum[:, :, -1:, :]
        state = (jnp.exp(last[:, :, 0, :, None]) * state
                 + jnp.einsum('bhsc,bhsv->bhcv', kc * jnp.exp(last - cum), vc))
        return state, out

    s_final, o = lax.scan(step, s0, (chunks(q), chunks(k), chunks(v), chunks(logf)))
    return jnp.moveaxis(o, 0, 2).reshape(b, h, t, v.shape[-1]), s_final


def _hgrn2(pa, pc, lb, onorm_g, need_ctx):
    def heads(a):
        return a.reshape(a.shape[0], a.shape[1], HG_HEADS, -1).transpose(0, 2, 1, 3)

    def parts(p):
        q = heads(p[..., :HG_W])
        fs = (heads(p[..., HG_W:2 * HG_W]), heads(p[..., 2 * HG_W:3 * HG_W]))
        i = heads(p[..., 3 * HG_W:4 * HG_W])
        g = p[..., 4 * HG_W:]
        return q, fs, i, g

    def gate(fraw, lbd):
        lbd = lbd[None, :, None, :]
        f = lbd + (1.0 - lbd) * jax.nn.sigmoid(fraw)
        return 1.0 - f, jnp.log(f)

    q, fs, i, g = parts(pa)
    qc, fcs, ic, gc = parts(pc)
    s_zero = jnp.zeros((pa.shape[0], HG_HEADS, HG_DK, HG_DV), F32)
    o = 0.0
    oc = 0.0
    for d in range(2):
        fl = (lambda a: jnp.flip(a, axis=2)) if d == 1 else (lambda a: a)
        kc_, lc_ = gate(fcs[d], lb[d])
        k_, l_ = gate(fs[d], lb[d])
        o_ctx, s_ctx = _gla_scan(fl(qc), fl(kc_), fl(ic), fl(lc_), s_zero)
        o_lat, _ = _gla_scan(fl(q), fl(k_), fl(i), fl(l_), s_ctx)
        o = o + fl(o_lat)
        if need_ctx:
            oc = oc + fl(o_ctx)

    def readout(o_, g_):
        o_ = _rms_norm(o_, onorm_g).transpose(0, 2, 1, 3)
        return o_.reshape(o_.shape[0], o_.shape[1], -1) * jax.nn.silu(g_)

    return readout(o, g), (readout(oc, gc) if need_ctx else None)


def _diag_scan(abar, bu, h0):
    bu = jnp.moveaxis(bu, 1, 0)
    if h0 is not None:
        bu = bu.at[0].add(abar * h0)
    a = jnp.broadcast_to(abar, (bu.shape[0], 1) + abar.shape)

    def combine(l, r):
        return l[0] * r[0], r[0] * l[1] + r[1]

    _, h = lax.associative_scan(combine, (a, bu))
    return jnp.moveaxis(h, 0, 1)


def _s5_core(ua, uc, a_re, a_im, log_dt, b_re, b_im, c_re, c_im, d_skip, need_ctx):
    def grp(u):
        return u.reshape(u.shape[0], u.shape[1], S5_GROUPS, S5_GROUP)

    u = grp(ua)
    ucg = grp(uc)
    dsk = d_skip.reshape(S5_GROUPS, S5_GROUP)
    y = dsk * u
    yc = dsk * ucg if need_ctx else None
    for d in range(2):
        fl = (lambda a: jnp.flip(a, axis=1)) if d == 1 else (lambda a: a)
        lam = lax.complex(a_re[d], a_im[d])
        dt = jnp.exp(log_dt[d])[:, None]
        abar = jnp.exp(lam * dt)
        bbar = ((abar - 1.0) / lam)[..., None] * lax.complex(b_re[d], b_im[d])
        cmat = lax.complex(c_re[d], c_im[d])
        h_ctx = _diag_scan(abar, jnp.einsum('btgc,gpc->btgp', fl(ucg), bbar), None)
        h_lat = _diag_scan(abar, jnp.einsum('btgc,gpc->btgp', fl(u), bbar), h_ctx[:, -1])
        y = y + fl(jnp.einsum('gcp,btgp->btgc', cmat, h_lat).real)
        if need_ctx:
            yc = yc + fl(jnp.einsum('gcp,btgp->btgc', cmat, h_ctx).real)
    shp = (ua.shape[0], -1, S5_WIDTH)
    return y.reshape(shp), (yc.reshape(shp) if need_ctx else None)


def _rope_tables(rows, rot_dim):
    axis_dim = rot_dim // 2
    inv = ROPE_THETA ** (-jnp.arange(0, axis_dim, 2, dtype=F32) / axis_dim)
    t = jnp.arange(rows * GRID_W)
    r = (t // GRID_W).astype(F32)[:, None] * inv
    c = (t % GRID_W).astype(F32)[:, None] * inv
    cos = jnp.concatenate([jnp.cos(r), jnp.cos(r), jnp.cos(c), jnp.cos(c)], axis=-1)
    sin = jnp.concatenate([-jnp.sin(r), jnp.sin(r), -jnp.sin(c), jnp.sin(c)], axis=-1)
    return cos, sin


def _rope(x, cos, sin):
    q = x.shape[-1] // 4
    swapped = jnp.concatenate([x[..., q:2 * q], x[..., :q], x[..., 3 * q:], x[..., 2 * q:3 * q]], axis=-1)
    return x * cos[:, None, :] + swapped * sin[:, None, :]


def _rope_latent(x, cos, sin):
    return jnp.concatenate([x[:, :CTX_LEN], _rope(x[:, CTX_LEN:], cos, sin)], axis=1)


def _attend_all(q, k, v, need_ctx):
    q, k, v = q.astype(BF16), k.astype(BF16), v.astype(BF16)
    o_lat = attention(q[:, :, CTX_LEN:], k, v)
    if need_ctx:
        o_ctx = attention(q[:, :, :CTX_LEN], k[:, :, :CTX_LEN], v[:, :, :CTX_LEN])
    else:
        o_ctx = jnp.zeros(q.shape[:2] + (CTX_LEN, q.shape[3]), F32)
    return jnp.concatenate([o_ctx, o_lat], axis=2)


def _gqa(p, qn_g, kn_g, cos, sin, need_ctx):
    b, s, _ = p.shape
    nq = GQA_Q_HEADS * HEAD_DIM
    nk = GQA_KV_HEADS * HEAD_DIM
    grp = GQA_Q_HEADS // GQA_KV_HEADS
    q = _rms_norm(p[..., :nq].reshape(b, s, GQA_Q_HEADS, HEAD_DIM), qn_g)
    k = _rms_norm(p[..., nq:nq + nk].reshape(b, s, GQA_KV_HEADS, HEAD_DIM), kn_g)
    v = p[..., nq + nk:nq + 2 * nk]
    q = _rope_latent(q, cos, sin) * (HEAD_DIM ** -0.5)
    k = _rope_latent(k, cos, sin)
    qh = q.transpose(0, 2, 1, 3)
    zeros = jnp.zeros_like(qh)
    first = (jnp.arange(GQA_Q_HEADS) < grp)[None, :, None, None]
    q_pad = jnp.concatenate([jnp.where(first, qh, zeros), jnp.where(first, zeros, qh)], axis=-1)
    k_all = k.reshape(b, 1, s, nk)
    v_all = v.reshape(b, 1, s, nk)
    o = _attend_all(q_pad, k_all, v_all, need_ctx)
    o = jnp.where(first, o[..., :HEAD_DIM], o[..., HEAD_DIM:])
    return o.transpose(0, 2, 1, 3).reshape(b, s, nq)


def _mla(p, qa_g, w_qup, kva_g, w_kvup, cos, sin, need_ctx):
    b, s, _ = p.shape
    r0 = MLA_Q_RANK
    r1 = MLA_Q_RANK + MLA_KV_RANK
    zero = jnp.zeros((1, 1, r0), F32)
    qa = p[..., :r0].reshape(b * s, r0)
    q = norm_mod_matmul(qa, qa_g[None, :], zero, zero, w_qup.astype(BF16), lambda i: 0)
    q = q.reshape(b, s, MLA_HEADS, MLA_NOPE + MLA_ROPE)
    zero = jnp.zeros((1, 1, MLA_KV_RANK), F32)
    kva = p[..., r0:r1].reshape(b * s, MLA_KV_RANK)
    kv = norm_mod_matmul(kva, kva_g[None, :], zero, zero, w_kvup.astype(BF16), lambda i: 0)
    kv = kv.reshape(b, s, MLA_HEADS, MLA_NOPE + MLA_V)
    q_rope = _rope_latent(q[..., MLA_NOPE:], cos, sin)
    k_rope = _rope_latent(p[..., r1:r1 + MLA_ROPE][:, :, None, :], cos, sin)
    scale = (MLA_NOPE + MLA_ROPE) ** -0.5
    pad = jnp.zeros((b, s, MLA_HEADS, V7X_LANES - MLA_NOPE - MLA_ROPE), F32)
    q_pad = jnp.concatenate([q[..., :MLA_NOPE] * scale, q_rope * scale, pad], axis=-1)
    k_pad = jnp.concatenate([kv[..., :MLA_NOPE], jnp.broadcast_to(k_rope, (b, s, MLA_HEADS, MLA_ROPE)), pad], axis=-1)
    v_pad = jnp.concatenate([kv[..., MLA_NOPE:], jnp.zeros((b, s, MLA_HEADS, V7X_LANES - MLA_V), F32)], axis=-1)
    tr = lambda a: a.transpose(0, 2, 1, 3)
    o = _attend_all(tr(q_pad), tr(k_pad), tr(v_pad), need_ctx)
    return o[..., :MLA_V].transpose(0, 2, 1, 3).reshape(b, s, MLA_HEADS * MLA_V)


def kernel(x, c, ctx, c_ctx, ada_w, ada_b, norm1_g, norm2_g, ev_w_in, ev_w_out, gqa_qn_g, gqa_kn_g,
           hg_lb_logits, hg_onorm_g, od_w_in, od_w_out, mla_qa_g, mla_w_qup, mla_kva_g, mla_w_kvup,
           s5_a_re, s5_a_im, s5_log_dt, s5_b_re, s5_b_im, s5_c_re, s5_c_im, s5_d, s5_w_glu, s5_b_glu,
           peer_wq, peer_keys, peer_u, peer_v, final_g):
    b, t, d = x.shape
    s = CTX_LEN + t
    depth = ada_w.shape[0]
    rows = t // GRID_W
    cos_g, sin_g = _rope_tables(rows, HEAD_DIM)
    cos_m, sin_m = _rope_tables(rows, MLA_ROPE)
    lb_all = jnp.cumsum(jax.nn.softmax(hg_lb_logits, axis=1), axis=1)

    def mod_index(tile):
        per, nctx = s // tile, CTX_LEN // tile
        return lambda i: jnp.where(i % per < nctx, b, i // per)

    mi_row = mod_index(ROW_TILE)
    mi_peer = mod_index(PEER_TILE)

    xa = jnp.concatenate([ctx, x], axis=1).reshape(b * s, d)
    s_all = jnp.concatenate([jax.nn.silu(c), jax.nn.silu(c_ctx)[None, :]], axis=0)
    s_pad = jnp.concatenate([s_all, jnp.zeros((V7X_SUBLANES - (b + 1) % V7X_SUBLANES, d), F32)], axis=0)

    for layer in range(depth):
        need_ctx = layer < depth - 1
        j = layer // 2
        mod = matmul(s_pad, ada_w[layer].astype(BF16), tm=s_pad.shape[0])[:b + 1] + ada_b[layer]
        mod = [m[:, None, :] for m in jnp.split(mod, 6, axis=-1)]
        if layer % 2 == 0:
            pa = norm_mod_matmul(xa, norm1_g[layer][None, :], mod[0], mod[1], ev_w_in[j].astype(BF16), mi_row)
            pa = pa.reshape(b, s, -1)
            ya = _gqa(pa[..., :GQA_IN], gqa_qn_g[j], gqa_kn_g[j], cos_g, sin_g, need_ctx)
            yb, ybc = _hgrn2(pa[:, CTX_LEN:, GQA_IN:], pa[:, :CTX_LEN, GQA_IN:],
                             lb_all[:, j].reshape(2, HG_HEADS, HG_DK), hg_onorm_g[j], need_ctx)
            w_out = ev_w_out[j]
        else:
            pa = norm_mod_matmul(xa, norm1_g[layer][None, :], mod[0], mod[1], od_w_in[j].astype(BF16), mi_row)
            pa = pa.reshape(b, s, -1)
            ya = _mla(pa[..., :MLA_IN], mla_qa_g[j], mla_w_qup[j], mla_kva_g[j], mla_w_kvup[j], cos_m, sin_m, need_ctx)
            y5, y5c = _s5_core(pa[:, CTX_LEN:, MLA_IN:], pa[:, :CTX_LEN, MLA_IN:], s5_a_re[j], s5_a_im[j],
                               s5_log_dt[j], s5_b_re[j], s5_b_im[j], s5_c_re[j], s5_c_im[j], s5_d[j], need_ctx)
            if y5c is None:
                y5c = jnp.zeros((b, CTX_LEN, S5_WIDTH), F32)
            z = jax.nn.gelu(jnp.concatenate([y5c, y5], axis=1), approximate=False).reshape(b * s, S5_WIDTH)
            gl = matmul(z, s5_w_glu[j].astype(BF16)) + s5_b_glu[j]
            yb, ybc = (z * jax.nn.sigmoid(gl)).reshape(b, s, S5_WIDTH), None
            w_out = od_w_out[j]
        if ybc is not None:
            yb = jnp.concatenate([ybc, yb], axis=1)
        elif yb.shape[1] != s:
            yb = jnp.concatenate([jnp.zeros((b, CTX_LEN, yb.shape[-1]), F32), yb], axis=1)
        y = jnp.concatenate([ya, yb], axis=-1).reshape(b * s, -1)
        xa = matmul_residual(y, w_out.astype(BF16), xa, mod[2], mi_row)
        out, _ = peer(xa, norm2_g[layer][None, :], mod[3], mod[4], mod[5],
                      peer_wq[layer], peer_keys[layer], peer_u[layer], peer_v[layer], mi_peer)
        gate = jnp.concatenate([jnp.broadcast_to(mod[5][b:b + 1], (b, CTX_LEN, d)),
                                jnp.broadcast_to(mod[5][:b], (b, t, d))], axis=1).reshape(b * s, d)
        xa = xa + gate * out
    xl = xa.reshape(b, s, d)[:, CTX_LEN:].reshape(b * t, d)
    return rmsnorm_rows(xl, final_g[None, :]).reshape(b, t, d)
```

```python
import functools
import math

import jax
import jax.numpy as jnp
from jax import lax
from jax.experimental import pallas as pl
from jax.experimental.pallas import tpu as pltpu

F32 = jnp.float32
BF16 = jnp.bfloat16

D_MODEL = 1024
GRID_W = 64
CTX_LEN = 256
EPS = 1e-6
ROPE_THETA = 10000.0

MIX_HALF = D_MODEL // 2
HEAD_DIM = 64
GQA_Q_HEADS = MIX_HALF // HEAD_DIM
GQA_KV_HEADS = GQA_Q_HEADS // 4
GQA_IN = (GQA_Q_HEADS + 2 * GQA_KV_HEADS) * HEAD_DIM

HG_DK = 128
HG_DV = 128
HG_HEADS = MIX_HALF // HG_DV
HG_W = HG_HEADS * HG_DK
GLA_CHUNK = 64

MLA_HEADS = MIX_HALF // HEAD_DIM
MLA_NOPE = 64
MLA_ROPE = 32
MLA_V = 64
MLA_Q_RANK = 384
MLA_KV_RANK = 256
MLA_IN = MLA_Q_RANK + MLA_KV_RANK + MLA_ROPE

S5_WIDTH = MIX_HALF
S5_GROUP = 16
S5_GROUPS = S5_WIDTH // S5_GROUP
S5_STATE = 64

PEER_HEADS = 8
PEER_KEYS = 128
N_EXPERTS = PEER_KEYS * PEER_KEYS
PEER_TOPK = 16
PEER_QDIM = 256
PEER_HALF = PEER_QDIM // 2
PEER_PAIRS = PEER_HEADS * PEER_TOPK

V7X_LANES = 128
V7X_SUBLANES = 8
V7X_VMEM_BYTES = 64 * 1024 * 1024
VMEM_LIMIT = V7X_VMEM_BYTES - 8 * 1024 * 1024

ROW_TILE = 256
PEER_TILE = 128
ATTN_Q_TILE = 256
EXPERT_WORDS = D_MODEL // 2
EXPERT_ROWS = EXPERT_WORDS // V7X_LANES


def _cparams(*sem):
    return pltpu.CompilerParams(dimension_semantics=sem, vmem_limit_bytes=VMEM_LIMIT)


def _norm_mod(x, g, shift, scale):
    ms = jnp.mean(x * x, axis=-1, keepdims=True)
    h = (x * lax.rsqrt(ms + EPS)) * g
    return h * (1.0 + scale) + shift


def _nmm_kernel(x_ref, g_ref, sh_ref, sc_ref, w_ref, o_ref):
    h = _norm_mod(x_ref[...], g_ref[...], sh_ref[0], sc_ref[0])
    o_ref[...] = jnp.dot(h.astype(BF16), w_ref[...], preferred_element_type=F32)


def norm_mod_matmul(x, g, shift, scale, w, mod_index, tm=ROW_TILE):
    r, k = x.shape
    n = w.shape[1]
    assert r % tm == 0 and w.shape[0] == k
    return pl.pallas_call(
        _nmm_kernel,
        grid=(r // tm,),
        in_specs=[
            pl.BlockSpec((tm, k), lambda i: (i, 0)),
            pl.BlockSpec((1, k), lambda i: (0, 0)),
            pl.BlockSpec((1, 1, k), lambda i: (mod_index(i), 0, 0)),
            pl.BlockSpec((1, 1, k), lambda i: (mod_index(i), 0, 0)),
            pl.BlockSpec((k, n), lambda i: (0, 0)),
        ],
        out_specs=pl.BlockSpec((tm, n), lambda i: (i, 0)),
        out_shape=jax.ShapeDtypeStruct((r, n), F32),
        compiler_params=_cparams("parallel"),
        name="norm_mod_matmul",
    )(x, g, shift, scale, w)


def _mm_kernel(a_ref, w_ref, o_ref):
    o_ref[...] = jnp.dot(a_ref[...].astype(BF16), w_ref[...], preferred_element_type=F32)


def matmul(a, w, tm=ROW_TILE):
    r, k = a.shape
    n = w.shape[1]
    assert r % tm == 0
    return pl.pallas_call(
        _mm_kernel,
        grid=(r // tm,),
        in_specs=[pl.BlockSpec((tm, k), lambda i: (i, 0)), pl.BlockSpec((k, n), lambda i: (0, 0))],
        out_specs=pl.BlockSpec((tm, n), lambda i: (i, 0)),
        out_shape=jax.ShapeDtypeStruct((r, n), F32),
        compiler_params=_cparams("parallel"),
        name="matmul",
    )(a, w)


def _mmres_kernel(a_ref, w_ref, x_ref, gate_ref, o_ref):
    y = jnp.dot(a_ref[...].astype(BF16), w_ref[...], preferred_element_type=F32)
    o_ref[...] = x_ref[...] + gate_ref[0] * y


def matmul_residual(a, w, x, gate, mod_index, tm=ROW_TILE):
    r, k = a.shape
    n = w.shape[1]
    assert r % tm == 0
    return pl.pallas_call(
        _mmres_kernel,
        grid=(r // tm,),
        in_specs=[
            pl.BlockSpec((tm, k), lambda i: (i, 0)),
            pl.BlockSpec((k, n), lambda i: (0, 0)),
            pl.BlockSpec((tm, n), lambda i: (i, 0)),
            pl.BlockSpec((1, 1, n), lambda i: (mod_index(i), 0, 0)),
        ],
        out_specs=pl.BlockSpec((tm, n), lambda i: (i, 0)),
        out_shape=jax.ShapeDtypeStruct((r, n), F32),
        compiler_params=_cparams("parallel"),
        name="matmul_residual",
    )(a, w, x, gate)


def _rms_kernel(x_ref, g_ref, o_ref):
    x = x_ref[...]
    ms = jnp.mean(x * x, axis=-1, keepdims=True)
    o_ref[...] = (x * lax.rsqrt(ms + EPS)) * g_ref[...]


def rmsnorm_rows(x, g, tm=ROW_TILE):
    r, k = x.shape
    return pl.pallas_call(
        _rms_kernel,
        grid=(r // tm,),
        in_specs=[pl.BlockSpec((tm, k), lambda i: (i, 0)), pl.BlockSpec((1, k), lambda i: (0, 0))],
        out_specs=pl.BlockSpec((tm, k), lambda i: (i, 0)),
        out_shape=jax.ShapeDtypeStruct((r, k), F32),
        compiler_params=_cparams("parallel"),
        name="rmsnorm_rows",
    )(x, g)


def _attn_kernel(q_ref, k_ref, v_ref, o_ref):
    q = q_ref[0, 0]
    s = lax.dot_general(q, k_ref[0, 0], (((1,), (1,)), ((), ())), preferred_element_type=F32)
    m = jnp.max(s, axis=-1, keepdims=True)
    p = jnp.exp(s - m)
    l = jnp.sum(p, axis=-1, keepdims=True)
    o = jnp.dot(p.astype(BF16), v_ref[0, 0], preferred_element_type=F32)
    o_ref[0, 0] = o / l


def attention(q, k, v, tq=ATTN_Q_TILE):
    b, h, sq, dl = q.shape
    hk, sk = k.shape[1], k.shape[2]
    grp = h // hk
    tq = min(tq, sq)
    assert sq % tq == 0
    return pl.pallas_call(
        _attn_kernel,
        grid=(b, h, sq // tq),
        in_specs=[
            pl.BlockSpec((1, 1, tq, dl), lambda bi, hi, qi: (bi, hi, qi, 0)),
            pl.BlockSpec((1, 1, sk, dl), lambda bi, hi, qi: (bi, hi // grp, 0, 0)),
            pl.BlockSpec((1, 1, sk, dl), lambda bi, hi, qi: (bi, hi // grp, 0, 0)),
        ],
        out_specs=pl.BlockSpec((1, 1, tq, dl), lambda bi, hi, qi: (bi, hi, qi, 0)),
        out_shape=jax.ShapeDtypeStruct((b, h, sq, dl), F32),
        compiler_params=_cparams("parallel", "parallel", "parallel"),
        name="attention",
    )(q, k, v)


def _topk_rows(s, iota, k):
    n = s.shape[0]
    row = lax.broadcasted_iota(jnp.int32, (k, s.shape[1]), 0)
    vals = jnp.zeros((k, s.shape[1]), F32)
    ids = jnp.zeros((k, s.shape[1]), F32)
    for r in range(k):
        m = jnp.max(s, axis=0, keepdims=True)
        am = jnp.min(jnp.where(s == m, iota, float(n)), axis=0, keepdims=True)
        vals = jnp.where(row == r, m, vals)
        ids = jnp.where(row == r, am, ids)
        s = jnp.where(iota == am, -jnp.inf, s)
    return vals, ids


def _peer_ret_kernel(x_ref, g_ref, sh_ref, sc_ref, wq_ref, keys_ref, h_ref, eidx_ref, gw_ref, hb_ref):
    h = _norm_mod(x_ref[...], g_ref[...], sh_ref[0], sc_ref[0])
    h_ref[...] = h
    hb_ref[...] = h.astype(BF16)
    tb = x_ref.shape[0]
    iota_n = lax.broadcasted_iota(jnp.int32, (PEER_KEYS, tb), 0).astype(F32)
    iota_c = lax.broadcasted_iota(jnp.int32, (PEER_TOPK * PEER_TOPK, tb), 0).astype(F32)

    def head(hd, carry):
        q = jnp.dot(hb_ref[...], wq_ref[hd], preferred_element_type=F32)
        tops = []
        for p in range(2):
            qp = q[:, p * PEER_HALF:(p + 1) * PEER_HALF].astype(BF16)
            s = lax.dot_general(keys_ref[hd, p], qp, (((1,), (1,)), ((), ())),
                                preferred_element_type=F32)
            tops.append(_topk_rows(s, iota_n, PEER_TOPK))
        (s1, i1), (s2, i2) = tops
        cand = jnp.concatenate([s1[a:a + 1] + s2 for a in range(PEER_TOPK)], axis=0)
        cidx = jnp.concatenate([i1[a:a + 1] * float(PEER_KEYS) + i2 for a in range(PEER_TOPK)], axis=0)
        row = lax.broadcasted_iota(jnp.int32, (PEER_TOPK, tb), 0)
        sc = jnp.zeros((PEER_TOPK, tb), F32)
        ex = jnp.zeros((PEER_TOPK, tb), F32)
        for r in range(PEER_TOPK):
            m = jnp.max(cand, axis=0, keepdims=True)
            am = jnp.min(jnp.where(cand == m, iota_c, float(PEER_TOPK * PEER_TOPK)), axis=0, keepdims=True)
            hit = iota_c == am
            e = jnp.max(jnp.where(hit, cidx, 0.0), axis=0, keepdims=True)
            sc = jnp.where(row == r, m, sc)
            ex = jnp.where(row == r, e, ex)
            cand = jnp.where(hit, -jnp.inf, cand)
        pexp = jnp.exp(sc - sc[0:1])
        gw_ref[hd] = pexp / jnp.sum(pexp, axis=0, keepdims=True)
        eidx_ref[hd] = ex.astype(jnp.int32)
        return carry

    lax.fori_loop(0, PEER_HEADS, head, 0)


def peer_retrieve(x, g, shift, scale, wq_heads, keys, mod_index, tb=PEER_TILE):
    r, d = x.shape
    nb = r // tb
    return pl.pallas_call(
        _peer_ret_kernel,
        grid=(nb,),
        in_specs=[
            pl.BlockSpec((tb, d), lambda i: (i, 0)),
            pl.BlockSpec((1, d), lambda i: (0, 0)),
            pl.BlockSpec((1, 1, d), lambda i: (mod_index(i), 0, 0)),
            pl.BlockSpec((1, 1, d), lambda i: (mod_index(i), 0, 0)),
            pl.BlockSpec((PEER_HEADS, d, PEER_QDIM), lambda i: (0, 0, 0)),
            pl.BlockSpec((PEER_HEADS, 2, PEER_KEYS, PEER_HALF), lambda i: (0, 0, 0, 0)),
        ],
        out_specs=[
            pl.BlockSpec((tb, d), lambda i: (i, 0)),
            pl.BlockSpec((None, PEER_HEADS, PEER_TOPK, tb), lambda i: (i, 0, 0, 0)),
            pl.BlockSpec((None, PEER_HEADS, PEER_TOPK, tb), lambda i: (i, 0, 0, 0)),
        ],
        out_shape=[
            jax.ShapeDtypeStruct((r, d), F32),
            jax.ShapeDtypeStruct((nb, PEER_HEADS, PEER_TOPK, tb), jnp.int32),
            jax.ShapeDtypeStruct((nb, PEER_HEADS, PEER_TOPK, tb), F32),
        ],
        scratch_shapes=[pltpu.VMEM((tb, d), BF16)],
        compiler_params=_cparams("parallel"),
        name="peer_retrieve",
    )(x, g, shift, scale, wq_heads, keys)


def pack_expert_table(tab):
    e, d = tab.shape
    tb16 = lax.bitcast_convert_type(tab.astype(BF16), jnp.uint16).astype(jnp.uint32)
    word = tb16[:, :d // 2] | (tb16[:, d // 2:] << 16)
    return lax.bitcast_convert_type(word, jnp.int32).reshape(e * EXPERT_ROWS, V7X_LANES)


def _unpack(w):
    lo = lax.bitcast_convert_type(w << 16, F32)
    hi = lax.bitcast_convert_type(w & jnp.int32(-65536), F32)
    return lo, hi


def _expert_slab(tab_ref, e):
    return tab_ref[pl.ds(pl.multiple_of(e * EXPERT_ROWS, EXPERT_ROWS), EXPERT_ROWS), :]


def _peer_u_kernel(idx_ref, tab_ref, h_ref, o_ref, s_ref):
    tb = o_ref.shape[0]
    ones = jnp.ones((V7X_SUBLANES, V7X_LANES), F32)

    def tok(t, carry):
        hq = h_ref[pl.ds(pl.multiple_of(t * V7X_SUBLANES, V7X_SUBLANES), V7X_SUBLANES), :]
        h_lo = hq[0:EXPERT_ROWS]
        h_hi = hq[EXPERT_ROWS:2 * EXPERT_ROWS]
        for j in range(PEER_PAIRS):
            lo, hi = _unpack(_expert_slab(tab_ref, idx_ref[t * PEER_PAIRS + j]))
            s_ref[pl.ds(j, 1), :] = jnp.sum(lo * h_lo + hi * h_hi, axis=0, keepdims=True)
        a = lax.dot_general(ones, s_ref[...], (((1,), (1,)), ((), ())),
                            precision=lax.Precision.HIGHEST, preferred_element_type=F32)
        o_ref[pl.ds(t, 1), :] = a[0:1]
        return carry

    lax.fori_loop(0, tb, tok, 0)


def peer_expert_dots(idx, tab, h8, tb=PEER_TILE):
    r = h8.shape[0] // V7X_SUBLANES
    return pl.pallas_call(
        _peer_u_kernel,
        grid=(r // tb,),
        in_specs=[
            pl.BlockSpec((tb * PEER_PAIRS,), lambda i: (i,), memory_space=pltpu.SMEM),
            pl.BlockSpec(tab.shape, lambda i: (0, 0), pipeline_mode=pl.Buffered(1)),
            pl.BlockSpec((tb * V7X_SUBLANES, V7X_LANES), lambda i: (i, 0)),
        ],
        out_specs=pl.BlockSpec((tb, PEER_PAIRS), lambda i: (i, 0)),
        out_shape=jax.ShapeDtypeStruct((r, PEER_PAIRS), F32),
        scratch_shapes=[pltpu.VMEM((PEER_PAIRS, V7X_LANES), F32)],
        compiler_params=_cparams("arbitrary"),
        name="peer_expert_dots",
    )(idx, tab, h8)


PEER_V_CHAINS = 4


def _peer_v_kernel(idx_ref, w_ref, tab_ref, o_ref):
    tb = o_ref.shape[0] // V7X_SUBLANES

    def tok(t, carry):
        acc = [[jnp.zeros((EXPERT_ROWS, V7X_LANES), F32) for _ in range(2)] for _ in range(PEER_V_CHAINS)]
        for j in range(PEER_PAIRS):
            lo, hi = _unpack(_expert_slab(tab_ref, idx_ref[t * PEER_PAIRS + j]))
            wj = w_ref[t * PEER_PAIRS + j]
            c = j % PEER_V_CHAINS
            acc[c][0] = acc[c][0] + wj * lo
            acc[c][1] = acc[c][1] + wj * hi
        lo = (acc[0][0] + acc[1][0]) + (acc[2][0] + acc[3][0])
        hi = (acc[0][1] + acc[1][1]) + (acc[2][1] + acc[3][1])
        base = t * V7X_SUBLANES
        o_ref[pl.ds(pl.multiple_of(base, EXPERT_ROWS), EXPERT_ROWS), :] = lo
        o_ref[pl.ds(pl.multiple_of(base + EXPERT_ROWS, EXPERT_ROWS), EXPERT_ROWS), :] = hi
        return carry

    lax.fori_loop(0, tb, tok, 0)


def peer_expert_mix(idx, w, tab, tb=PEER_TILE):
    r = idx.shape[0] // PEER_PAIRS
    return pl.pallas_call(
        _peer_v_kernel,
        grid=(r // tb,),
        in_specs=[
            pl.BlockSpec((tb * PEER_PAIRS,), lambda i: (i,), memory_space=pltpu.SMEM),
            pl.BlockSpec((tb * PEER_PAIRS,), lambda i: (i,), memory_space=pltpu.SMEM),
            pl.BlockSpec(tab.shape, lambda i: (0, 0), pipeline_mode=pl.Buffered(1)),
        ],
        out_specs=pl.BlockSpec((tb * V7X_SUBLANES, V7X_LANES), lambda i: (i, 0)),
        out_shape=jax.ShapeDtypeStruct((r * V7X_SUBLANES, V7X_LANES), F32),
        compiler_params=_cparams("arbitrary"),
        name="peer_expert_mix",
    )(idx, w, tab)


def peer(x, g, shift, scale, gate, wq, keys, u_tab, v_tab, mod_index_peer):
    r, d = x.shape
    wq_heads = wq.astype(BF16).reshape(d, PEER_HEADS, PEER_QDIM).transpose(1, 0, 2)
    h, eidx, gw = peer_retrieve(x, g, shift, scale, wq_heads, keys.astype(BF16), mod_index_peer)
    nb = eidx.shape[0]
    idx = eidx.transpose(0, 3, 1, 2).reshape(r * PEER_PAIRS)
    gw = gw.transpose(0, 3, 1, 2).reshape(r, PEER_PAIRS)
    h8 = h.reshape(r * V7X_SUBLANES, V7X_LANES)
    a = peer_expert_dots(idx, pack_expert_table(u_tab), h8)
    w = (jax.nn.gelu(a, approximate=False) * gw).reshape(r * PEER_PAIRS)
    out = peer_expert_mix(idx, w, pack_expert_table(v_tab)).reshape(r, d)
    return out, nb


def _rms_norm(x, g):
    xf = x.astype(F32)
    y = xf * lax.rsqrt(jnp.mean(xf * xf, axis=-1, keepdims=True) + EPS)
    return y * g.astype(F32)


def _gla_scan(q, k, v, logf, s0):
    b, h, t, dk = q.shape
    n = t // GLA_CHUNK

    def chunks(a):
        return jnp.moveaxis(a.reshape(b, h, n, GLA_CHUNK, a.shape[-1]), 2, 0)

    lower = jnp.tril(jnp.ones((GLA_CHUNK, GLA_CHUNK), dtype=bool))[:, :, None]

    def step(state, blk):
        qc, kc, vc, lc = blk
        cum = jnp.cumsum(lc, axis=2)
        rel = cum[:, :, :, None, :] - cum[:, :, None, :, :]
        dec = jnp.exp(jnp.where(lower, rel, -jnp.inf))
        att = jnp.einsum('bhtc,bhtsc,bhsc->bhts', qc, dec, kc)
        out = (jnp.einsum('bhts,bhsv->bhtv', att, vc)
               + jnp.einsum('bhtc,bhcv->bhtv', qc * jnp.exp(cum), state))
        last = cum[:, :, -1:, :]
        state = (jnp.exp(last[:, :, 0, :, None]) * state
                 + jnp.einsum('bhsc,bhsv->bhcv', kc * jnp.exp(last - cum), vc))
        return state, out

    s_final, o = lax.scan(step, s0, (chunks(q), chunks(k), chunks(v), chunks(logf)))
    return jnp.moveaxis(o, 0, 2).reshape(b, h, t, v.shape[-1]), s_final


def _hgrn2(pa, pc, lb, onorm_g, need_ctx):
    def heads(a):
        return a.reshape(a.shape[0], a.shape[1], HG_HEADS, -1).transpose(0, 2, 1, 3)

    def parts(p):
        q = heads(p[..., :HG_W])
        fs = (heads(p[..., HG_W:2 * HG_W]), heads(p[..., 2 * HG_W:3 * HG_W]))
        i = heads(p[..., 3 * HG_W:4 * HG_W])
        g = p[..., 4 * HG_W:]
        return q, fs, i, g

    def gate(fraw, lbd):
        lbd = lbd[None, :, None, :]
        f = lbd + (1.0 - lbd) * jax.nn.sigmoid(fraw)
        return 1.0 - f, jnp.log(f)

    q, fs, i, g = parts(pa)
    qc, fcs, ic, gc = parts(pc)
    s_zero = jnp.zeros((pa.shape[0], HG_HEADS, HG_DK, HG_DV), F32)
    o = 0.0
    oc = 0.0
    for d in range(2):
        fl = (lambda a: jnp.flip(a, axis=2)) if d == 1 else (lambda a: a)
        kc_, lc_ = gate(fcs[d], lb[d])
        k_, l_ = gate(fs[d], lb[d])
        o_ctx, s_ctx = _gla_scan(fl(qc), fl(kc_), fl(ic), fl(lc_), s_zero)
        o_lat, _ = _gla_scan(fl(q), fl(k_), fl(i), fl(l_), s_ctx)
        o = o + fl(o_lat)
        if need_ctx:
            oc = oc + fl(o_ctx)

    def readout(o_, g_):
        o_ = _rms_norm(o_, onorm_g).transpose(0, 2, 1, 3)
        return o_.reshape(o_.shape[0], o_.shape[1], -1) * jax.nn.silu(g_)

    return readout(o, g), (readout(oc, gc) if need_ctx else None)


S5_CHUNK = 64
S5_LANES = S5_GROUPS * S5_STATE


def _s5_kernel(u_ref, bre_ref, bim_ref, cre_ref, cim_ref, are_ref, aim_ref, y_ref, xre, xim, hre, him):
    nb = V7X_SUBLANES // 2
    tiles = u_ref.shape[0] // V7X_SUBLANES

    @pl.when(pl.program_id(1) == 0)
    def _():
        hre[...] = jnp.zeros_like(hre)
        him[...] = jnp.zeros_like(him)

    ub = u_ref[...].astype(BF16)
    xre[...] = jnp.dot(ub, bre_ref[...], preferred_element_type=F32)
    xim[...] = jnp.dot(ub, bim_ref[...], preferred_element_type=F32)
    shape = (V7X_SUBLANES, S5_LANES)
    second = lax.broadcasted_iota(jnp.int32, shape, 0) >= nb
    are = jnp.broadcast_to(are_ref[...], shape)
    aim = jnp.broadcast_to(aim_ref[...], shape)
    cre = jnp.where(second, are * are - aim * aim, are)
    cim = jnp.where(second, 2.0 * are * aim, aim)

    def step(k, carry):
        pr, pi = carry
        r = pl.ds(pl.multiple_of(k * V7X_SUBLANES, V7X_SUBLANES), V7X_SUBLANES)
        xr, xi = xre[r, :], xim[r, :]
        sr = jnp.where(second, pltpu.roll(xr, nb, axis=0), 0.0)
        si = jnp.where(second, pltpu.roll(xi, nb, axis=0), 0.0)
        nr = xr + (are * sr - aim * si) + (cre * pr - cim * pi)
        ni = xi + (are * si + aim * sr) + (cre * pi + cim * pr)
        xre[r, :] = nr
        xim[r, :] = ni
        return (jnp.where(second, nr, pltpu.roll(nr, nb, axis=0)),
                jnp.where(second, ni, pltpu.roll(ni, nb, axis=0)))

    hr, hi = lax.fori_loop(0, tiles, step, (hre[...], him[...]))
    hre[...] = hr
    him[...] = hi
    y_ref[...] = (jnp.dot(xre[...].astype(BF16), cre_ref[...], preferred_element_type=F32)
                  - jnp.dot(xim[...].astype(BF16), cim_ref[...], preferred_element_type=F32))


def s5_scan(u2, bre, bim, cre, cim, are, aim, nb):
    _, rows, w = u2.shape
    blk = S5_CHUNK * nb
    assert rows % blk == 0 and 2 * nb == V7X_SUBLANES
    wspec = lambda shape: pl.BlockSpec((None,) + shape, lambda d, i: (d, 0, 0))
    return pl.pallas_call(
        _s5_kernel,
        grid=(2, rows // blk),
        in_specs=[
            pl.BlockSpec((None, blk, w), lambda d, i: (d, i, 0)),
            wspec((w, S5_LANES)), wspec((w, S5_LANES)), wspec((S5_LANES, w)), wspec((S5_LANES, w)),
            wspec((1, S5_LANES)), wspec((1, S5_LANES)),
        ],
        out_specs=pl.BlockSpec((None, blk, w), lambda d, i: (d, i, 0)),
        out_shape=jax.ShapeDtypeStruct(u2.shape, F32),
        scratch_shapes=[pltpu.VMEM((blk, S5_LANES), F32), pltpu.VMEM((blk, S5_LANES), F32),
                        pltpu.VMEM((V7X_SUBLANES, S5_LANES), F32), pltpu.VMEM((V7X_SUBLANES, S5_LANES), F32)],
        compiler_params=_cparams("arbitrary", "arbitrary"),
        name="s5_scan",
    )(u2, bre, bim, cre, cim, are, aim)


def _flip_parts(a):
    return jnp.concatenate([jnp.flip(a[:, :CTX_LEN], axis=1), jnp.flip(a[:, CTX_LEN:], axis=1)], axis=1)


def _s5_core(u, a_re, a_im, log_dt, b_re, b_im, c_re, c_im, d_skip):
    b, s, w = u.shape
    dt = jnp.exp(log_dt)[..., None]
    mag = jnp.exp(a_re * dt)
    abar_re, abar_im = mag * jnp.cos(a_im * dt), mag * jnp.sin(a_im * dt)
    den = a_re * a_re + a_im * a_im
    k_re = ((abar_re - 1.0) * a_re + abar_im * a_im) / den
    k_im = (abar_im * a_re - (abar_re - 1.0) * a_im) / den
    bb_re = k_re[..., None] * b_re - k_im[..., None] * b_im
    bb_im = k_re[..., None] * b_im + k_im[..., None] * b_re
    eye = jnp.eye(S5_GROUPS, dtype=F32)
    bd_in = lambda m: jnp.einsum('dgpc,gh->dgchp', m, eye).reshape(2, w, S5_LANES).astype(BF16)
    bd_out = lambda m: jnp.einsum('dgcp,gh->dgphc', m, eye).reshape(2, S5_LANES, w).astype(BF16)
    u2 = jnp.stack([u, _flip_parts(u)]).transpose(0, 2, 1, 3).reshape(2, s * b, w)
    y2 = s5_scan(u2, bd_in(bb_re), bd_in(bb_im), bd_out(c_re), bd_out(c_im),
                 abar_re.reshape(2, 1, S5_LANES), abar_im.reshape(2, 1, S5_LANES), b)
    y2 = y2.reshape(2, s, b, w).transpose(0, 2, 1, 3)
    return d_skip * u + y2[0] + _flip_parts(y2[1])


def _rope_tables(rows, rot_dim):
    axis_dim = rot_dim // 2
    inv = ROPE_THETA ** (-jnp.arange(0, axis_dim, 2, dtype=F32) / axis_dim)
    t = jnp.arange(rows * GRID_W)
    r = (t // GRID_W).astype(F32)[:, None] * inv
    c = (t % GRID_W).astype(F32)[:, None] * inv
    cos = jnp.concatenate([jnp.cos(r), jnp.cos(r), jnp.cos(c), jnp.cos(c)], axis=-1)
    sin = jnp.concatenate([-jnp.sin(r), jnp.sin(r), -jnp.sin(c), jnp.sin(c)], axis=-1)
    return cos, sin


def _rope(x, cos, sin):
    q = x.shape[-1] // 4
    swapped = jnp.concatenate([x[..., q:2 * q], x[..., :q], x[..., 3 * q:], x[..., 2 * q:3 * q]], axis=-1)
    return x * cos[:, None, :] + swapped * sin[:, None, :]


def _rope_latent(x, cos, sin):
    return jnp.concatenate([x[:, :CTX_LEN], _rope(x[:, CTX_LEN:], cos, sin)], axis=1)


def _attend_all(q, k, v, need_ctx):
    q, k, v = q.astype(BF16), k.astype(BF16), v.astype(BF16)
    o_lat = attention(q[:, :, CTX_LEN:], k, v)
    if need_ctx:
        o_ctx = attention(q[:, :, :CTX_LEN], k[:, :, :CTX_LEN], v[:, :, :CTX_LEN])
    else:
        o_ctx = jnp.zeros(q.shape[:2] + (CTX_LEN, q.shape[3]), F32)
    return jnp.concatenate([o_ctx, o_lat], axis=2)


def _gqa(p, qn_g, kn_g, cos, sin, need_ctx):
    b, s, _ = p.shape
    nq = GQA_Q_HEADS * HEAD_DIM
    nk = GQA_KV_HEADS * HEAD_DIM
    grp = GQA_Q_HEADS // GQA_KV_HEADS
    q = _rms_norm(p[..., :nq].reshape(b, s, GQA_Q_HEADS, HEAD_DIM), qn_g)
    k = _rms_norm(p[..., nq:nq + nk].reshape(b, s, GQA_KV_HEADS, HEAD_DIM), kn_g)
    v = p[..., nq + nk:nq + 2 * nk]
    q = _rope_latent(q, cos, sin) * (HEAD_DIM ** -0.5)
    k = _rope_latent(k, cos, sin)
    qh = q.transpose(0, 2, 1, 3)
    zeros = jnp.zeros_like(qh)
    first = (jnp.arange(GQA_Q_HEADS) < grp)[None, :, None, None]
    q_pad = jnp.concatenate([jnp.where(first, qh, zeros), jnp.where(first, zeros, qh)], axis=-1)
    k_all = k.reshape(b, 1, s, nk)
    v_all = v.reshape(b, 1, s, nk)
    o = _attend_all(q_pad, k_all, v_all, need_ctx)
    o = jnp.where(first, o[..., :HEAD_DIM], o[..., HEAD_DIM:])
    return o.transpose(0, 2, 1, 3).reshape(b, s, nq)


def _mla(p, qa_g, w_qup, kva_g, w_kvup, cos, sin, need_ctx):
    b, s, _ = p.shape
    r0 = MLA_Q_RANK
    r1 = MLA_Q_RANK + MLA_KV_RANK
    zero = jnp.zeros((1, 1, r0), F32)
    qa = p[..., :r0].reshape(b * s, r0)
    q = norm_mod_matmul(qa, qa_g[None, :], zero, zero, w_qup.astype(BF16), lambda i: 0)
    q = q.reshape(b, s, MLA_HEADS, MLA_NOPE + MLA_ROPE)
    zero = jnp.zeros((1, 1, MLA_KV_RANK), F32)
    kva = p[..., r0:r1].reshape(b * s, MLA_KV_RANK)
    kv = norm_mod_matmul(kva, kva_g[None, :], zero, zero, w_kvup.astype(BF16), lambda i: 0)
    kv = kv.reshape(b, s, MLA_HEADS, MLA_NOPE + MLA_V)
    q_rope = _rope_latent(q[..., MLA_NOPE:], cos, sin)
    k_rope = _rope_latent(p[..., r1:r1 + MLA_ROPE][:, :, None, :], cos, sin)
    scale = (MLA_NOPE + MLA_ROPE) ** -0.5
    pad = jnp.zeros((b, s, MLA_HEADS, V7X_LANES - MLA_NOPE - MLA_ROPE), F32)
    q_pad = jnp.concatenate([q[..., :MLA_NOPE] * scale, q_rope * scale, pad], axis=-1)
    k_pad = jnp.concatenate([kv[..., :MLA_NOPE], jnp.broadcast_to(k_rope, (b, s, MLA_HEADS, MLA_ROPE)), pad], axis=-1)
    v_pad = jnp.concatenate([kv[..., MLA_NOPE:], jnp.zeros((b, s, MLA_HEADS, V7X_LANES - MLA_V), F32)], axis=-1)
    tr = lambda a: a.transpose(0, 2, 1, 3)
    o = _attend_all(tr(q_pad), tr(k_pad), tr(v_pad), need_ctx)
    return o[..., :MLA_V].transpose(0, 2, 1, 3).reshape(b, s, MLA_HEADS * MLA_V)


def kernel(x, c, ctx, c_ctx, ada_w, ada_b, norm1_g, norm2_g, ev_w_in, ev_w_out, gqa_qn_g, gqa_kn_g,
           hg_lb_logits, hg_onorm_g, od_w_in, od_w_out, mla_qa_g, mla_w_qup, mla_kva_g, mla_w_kvup,
           s5_a_re, s5_a_im, s5_log_dt, s5_b_re, s5_b_im, s5_c_re, s5_c_im, s5_d, s5_w_glu, s5_b_glu,
           peer_wq, peer_keys, peer_u, peer_v, final_g):
    b, t, d = x.shape
    s = CTX_LEN + t
    depth = ada_w.shape[0]
    rows = t // GRID_W
    cos_g, sin_g = _rope_tables(rows, HEAD_DIM)
    cos_m, sin_m = _rope_tables(rows, MLA_ROPE)
    lb_all = jnp.cumsum(jax.nn.softmax(hg_lb_logits, axis=1), axis=1)

    def mod_index(tile):
        per, nctx = s // tile, CTX_LEN // tile
        return lambda i: jnp.where(i % per < nctx, b, i // per)

    mi_row = mod_index(ROW_TILE)
    mi_peer = mod_index(PEER_TILE)

    xa = jnp.concatenate([ctx, x], axis=1).reshape(b * s, d)
    s_all = jnp.concatenate([jax.nn.silu(c), jax.nn.silu(c_ctx)[None, :]], axis=0)
    s_pad = jnp.concatenate([s_all, jnp.zeros((V7X_SUBLANES - (b + 1) % V7X_SUBLANES, d), F32)], axis=0)

    for layer in range(depth):
        need_ctx = layer < depth - 1
        j = layer // 2
        mod = matmul(s_pad, ada_w[layer].astype(BF16), tm=s_pad.shape[0])[:b + 1] + ada_b[layer]
        mod = [m[:, None, :] for m in jnp.split(mod, 6, axis=-1)]
        if layer % 2 == 0:
            pa = norm_mod_matmul(xa, norm1_g[layer][None, :], mod[0], mod[1], ev_w_in[j].astype(BF16), mi_row)
            pa = pa.reshape(b, s, -1)
            ya = _gqa(pa[..., :GQA_IN], gqa_qn_g[j], gqa_kn_g[j], cos_g, sin_g, need_ctx)
            yb, ybc = _hgrn2(pa[:, CTX_LEN:, GQA_IN:], pa[:, :CTX_LEN, GQA_IN:],
                             lb_all[:, j].reshape(2, HG_HEADS, HG_DK), hg_onorm_g[j], need_ctx)
            w_out = ev_w_out[j]
        else:
            pa = norm_mod_matmul(xa, norm1_g[layer][None, :], mod[0], mod[1], od_w_in[j].astype(BF16), mi_row)
            pa = pa.reshape(b, s, -1)
            ya = _mla(pa[..., :MLA_IN], mla_qa_g[j], mla_w_qup[j], mla_kva_g[j], mla_w_kvup[j], cos_m, sin_m, need_ctx)
            y5 = _s5_core(pa[..., MLA_IN:], s5_a_re[j], s5_a_im[j], s5_log_dt[j], s5_b_re[j], s5_b_im[j],
                          s5_c_re[j], s5_c_im[j], s5_d[j])
            z = jax.nn.gelu(y5, approximate=False).reshape(b * s, S5_WIDTH)
            gl = matmul(z, s5_w_glu[j].astype(BF16)) + s5_b_glu[j]
            yb, ybc = (z * jax.nn.sigmoid(gl)).reshape(b, s, S5_WIDTH), None
            w_out = od_w_out[j]
        if ybc is not None:
            yb = jnp.concatenate([ybc, yb], axis=1)
        elif yb.shape[1] != s:
            yb = jnp.concatenate([jnp.zeros((b, CTX_LEN, yb.shape[-1]), F32), yb], axis=1)
        y = jnp.concatenate([ya, yb], axis=-1).reshape(b * s, -1)
        xa = matmul_residual(y, w_out.astype(BF16), xa, mod[2], mi_row)
        out, _ = peer(xa, norm2_g[layer][None, :], mod[3], mod[4], mod[5],
                      peer_wq[layer], peer_keys[layer], peer_u[layer], peer_v[layer], mi_peer)
        gate = jnp.concatenate([jnp.broadcast_to(mod[5][b:b + 1], (b, CTX_LEN, d)),
                                jnp.broadcast_to(mod[5][:b], (b, t, d))], axis=1).reshape(b * s, d)
        xa = xa + gate * out
    xl = xa.reshape(b, s, d)[:, CTX_LEN:].reshape(b * t, d)
    return rmsnorm_rows(xl, final_g[None, :]).reshape(b, t, d)
```

```python
import functools
import math

import jax
import jax.numpy as jnp
from jax import lax
from jax.experimental import pallas as pl
from jax.experimental.pallas import tpu as pltpu

F32 = jnp.float32
BF16 = jnp.bfloat16

D_MODEL = 1024
GRID_W = 64
CTX_LEN = 256
EPS = 1e-6
ROPE_THETA = 10000.0

MIX_HALF = D_MODEL // 2
HEAD_DIM = 64
GQA_Q_HEADS = MIX_HALF // HEAD_DIM
GQA_KV_HEADS = GQA_Q_HEADS // 4
GQA_IN = (GQA_Q_HEADS + 2 * GQA_KV_HEADS) * HEAD_DIM

HG_DK = 128
HG_DV = 128
HG_HEADS = MIX_HALF // HG_DV
HG_W = HG_HEADS * HG_DK
GLA_CHUNK = 64

MLA_HEADS = MIX_HALF // HEAD_DIM
MLA_NOPE = 64
MLA_ROPE = 32
MLA_V = 64
MLA_Q_RANK = 384
MLA_KV_RANK = 256
MLA_IN = MLA_Q_RANK + MLA_KV_RANK + MLA_ROPE

S5_WIDTH = MIX_HALF
S5_GROUP = 16
S5_GROUPS = S5_WIDTH // S5_GROUP
S5_STATE = 64

PEER_HEADS = 8
PEER_KEYS = 128
N_EXPERTS = PEER_KEYS * PEER_KEYS
PEER_TOPK = 16
PEER_QDIM = 256
PEER_HALF = PEER_QDIM // 2
PEER_PAIRS = PEER_HEADS * PEER_TOPK

V7X_LANES = 128
V7X_SUBLANES = 8
V7X_VMEM_BYTES = 64 * 1024 * 1024
VMEM_LIMIT = V7X_VMEM_BYTES - 8 * 1024 * 1024

ROW_TILE = 256
PEER_TILE = 128
ATTN_Q_TILE = 256
EXPERT_WORDS = D_MODEL // 2
EXPERT_ROWS = EXPERT_WORDS // V7X_LANES


def _cparams(*sem):
    return pltpu.CompilerParams(dimension_semantics=sem, vmem_limit_bytes=VMEM_LIMIT)


def _norm_mod(x, g, shift, scale):
    ms = jnp.mean(x * x, axis=-1, keepdims=True)
    h = (x * lax.rsqrt(ms + EPS)) * g
    return h * (1.0 + scale) + shift


def _nmm_kernel(x_ref, g_ref, sh_ref, sc_ref, w_ref, o_ref):
    h = _norm_mod(x_ref[...], g_ref[...], sh_ref[0], sc_ref[0])
    o_ref[...] = jnp.dot(h.astype(BF16), w_ref[...], preferred_element_type=F32)


def norm_mod_matmul(x, g, shift, scale, w, mod_index, tm=ROW_TILE):
    r, k = x.shape
    n = w.shape[1]
    assert r % tm == 0 and w.shape[0] == k
    return pl.pallas_call(
        _nmm_kernel,
        grid=(r // tm,),
        in_specs=[
            pl.BlockSpec((tm, k), lambda i: (i, 0)),
            pl.BlockSpec((1, k), lambda i: (0, 0)),
            pl.BlockSpec((1, 1, k), lambda i: (mod_index(i), 0, 0)),
            pl.BlockSpec((1, 1, k), lambda i: (mod_index(i), 0, 0)),
            pl.BlockSpec((k, n), lambda i: (0, 0)),
        ],
        out_specs=pl.BlockSpec((tm, n), lambda i: (i, 0)),
        out_shape=jax.ShapeDtypeStruct((r, n), F32),
        compiler_params=_cparams("parallel"),
        name="norm_mod_matmul",
    )(x, g, shift, scale, w)


def _mm_kernel(a_ref, w_ref, o_ref):
    o_ref[...] = jnp.dot(a_ref[...].astype(BF16), w_ref[...], preferred_element_type=F32)


def matmul(a, w, tm=ROW_TILE):
    r, k = a.shape
    n = w.shape[1]
    assert r % tm == 0
    return pl.pallas_call(
        _mm_kernel,
        grid=(r // tm,),
        in_specs=[pl.BlockSpec((tm, k), lambda i: (i, 0)), pl.BlockSpec((k, n), lambda i: (0, 0))],
        out_specs=pl.BlockSpec((tm, n), lambda i: (i, 0)),
        out_shape=jax.ShapeDtypeStruct((r, n), F32),
        compiler_params=_cparams("parallel"),
        name="matmul",
    )(a, w)


def _mmres_kernel(a_ref, w_ref, x_ref, gate_ref, o_ref):
    y = jnp.dot(a_ref[...].astype(BF16), w_ref[...], preferred_element_type=F32)
    o_ref[...] = x_ref[...] + gate_ref[0] * y


def matmul_residual(a, w, x, gate, mod_index, tm=ROW_TILE):
    r, k = a.shape
    n = w.shape[1]
    assert r % tm == 0
    return pl.pallas_call(
        _mmres_kernel,
        grid=(r // tm,),
        in_specs=[
            pl.BlockSpec((tm, k), lambda i: (i, 0)),
            pl.BlockSpec((k, n), lambda i: (0, 0)),
            pl.BlockSpec((tm, n), lambda i: (i, 0)),
            pl.BlockSpec((1, 1, n), lambda i: (mod_index(i), 0, 0)),
        ],
        out_specs=pl.BlockSpec((tm, n), lambda i: (i, 0)),
        out_shape=jax.ShapeDtypeStruct((r, n), F32),
        compiler_params=_cparams("parallel"),
        name="matmul_residual",
    )(a, w, x, gate)


def _rms_kernel(x_ref, g_ref, o_ref):
    x = x_ref[...]
    ms = jnp.mean(x * x, axis=-1, keepdims=True)
    o_ref[...] = (x * lax.rsqrt(ms + EPS)) * g_ref[...]


def rmsnorm_rows(x, g, tm=ROW_TILE):
    r, k = x.shape
    return pl.pallas_call(
        _rms_kernel,
        grid=(r // tm,),
        in_specs=[pl.BlockSpec((tm, k), lambda i: (i, 0)), pl.BlockSpec((1, k), lambda i: (0, 0))],
        out_specs=pl.BlockSpec((tm, k), lambda i: (i, 0)),
        out_shape=jax.ShapeDtypeStruct((r, k), F32),
        compiler_params=_cparams("parallel"),
        name="rmsnorm_rows",
    )(x, g)


def _attn_kernel(q_ref, k_ref, v_ref, o_ref):
    q = q_ref[0, 0]
    s = lax.dot_general(q, k_ref[0, 0], (((1,), (1,)), ((), ())), preferred_element_type=F32)
    m = jnp.max(s, axis=-1, keepdims=True)
    p = jnp.exp(s - m)
    l = jnp.sum(p, axis=-1, keepdims=True)
    o = jnp.dot(p.astype(BF16), v_ref[0, 0], preferred_element_type=F32)
    o_ref[0, 0] = o / l


def attention(q, k, v, tq=ATTN_Q_TILE):
    b, h, sq, dl = q.shape
    hk, sk = k.shape[1], k.shape[2]
    grp = h // hk
    tq = min(tq, sq)
    assert sq % tq == 0
    return pl.pallas_call(
        _attn_kernel,
        grid=(b, h, sq // tq),
        in_specs=[
            pl.BlockSpec((1, 1, tq, dl), lambda bi, hi, qi: (bi, hi, qi, 0)),
            pl.BlockSpec((1, 1, sk, dl), lambda bi, hi, qi: (bi, hi // grp, 0, 0)),
            pl.BlockSpec((1, 1, sk, dl), lambda bi, hi, qi: (bi, hi // grp, 0, 0)),
        ],
        out_specs=pl.BlockSpec((1, 1, tq, dl), lambda bi, hi, qi: (bi, hi, qi, 0)),
        out_shape=jax.ShapeDtypeStruct((b, h, sq, dl), F32),
        compiler_params=_cparams("parallel", "parallel", "parallel"),
        name="attention",
    )(q, k, v)


def _topk_rows(s, iota, k):
    n = s.shape[0]
    row = lax.broadcasted_iota(jnp.int32, (k, s.shape[1]), 0)
    vals = jnp.zeros((k, s.shape[1]), F32)
    ids = jnp.zeros((k, s.shape[1]), F32)
    for r in range(k):
        m = jnp.max(s, axis=0, keepdims=True)
        am = jnp.min(jnp.where(s == m, iota, float(n)), axis=0, keepdims=True)
        vals = jnp.where(row == r, m, vals)
        ids = jnp.where(row == r, am, ids)
        s = jnp.where(iota == am, -jnp.inf, s)
    return vals, ids


def _peer_ret_kernel(x_ref, g_ref, sh_ref, sc_ref, wq_ref, keys_ref, h_ref, eidx_ref, gw_ref, hb_ref):
    h = _norm_mod(x_ref[...], g_ref[...], sh_ref[0], sc_ref[0])
    h_ref[...] = h
    hb_ref[...] = h.astype(BF16)
    tb = x_ref.shape[0]
    iota_n = lax.broadcasted_iota(jnp.int32, (PEER_KEYS, tb), 0).astype(F32)
    iota_c = lax.broadcasted_iota(jnp.int32, (PEER_TOPK * PEER_TOPK, tb), 0).astype(F32)

    def head(hd, carry):
        q = jnp.dot(hb_ref[...], wq_ref[hd], preferred_element_type=F32)
        tops = []
        for p in range(2):
            qp = q[:, p * PEER_HALF:(p + 1) * PEER_HALF].astype(BF16)
            s = lax.dot_general(keys_ref[hd, p], qp, (((1,), (1,)), ((), ())),
                                preferred_element_type=F32)
            tops.append(_topk_rows(s, iota_n, PEER_TOPK))
        (s1, i1), (s2, i2) = tops
        cand = jnp.concatenate([s1[a:a + 1] + s2 for a in range(PEER_TOPK)], axis=0)
        cidx = jnp.concatenate([i1[a:a + 1] * float(PEER_KEYS) + i2 for a in range(PEER_TOPK)], axis=0)
        row = lax.broadcasted_iota(jnp.int32, (PEER_TOPK, tb), 0)
        sc = jnp.zeros((PEER_TOPK, tb), F32)
        ex = jnp.zeros((PEER_TOPK, tb), F32)
        for r in range(PEER_TOPK):
            m = jnp.max(cand, axis=0, keepdims=True)
            am = jnp.min(jnp.where(cand == m, iota_c, float(PEER_TOPK * PEER_TOPK)), axis=0, keepdims=True)
            hit = iota_c == am
            e = jnp.max(jnp.where(hit, cidx, 0.0), axis=0, keepdims=True)
            sc = jnp.where(row == r, m, sc)
            ex = jnp.where(row == r, e, ex)
            cand = jnp.where(hit, -jnp.inf, cand)
        pexp = jnp.exp(sc - sc[0:1])
        gw_ref[hd] = pexp / jnp.sum(pexp, axis=0, keepdims=True)
        eidx_ref[hd] = ex.astype(jnp.int32)
        return carry

    lax.fori_loop(0, PEER_HEADS, head, 0)


def peer_retrieve(x, g, shift, scale, wq_heads, keys, mod_index, tb=PEER_TILE):
    r, d = x.shape
    nb = r // tb
    return pl.pallas_call(
        _peer_ret_kernel,
        grid=(nb,),
        in_specs=[
            pl.BlockSpec((tb, d), lambda i: (i, 0)),
            pl.BlockSpec((1, d), lambda i: (0, 0)),
            pl.BlockSpec((1, 1, d), lambda i: (mod_index(i), 0, 0)),
            pl.BlockSpec((1, 1, d), lambda i: (mod_index(i), 0, 0)),
            pl.BlockSpec((PEER_HEADS, d, PEER_QDIM), lambda i: (0, 0, 0)),
            pl.BlockSpec((PEER_HEADS, 2, PEER_KEYS, PEER_HALF), lambda i: (0, 0, 0, 0)),
        ],
        out_specs=[
            pl.BlockSpec((tb, d), lambda i: (i, 0)),
            pl.BlockSpec((None, PEER_HEADS, PEER_TOPK, tb), lambda i: (i, 0, 0, 0)),
            pl.BlockSpec((None, PEER_HEADS, PEER_TOPK, tb), lambda i: (i, 0, 0, 0)),
        ],
        out_shape=[
            jax.ShapeDtypeStruct((r, d), F32),
            jax.ShapeDtypeStruct((nb, PEER_HEADS, PEER_TOPK, tb), jnp.int32),
            jax.ShapeDtypeStruct((nb, PEER_HEADS, PEER_TOPK, tb), F32),
        ],
        scratch_shapes=[pltpu.VMEM((tb, d), BF16)],
        compiler_params=_cparams("parallel"),
        name="peer_retrieve",
    )(x, g, shift, scale, wq_heads, keys)


def pack_rows_bf16(tab):
    e, d = tab.shape
    t16 = lax.bitcast_convert_type(tab.astype(BF16), jnp.uint16).astype(jnp.uint32)
    t16 = t16.reshape(e, EXPERT_ROWS, 2, V7X_LANES)
    word = t16[:, :, 0, :] | (t16[:, :, 1, :] << 16)
    return lax.bitcast_convert_type(word, jnp.int32).reshape(e * EXPERT_ROWS, V7X_LANES)


PEER_GROUP = V7X_SUBLANES
SLAB_ROWS_BF16 = 2 * EXPERT_ROWS
PAIR_LANES = PEER_PAIRS * SLAB_ROWS_BF16


def _slab(tab_ref, row):
    return tab_ref[pl.ds(pl.multiple_of(row, EXPERT_ROWS), EXPERT_ROWS), :]


def _peer_u_kernel(idx_ref, tab_ref, h_ref, fold_ref, o_ref, s_ref):
    groups = o_ref.shape[0] // PEER_GROUP
    row = lax.broadcasted_iota(jnp.int32, (PEER_GROUP, V7X_LANES), 0)

    def group(g, carry):
        sums = jnp.zeros((PEER_GROUP, PAIR_LANES), F32)
        for i in range(PEER_GROUP):
            t = g * PEER_GROUP + i
            hb = pltpu.bitcast(_slab(h_ref, t * EXPERT_ROWS), BF16)
            tok_idx = idx_ref.at[pl.ds(t * PEER_PAIRS, PEER_PAIRS)]
            for j in range(PEER_PAIRS):
                u = pltpu.bitcast(_slab(tab_ref, tok_idx[j]), BF16)
                s_ref[pl.ds((i * PEER_PAIRS + j) * EXPERT_ROWS, EXPERT_ROWS), :] = pltpu.bitcast(u * hb, jnp.int32)
            prod = pltpu.bitcast(s_ref[pl.ds(i * PEER_PAIRS * EXPERT_ROWS, PEER_PAIRS * EXPERT_ROWS), :], BF16)
            pick = jnp.where(row == i, 1.0, 0.0).astype(BF16)
            sums = sums + lax.dot_general(pick, prod, (((1,), (1,)), ((), ())), preferred_element_type=F32)
        hi = sums.astype(BF16)
        lo = (sums - hi.astype(F32)).astype(BF16)
        a = (jnp.dot(hi, fold_ref[...], preferred_element_type=F32)
             + jnp.dot(lo, fold_ref[...], preferred_element_type=F32))
        o_ref[pl.ds(pl.multiple_of(g * PEER_GROUP, PEER_GROUP), PEER_GROUP), :] = a
        return carry

    lax.fori_loop(0, groups, group, 0)


def peer_expert_dots(idx4, tab, h4, tb=PEER_TILE):
    r = h4.shape[0] // EXPERT_ROWS
    fold = jnp.repeat(jnp.eye(PEER_PAIRS, dtype=BF16), SLAB_ROWS_BF16, axis=0)
    return pl.pallas_call(
        _peer_u_kernel,
        grid=(r // tb,),
        in_specs=[
            pl.BlockSpec((tb * PEER_PAIRS,), lambda i: (i,), memory_space=pltpu.SMEM),
            pl.BlockSpec(tab.shape, lambda i: (0, 0), pipeline_mode=pl.Buffered(1)),
            pl.BlockSpec((tb * EXPERT_ROWS, V7X_LANES), lambda i: (i, 0)),
            pl.BlockSpec((PAIR_LANES, PEER_PAIRS), lambda i: (0, 0)),
        ],
        out_specs=pl.BlockSpec((tb, PEER_PAIRS), lambda i: (i, 0)),
        out_shape=jax.ShapeDtypeStruct((r, PEER_PAIRS), F32),
        scratch_shapes=[pltpu.VMEM((PEER_GROUP * PEER_PAIRS * EXPERT_ROWS, V7X_LANES), jnp.int32)],
        compiler_params=_cparams("arbitrary"),
        name="peer_expert_dots",
    )(idx4, tab, h4, fold)


def _peer_v_kernel(idx_ref, w_ref, tab_ref, spread_ref, o_ref, g_ref):
    groups = w_ref.shape[0] // PEER_GROUP
    shape = (SLAB_ROWS_BF16, PAIR_LANES)
    diag = (lax.broadcasted_iota(jnp.int32, shape, 1) % SLAB_ROWS_BF16) == lax.broadcasted_iota(jnp.int32, shape, 0)

    def group(g, carry):
        w8 = w_ref[pl.ds(pl.multiple_of(g * PEER_GROUP, PEER_GROUP), PEER_GROUP), :]
        wide = jnp.dot(w8.astype(BF16), spread_ref[...], preferred_element_type=F32)
        for i in range(PEER_GROUP):
            t = g * PEER_GROUP + i
            tok_idx = idx_ref.at[pl.ds(t * PEER_PAIRS, PEER_PAIRS)]
            for j in range(PEER_PAIRS):
                g_ref[pl.ds((i * PEER_PAIRS + j) * EXPERT_ROWS, EXPERT_ROWS), :] = _slab(tab_ref, tok_idx[j])
            rows = pltpu.bitcast(g_ref[pl.ds(i * PEER_PAIRS * EXPERT_ROWS, PEER_PAIRS * EXPERT_ROWS), :], BF16)
            wi = jnp.where(diag, jnp.broadcast_to(wide[i:i + 1], shape), 0.0).astype(BF16)
            o_ref[pl.ds(pl.multiple_of(t * SLAB_ROWS_BF16, SLAB_ROWS_BF16), SLAB_ROWS_BF16), :] = jnp.dot(
                wi, rows, preferred_element_type=F32)
        return carry

    lax.fori_loop(0, groups, group, 0)


def peer_expert_mix(idx4, w, tab, tb=PEER_TILE):
    r = w.shape[0]
    spread = jnp.repeat(jnp.eye(PEER_PAIRS, dtype=BF16), SLAB_ROWS_BF16, axis=1)
    return pl.pallas_call(
        _peer_v_kernel,
        grid=(r // tb,),
        in_specs=[
            pl.BlockSpec((tb * PEER_PAIRS,), lambda i: (i,), memory_space=pltpu.SMEM),
            pl.BlockSpec((tb, PEER_PAIRS), lambda i: (i, 0)),
            pl.BlockSpec(tab.shape, lambda i: (0, 0), pipeline_mode=pl.Buffered(1)),
            pl.BlockSpec((PEER_PAIRS, PAIR_LANES), lambda i: (0, 0)),
        ],
        out_specs=pl.BlockSpec((tb * SLAB_ROWS_BF16, V7X_LANES), lambda i: (i, 0)),
        out_shape=jax.ShapeDtypeStruct((r * SLAB_ROWS_BF16, V7X_LANES), F32),
        scratch_shapes=[pltpu.VMEM((PEER_GROUP * PEER_PAIRS * EXPERT_ROWS, V7X_LANES), jnp.int32)],
        compiler_params=_cparams("arbitrary"),
        name="peer_expert_mix",
    )(idx4, w, tab, spread)


def peer(x, g, shift, scale, wq, keys, u_tab, v_tab, mod_index_peer):
    r, d = x.shape
    wq_heads = wq.astype(BF16).reshape(d, PEER_HEADS, PEER_QDIM).transpose(1, 0, 2)
    h, eidx, gw = peer_retrieve(x, g, shift, scale, wq_heads, keys.astype(BF16), mod_index_peer)
    idx4 = (eidx * EXPERT_ROWS).transpose(0, 3, 1, 2).reshape(r * PEER_PAIRS)
    gw = gw.transpose(0, 3, 1, 2).reshape(r, PEER_PAIRS)
    a = peer_expert_dots(idx4, pack_rows_bf16(u_tab), pack_rows_bf16(h))
    w = jax.nn.gelu(a, approximate=False) * gw
    return peer_expert_mix(idx4, w, pack_rows_bf16(v_tab)).reshape(r, d)


HG_TILE = 128
HG_SUB = 16
HG_PAIR = 2 * HG_DK


def _hgrn_kernel(q_ref, f_ref, v_ref, lb_ref, o_ref, st_ref, *, rev):
    @pl.when(pl.program_id(2) == 0)
    def _():
        st_ref[...] = jnp.zeros_like(st_ref)

    n = q_ref.shape[0]
    lb = lb_ref[...]
    f = lb + (1.0 - lb) * jax.nn.sigmoid(f_ref[...])
    kk = 1.0 - f
    lf = jnp.log(f)
    t = lax.broadcasted_iota(jnp.int32, (n, n), 0)
    s = lax.broadcasted_iota(jnp.int32, (n, n), 1)
    same = (t // HG_SUB) == (s // HG_SUB)
    mid = (t // HG_SUB) * HG_SUB + (HG_SUB // 2 if rev else HG_SUB // 2 - 1)
    seen = (s >= t) if rev else (s <= t)
    seen_mid = (s >= mid) if rev else (s <= mid)
    one = lambda m: jnp.where(m, 1.0, 0.0).astype(F32)
    hp = lambda a, b: jnp.dot(a, b, precision=lax.Precision.HIGHEST, preferred_element_type=F32)
    cum = hp(one(same & seen), lf)
    ref = hp(one(same & seen_mid), lf)
    last = hp(one(same), lf)
    q = q_ref[...]
    qe = (q * jnp.exp(cum)).astype(BF16)
    qm = (q * jnp.exp(cum - ref)).astype(BF16)
    km = (kk * jnp.exp(ref - cum)).astype(BF16)
    kd = (kk * jnp.exp(last - cum)).astype(BF16)
    dec = jnp.exp(last)
    vb = v_ref[...].astype(BF16)
    ti = lax.broadcasted_iota(jnp.int32, (HG_SUB, HG_SUB), 0)
    si = lax.broadcasted_iota(jnp.int32, (HG_SUB, HG_SUB), 1)
    causal = (si >= ti) if rev else (si <= ti)
    steps = range(n // HG_SUB)
    for c in (reversed(steps) if rev else steps):
        rows = slice(c * HG_SUB, (c + 1) * HG_SUB)
        for h in range(HG_PAIR // HG_DK):
            cols = slice(h * HG_DK, (h + 1) * HG_DK)
            st = st_ref[h]
            att = lax.dot_general(qm[rows, cols], km[rows, cols], (((1,), (1,)), ((), ())),
                                  preferred_element_type=F32)
            att = jnp.where(causal, att, 0.0).astype(BF16)
            o = (lax.dot_general(qe[rows, cols], st.astype(BF16), (((1,), (1,)), ((), ())),
                                 preferred_element_type=F32)
                 + jnp.dot(att, vb[rows, cols], preferred_element_type=F32))
            o_ref[rows, cols] = o
            upd = lax.dot_general(vb[rows, cols], kd[rows, cols], (((0,), (0,)), ((), ())),
                                  preferred_element_type=F32)
            st_ref[h] = st * dec[c * HG_SUB:c * HG_SUB + 1, cols] + upd


def hgrn_scan(pa, lb, rev, batch):
    rows = pa.shape[0]
    per = rows // batch // HG_TILE
    nctx = CTX_LEN // HG_TILE
    col0 = GQA_IN // HG_PAIR
    blocks = HG_W // HG_PAIR

    def tok(b, k):
        if rev:
            k = jnp.where(k < nctx, nctx - 1 - k, per - 1 - (k - nctx))
        return b * per + k

    spec = lambda cb: pl.BlockSpec((HG_TILE, HG_PAIR), lambda b, p, k: (tok(b, k), cb + p))
    d = 1 if rev else 0
    return pl.pallas_call(
        functools.partial(_hgrn_kernel, rev=rev),
        grid=(batch, blocks, per),
        in_specs=[spec(col0), spec(col0 + (1 + d) * blocks), spec(col0 + 3 * blocks),
                  pl.BlockSpec((None, 1, HG_PAIR), lambda b, p, k: (d, 0, p))],
        out_specs=pl.BlockSpec((HG_TILE, HG_PAIR), lambda b, p, k: (tok(b, k), p)),
        out_shape=jax.ShapeDtypeStruct((rows, HG_W), F32),
        scratch_shapes=[pltpu.VMEM((HG_PAIR // HG_DK, HG_DV, HG_DK), F32)],
        compiler_params=_cparams("parallel", "parallel", "arbitrary"),
        name="hgrn_scan_rev" if rev else "hgrn_scan_fwd",
    )(pa, pa, pa, lb)


def _rms_norm(x, g):
    xf = x.astype(F32)
    y = xf * lax.rsqrt(jnp.mean(xf * xf, axis=-1, keepdims=True) + EPS)
    return y * g.astype(F32)


def _hgrn2(pa2, batch, lb, onorm_g):
    lb = lb.reshape(2, 1, HG_W)
    o = hgrn_scan(pa2, lb, False, batch) + hgrn_scan(pa2, lb, True, batch)
    o = _rms_norm(o.reshape(-1, HG_HEADS, HG_DV), onorm_g).reshape(-1, HG_W)
    return o * jax.nn.silu(pa2[:, GQA_IN + 4 * HG_W:])


S5_CHUNK = 64
S5_LANES = S5_GROUPS * S5_STATE


def _s5_kernel(u_ref, bre_ref, bim_ref, cre_ref, cim_ref, are_ref, aim_ref, y_ref, xre, xim, hre, him):
    nb = V7X_SUBLANES // 2
    tiles = u_ref.shape[0] // V7X_SUBLANES

    @pl.when(pl.program_id(1) == 0)
    def _():
        hre[...] = jnp.zeros_like(hre)
        him[...] = jnp.zeros_like(him)

    ub = u_ref[...].astype(BF16)
    xre[...] = jnp.dot(ub, bre_ref[...], preferred_element_type=F32)
    xim[...] = jnp.dot(ub, bim_ref[...], preferred_element_type=F32)
    shape = (V7X_SUBLANES, S5_LANES)
    second = lax.broadcasted_iota(jnp.int32, shape, 0) >= nb
    are = jnp.broadcast_to(are_ref[...], shape)
    aim = jnp.broadcast_to(aim_ref[...], shape)
    cre = jnp.where(second, are * are - aim * aim, are)
    cim = jnp.where(second, 2.0 * are * aim, aim)

    def step(k, carry):
        pr, pi = carry
        r = pl.ds(pl.multiple_of(k * V7X_SUBLANES, V7X_SUBLANES), V7X_SUBLANES)
        xr, xi = xre[r, :], xim[r, :]
        sr = jnp.where(second, pltpu.roll(xr, nb, axis=0), 0.0)
        si = jnp.where(second, pltpu.roll(xi, nb, axis=0), 0.0)
        nr = xr + (are * sr - aim * si) + (cre * pr - cim * pi)
        ni = xi + (are * si + aim * sr) + (cre * pi + cim * pr)
        xre[r, :] = nr
        xim[r, :] = ni
        return (jnp.where(second, nr, pltpu.roll(nr, nb, axis=0)),
                jnp.where(second, ni, pltpu.roll(ni, nb, axis=0)))

    hr, hi = lax.fori_loop(0, tiles, step, (hre[...], him[...]))
    hre[...] = hr
    him[...] = hi
    y_ref[...] = (jnp.dot(xre[...].astype(BF16), cre_ref[...], preferred_element_type=F32)
                  - jnp.dot(xim[...].astype(BF16), cim_ref[...], preferred_element_type=F32))


def s5_scan(u2, bre, bim, cre, cim, are, aim, nb):
    _, rows, w = u2.shape
    blk = S5_CHUNK * nb
    assert rows % blk == 0 and 2 * nb == V7X_SUBLANES
    wspec = lambda shape: pl.BlockSpec((None,) + shape, lambda d, i: (d, 0, 0))
    return pl.pallas_call(
        _s5_kernel,
        grid=(2, rows // blk),
        in_specs=[
            pl.BlockSpec((None, blk, w), lambda d, i: (d, i, 0)),
            wspec((w, S5_LANES)), wspec((w, S5_LANES)), wspec((S5_LANES, w)), wspec((S5_LANES, w)),
            wspec((1, S5_LANES)), wspec((1, S5_LANES)),
        ],
        out_specs=pl.BlockSpec((None, blk, w), lambda d, i: (d, i, 0)),
        out_shape=jax.ShapeDtypeStruct(u2.shape, F32),
        scratch_shapes=[pltpu.VMEM((blk, S5_LANES), F32), pltpu.VMEM((blk, S5_LANES), F32),
                        pltpu.VMEM((V7X_SUBLANES, S5_LANES), F32), pltpu.VMEM((V7X_SUBLANES, S5_LANES), F32)],
        compiler_params=_cparams("arbitrary", "arbitrary"),
        name="s5_scan",
    )(u2, bre, bim, cre, cim, are, aim)


def _flip_parts(a):
    return jnp.concatenate([jnp.flip(a[:, :CTX_LEN], axis=1), jnp.flip(a[:, CTX_LEN:], axis=1)], axis=1)


def _s5_core(u, a_re, a_im, log_dt, b_re, b_im, c_re, c_im, d_skip):
    b, s, w = u.shape
    dt = jnp.exp(log_dt)[..., None]
    mag = jnp.exp(a_re * dt)
    abar_re, abar_im = mag * jnp.cos(a_im * dt), mag * jnp.sin(a_im * dt)
    den = a_re * a_re + a_im * a_im
    k_re = ((abar_re - 1.0) * a_re + abar_im * a_im) / den
    k_im = (abar_im * a_re - (abar_re - 1.0) * a_im) / den
    bb_re = k_re[..., None] * b_re - k_im[..., None] * b_im
    bb_im = k_re[..., None] * b_im + k_im[..., None] * b_re
    eye = jnp.eye(S5_GROUPS, dtype=F32)
    bd_in = lambda m: jnp.einsum('dgpc,gh->dgchp', m, eye).reshape(2, w, S5_LANES).astype(BF16)
    bd_out = lambda m: jnp.einsum('dgcp,gh->dgphc', m, eye).reshape(2, S5_LANES, w).astype(BF16)
    u2 = jnp.stack([u, _flip_parts(u)]).transpose(0, 2, 1, 3).reshape(2, s * b, w)
    y2 = s5_scan(u2, bd_in(bb_re), bd_in(bb_im), bd_out(c_re), bd_out(c_im),
                 abar_re.reshape(2, 1, S5_LANES), abar_im.reshape(2, 1, S5_LANES), b)
    y2 = y2.reshape(2, s, b, w).transpose(0, 2, 1, 3)
    return d_skip * u + y2[0] + _flip_parts(y2[1])


def _rope_tables(rows, rot_dim):
    axis_dim = rot_dim // 2
    inv = ROPE_THETA ** (-jnp.arange(0, axis_dim, 2, dtype=F32) / axis_dim)
    t = jnp.arange(rows * GRID_W)
    r = (t // GRID_W).astype(F32)[:, None] * inv
    c = (t % GRID_W).astype(F32)[:, None] * inv
    cos = jnp.concatenate([jnp.cos(r), jnp.cos(r), jnp.cos(c), jnp.cos(c)], axis=-1)
    sin = jnp.concatenate([-jnp.sin(r), jnp.sin(r), -jnp.sin(c), jnp.sin(c)], axis=-1)
    return cos, sin


def _rope(x, cos, sin):
    q = x.shape[-1] // 4
    swapped = jnp.concatenate([x[..., q:2 * q], x[..., :q], x[..., 3 * q:], x[..., 2 * q:3 * q]], axis=-1)
    return x * cos[:, None, :] + swapped * sin[:, None, :]


def _rope_latent(x, cos, sin):
    return jnp.concatenate([x[:, :CTX_LEN], _rope(x[:, CTX_LEN:], cos, sin)], axis=1)


def _attend_all(q, k, v, need_ctx):
    q, k, v = q.astype(BF16), k.astype(BF16), v.astype(BF16)
    o_lat = attention(q[:, :, CTX_LEN:], k, v)
    if need_ctx:
        o_ctx = attention(q[:, :, :CTX_LEN], k[:, :, :CTX_LEN], v[:, :, :CTX_LEN])
    else:
        o_ctx = jnp.zeros(q.shape[:2] + (CTX_LEN, q.shape[3]), F32)
    return jnp.concatenate([o_ctx, o_lat], axis=2)


def _gqa(p, qn_g, kn_g, cos, sin, need_ctx):
    b, s, _ = p.shape
    nq = GQA_Q_HEADS * HEAD_DIM
    nk = GQA_KV_HEADS * HEAD_DIM
    grp = GQA_Q_HEADS // GQA_KV_HEADS
    q = _rms_norm(p[..., :nq].reshape(b, s, GQA_Q_HEADS, HEAD_DIM), qn_g)
    k = _rms_norm(p[..., nq:nq + nk].reshape(b, s, GQA_KV_HEADS, HEAD_DIM), kn_g)
    v = p[..., nq + nk:nq + 2 * nk]
    q = _rope_latent(q, cos, sin) * (HEAD_DIM ** -0.5)
    k = _rope_latent(k, cos, sin)
    qh = q.transpose(0, 2, 1, 3)
    zeros = jnp.zeros_like(qh)
    first = (jnp.arange(GQA_Q_HEADS) < grp)[None, :, None, None]
    q_pad = jnp.concatenate([jnp.where(first, qh, zeros), jnp.where(first, zeros, qh)], axis=-1)
    k_all = k.reshape(b, 1, s, nk)
    v_all = v.reshape(b, 1, s, nk)
    o = _attend_all(q_pad, k_all, v_all, need_ctx)
    o = jnp.where(first, o[..., :HEAD_DIM], o[..., HEAD_DIM:])
    return o.transpose(0, 2, 1, 3).reshape(b, s, nq)


def _mla(p, qa_g, w_qup, kva_g, w_kvup, cos, sin, need_ctx):
    b, s, _ = p.shape
    r0 = MLA_Q_RANK
    r1 = MLA_Q_RANK + MLA_KV_RANK
    zero = jnp.zeros((1, 1, r0), F32)
    qa = p[..., :r0].reshape(b * s, r0)
    q = norm_mod_matmul(qa, qa_g[None, :], zero, zero, w_qup.astype(BF16), lambda i: 0)
    q = q.reshape(b, s, MLA_HEADS, MLA_NOPE + MLA_ROPE)
    zero = jnp.zeros((1, 1, MLA_KV_RANK), F32)
    kva = p[..., r0:r1].reshape(b * s, MLA_KV_RANK)
    kv = norm_mod_matmul(kva, kva_g[None, :], zero, zero, w_kvup.astype(BF16), lambda i: 0)
    kv = kv.reshape(b, s, MLA_HEADS, MLA_NOPE + MLA_V)
    q_rope = _rope_latent(q[..., MLA_NOPE:], cos, sin)
    k_rope = _rope_latent(p[..., r1:r1 + MLA_ROPE][:, :, None, :], cos, sin)
    scale = (MLA_NOPE + MLA_ROPE) ** -0.5
    pad = jnp.zeros((b, s, MLA_HEADS, V7X_LANES - MLA_NOPE - MLA_ROPE), F32)
    q_pad = jnp.concatenate([q[..., :MLA_NOPE] * scale, q_rope * scale, pad], axis=-1)
    k_pad = jnp.concatenate([kv[..., :MLA_NOPE], jnp.broadcast_to(k_rope, (b, s, MLA_HEADS, MLA_ROPE)), pad], axis=-1)
    v_pad = jnp.concatenate([kv[..., MLA_NOPE:], jnp.zeros((b, s, MLA_HEADS, V7X_LANES - MLA_V), F32)], axis=-1)
    tr = lambda a: a.transpose(0, 2, 1, 3)
    o = _attend_all(tr(q_pad), tr(k_pad), tr(v_pad), need_ctx)
    return o[..., :MLA_V].transpose(0, 2, 1, 3).reshape(b, s, MLA_HEADS * MLA_V)


def kernel(x, c, ctx, c_ctx, ada_w, ada_b, norm1_g, norm2_g, ev_w_in, ev_w_out, gqa_qn_g, gqa_kn_g,
           hg_lb_logits, hg_onorm_g, od_w_in, od_w_out, mla_qa_g, mla_w_qup, mla_kva_g, mla_w_kvup,
           s5_a_re, s5_a_im, s5_log_dt, s5_b_re, s5_b_im, s5_c_re, s5_c_im, s5_d, s5_w_glu, s5_b_glu,
           peer_wq, peer_keys, peer_u, peer_v, final_g):
    b, t, d = x.shape
    s = CTX_LEN + t
    depth = ada_w.shape[0]
    rows = t // GRID_W
    cos_g, sin_g = _rope_tables(rows, HEAD_DIM)
    cos_m, sin_m = _rope_tables(rows, MLA_ROPE)
    lb_all = jnp.cumsum(jax.nn.softmax(hg_lb_logits, axis=1), axis=1)

    def mod_index(tile):
        per, nctx = s // tile, CTX_LEN // tile
        return lambda i: jnp.where(i % per < nctx, b, i // per)

    mi_row = mod_index(ROW_TILE)
    mi_peer = mod_index(PEER_TILE)

    xa = jnp.concatenate([ctx, x], axis=1).reshape(b * s, d)
    s_all = jnp.concatenate([jax.nn.silu(c), jax.nn.silu(c_ctx)[None, :]], axis=0)
    s_pad = jnp.concatenate([s_all, jnp.zeros((V7X_SUBLANES - (b + 1) % V7X_SUBLANES, d), F32)], axis=0)

    for layer in range(depth):
        need_ctx = layer < depth - 1
        j = layer // 2
        mod = matmul(s_pad, ada_w[layer].astype(BF16), tm=s_pad.shape[0])[:b + 1] + ada_b[layer]
        mod = [m[:, None, :] for m in jnp.split(mod, 6, axis=-1)]
        if layer % 2 == 0:
            pa2 = norm_mod_matmul(xa, norm1_g[layer][None, :], mod[0], mod[1], ev_w_in[j].astype(BF16), mi_row)
            pa = pa2.reshape(b, s, -1)
            ya = _gqa(pa[..., :GQA_IN], gqa_qn_g[j], gqa_kn_g[j], cos_g, sin_g, need_ctx)
            yb = _hgrn2(pa2, b, lb_all[:, j], hg_onorm_g[j]).reshape(b, s, HG_W)
            w_out = ev_w_out[j]
        else:
            pa = norm_mod_matmul(xa, norm1_g[layer][None, :], mod[0], mod[1], od_w_in[j].astype(BF16), mi_row)
            pa = pa.reshape(b, s, -1)
            ya = _mla(pa[..., :MLA_IN], mla_qa_g[j], mla_w_qup[j], mla_kva_g[j], mla_w_kvup[j], cos_m, sin_m, need_ctx)
            y5 = _s5_core(pa[..., MLA_IN:], s5_a_re[j], s5_a_im[j], s5_log_dt[j], s5_b_re[j], s5_b_im[j],
                          s5_c_re[j], s5_c_im[j], s5_d[j])
            z = jax.nn.gelu(y5, approximate=False).reshape(b * s, S5_WIDTH)
            gl = matmul(z, s5_w_glu[j].astype(BF16)) + s5_b_glu[j]
            yb = (z * jax.nn.sigmoid(gl)).reshape(b, s, S5_WIDTH)
            w_out = od_w_out[j]
        y =jnp.concatenate([ya, yb], axis=-1).reshape(b * s, -1)
        xa = matmul_residual(y, w_out.astype(BF16), xa, mod[2], mi_row)
        out = peer(xa, norm2_g[layer][None, :], mod[3], mod[4],
                   peer_wq[layer], peer_keys[layer], peer_u[layer], peer_v[layer], mi_peer)
        gate = jnp.concatenate([jnp.broadcast_to(mod[5][b:b + 1], (b, CTX_LEN, d)),
                                jnp.broadcast_to(mod[5][:b], (b, t, d))], axis=1).reshape(b * s, d)
        xa = xa + gate * out
    xl = xa.reshape(b, s, d)[:, CTX_LEN:].reshape(b * t, d)
    return rmsnorm_rows(xl, final_g[None, :]).reshape(b, t, d)
```

```python
import functools
import math

import jax
import jax.numpy as jnp
from jax import lax
from jax.experimental import pallas as pl
from jax.experimental.pallas import tpu as pltpu

F32 = jnp.float32
BF16 = jnp.bfloat16

D_MODEL = 1024
GRID_W = 64
CTX_LEN = 256
EPS = 1e-6
ROPE_THETA = 10000.0

MIX_HALF = D_MODEL // 2
HEAD_DIM = 64
GQA_Q_HEADS = MIX_HALF // HEAD_DIM
GQA_KV_HEADS = GQA_Q_HEADS // 4
GQA_IN = (GQA_Q_HEADS + 2 * GQA_KV_HEADS) * HEAD_DIM

HG_DK = 128
HG_DV = 128
HG_HEADS = MIX_HALF // HG_DV
HG_W = HG_HEADS * HG_DK
GLA_CHUNK = 64

MLA_HEADS = MIX_HALF // HEAD_DIM
MLA_NOPE = 64
MLA_ROPE = 32
MLA_V = 64
MLA_Q_RANK = 384
MLA_KV_RANK = 256
MLA_IN = MLA_Q_RANK + MLA_KV_RANK + MLA_ROPE

S5_WIDTH = MIX_HALF
S5_GROUP = 16
S5_GROUPS = S5_WIDTH // S5_GROUP
S5_STATE = 64

PEER_HEADS = 8
PEER_KEYS = 128
N_EXPERTS = PEER_KEYS * PEER_KEYS
PEER_TOPK = 16
PEER_QDIM = 256
PEER_HALF = PEER_QDIM // 2
PEER_PAIRS = PEER_HEADS * PEER_TOPK

V7X_LANES = 128
V7X_SUBLANES = 8
V7X_VMEM_BYTES = 64 * 1024 * 1024
VMEM_LIMIT = V7X_VMEM_BYTES - 8 * 1024 * 1024

ROW_TILE = 256
PEER_TILE = 128
ATTN_Q_TILE = 256
EXPERT_WORDS = D_MODEL // 2
EXPERT_ROWS = EXPERT_WORDS // V7X_LANES


def _cparams(*sem):
    return pltpu.CompilerParams(dimension_semantics=sem, vmem_limit_bytes=VMEM_LIMIT)


def _norm_mod(x, g, shift, scale):
    ms = jnp.mean(x * x, axis=-1, keepdims=True)
    h = (x * lax.rsqrt(ms + EPS)) * g
    return h * (1.0 + scale) + shift


def _nmm_kernel(x_ref, g_ref, sh_ref, sc_ref, w_ref, o_ref):
    h = _norm_mod(x_ref[...], g_ref[...], sh_ref[0], sc_ref[0])
    o_ref[...] = jnp.dot(h.astype(BF16), w_ref[...], preferred_element_type=F32)


def norm_mod_matmul(x, g, shift, scale, w, mod_index, tm=ROW_TILE):
    r, k = x.shape
    n = w.shape[1]
    assert r % tm == 0 and w.shape[0] == k
    return pl.pallas_call(
        _nmm_kernel,
        grid=(r // tm,),
        in_specs=[
            pl.BlockSpec((tm, k), lambda i: (i, 0)),
            pl.BlockSpec((1, k), lambda i: (0, 0)),
            pl.BlockSpec((1, 1, k), lambda i: (mod_index(i), 0, 0)),
            pl.BlockSpec((1, 1, k), lambda i: (mod_index(i), 0, 0)),
            pl.BlockSpec((k, n), lambda i: (0, 0)),
        ],
        out_specs=pl.BlockSpec((tm, n), lambda i: (i, 0)),
        out_shape=jax.ShapeDtypeStruct((r, n), F32),
        compiler_params=_cparams("parallel"),
        name="norm_mod_matmul",
    )(x, g, shift, scale, w)


def _mm_kernel(a_ref, w_ref, o_ref):
    o_ref[...] = jnp.dot(a_ref[...].astype(BF16), w_ref[...], preferred_element_type=F32)


def matmul(a, w, tm=ROW_TILE):
    r, k = a.shape
    n = w.shape[1]
    assert r % tm == 0
    return pl.pallas_call(
        _mm_kernel,
        grid=(r // tm,),
        in_specs=[pl.BlockSpec((tm, k), lambda i: (i, 0)), pl.BlockSpec((k, n), lambda i: (0, 0))],
        out_specs=pl.BlockSpec((tm, n), lambda i: (i, 0)),
        out_shape=jax.ShapeDtypeStruct((r, n), F32),
        compiler_params=_cparams("parallel"),
        name="matmul",
    )(a, w)


def _mmres_kernel(a_ref, w_ref, x_ref, gate_ref, o_ref):
    y = jnp.dot(a_ref[...].astype(BF16), w_ref[...], preferred_element_type=F32)
    o_ref[...] = x_ref[...] + gate_ref[0] * y


def matmul_residual(a, w, x, gate, mod_index, tm=ROW_TILE):
    r, k = a.shape
    n = w.shape[1]
    assert r % tm == 0
    return pl.pallas_call(
        _mmres_kernel,
        grid=(r // tm,),
        in_specs=[
            pl.BlockSpec((tm, k), lambda i: (i, 0)),
            pl.BlockSpec((k, n), lambda i: (0, 0)),
            pl.BlockSpec((tm, n), lambda i: (i, 0)),
            pl.BlockSpec((1, 1, n), lambda i: (mod_index(i), 0, 0)),
        ],
        out_specs=pl.BlockSpec((tm, n), lambda i: (i, 0)),
        out_shape=jax.ShapeDtypeStruct((r, n), F32),
        compiler_params=_cparams("parallel"),
        name="matmul_residual",
    )(a, w, x, gate)


def _rms_kernel(x_ref, g_ref, o_ref):
    x = x_ref[...]
    ms = jnp.mean(x * x, axis=-1, keepdims=True)
    o_ref[...] = (x * lax.rsqrt(ms + EPS)) * g_ref[...]


def rmsnorm_rows(x, g, tm=ROW_TILE):
    r, k = x.shape
    return pl.pallas_call(
        _rms_kernel,
        grid=(r // tm,),
        in_specs=[pl.BlockSpec((tm, k), lambda i: (i, 0)), pl.BlockSpec((1, k), lambda i: (0, 0))],
        out_specs=pl.BlockSpec((tm, k), lambda i: (i, 0)),
        out_shape=jax.ShapeDtypeStruct((r, k), F32),
        compiler_params=_cparams("parallel"),
        name="rmsnorm_rows",
    )(x, g)


def _attn_kernel(q_ref, k_ref, v_ref, o_ref):
    q = q_ref[0, 0]
    s = lax.dot_general(q, k_ref[0, 0], (((1,), (1,)), ((), ())), preferred_element_type=F32)
    m = jnp.max(s, axis=-1, keepdims=True)
    p = jnp.exp(s - m)
    l = jnp.sum(p, axis=-1, keepdims=True)
    o = jnp.dot(p.astype(BF16), v_ref[0, 0], preferred_element_type=F32)
    o_ref[0, 0] = o / l


def attention(q, k, v, tq=ATTN_Q_TILE):
    b, h, sq, dl = q.shape
    hk, sk = k.shape[1], k.shape[2]
    grp = h // hk
    tq = min(tq, sq)
    assert sq % tq == 0
    return pl.pallas_call(
        _attn_kernel,
        grid=(b, h, sq // tq),
        in_specs=[
            pl.BlockSpec((1, 1, tq, dl), lambda bi, hi, qi: (bi, hi, qi, 0)),
            pl.BlockSpec((1, 1, sk, dl), lambda bi, hi, qi: (bi, hi // grp, 0, 0)),
            pl.BlockSpec((1, 1, sk, dl), lambda bi, hi, qi: (bi, hi // grp, 0, 0)),
        ],
        out_specs=pl.BlockSpec((1, 1, tq, dl), lambda bi, hi, qi: (bi, hi, qi, 0)),
        out_shape=jax.ShapeDtypeStruct((b, h, sq, dl), F32),
        compiler_params=_cparams("parallel", "parallel", "parallel"),
        name="attention",
    )(q, k, v)


def _topk_rows(s, iota, k):
    n = s.shape[0]
    row = lax.broadcasted_iota(jnp.int32, (k, s.shape[1]), 0)
    vals = jnp.zeros((k, s.shape[1]), F32)
    ids = jnp.zeros((k, s.shape[1]), F32)
    for r in range(k):
        m = jnp.max(s, axis=0, keepdims=True)
        am = jnp.min(jnp.where(s == m, iota, float(n)), axis=0, keepdims=True)
        vals = jnp.where(row == r, m, vals)
        ids = jnp.where(row == r, am, ids)
        s = jnp.where(iota == am, -jnp.inf, s)
    return vals, ids


def _peer_ret_kernel(x_ref, g_ref, sh_ref, sc_ref, wq_ref, keys_ref, h_ref, eidx_ref, gw_ref, hb_ref, e_scr, w_scr):
    h = _norm_mod(x_ref[...], g_ref[...], sh_ref[0], sc_ref[0])
    h_ref[...] = h
    hb_ref[...] = h.astype(BF16)
    tb = x_ref.shape[0]
    iota_n = lax.broadcasted_iota(jnp.int32, (PEER_KEYS, tb), 0).astype(F32)
    half = PEER_TOPK // 2
    sub = lambda n: lax.broadcasted_iota(jnp.int32, (n, tb), 0).astype(F32)
    pos_c = jnp.concatenate([sub(PEER_TOPK)] + [sub(half) + float(a * PEER_TOPK) for a in range(1, half)]
                            + [(sub(half) + float(half)) * float(PEER_TOPK)], axis=0)

    def pair_up(first, second, scale):
        return jnp.concatenate([first[0:1] * scale + second]
                               + [first[a:a + 1] * scale + second[0:half] for a in range(1, half)]
                               + [first[half:] * scale + second[0:1]], axis=0)

    def head(hd, carry):
        q = jnp.dot(hb_ref[...], wq_ref[hd], preferred_element_type=F32)
        tops = []
        for p in range(2):
            qp = q[:, p * PEER_HALF:(p + 1) * PEER_HALF].astype(BF16)
            s = lax.dot_general(keys_ref[hd, p], qp, (((1,), (1,)), ((), ())),
                                preferred_element_type=F32)
            tops.append(_topk_rows(s, iota_n, PEER_TOPK))
        (s1, i1), (s2, i2) = tops
        cand = pair_up(s1, s2, 1.0)
        cidx = pair_up(i1, i2, float(PEER_KEYS))
        row = lax.broadcasted_iota(jnp.int32, (PEER_TOPK, tb), 0)
        sc = jnp.zeros((PEER_TOPK, tb), F32)
        ex = jnp.zeros((PEER_TOPK, tb), F32)
        for r in range(PEER_TOPK):
            m = jnp.max(cand, axis=0, keepdims=True)
            am = jnp.min(jnp.where(cand == m, pos_c, float(PEER_TOPK * PEER_TOPK)), axis=0, keepdims=True)
            hit = pos_c == am
            e = jnp.max(jnp.where(hit, cidx, 0.0), axis=0, keepdims=True)
            sc = jnp.where(row == r, m, sc)
            ex = jnp.where(row == r, e, ex)
            cand = jnp.where(hit, -jnp.inf, cand)
        pexp = jnp.exp(sc - sc[0:1])
        rows = pl.ds(pl.multiple_of(hd * PEER_TOPK, PEER_TOPK), PEER_TOPK)
        w_scr[rows, :] = pexp / jnp.sum(pexp, axis=0, keepdims=True)
        e_scr[rows, :] = ex
        return carry

    lax.fori_loop(0, PEER_HEADS, head, 0)
    gw_ref[...] = w_scr[...].T
    eidx_ref[...] = (e_scr[...].T * float(EXPERT_ROWS)).astype(jnp.int32)


def peer_retrieve(x, g, shift, scale, wq_heads, keys, mod_index, tb=PEER_TILE):
    r, d = x.shape
    nb = r // tb
    return pl.pallas_call(
        _peer_ret_kernel,
        grid=(nb,),
        in_specs=[
            pl.BlockSpec((tb, d), lambda i: (i, 0)),
            pl.BlockSpec((1, d), lambda i: (0, 0)),
            pl.BlockSpec((1, 1, d), lambda i: (mod_index(i), 0, 0)),
            pl.BlockSpec((1, 1, d), lambda i: (mod_index(i), 0, 0)),
            pl.BlockSpec((PEER_HEADS, d, PEER_QDIM), lambda i: (0, 0, 0)),
            pl.BlockSpec((PEER_HEADS, 2, PEER_KEYS, PEER_HALF), lambda i: (0, 0, 0, 0)),
        ],
        out_specs=[
            pl.BlockSpec((tb, d), lambda i: (i, 0)),
            pl.BlockSpec((tb, PEER_PAIRS), lambda i: (i, 0)),
            pl.BlockSpec((tb, PEER_PAIRS), lambda i: (i, 0)),
        ],
        out_shape=[
            jax.ShapeDtypeStruct((r, d), F32),
            jax.ShapeDtypeStruct((r, PEER_PAIRS), jnp.int32),
            jax.ShapeDtypeStruct((r, PEER_PAIRS), F32),
        ],
        scratch_shapes=[pltpu.VMEM((tb, d), BF16), pltpu.VMEM((PEER_PAIRS, tb), F32),
                        pltpu.VMEM((PEER_PAIRS, tb), F32)],
        compiler_params=_cparams("parallel"),
        name="peer_retrieve",
    )(x, g, shift, scale, wq_heads, keys)


def pack_rows_bf16(tab):
    e, d = tab.shape
    t16 = lax.bitcast_convert_type(tab.astype(BF16), jnp.uint16).astype(jnp.uint32)
    t16 = t16.reshape(e, EXPERT_ROWS, 2, V7X_LANES)
    word = t16[:, :, 0, :] | (t16[:, :, 1, :] << 16)
    return lax.bitcast_convert_type(word, jnp.int32).reshape(e * EXPERT_ROWS, V7X_LANES)


PEER_GROUP = V7X_SUBLANES
SLAB_ROWS_BF16 = 2 * EXPERT_ROWS
PAIR_LANES = PEER_PAIRS * SLAB_ROWS_BF16


def _slab(tab_ref, row):
    return tab_ref[pl.ds(pl.multiple_of(row, EXPERT_ROWS), EXPERT_ROWS), :]


def _token_rows_bf16(h8, i):
    row = lax.broadcasted_iota(jnp.int32, (SLAB_ROWS_BF16, V7X_LANES), 0)
    hq = jnp.zeros((SLAB_ROWS_BF16, V7X_LANES), F32)
    for r in range(SLAB_ROWS_BF16):
        hq = jnp.where(row == r, h8[i:i + 1, r * V7X_LANES:(r + 1) * V7X_LANES], hq)
    return hq.astype(BF16)


def _peer_u_kernel(idx_ref, tab_ref, h_ref, gw_ref, fold_ref, o_ref, s_ref):
    groups = o_ref.shape[0] // PEER_GROUP
    row = lax.broadcasted_iota(jnp.int32, (PEER_GROUP, V7X_LANES), 0)

    def finish(g, sums):
        hi = sums.astype(BF16)
        lo = (sums - hi.astype(F32)).astype(BF16)
        a = (jnp.dot(hi, fold_ref[...], preferred_element_type=F32)
             + jnp.dot(lo, fold_ref[...], preferred_element_type=F32))
        rows = pl.ds(pl.multiple_of(g * PEER_GROUP, PEER_GROUP), PEER_GROUP)
        gelu = 0.5 * a * (1.0 + lax.erf(a * (2.0 ** -0.5)))
        o_ref[rows, :] = gelu * gw_ref[rows, :]

    def group(g, prev):
        finish(jnp.maximum(g - 1, 0), prev)
        sums = jnp.zeros((PEER_GROUP, PAIR_LANES), F32)
        h8 = h_ref[pl.ds(pl.multiple_of(g * PEER_GROUP, PEER_GROUP), PEER_GROUP), :]
        for i in range(PEER_GROUP):
            t = g * PEER_GROUP + i
            hb = _token_rows_bf16(h8, i)
            tok_idx = idx_ref.at[pl.ds(t * PEER_PAIRS, PEER_PAIRS)]
            for j in range(PEER_PAIRS):
                u = pltpu.bitcast(_slab(tab_ref, tok_idx[j]), BF16)
                s_ref[pl.ds((i * PEER_PAIRS + j) * EXPERT_ROWS, EXPERT_ROWS), :] = pltpu.bitcast(u * hb, jnp.int32)
            prod = pltpu.bitcast(s_ref[pl.ds(i * PEER_PAIRS * EXPERT_ROWS, PEER_PAIRS * EXPERT_ROWS), :], BF16)
            pick = jnp.where(row == i, 1.0, 0.0).astype(BF16)
            sums = sums + lax.dot_general(pick, prod, (((1,), (1,)), ((), ())), preferred_element_type=F32)
        return sums

    last = lax.fori_loop(0, groups, group, jnp.zeros((PEER_GROUP, PAIR_LANES), F32))
    finish(groups - 1, last)


def peer_expert_gates(idx4, tab, h, gw, tb=PEER_TILE):
    r = h.shape[0]
    fold = jnp.repeat(jnp.eye(PEER_PAIRS, dtype=BF16), SLAB_ROWS_BF16, axis=0)
    return pl.pallas_call(
        _peer_u_kernel,
        grid=(r // tb,),
        in_specs=[
            pl.BlockSpec((tb * PEER_PAIRS,), lambda i: (i,), memory_space=pltpu.SMEM),
            pl.BlockSpec(tab.shape, lambda i: (0, 0), pipeline_mode=pl.Buffered(1)),
            pl.BlockSpec((tb, h.shape[1]), lambda i: (i, 0)),
            pl.BlockSpec((tb, PEER_PAIRS), lambda i: (i, 0)),
            pl.BlockSpec((PAIR_LANES, PEER_PAIRS), lambda i: (0, 0)),
        ],
        out_specs=pl.BlockSpec((tb, PEER_PAIRS), lambda i: (i, 0)),
        out_shape=jax.ShapeDtypeStruct((r, PEER_PAIRS), F32),
        scratch_shapes=[pltpu.VMEM((PEER_GROUP * PEER_PAIRS * EXPERT_ROWS, V7X_LANES), jnp.int32)],
        compiler_params=_cparams("arbitrary"),
        name="peer_expert_gates",
    )(idx4, tab, h, gw, fold)


def _peer_v_kernel(idx_ref, w_ref, tab_ref, spread_ref, o_ref, g_ref):
    groups = w_ref.shape[0] // PEER_GROUP
    last_slot = PEER_GROUP - 1
    shape = (SLAB_ROWS_BF16, PAIR_LANES)
    diag = (lax.broadcasted_iota(jnp.int32, shape, 1) % SLAB_ROWS_BF16) == lax.broadcasted_iota(jnp.int32, shape, 0)
    slot_rows = PEER_PAIRS * EXPERT_ROWS

    @pl.when(pl.program_id(0) == 0)
    def _():
        g_ref[pl.ds(last_slot * slot_rows, slot_rows), :] = jnp.zeros((slot_rows, V7X_LANES), jnp.int32)

    def contract(slot, wrow, t):
        rows = pltpu.bitcast(g_ref[pl.ds(slot * slot_rows, slot_rows), :], BF16)
        wi = jnp.where(diag, jnp.broadcast_to(wrow, shape), 0.0).astype(BF16)
        o_ref[pl.ds(pl.multiple_of(t * SLAB_ROWS_BF16, SLAB_ROWS_BF16), SLAB_ROWS_BF16), :] = jnp.dot(
            wi, rows, preferred_element_type=F32)

    def group(g, prev_wide):
        contract(last_slot, prev_wide[last_slot:], jnp.maximum(g * PEER_GROUP - 1, 0))
        w8 = w_ref[pl.ds(pl.multiple_of(g * PEER_GROUP, PEER_GROUP), PEER_GROUP), :]
        wide = jnp.dot(w8.astype(BF16), spread_ref[...], preferred_element_type=F32)
        for i in range(PEER_GROUP):
            t = g * PEER_GROUP + i
            tok_idx = idx_ref.at[pl.ds(t * PEER_PAIRS, PEER_PAIRS)]
            for j in range(PEER_PAIRS):
                g_ref[pl.ds((i * PEER_PAIRS + j) * EXPERT_ROWS, EXPERT_ROWS), :] = _slab(tab_ref, tok_idx[j])
            if i < last_slot:
                contract(i, wide[i:i + 1], t)
        return wide

    wide = lax.fori_loop(0, groups, group, jnp.zeros((PEER_GROUP, PAIR_LANES), F32))
    contract(last_slot, wide[last_slot:], groups * PEER_GROUP - 1)


def peer_expert_mix(idx4, w, tab, tb=PEER_TILE):
    r = w.shape[0]
    spread = jnp.repeat(jnp.eye(PEER_PAIRS, dtype=BF16), SLAB_ROWS_BF16, axis=1)
    return pl.pallas_call(
        _peer_v_kernel,
        grid=(r // tb,),
        in_specs=[
            pl.BlockSpec((tb * PEER_PAIRS,), lambda i: (i,), memory_space=pltpu.SMEM),
            pl.BlockSpec((tb, PEER_PAIRS), lambda i: (i, 0)),
            pl.BlockSpec(tab.shape, lambda i: (0, 0), pipeline_mode=pl.Buffered(1)),
            pl.BlockSpec((PEER_PAIRS, PAIR_LANES), lambda i: (0, 0)),
        ],
        out_specs=pl.BlockSpec((tb * SLAB_ROWS_BF16, V7X_LANES), lambda i: (i, 0)),
        out_shape=jax.ShapeDtypeStruct((r * SLAB_ROWS_BF16, V7X_LANES), F32),
        scratch_shapes=[pltpu.VMEM((PEER_GROUP * PEER_PAIRS * EXPERT_ROWS, V7X_LANES), jnp.int32)],
        compiler_params=_cparams("arbitrary"),
        name="peer_expert_mix",
    )(idx4, w, tab, spread)


def peer(x, g, shift, scale, wq, keys, u_tab, v_tab, mod_index_peer):
    r, d = x.shape
    wq_heads = wq.astype(BF16).reshape(d, PEER_HEADS, PEER_QDIM).transpose(1, 0, 2)
    h, idx4, gw = peer_retrieve(x, g, shift, scale, wq_heads, keys.astype(BF16), mod_index_peer)
    idx4 = idx4.reshape(r * PEER_PAIRS)
    w = peer_expert_gates(idx4, pack_rows_bf16(u_tab), h, gw)
    return peer_expert_mix(idx4, w, pack_rows_bf16(v_tab)).reshape(r, d)


HG_TILE = 128
HG_SUB = 16
HG_PAIR = 2 * HG_DK


def _hgrn_kernel(q_ref, f_ref, v_ref, lb_ref, o_ref, st_ref, *, rev):
    @pl.when(pl.program_id(2) == 0)
    def _():
        st_ref[...] = jnp.zeros_like(st_ref)

    n = q_ref.shape[0]
    lb = lb_ref[...]
    f = lb + (1.0 - lb) * jax.nn.sigmoid(f_ref[...])
    kk = 1.0 - f
    lf = jnp.log(f)
    t = lax.broadcasted_iota(jnp.int32, (n, n), 0)
    s = lax.broadcasted_iota(jnp.int32, (n, n), 1)
    same = (t // HG_SUB) == (s // HG_SUB)
    mid = (t // HG_SUB) * HG_SUB + (HG_SUB // 2 if rev else HG_SUB // 2 - 1)
    seen = (s >= t) if rev else (s <= t)
    seen_mid = (s >= mid) if rev else (s <= mid)
    one = lambda m: jnp.where(m, 1.0, 0.0).astype(F32)
    hp = lambda a, b: jnp.dot(a, b, precision=lax.Precision.HIGHEST, preferred_element_type=F32)
    cum = hp(one(same & seen), lf)
    ref = hp(one(same & seen_mid), lf)
    last = hp(one(same), lf)
    q = q_ref[...]
    qe = (q * jnp.exp(cum)).astype(BF16)
    qm = (q * jnp.exp(cum - ref)).astype(BF16)
    km = (kk * jnp.exp(ref - cum)).astype(BF16)
    kd = (kk * jnp.exp(last - cum)).astype(BF16)
    dec = jnp.exp(last)
    vb = v_ref[...].astype(BF16)
    ti = lax.broadcasted_iota(jnp.int32, (HG_SUB, HG_SUB), 0)
    si = lax.broadcasted_iota(jnp.int32, (HG_SUB, HG_SUB), 1)
    causal = (si >= ti) if rev else (si <= ti)
    steps = range(n // HG_SUB)
    for c in (reversed(steps) if rev else steps):
        rows = slice(c * HG_SUB, (c + 1) * HG_SUB)
        for h in range(HG_PAIR // HG_DK):
            cols = slice(h * HG_DK, (h + 1) * HG_DK)
            st = st_ref[h]
            att = lax.dot_general(qm[rows, cols], km[rows, cols], (((1,), (1,)), ((), ())),
                                  preferred_element_type=F32)
            att = jnp.where(causal, att, 0.0).astype(BF16)
            o = (lax.dot_general(qe[rows, cols], st.astype(BF16), (((1,), (1,)), ((), ())),
                                 preferred_element_type=F32)
                 + jnp.dot(att, vb[rows, cols], preferred_element_type=F32))
            o_ref[rows, cols] = o
            upd = lax.dot_general(vb[rows, cols], kd[rows, cols], (((0,), (0,)), ((), ())),
                                  preferred_element_type=F32)
            st_ref[h] = st * dec[c * HG_SUB:c * HG_SUB + 1, cols] + upd


def hgrn_scan(pa, lb, rev, batch):
    rows = pa.shape[0]
    per = rows // batch // HG_TILE
    nctx = CTX_LEN // HG_TILE
    col0 = GQA_IN // HG_PAIR
    blocks = HG_W // HG_PAIR

    def tok(b, k):
        if rev:
            k = jnp.where(k < nctx, nctx - 1 - k, per - 1 - (k - nctx))
        return b * per + k

    spec = lambda cb: pl.BlockSpec((HG_TILE, HG_PAIR), lambda b, p, k: (tok(b, k), cb + p))
    d = 1 if rev else 0
    return pl.pallas_call(
        functools.partial(_hgrn_kernel, rev=rev),
        grid=(batch, blocks, per),
        in_specs=[spec(col0), spec(col0 + (1 + d) * blocks), spec(col0 + 3 * blocks),
                  pl.BlockSpec((None, 1, HG_PAIR), lambda b, p, k: (d, 0, p))],
        out_specs=pl.BlockSpec((HG_TILE, HG_PAIR), lambda b, p, k: (tok(b, k), p)),
        out_shape=jax.ShapeDtypeStruct((rows, HG_W), F32),
        scratch_shapes=[pltpu.VMEM((HG_PAIR // HG_DK, HG_DV, HG_DK), F32)],
        compiler_params=_cparams("parallel", "parallel", "arbitrary"),
        name="hgrn_scan_rev" if rev else "hgrn_scan_fwd",
    )(pa, pa, pa, lb)


def _rms_norm(x, g):
    xf = x.astype(F32)
    y = xf * lax.rsqrt(jnp.mean(xf * xf, axis=-1, keepdims=True) + EPS)
    return y * g.astype(F32)


def _hgrn2(pa2, batch, lb, onorm_g):
    lb = lb.reshape(2, 1, HG_W)
    o = hgrn_scan(pa2, lb, False, batch) + hgrn_scan(pa2, lb, True, batch)
    o = _rms_norm(o.reshape(-1, HG_HEADS, HG_DV), onorm_g).reshape(-1, HG_W)
    return o * jax.nn.silu(pa2[:, GQA_IN + 4 * HG_W:])


S5_CHUNK = 64
S5_LANES = S5_GROUPS * S5_STATE


def _s5_kernel(u_ref, bre_ref, bim_ref, cre_ref, cim_ref, are_ref, aim_ref, y_ref, xre, xim, hre, him):
    nb = V7X_SUBLANES // 2
    tiles = u_ref.shape[0] // V7X_SUBLANES

    @pl.when(pl.program_id(1) == 0)
    def _():
        hre[...] = jnp.zeros_like(hre)
        him[...] = jnp.zeros_like(him)

    ub = u_ref[...].astype(BF16)
    xre[...] = jnp.dot(ub, bre_ref[...], preferred_element_type=F32)
    xim[...] = jnp.dot(ub, bim_ref[...], preferred_element_type=F32)
    shape = (V7X_SUBLANES, S5_LANES)
    second = lax.broadcasted_iota(jnp.int32, shape, 0) >= nb
    are = jnp.broadcast_to(are_ref[...], shape)
    aim = jnp.broadcast_to(aim_ref[...], shape)
    cre = jnp.where(second, are * are - aim * aim, are)
    cim = jnp.where(second, 2.0 * are * aim, aim)

    def step(k, carry):
        pr, pi = carry
        r = pl.ds(pl.multiple_of(k * V7X_SUBLANES, V7X_SUBLANES), V7X_SUBLANES)
        xr, xi = xre[r, :], xim[r, :]
        sr = jnp.where(second, pltpu.roll(xr, nb, axis=0), 0.0)
        si = jnp.where(second, pltpu.roll(xi, nb, axis=0), 0.0)
        nr = xr + (are * sr - aim * si) + (cre * pr - cim * pi)
        ni = xi + (are * si + aim * sr) + (cre * pi + cim * pr)
        xre[r, :] = nr
        xim[r, :] = ni
        return (jnp.where(second, nr, pltpu.roll(nr, nb, axis=0)),
                jnp.where(second, ni, pltpu.roll(ni, nb, axis=0)))

    hr, hi = lax.fori_loop(0, tiles, step, (hre[...], him[...]))
    hre[...] = hr
    him[...] = hi
    y_ref[...] = (jnp.dot(xre[...].astype(BF16), cre_ref[...], preferred_element_type=F32)
                  - jnp.dot(xim[...].astype(BF16), cim_ref[...], preferred_element_type=F32))


def s5_scan(u2, bre, bim, cre, cim, are, aim, nb):
    _, rows, w = u2.shape
    blk = S5_CHUNK * nb
    assert rows % blk == 0 and 2 * nb == V7X_SUBLANES
    wspec = lambda shape: pl.BlockSpec((None,) + shape, lambda d, i: (d, 0, 0))
    return pl.pallas_call(
        _s5_kernel,
        grid=(2, rows // blk),
        in_specs=[
            pl.BlockSpec((None, blk, w), lambda d, i: (d, i, 0)),
            wspec((w, S5_LANES)), wspec((w, S5_LANES)), wspec((S5_LANES, w)), wspec((S5_LANES, w)),
            wspec((1, S5_LANES)), wspec((1, S5_LANES)),
        ],
        out_specs=pl.BlockSpec((None, blk, w), lambda d, i: (d, i, 0)),
        out_shape=jax.ShapeDtypeStruct(u2.shape, F32),
        scratch_shapes=[pltpu.VMEM((blk, S5_LANES), F32), pltpu.VMEM((blk, S5_LANES), F32),
                        pltpu.VMEM((V7X_SUBLANES, S5_LANES), F32), pltpu.VMEM((V7X_SUBLANES, S5_LANES), F32)],
        compiler_params=_cparams("arbitrary", "arbitrary"),
        name="s5_scan",
    )(u2, bre, bim, cre, cim, are, aim)


def _flip_parts(a):
    return jnp.concatenate([jnp.flip(a[:, :CTX_LEN], axis=1), jnp.flip(a[:, CTX_LEN:], axis=1)], axis=1)


def _s5_core(u, a_re, a_im, log_dt, b_re, b_im, c_re, c_im, d_skip):
    b, s, w = u.shape
    dt = jnp.exp(log_dt)[..., None]
    mag = jnp.exp(a_re * dt)
    abar_re, abar_im = mag * jnp.cos(a_im * dt), mag * jnp.sin(a_im * dt)
    den = a_re * a_re + a_im * a_im
    k_re = ((abar_re - 1.0) * a_re + abar_im * a_im) / den
    k_im = (abar_im * a_re - (abar_re - 1.0) * a_im) / den
    bb_re = k_re[..., None] * b_re - k_im[..., None] * b_im
    bb_im = k_re[..., None] * b_im + k_im[..., None] * b_re
    eye = jnp.eye(S5_GROUPS, dtype=F32)
    bd_in = lambda m: jnp.einsum('dgpc,gh->dgchp', m, eye).reshape(2, w, S5_LANES).astype(BF16)
    bd_out = lambda m: jnp.einsum('dgcp,gh->dgphc', m, eye).reshape(2, S5_LANES, w).astype(BF16)
    u2 = jnp.stack([u, _flip_parts(u)]).transpose(0, 2, 1, 3).reshape(2, s * b, w)
    y2 = s5_scan(u2, bd_in(bb_re), bd_in(bb_im), bd_out(c_re), bd_out(c_im),
                 abar_re.reshape(2, 1, S5_LANES), abar_im.reshape(2, 1, S5_LANES), b)
    y2 = y2.reshape(2, s, b, w).transpose(0, 2, 1, 3)
    return d_skip * u + y2[0] + _flip_parts(y2[1])


def _rope_tables(rows, rot_dim):
    axis_dim = rot_dim // 2
    inv = ROPE_THETA ** (-jnp.arange(0, axis_dim, 2, dtype=F32) / axis_dim)
    t = jnp.arange(rows * GRID_W)
    r = (t // GRID_W).astype(F32)[:, None] * inv
    c = (t % GRID_W).astype(F32)[:, None] * inv
    cos = jnp.concatenate([jnp.cos(r), jnp.cos(r), jnp.cos(c), jnp.cos(c)], axis=-1)
    sin = jnp.concatenate([-jnp.sin(r), jnp.sin(r), -jnp.sin(c), jnp.sin(c)], axis=-1)
    return cos, sin


def _rope(x, cos, sin):
    q = x.shape[-1] // 4
    swapped = jnp.concatenate([x[..., q:2 * q], x[..., :q], x[..., 3 * q:], x[..., 2 * q:3 * q]], axis=-1)
    return x * cos[:, None, :] + swapped * sin[:, None, :]


def _rope_latent(x, cos, sin):
    return jnp.concatenate([x[:, :CTX_LEN], _rope(x[:, CTX_LEN:], cos, sin)], axis=1)


def _attend_all(q, k, v, need_ctx):
    q, k, v = q.astype(BF16), k.astype(BF16), v.astype(BF16)
    o_lat = attention(q[:, :, CTX_LEN:], k, v)
    if need_ctx:
        o_ctx = attention(q[:, :, :CTX_LEN], k[:, :, :CTX_LEN], v[:, :, :CTX_LEN])
    else:
        o_ctx = jnp.zeros(q.shape[:2] + (CTX_LEN, q.shape[3]), F32)
    return jnp.concatenate([o_ctx, o_lat], axis=2)


def _gqa(p, qn_g, kn_g, cos, sin, need_ctx):
    b, s, _ = p.shape
    nq = GQA_Q_HEADS * HEAD_DIM
    nk = GQA_KV_HEADS * HEAD_DIM
    grp = GQA_Q_HEADS // GQA_KV_HEADS
    q = _rms_norm(p[..., :nq].reshape(b, s, GQA_Q_HEADS, HEAD_DIM), qn_g)
    k = _rms_norm(p[..., nq:nq + nk].reshape(b, s, GQA_KV_HEADS, HEAD_DIM), kn_g)
    v = p[..., nq + nk:nq + 2 * nk]
    q = _rope_latent(q, cos, sin) * (HEAD_DIM ** -0.5)
    k = _rope_latent(k, cos, sin)
    qh = q.transpose(0, 2, 1, 3)
    zeros = jnp.zeros_like(qh)
    first = (jnp.arange(GQA_Q_HEADS) < grp)[None, :, None, None]
    q_pad = jnp.concatenate([jnp.where(first, qh, zeros), jnp.where(first, zeros, qh)], axis=-1)
    k_all = k.reshape(b, 1, s, nk)
    v_all = v.reshape(b, 1, s, nk)
    o = _attend_all(q_pad, k_all, v_all, need_ctx)
    o = jnp.where(first, o[..., :HEAD_DIM], o[..., HEAD_DIM:])
    return o.transpose(0, 2, 1, 3).reshape(b, s, nq)


def _mla(p, qa_g, w_qup, kva_g, w_kvup, cos, sin, need_ctx):
    b, s, _ = p.shape
    r0 = MLA_Q_RANK
    r1 = MLA_Q_RANK + MLA_KV_RANK
    zero = jnp.zeros((1, 1, r0), F32)
    qa = p[..., :r0].reshape(b * s, r0)
    q = norm_mod_matmul(qa, qa_g[None, :], zero, zero, w_qup.astype(BF16), lambda i: 0)
    q = q.reshape(b, s, MLA_HEADS, MLA_NOPE + MLA_ROPE)
    zero = jnp.zeros((1, 1, MLA_KV_RANK), F32)
    kva = p[..., r0:r1].reshape(b * s, MLA_KV_RANK)
    kv = norm_mod_matmul(kva, kva_g[None, :], zero, zero, w_kvup.astype(BF16), lambda i: 0)
    kv = kv.reshape(b, s, MLA_HEADS, MLA_NOPE + MLA_V)
    q_rope = _rope_latent(q[..., MLA_NOPE:], cos, sin)
    k_rope = _rope_latent(p[..., r1:r1 + MLA_ROPE][:, :, None, :], cos, sin)
    scale = (MLA_NOPE + MLA_ROPE) ** -0.5
    pad = jnp.zeros((b, s, MLA_HEADS, V7X_LANES - MLA_NOPE - MLA_ROPE), F32)
    q_pad = jnp.concatenate([q[..., :MLA_NOPE] * scale, q_rope * scale, pad], axis=-1)
    k_pad = jnp.concatenate([kv[..., :MLA_NOPE], jnp.broadcast_to(k_rope, (b, s, MLA_HEADS, MLA_ROPE)), pad], axis=-1)
    v_pad = jnp.concatenate([kv[..., MLA_NOPE:], jnp.zeros((b, s, MLA_HEADS, V7X_LANES - MLA_V), F32)], axis=-1)
    tr = lambda a: a.transpose(0, 2, 1, 3)
    o = _attend_all(tr(q_pad), tr(k_pad), tr(v_pad), need_ctx)
    return o[..., :MLA_V].transpose(0, 2, 1, 3).reshape(b, s, MLA_HEADS * MLA_V)


def kernel(x, c, ctx, c_ctx, ada_w, ada_b, norm1_g, norm2_g, ev_w_in, ev_w_out, gqa_qn_g, gqa_kn_g,
           hg_lb_logits, hg_onorm_g, od_w_in, od_w_out, mla_qa_g, mla_w_qup, mla_kva_g, mla_w_kvup,
           s5_a_re, s5_a_im, s5_log_dt, s5_b_re, s5_b_im, s5_c_re, s5_c_im, s5_d, s5_w_glu, s5_b_glu,
           peer_wq, peer_keys, peer_u, peer_v, final_g):
    b, t, d = x.shape
    s = CTX_LEN + t
    depth = ada_w.shape[0]
    rows = t // GRID_W
    cos_g, sin_g = _rope_tables(rows, HEAD_DIM)
    cos_m, sin_m = _rope_tables(rows, MLA_ROPE)
    lb_all = jnp.cumsum(jax.nn.softmax(hg_lb_logits, axis=1), axis=1)

    def mod_index(tile):
        per, nctx = s // tile, CTX_LEN // tile
        return lambda i: jnp.where(i % per < nctx, b, i // per)

    mi_row = mod_index(ROW_TILE)
    mi_peer = mod_index(PEER_TILE)

    xa = jnp.concatenate([ctx, x], axis=1).reshape(b * s, d)
    s_all = jnp.concatenate([jax.nn.silu(c), jax.nn.silu(c_ctx)[None, :]], axis=0)
    s_pad = jnp.concatenate([s_all, jnp.zeros((V7X_SUBLANES - (b + 1) % V7X_SUBLANES, d), F32)], axis=0)

    for layer in range(depth):
        need_ctx = layer < depth - 1
        j = layer // 2
        mod = matmul(s_pad, ada_w[layer].astype(BF16), tm=s_pad.shape[0])[:b + 1] + ada_b[layer]
        mod = [m[:, None, :] for m in jnp.split(mod, 6, axis=-1)]
        if layer % 2 == 0:
            pa2 = norm_mod_matmul(xa, norm1_g[layer][None, :], mod[0], mod[1], ev_w_in[j].astype(BF16), mi_row)
            pa = pa2.reshape(b, s, -1)
            ya = _gqa(pa[..., :GQA_IN], gqa_qn_g[j], gqa_kn_g[j], cos_g, sin_g, need_ctx)
            yb = _hgrn2(pa2, b, lb_all[:, j], hg_onorm_g[j]).reshape(b, s, HG_W)
            w_out = ev_w_out[j]
        else:
            pa = norm_mod_matmul(xa, norm1_g[layer][None, :], mod[0], mod[1], od_w_in[j].astype(BF16), mi_row)
            pa = pa.reshape(b, s, -1)
            ya = _mla(pa[..., :MLA_IN], mla_qa_g[j], mla_w_qup[j], mla_kva_g[j], mla_w_kvup[j], cos_m, sin_m, need_ctx)
            y5 = _s5_core(pa[..., MLA_IN:], s5_a_re[j], s5_a_im[j], s5_log_dt[j], s5_b_re[j], s5_b_im[j],
                          s5_c_re[j], s5_c_im[j], s5_d[j])
            z = jax.nn.gelu(y5, approximate=False).reshape(b * s, S5_WIDTH)
            gl = matmul(z, s5_w_glu[j].astype(BF16)) + s5_b_glu[j]
            yb = (z * jax.nn.sigmoid(gl)).reshape(b, s, S5_WIDTH)
            w_out = od_w_out[j]
        y =jnp.concatenate([ya, yb], axis=-1).reshape(b * s, -1)
        xa = matmul_residual(y, w_out.astype(BF16), xa, mod[2], mi_row)
        out = peer(xa, norm2_g[layer][None, :], mod[3], mod[4],
                   peer_wq[layer], peer_keys[layer], peer_u[layer], peer_v[layer], mi_peer)
        gate = jnp.concatenate([jnp.broadcast_to(mod[5][b:b + 1], (b, CTX_LEN, d)),
                                jnp.broadcast_to(mod[5][:b], (b, t, d))], axis=1).reshape(b * s, d)
        xa = xa + gate * out
    xl = xa.reshape(b, s, d)[:, CTX_LEN:].reshape(b * t, d)
    return rmsnorm_rows(xl, final_g[None, :]).reshape(b, t, d)
```

```python
import functools
import math

import jax
import jax.numpy as jnp
from jax import lax
from jax.experimental import pallas as pl
from jax.experimental.pallas import tpu as pltpu

F32 = jnp.float32
BF16 = jnp.bfloat16

D_MODEL = 1024
GRID_W = 64
CTX_LEN = 256
EPS = 1e-6
ROPE_THETA = 10000.0

MIX_HALF = D_MODEL // 2
HEAD_DIM = 64
GQA_Q_HEADS = MIX_HALF // HEAD_DIM
GQA_KV_HEADS = GQA_Q_HEADS // 4
GQA_IN = (GQA_Q_HEADS + 2 * GQA_KV_HEADS) * HEAD_DIM

HG_DK = 128
HG_DV = 128
HG_HEADS = MIX_HALF // HG_DV
HG_W = HG_HEADS * HG_DK
GLA_CHUNK = 64

MLA_HEADS = MIX_HALF // HEAD_DIM
MLA_NOPE = 64
MLA_ROPE = 32
MLA_V = 64
MLA_Q_RANK = 384
MLA_KV_RANK = 256
MLA_IN = MLA_Q_RANK + MLA_KV_RANK + MLA_ROPE

S5_WIDTH = MIX_HALF
S5_GROUP = 16
S5_GROUPS = S5_WIDTH // S5_GROUP
S5_STATE = 64

PEER_HEADS = 8
PEER_KEYS = 128
N_EXPERTS = PEER_KEYS * PEER_KEYS
PEER_TOPK = 16
PEER_QDIM = 256
PEER_HALF = PEER_QDIM // 2
PEER_PAIRS = PEER_HEADS * PEER_TOPK

V7X_LANES = 128
V7X_SUBLANES = 8
V7X_VMEM_BYTES = 64 * 1024 * 1024
VMEM_LIMIT = V7X_VMEM_BYTES - 8 * 1024 * 1024

ROW_TILE = 256
PEER_TILE = 128
ATTN_Q_TILE = 512
EXPERT_WORDS = D_MODEL // 2
EXPERT_ROWS = EXPERT_WORDS // V7X_LANES


def _cparams(*sem):
    return pltpu.CompilerParams(dimension_semantics=sem, vmem_limit_bytes=VMEM_LIMIT)


def _norm_mod(x, g, shift, scale):
    ms = jnp.mean(x * x, axis=-1, keepdims=True)
    h = (x * lax.rsqrt(ms + EPS)) * g
    return h * (1.0 + scale) + shift


def _nmm_kernel(x_ref, g_ref, sh_ref, sc_ref, w_ref, o_ref):
    h = _norm_mod(x_ref[...], g_ref[...], sh_ref[0], sc_ref[0])
    o_ref[...] = jnp.dot(h.astype(BF16), w_ref[...], preferred_element_type=F32)


def norm_mod_matmul(x, g, shift, scale, w, mod_index, tm=ROW_TILE):
    r, k = x.shape
    n = w.shape[1]
    assert r % tm == 0 and w.shape[0] == k
    return pl.pallas_call(
        _nmm_kernel,
        grid=(r // tm,),
        in_specs=[
            pl.BlockSpec((tm, k), lambda i: (i, 0)),
            pl.BlockSpec((1, k), lambda i: (0, 0)),
            pl.BlockSpec((1, 1, k), lambda i: (mod_index(i), 0, 0)),
            pl.BlockSpec((1, 1, k), lambda i: (mod_index(i), 0, 0)),
            pl.BlockSpec((k, n), lambda i: (0, 0)),
        ],
        out_specs=pl.BlockSpec((tm, n), lambda i: (i, 0)),
        out_shape=jax.ShapeDtypeStruct((r, n), F32),
        compiler_params=_cparams("parallel"),
        name="norm_mod_matmul",
    )(x, g, shift, scale, w)


def _mm_kernel(a_ref, w_ref, o_ref):
    o_ref[...] = jnp.dot(a_ref[...].astype(BF16), w_ref[...], preferred_element_type=F32)


def matmul(a, w, tm=ROW_TILE):
    r, k = a.shape
    n = w.shape[1]
    assert r % tm == 0
    return pl.pallas_call(
        _mm_kernel,
        grid=(r // tm,),
        in_specs=[pl.BlockSpec((tm, k), lambda i: (i, 0)), pl.BlockSpec((k, n), lambda i: (0, 0))],
        out_specs=pl.BlockSpec((tm, n), lambda i: (i, 0)),
        out_shape=jax.ShapeDtypeStruct((r, n), F32),
        compiler_params=_cparams("parallel"),
        name="matmul",
    )(a, w)


def _mmres_kernel(a_ref, w_ref, x_ref, gate_ref, o_ref):
    y = jnp.dot(a_ref[...].astype(BF16), w_ref[...], preferred_element_type=F32)
    o_ref[...] = x_ref[...] + gate_ref[0] * y


def matmul_residual(a, w, x, gate, mod_index, tm=ROW_TILE):
    r, k = a.shape
    n = w.shape[1]
    assert r % tm == 0
    return pl.pallas_call(
        _mmres_kernel,
        grid=(r // tm,),
        in_specs=[
            pl.BlockSpec((tm, k), lambda i: (i, 0)),
            pl.BlockSpec((k, n), lambda i: (0, 0)),
            pl.BlockSpec((tm, n), lambda i: (i, 0)),
            pl.BlockSpec((1, 1, n), lambda i: (mod_index(i), 0, 0)),
        ],
        out_specs=pl.BlockSpec((tm, n), lambda i: (i, 0)),
        out_shape=jax.ShapeDtypeStruct((r, n), F32),
        compiler_params=_cparams("parallel"),
        name="matmul_residual",
    )(a, w, x, gate)


def _rms_kernel(x_ref, g_ref, o_ref):
    x = x_ref[...]
    ms = jnp.mean(x * x, axis=-1, keepdims=True)
    o_ref[...] = (x * lax.rsqrt(ms + EPS)) * g_ref[...]


def rmsnorm_rows(x, g, tm=ROW_TILE):
    r, k = x.shape
    return pl.pallas_call(
        _rms_kernel,
        grid=(r // tm,),
        in_specs=[pl.BlockSpec((tm, k), lambda i: (i, 0)), pl.BlockSpec((1, k), lambda i: (0, 0))],
        out_specs=pl.BlockSpec((tm, k), lambda i: (i, 0)),
        out_shape=jax.ShapeDtypeStruct((r, k), F32),
        compiler_params=_cparams("parallel"),
        name="rmsnorm_rows",
    )(x, g)


ATTN_KV_CHUNKS = 3
V7X_MXU_DIM = 256
LOG2E = math.log2(math.e)


def _attn_kernel(q_ref, k_ref, v_ref, o_ref):
    q = q_ref[0, 0]
    sk = k_ref.shape[2]
    nch = ATTN_KV_CHUNKS if sk % (ATTN_KV_CHUNKS * V7X_MXU_DIM) == 0 else 1
    ck = sk // nch
    m = l = acc = None
    for c in range(nch):
        kc = k_ref[0, 0, c * ck:(c + 1) * ck, :]
        vc = v_ref[0, 0, c * ck:(c + 1) * ck, :]
        s = lax.dot_general(q, kc, (((1,), (1,)), ((), ())), preferred_element_type=F32)
        mc = jnp.max(s, axis=-1, keepdims=True)
        if c == 0:
            m = mc
            p = jnp.exp2(s - m)
            l = jnp.sum(p, axis=-1, keepdims=True)
            acc = jnp.dot(p.astype(BF16), vc, preferred_element_type=F32)
        else:
            m_new = jnp.maximum(m, mc)
            a = jnp.exp2(m - m_new)
            p = jnp.exp2(s - m_new)
            l = a * l + jnp.sum(p, axis=-1, keepdims=True)
            acc = a * acc + jnp.dot(p.astype(BF16), vc, preferred_element_type=F32)
            m = m_new
    o_ref[0, 0] = acc / l


def attention(q, k, v, tq=ATTN_Q_TILE):
    b, h, sq, dl = q.shape
    hk, sk = k.shape[1], k.shape[2]
    grp = h // hk
    tq = min(tq, sq)
    assert sq % tq == 0
    return pl.pallas_call(
        _attn_kernel,
        grid=(b, h, sq // tq),
        in_specs=[
            pl.BlockSpec((1, 1, tq, dl), lambda bi, hi, qi: (bi, hi, qi, 0)),
            pl.BlockSpec((1, 1, sk, dl), lambda bi, hi, qi: (bi, hi // grp, 0, 0)),
            pl.BlockSpec((1, 1, sk, dl), lambda bi, hi, qi: (bi, hi // grp, 0, 0)),
        ],
        out_specs=pl.BlockSpec((1, 1, tq, dl), lambda bi, hi, qi: (bi, hi, qi, 0)),
        out_shape=jax.ShapeDtypeStruct((b, h, sq, dl), F32),
        compiler_params=_cparams("parallel", "parallel", "parallel"),
        name="attention",
    )(q, k, v)


def _topk_rows(s, iota, k):
    n = s.shape[0]
    row = lax.broadcasted_iota(jnp.int32, (k, s.shape[1]), 0)
    vals = jnp.zeros((k, s.shape[1]), F32)
    ids = jnp.zeros((k, s.shape[1]), F32)
    for r in range(k):
        m = jnp.max(s, axis=0, keepdims=True)
        am = jnp.min(jnp.where(s == m, iota, float(n)), axis=0, keepdims=True)
        vals = jnp.where(row == r, m, vals)
        ids = jnp.where(row == r, am, ids)
        s = jnp.where(iota == am, -jnp.inf, s)
    return vals, ids


def _peer_ret_kernel(x_ref, g_ref, sh_ref, sc_ref, wq_ref, keys_ref, h_ref, eidx_ref, gw_ref, hb_ref, e_scr, w_scr):
    h = _norm_mod(x_ref[...], g_ref[...], sh_ref[0], sc_ref[0])
    h_ref[...] = h
    hb_ref[...] = h.astype(BF16)
    tb = x_ref.shape[0]
    iota_n = lax.broadcasted_iota(jnp.int32, (PEER_KEYS, tb), 0).astype(F32)
    half = PEER_TOPK // 2
    sub = lambda n: lax.broadcasted_iota(jnp.int32, (n, tb), 0).astype(F32)
    pos_c = jnp.concatenate([sub(PEER_TOPK)] + [sub(half) + float(a * PEER_TOPK) for a in range(1, half)]
                            + [(sub(half) + float(half)) * float(PEER_TOPK)], axis=0)

    def pair_up(first, second, scale):
        return jnp.concatenate([first[0:1] * scale + second]
                               + [first[a:a + 1] * scale + second[0:half] for a in range(1, half)]
                               + [first[half:] * scale + second[0:1]], axis=0)

    def head(hd, carry):
        q = jnp.dot(hb_ref[...], wq_ref[hd], preferred_element_type=F32)
        tops = []
        for p in range(2):
            qp = q[:, p * PEER_HALF:(p + 1) * PEER_HALF].astype(BF16)
            s = lax.dot_general(keys_ref[hd, p], qp, (((1,), (1,)), ((), ())),
                                preferred_element_type=F32)
            tops.append(_topk_rows(s, iota_n, PEER_TOPK))
        (s1, i1), (s2, i2) = tops
        cand = pair_up(s1, s2, 1.0)
        cidx = pair_up(i1, i2, float(PEER_KEYS))
        row = lax.broadcasted_iota(jnp.int32, (PEER_TOPK, tb), 0)
        sc = jnp.zeros((PEER_TOPK, tb), F32)
        ex = jnp.zeros((PEER_TOPK, tb), F32)
        for r in range(PEER_TOPK):
            m = jnp.max(cand, axis=0, keepdims=True)
            am = jnp.min(jnp.where(cand == m, pos_c, float(PEER_TOPK * PEER_TOPK)), axis=0, keepdims=True)
            hit = pos_c == am
            e = jnp.max(jnp.where(hit, cidx, 0.0), axis=0, keepdims=True)
            sc = jnp.where(row == r, m, sc)
            ex = jnp.where(row == r, e, ex)
            cand = jnp.where(hit, -jnp.inf, cand)
        pexp = jnp.exp(sc - sc[0:1])
        rows = pl.ds(pl.multiple_of(hd * PEER_TOPK, PEER_TOPK), PEER_TOPK)
        w_scr[rows, :] = pexp / jnp.sum(pexp, axis=0, keepdims=True)
        e_scr[rows, :] = ex
        return carry

    lax.fori_loop(0, PEER_HEADS, head, 0)
    gw_ref[...] = w_scr[...].T
    eidx_ref[...] = (e_scr[...].T * float(EXPERT_ROWS)).astype(jnp.int32)


def peer_retrieve(x, g, shift, scale, wq_heads, keys, mod_index, tb=PEER_TILE):
    r, d = x.shape
    nb = r // tb
    return pl.pallas_call(
        _peer_ret_kernel,
        grid=(nb,),
        in_specs=[
            pl.BlockSpec((tb, d), lambda i: (i, 0)),
            pl.BlockSpec((1, d), lambda i: (0, 0)),
            pl.BlockSpec((1, 1, d), lambda i: (mod_index(i), 0, 0)),
            pl.BlockSpec((1, 1, d), lambda i: (mod_index(i), 0, 0)),
            pl.BlockSpec((PEER_HEADS, d, PEER_QDIM), lambda i: (0, 0, 0)),
            pl.BlockSpec((PEER_HEADS, 2, PEER_KEYS, PEER_HALF), lambda i: (0, 0, 0, 0)),
        ],
        out_specs=[
            pl.BlockSpec((tb, d), lambda i: (i, 0)),
            pl.BlockSpec((tb, PEER_PAIRS), lambda i: (i, 0)),
            pl.BlockSpec((tb, PEER_PAIRS), lambda i: (i, 0)),
        ],
        out_shape=[
            jax.ShapeDtypeStruct((r, d), F32),
            jax.ShapeDtypeStruct((r, PEER_PAIRS), jnp.int32),
            jax.ShapeDtypeStruct((r, PEER_PAIRS), F32),
        ],
        scratch_shapes=[pltpu.VMEM((tb, d), BF16), pltpu.VMEM((PEER_PAIRS, tb), F32),
                        pltpu.VMEM((PEER_PAIRS, tb), F32)],
        compiler_params=_cparams("parallel"),
        name="peer_retrieve",
    )(x, g, shift, scale, wq_heads, keys)


def pack_rows_bf16(tab):
    e, d = tab.shape
    t16 = lax.bitcast_convert_type(tab.astype(BF16), jnp.uint16).astype(jnp.uint32)
    t16 = t16.reshape(e, EXPERT_ROWS, 2, V7X_LANES)
    word = t16[:, :, 0, :] | (t16[:, :, 1, :] << 16)
    return lax.bitcast_convert_type(word, jnp.int32).reshape(e * EXPERT_ROWS, V7X_LANES)


PEER_GROUP = V7X_SUBLANES
SLAB_ROWS_BF16 = 2 * EXPERT_ROWS
PAIR_LANES = PEER_PAIRS * SLAB_ROWS_BF16


def _slab(tab_ref, row):
    return tab_ref[pl.ds(pl.multiple_of(row, EXPERT_ROWS), EXPERT_ROWS), :]


def _token_rows_bf16(h8, i):
    row = lax.broadcasted_iota(jnp.int32, (SLAB_ROWS_BF16, V7X_LANES), 0)
    hq = jnp.zeros((SLAB_ROWS_BF16, V7X_LANES), F32)
    for r in range(SLAB_ROWS_BF16):
        hq = jnp.where(row == r, h8[i:i + 1, r * V7X_LANES:(r + 1) * V7X_LANES], hq)
    return hq.astype(BF16)


def _peer_u_kernel(idx_ref, tab_ref, h_ref, gw_ref, fold_ref, o_ref, s_ref):
    groups = o_ref.shape[0] // PEER_GROUP
    row = lax.broadcasted_iota(jnp.int32, (PEER_GROUP, V7X_LANES), 0)

    def finish(g, sums):
        hi = sums.astype(BF16)
        lo = (sums - hi.astype(F32)).astype(BF16)
        a = (jnp.dot(hi, fold_ref[...], preferred_element_type=F32)
             + jnp.dot(lo, fold_ref[...], preferred_element_type=F32))
        rows = pl.ds(pl.multiple_of(g * PEER_GROUP, PEER_GROUP), PEER_GROUP)
        gelu = 0.5 * a * (1.0 + lax.erf(a * (2.0 ** -0.5)))
        o_ref[rows, :] = gelu * gw_ref[rows, :]

    def group(g, prev):
        finish(jnp.maximum(g - 1, 0), prev)
        sums = jnp.zeros((PEER_GROUP, PAIR_LANES), F32)
        h8 = h_ref[pl.ds(pl.multiple_of(g * PEER_GROUP, PEER_GROUP), PEER_GROUP), :]
        for i in range(PEER_GROUP):
            t = g * PEER_GROUP + i
            hb = _token_rows_bf16(h8, i)
            tok_idx = idx_ref.at[pl.ds(t * PEER_PAIRS, PEER_PAIRS)]
            for j in range(PEER_PAIRS):
                u = pltpu.bitcast(_slab(tab_ref, tok_idx[j]), BF16)
                s_ref[pl.ds((i * PEER_PAIRS + j) * EXPERT_ROWS, EXPERT_ROWS), :] = pltpu.bitcast(u * hb, jnp.int32)
            prod = pltpu.bitcast(s_ref[pl.ds(i * PEER_PAIRS * EXPERT_ROWS, PEER_PAIRS * EXPERT_ROWS), :], BF16)
            pick = jnp.where(row == i, 1.0, 0.0).astype(BF16)
            sums = sums + lax.dot_general(pick, prod, (((1,), (1,)), ((), ())), preferred_element_type=F32)
        return sums

    last = lax.fori_loop(0, groups, group, jnp.zeros((PEER_GROUP, PAIR_LANES), F32))
    finish(groups - 1, last)


def peer_expert_gates(idx4, tab, h, gw, tb=PEER_TILE):
    r = h.shape[0]
    fold = jnp.repeat(jnp.eye(PEER_PAIRS, dtype=BF16), SLAB_ROWS_BF16, axis=0)
    return pl.pallas_call(
        _peer_u_kernel,
        grid=(r // tb,),
        in_specs=[
            pl.BlockSpec((tb * PEER_PAIRS,), lambda i: (i,), memory_space=pltpu.SMEM),
            pl.BlockSpec(tab.shape, lambda i: (0, 0), pipeline_mode=pl.Buffered(1)),
            pl.BlockSpec((tb, h.shape[1]), lambda i: (i, 0)),
            pl.BlockSpec((tb, PEER_PAIRS), lambda i: (i, 0)),
            pl.BlockSpec((PAIR_LANES, PEER_PAIRS), lambda i: (0, 0)),
        ],
        out_specs=pl.BlockSpec((tb, PEER_PAIRS), lambda i: (i, 0)),
        out_shape=jax.ShapeDtypeStruct((r, PEER_PAIRS), F32),
        scratch_shapes=[pltpu.VMEM((PEER_GROUP * PEER_PAIRS * EXPERT_ROWS, V7X_LANES), jnp.int32)],
        compiler_params=_cparams("arbitrary"),
        name="peer_expert_gates",
    )(idx4, tab, h, gw, fold)


def _peer_v_kernel(idx_ref, w_ref, tab_ref, spread_ref, o_ref, g_ref):
    groups = w_ref.shape[0] // PEER_GROUP
    last_slot = PEER_GROUP - 1
    shape = (SLAB_ROWS_BF16, PAIR_LANES)
    diag = (lax.broadcasted_iota(jnp.int32, shape, 1) % SLAB_ROWS_BF16) == lax.broadcasted_iota(jnp.int32, shape, 0)
    slot_rows = PEER_PAIRS * EXPERT_ROWS

    @pl.when(pl.program_id(0) == 0)
    def _():
        g_ref[pl.ds(last_slot * slot_rows, slot_rows), :] = jnp.zeros((slot_rows, V7X_LANES), jnp.int32)

    def contract(slot, wrow, t):
        rows = pltpu.bitcast(g_ref[pl.ds(slot * slot_rows, slot_rows), :], BF16)
        wi = jnp.where(diag, jnp.broadcast_to(wrow, shape), 0.0).astype(BF16)
        o_ref[pl.ds(pl.multiple_of(t * SLAB_ROWS_BF16, SLAB_ROWS_BF16), SLAB_ROWS_BF16), :] = jnp.dot(
            wi, rows, preferred_element_type=F32)

    def group(g, prev_wide):
        contract(last_slot, prev_wide[last_slot:], jnp.maximum(g * PEER_GROUP - 1, 0))
        w8 = w_ref[pl.ds(pl.multiple_of(g * PEER_GROUP, PEER_GROUP), PEER_GROUP), :]
        wide = jnp.dot(w8.astype(BF16), spread_ref[...], preferred_element_type=F32)
        for i in range(PEER_GROUP):
            t = g * PEER_GROUP + i
            tok_idx = idx_ref.at[pl.ds(t * PEER_PAIRS, PEER_PAIRS)]
            for j in range(PEER_PAIRS):
                g_ref[pl.ds((i * PEER_PAIRS + j) * EXPERT_ROWS, EXPERT_ROWS), :] = _slab(tab_ref, tok_idx[j])
            if i < last_slot:
                contract(i, wide[i:i + 1], t)
        return wide

    wide = lax.fori_loop(0, groups, group, jnp.zeros((PEER_GROUP, PAIR_LANES), F32))
    contract(last_slot, wide[last_slot:], groups * PEER_GROUP - 1)


def peer_expert_mix(idx4, w, tab, tb=PEER_TILE):
    r = w.shape[0]
    spread = jnp.repeat(jnp.eye(PEER_PAIRS, dtype=BF16), SLAB_ROWS_BF16, axis=1)
    return pl.pallas_call(
        _peer_v_kernel,
        grid=(r // tb,),
        in_specs=[
            pl.BlockSpec((tb * PEER_PAIRS,), lambda i: (i,), memory_space=pltpu.SMEM),
            pl.BlockSpec((tb, PEER_PAIRS), lambda i: (i, 0)),
            pl.BlockSpec(tab.shape, lambda i: (0, 0), pipeline_mode=pl.Buffered(1)),
            pl.BlockSpec((PEER_PAIRS, PAIR_LANES), lambda i: (0, 0)),
        ],
        out_specs=pl.BlockSpec((tb * SLAB_ROWS_BF16, V7X_LANES), lambda i: (i, 0)),
        out_shape=jax.ShapeDtypeStruct((r * SLAB_ROWS_BF16, V7X_LANES), F32),
        scratch_shapes=[pltpu.VMEM((PEER_GROUP * PEER_PAIRS * EXPERT_ROWS, V7X_LANES), jnp.int32)],
        compiler_params=_cparams("arbitrary"),
        name="peer_expert_mix",
    )(idx4, w, tab, spread)


def peer(x, g, shift, scale, wq, keys, u_tab, v_tab, mod_index_peer):
    r, d = x.shape
    wq_heads = wq.astype(BF16).reshape(d, PEER_HEADS, PEER_QDIM).transpose(1, 0, 2)
    h, idx4, gw = peer_retrieve(x, g, shift, scale, wq_heads, keys.astype(BF16), mod_index_peer)
    idx4 = idx4.reshape(r * PEER_PAIRS)
    w = peer_expert_gates(idx4, pack_rows_bf16(u_tab), h, gw)
    return peer_expert_mix(idx4, w, pack_rows_bf16(v_tab)).reshape(r, d)


HG_TILE = 128
HG_SUB = 16
HG_PAIR = 2 * HG_DK


def _hgrn_kernel(q_ref, f_ref, v_ref, lb_ref, o_ref, st_ref, *, rev):
    @pl.when(pl.program_id(2) == 0)
    def _():
        st_ref[...] = jnp.zeros_like(st_ref)

    n = q_ref.shape[0]
    lb = lb_ref[...]
    f = lb + (1.0 - lb) * jax.nn.sigmoid(f_ref[...])
    kk = 1.0 - f
    lf = jnp.log(f)
    t = lax.broadcasted_iota(jnp.int32, (n, n), 0)
    s = lax.broadcasted_iota(jnp.int32, (n, n), 1)
    same = (t // HG_SUB) == (s // HG_SUB)
    mid = (t // HG_SUB) * HG_SUB + (HG_SUB // 2 if rev else HG_SUB // 2 - 1)
    seen = (s >= t) if rev else (s <= t)
    seen_mid = (s >= mid) if rev else (s <= mid)
    one = lambda m: jnp.where(m, 1.0, 0.0).astype(F32)
    hp = lambda a, b: jnp.dot(a, b, precision=lax.Precision.HIGHEST, preferred_element_type=F32)
    cum = hp(one(same & seen), lf)
    ref = hp(one(same & seen_mid), lf)
    last = hp(one(same), lf)
    q = q_ref[...]
    qe = (q * jnp.exp(cum)).astype(BF16)
    qm = (q * jnp.exp(cum - ref)).astype(BF16)
    km = (kk * jnp.exp(ref - cum)).astype(BF16)
    kd = (kk * jnp.exp(last - cum)).astype(BF16)
    dec = jnp.exp(last)
    vb = v_ref[...].astype(BF16)
    ti = lax.broadcasted_iota(jnp.int32, (HG_SUB, HG_SUB), 0)
    si = lax.broadcasted_iota(jnp.int32, (HG_SUB, HG_SUB), 1)
    causal = (si >= ti) if rev else (si <= ti)
    steps = range(n // HG_SUB)
    for c in (reversed(steps) if rev else steps):
        rows = slice(c * HG_SUB, (c + 1) * HG_SUB)
        for h in range(HG_PAIR // HG_DK):
            cols = slice(h * HG_DK, (h + 1) * HG_DK)
            st = st_ref[h]
            att = lax.dot_general(qm[rows, cols], km[rows, cols], (((1,), (1,)), ((), ())),
                                  preferred_element_type=F32)
            att = jnp.where(causal, att, 0.0).astype(BF16)
            o = (lax.dot_general(qe[rows, cols], st.astype(BF16), (((1,), (1,)), ((), ())),
                                 preferred_element_type=F32)
                 + jnp.dot(att, vb[rows, cols], preferred_element_type=F32))
            o_ref[rows, cols] = o
            upd = lax.dot_general(vb[rows, cols], kd[rows, cols], (((0,), (0,)), ((), ())),
                                  preferred_element_type=F32)
            st_ref[h] = st * dec[c * HG_SUB:c * HG_SUB + 1, cols] + upd


def hgrn_scan(pa, lb, rev, batch):
    rows = pa.shape[0]
    per = rows // batch // HG_TILE
    nctx = CTX_LEN // HG_TILE
    col0 = GQA_IN // HG_PAIR
    blocks = HG_W // HG_PAIR

    def tok(b, k):
        if rev:
            k = jnp.where(k < nctx, nctx - 1 - k, per - 1 - (k - nctx))
        return b * per + k

    spec = lambda cb: pl.BlockSpec((HG_TILE, HG_PAIR), lambda b, p, k: (tok(b, k), cb + p))
    d = 1 if rev else 0
    return pl.pallas_call(
        functools.partial(_hgrn_kernel, rev=rev),
        grid=(batch, blocks, per),
        in_specs=[spec(col0), spec(col0 + (1 + d) * blocks), spec(col0 + 3 * blocks),
                  pl.BlockSpec((None, 1, HG_PAIR), lambda b, p, k: (d, 0, p))],
        out_specs=pl.BlockSpec((HG_TILE, HG_PAIR), lambda b, p, k: (tok(b, k), p)),
        out_shape=jax.ShapeDtypeStruct((rows, HG_W), F32),
        scratch_shapes=[pltpu.VMEM((HG_PAIR // HG_DK, HG_DV, HG_DK), F32)],
        compiler_params=_cparams("parallel", "parallel", "arbitrary"),
        name="hgrn_scan_rev" if rev else "hgrn_scan_fwd",
    )(pa, pa, pa, lb)


def _rms_norm(x, g):
    xf = x.astype(F32)
    y = xf * lax.rsqrt(jnp.mean(xf * xf, axis=-1, keepdims=True) + EPS)
    return y * g.astype(F32)


def _hgrn2(pa2, batch, lb, onorm_g):
    lb = lb.reshape(2, 1, HG_W)
    o = hgrn_scan(pa2, lb, False, batch) + hgrn_scan(pa2, lb, True, batch)
    o = _rms_norm(o.reshape(-1, HG_HEADS, HG_DV), onorm_g).reshape(-1, HG_W)
    return o * jax.nn.silu(pa2[:, GQA_IN + 4 * HG_W:])


S5_CHUNK = 64
S5_LANES = S5_GROUPS * S5_STATE


def _s5_kernel(u_ref, bre_ref, bim_ref, cre_ref, cim_ref, are_ref, aim_ref, y_ref, xre, xim, hre, him):
    nb = V7X_SUBLANES // 2
    tiles = u_ref.shape[0] // V7X_SUBLANES
    direction = pl.program_id(0)
    rev = direction == 1

    @pl.when(pl.program_id(1) == 0)
    def _():
        hre[...] = jnp.zeros_like(hre)
        him[...] = jnp.zeros_like(him)

    ub = u_ref[...].astype(BF16)
    xre[...] = jnp.dot(ub, bre_ref[...], preferred_element_type=F32)
    xim[...] = jnp.dot(ub, bim_ref[...], preferred_element_type=F32)
    shape = (V7X_SUBLANES, S5_LANES)
    second = (lax.broadcasted_iota(jnp.int32, shape, 0) // nb) != direction
    are = jnp.broadcast_to(are_ref[...], shape)
    aim = jnp.broadcast_to(aim_ref[...], shape)
    cre = jnp.where(second, are * are - aim * aim, are)
    cim = jnp.where(second, 2.0 * are * aim, aim)

    def step(k, carry):
        pr, pi = carry
        k = jnp.where(rev, tiles - 1 - k, k)
        r = pl.ds(pl.multiple_of(k * V7X_SUBLANES, V7X_SUBLANES), V7X_SUBLANES)
        xr, xi = xre[r, :], xim[r, :]
        sr = jnp.where(second, pltpu.roll(xr, nb, axis=0), 0.0)
        si = jnp.where(second, pltpu.roll(xi, nb, axis=0), 0.0)
        nr = xr + (are * sr - aim * si) + (cre * pr - cim * pi)
        ni = xi + (are * si + aim * sr) + (cre * pi + cim * pr)
        xre[r, :] = nr
        xim[r, :] = ni
        return (jnp.where(second, nr, pltpu.roll(nr, nb, axis=0)),
                jnp.where(second, ni, pltpu.roll(ni, nb, axis=0)))

    hr, hi = lax.fori_loop(0, tiles, step, (hre[...], him[...]))
    hre[...] = hr
    him[...] = hi
    y_ref[...] = (jnp.dot(xre[...].astype(BF16), cre_ref[...], preferred_element_type=F32)
                  - jnp.dot(xim[...].astype(BF16), cim_ref[...], preferred_element_type=F32))


def s5_scan(u, bre, bim, cre, cim, are, aim, nb):
    rows, w = u.shape
    blk = S5_CHUNK * nb
    assert rows % blk == 0 and 2 * nb == V7X_SUBLANES
    nblk, nctx = rows // blk, CTX_LEN // S5_CHUNK
    wspec = lambda shape: pl.BlockSpec((None,) + shape, lambda d, i: (d, 0, 0))

    def chunk(d, i):
        back = jnp.where(i < nctx, nctx - 1 - i, nblk - 1 - (i - nctx))
        return jnp.where(d == 1, back, i)
    return pl.pallas_call(
        _s5_kernel,
        grid=(2, rows // blk),
        in_specs=[
            pl.BlockSpec((blk, w), lambda d, i: (chunk(d, i), 0)),
            wspec((w, S5_LANES)), wspec((w, S5_LANES)), wspec((S5_LANES, w)), wspec((S5_LANES, w)),
            wspec((1, S5_LANES)), wspec((1, S5_LANES)),
        ],
        out_specs=pl.BlockSpec((None, blk, w), lambda d, i: (d, chunk(d, i), 0)),
        out_shape=jax.ShapeDtypeStruct((2, rows, w), F32),
        scratch_shapes=[pltpu.VMEM((blk, S5_LANES), F32), pltpu.VMEM((blk, S5_LANES), F32),
                        pltpu.VMEM((V7X_SUBLANES, S5_LANES), F32), pltpu.VMEM((V7X_SUBLANES, S5_LANES), F32)],
        compiler_params=_cparams("arbitrary", "arbitrary"),
        name="s5_scan",
    )(u, bre, bim, cre, cim, are, aim)


def _s5_core(u, a_re, a_im, log_dt, b_re, b_im, c_re, c_im, d_skip):
    b, s, w = u.shape
    dt = jnp.exp(log_dt)[..., None]
    mag = jnp.exp(a_re * dt)
    abar_re, abar_im = mag * jnp.cos(a_im * dt), mag * jnp.sin(a_im * dt)
    den = a_re * a_re + a_im * a_im
    k_re = ((abar_re - 1.0) * a_re + abar_im * a_im) / den
    k_im = (abar_im * a_re - (abar_re - 1.0) * a_im) / den
    bb_re = k_re[..., None] * b_re - k_im[..., None] * b_im
    bb_im = k_re[..., None] * b_im + k_im[..., None] * b_re
    eye = jnp.eye(S5_GROUPS, dtype=F32)
    bd_in = lambda m: jnp.einsum('dgpc,gh->dgchp', m, eye).reshape(2, w, S5_LANES).astype(BF16)
    bd_out = lambda m: jnp.einsum('dgcp,gh->dgphc', m, eye).reshape(2, S5_LANES, w).astype(BF16)
    y2 = s5_scan(u.transpose(1, 0, 2).reshape(s * b, w), bd_in(bb_re), bd_in(bb_im), bd_out(c_re), bd_out(c_im),
                 abar_re.reshape(2, 1, S5_LANES), abar_im.reshape(2, 1, S5_LANES), b)
    return d_skip * u + (y2[0] + y2[1]).reshape(s, b, w).transpose(1, 0, 2)


def _rope_tables(rows, rot_dim):
    axis_dim = rot_dim // 2
    inv = ROPE_THETA ** (-jnp.arange(0, axis_dim, 2, dtype=F32) / axis_dim)
    t = jnp.arange(rows * GRID_W)
    r = (t // GRID_W).astype(F32)[:, None] * inv
    c = (t % GRID_W).astype(F32)[:, None] * inv
    cos = jnp.concatenate([jnp.cos(r), jnp.cos(r), jnp.cos(c), jnp.cos(c)], axis=-1)
    sin = jnp.concatenate([-jnp.sin(r), jnp.sin(r), -jnp.sin(c), jnp.sin(c)], axis=-1)
    return cos, sin


def _rope(x, cos, sin):
    q = x.shape[-1] // 4
    swapped = jnp.concatenate([x[..., q:2 * q], x[..., :q], x[..., 3 * q:], x[..., 2 * q:3 * q]], axis=-1)
    return x * cos[:, None, :] + swapped * sin[:, None, :]


def _rope_latent(x, cos, sin):
    return jnp.concatenate([x[:, :CTX_LEN], _rope(x[:, CTX_LEN:], cos, sin)], axis=1)


def _attend_all(q, k, v, need_ctx):
    q, k, v = q.astype(BF16), k.astype(BF16), v.astype(BF16)
    o_lat = attention(q[:, :, CTX_LEN:], k, v)
    if need_ctx:
        o_ctx = attention(q[:, :, :CTX_LEN], k[:, :, :CTX_LEN], v[:, :, :CTX_LEN])
    else:
        o_ctx = jnp.zeros(q.shape[:2] + (CTX_LEN, q.shape[3]), F32)
    return jnp.concatenate([o_ctx, o_lat], axis=2)


def _gqa(p, qn_g, kn_g, cos, sin, need_ctx):
    b, s, _ = p.shape
    nq = GQA_Q_HEADS * HEAD_DIM
    nk = GQA_KV_HEADS * HEAD_DIM
    grp = GQA_Q_HEADS // GQA_KV_HEADS
    q = _rms_norm(p[..., :nq].reshape(b, s, GQA_Q_HEADS, HEAD_DIM), qn_g)
    k = _rms_norm(p[..., nq:nq + nk].reshape(b, s, GQA_KV_HEADS, HEAD_DIM), kn_g)
    v = p[..., nq + nk:nq + 2 * nk]
    q = _rope_latent(q, cos, sin) * (HEAD_DIM ** -0.5 * LOG2E)
    k = _rope_latent(k, cos, sin)
    qh = q.transpose(0, 2, 1, 3)
    zeros = jnp.zeros_like(qh)
    first = (jnp.arange(GQA_Q_HEADS) < grp)[None, :, None, None]
    q_pad = jnp.concatenate([jnp.where(first, qh, zeros), jnp.where(first, zeros, qh)], axis=-1)
    k_all = k.reshape(b, 1, s, nk)
    v_all = v.reshape(b, 1, s, nk)
    o = _attend_all(q_pad, k_all, v_all, need_ctx)
    o = jnp.where(first, o[..., :HEAD_DIM], o[..., HEAD_DIM:])
    return o.transpose(0, 2, 1, 3).reshape(b, s, nq)


def _mla(p, qa_g, w_qup, kva_g, w_kvup, cos, sin, need_ctx):
    b, s, _ = p.shape
    r0 = MLA_Q_RANK
    r1 = MLA_Q_RANK + MLA_KV_RANK
    zero = jnp.zeros((1, 1, r0), F32)
    qa = p[..., :r0].reshape(b * s, r0)
    q = norm_mod_matmul(qa, qa_g[None, :], zero, zero, w_qup.astype(BF16), lambda i: 0)
    q = q.reshape(b, s, MLA_HEADS, MLA_NOPE + MLA_ROPE)
    zero = jnp.zeros((1, 1, MLA_KV_RANK), F32)
    kva = p[..., r0:r1].reshape(b * s, MLA_KV_RANK)
    kv = norm_mod_matmul(kva, kva_g[None, :], zero, zero, w_kvup.astype(BF16), lambda i: 0)
    kv = kv.reshape(b, s, MLA_HEADS, MLA_NOPE + MLA_V)
    q_rope = _rope_latent(q[..., MLA_NOPE:], cos, sin)
    k_rope = _rope_latent(p[..., r1:r1 + MLA_ROPE][:, :, None, :], cos, sin)
    scale = (MLA_NOPE + MLA_ROPE) ** -0.5 * LOG2E
    pad = jnp.zeros((b, s, MLA_HEADS, V7X_LANES - MLA_NOPE - MLA_ROPE), F32)
    q_pad = jnp.concatenate([q[..., :MLA_NOPE] * scale, q_rope * scale, pad], axis=-1)
    k_pad = jnp.concatenate([kv[..., :MLA_NOPE], jnp.broadcast_to(k_rope, (b, s, MLA_HEADS, MLA_ROPE)), pad], axis=-1)
    v_pad = jnp.concatenate([kv[..., MLA_NOPE:], jnp.zeros((b, s, MLA_HEADS, V7X_LANES - MLA_V), F32)], axis=-1)
    tr = lambda a: a.transpose(0, 2, 1, 3)
    o = _attend_all(tr(q_pad), tr(k_pad), tr(v_pad), need_ctx)
    return o[..., :MLA_V].transpose(0, 2, 1, 3).reshape(b, s, MLA_HEADS * MLA_V)


def kernel(x, c, ctx, c_ctx, ada_w, ada_b, norm1_g, norm2_g, ev_w_in, ev_w_out, gqa_qn_g, gqa_kn_g,
           hg_lb_logits, hg_onorm_g, od_w_in, od_w_out, mla_qa_g, mla_w_qup, mla_kva_g, mla_w_kvup,
           s5_a_re, s5_a_im, s5_log_dt, s5_b_re, s5_b_im, s5_c_re, s5_c_im, s5_d, s5_w_glu, s5_b_glu,
           peer_wq, peer_keys, peer_u, peer_v, final_g):
    b, t, d = x.shape
    s = CTX_LEN + t
    depth = ada_w.shape[0]
    rows = t // GRID_W
    cos_g, sin_g = _rope_tables(rows, HEAD_DIM)
    cos_m, sin_m = _rope_tables(rows, MLA_ROPE)
    lb_all = jnp.cumsum(jax.nn.softmax(hg_lb_logits, axis=1), axis=1)

    def mod_index(tile):
        per, nctx = s // tile, CTX_LEN // tile
        return lambda i: jnp.where(i % per < nctx, b, i // per)

    mi_row = mod_index(ROW_TILE)
    mi_peer = mod_index(PEER_TILE)

    xa = jnp.concatenate([ctx, x], axis=1).reshape(b * s, d)
    s_all = jnp.concatenate([jax.nn.silu(c), jax.nn.silu(c_ctx)[None, :]], axis=0)
    s_pad = jnp.concatenate([s_all, jnp.zeros((V7X_SUBLANES - (b + 1) % V7X_SUBLANES, d), F32)], axis=0)

    for layer in range(depth):
        need_ctx = layer < depth - 1
        j = layer // 2
        mod = matmul(s_pad, ada_w[layer].astype(BF16), tm=s_pad.shape[0])[:b + 1] + ada_b[layer]
        mod = [m[:, None, :] for m in jnp.split(mod, 6, axis=-1)]
        if layer % 2 == 0:
            pa2 = norm_mod_matmul(xa, norm1_g[layer][None, :], mod[0], mod[1], ev_w_in[j].astype(BF16), mi_row)
            pa = pa2.reshape(b, s, -1)
            ya = _gqa(pa[..., :GQA_IN], gqa_qn_g[j], gqa_kn_g[j], cos_g, sin_g, need_ctx)
            yb = _hgrn2(pa2, b, lb_all[:, j], hg_onorm_g[j]).reshape(b, s, HG_W)
            w_out = ev_w_out[j]
        else:
            pa = norm_mod_matmul(xa, norm1_g[layer][None, :], mod[0], mod[1], od_w_in[j].astype(BF16), mi_row)
            pa = pa.reshape(b, s, -1)
            ya = _mla(pa[..., :MLA_IN], mla_qa_g[j], mla_w_qup[j], mla_kva_g[j], mla_w_kvup[j], cos_m, sin_m, need_ctx)
            y5 = _s5_core(pa[..., MLA_IN:], s5_a_re[j], s5_a_im[j], s5_log_dt[j], s5_b_re[j], s5_b_im[j],
                          s5_c_re[j], s5_c_im[j], s5_d[j])
            z = jax.nn.gelu(y5, approximate=False).reshape(b * s, S5_WIDTH)
            gl = matmul(z, s5_w_glu[j].astype(BF16)) + s5_b_glu[j]
            yb = (z * jax.nn.sigmoid(gl)).reshape(b, s, S5_WIDTH)
            w_out = od_w_out[j]
        y =jnp.concatenate([ya, yb], axis=-1).reshape(b * s, -1)
        xa = matmul_residual(y, w_out.astype(BF16), xa, mod[2], mi_row)
        out = peer(xa, norm2_g[layer][None, :], mod[3], mod[4],
                   peer_wq[layer], peer_keys[layer], peer_u[layer], peer_v[layer], mi_peer)
        gate = jnp.concatenate([jnp.broadcast_to(mod[5][b:b + 1], (b, CTX_LEN, d)),
                                jnp.broadcast_to(mod[5][:b], (b, t, d))], axis=1).reshape(b * s, d)
        xa = xa + gate * out
    xl = xa.reshape(b, s, d)[:, CTX_LEN:].reshape(b * t, d)
    return rmsnorm_rows(xl, final_g[None, :]).reshape(b, t, d)
```

```python
import functools
import math

import jax
import jax.numpy as jnp
from jax import lax
from jax.experimental import pallas as pl
from jax.experimental.pallas import tpu as pltpu

F32 = jnp.float32
BF16 = jnp.bfloat16

D_MODEL = 1024
GRID_W = 64
CTX_LEN = 256
EPS = 1e-6
ROPE_THETA = 10000.0

MIX_HALF = D_MODEL // 2
HEAD_DIM = 64
GQA_Q_HEADS = MIX_HALF // HEAD_DIM
GQA_KV_HEADS = GQA_Q_HEADS // 4
GQA_IN = (GQA_Q_HEADS + 2 * GQA_KV_HEADS) * HEAD_DIM

HG_DK = 128
HG_DV = 128
HG_HEADS = MIX_HALF // HG_DV
HG_W = HG_HEADS * HG_DK
GLA_CHUNK = 64

MLA_HEADS = MIX_HALF // HEAD_DIM
MLA_NOPE = 64
MLA_ROPE = 32
MLA_V = 64
MLA_Q_RANK = 384
MLA_KV_RANK = 256
MLA_IN = MLA_Q_RANK + MLA_KV_RANK + MLA_ROPE

S5_WIDTH = MIX_HALF
S5_GROUP = 16
S5_GROUPS = S5_WIDTH // S5_GROUP
S5_STATE = 64

PEER_HEADS = 8
PEER_KEYS = 128
N_EXPERTS = PEER_KEYS * PEER_KEYS
PEER_TOPK = 16
PEER_QDIM = 256
PEER_HALF = PEER_QDIM // 2
PEER_PAIRS = PEER_HEADS * PEER_TOPK

V7X_LANES = 128
V7X_SUBLANES = 8
V7X_VMEM_BYTES = 64 * 1024 * 1024
VMEM_LIMIT = V7X_VMEM_BYTES - 8 * 1024 * 1024

ROW_TILE = 256
PEER_TILE = 128
ATTN_Q_TILE = 512
EXPERT_WORDS = D_MODEL // 2
EXPERT_ROWS = EXPERT_WORDS // V7X_LANES


def _cparams(*sem):
    return pltpu.CompilerParams(dimension_semantics=sem, vmem_limit_bytes=VMEM_LIMIT)


def _norm_mod(x, g, shift, scale):
    ms = jnp.mean(x * x, axis=-1, keepdims=True)
    h = (x * lax.rsqrt(ms + EPS)) * g
    return h * (1.0 + scale) + shift


def _nmm_kernel(x_ref, g_ref, sh_ref, sc_ref, w_ref, o_ref):
    h = _norm_mod(x_ref[...], g_ref[...], sh_ref[0], sc_ref[0])
    o_ref[...] = jnp.dot(h.astype(BF16), w_ref[...], preferred_element_type=F32)


def norm_mod_matmul(x, g, shift, scale, w, mod_index, tm=ROW_TILE):
    r, k = x.shape
    n = w.shape[1]
    assert r % tm == 0 and w.shape[0] == k
    return pl.pallas_call(
        _nmm_kernel,
        grid=(r // tm,),
        in_specs=[
            pl.BlockSpec((tm, k), lambda i: (i, 0)),
            pl.BlockSpec((1, k), lambda i: (0, 0)),
            pl.BlockSpec((1, 1, k), lambda i: (mod_index(i), 0, 0)),
            pl.BlockSpec((1, 1, k), lambda i: (mod_index(i), 0, 0)),
            pl.BlockSpec((k, n), lambda i: (0, 0)),
        ],
        out_specs=pl.BlockSpec((tm, n), lambda i: (i, 0)),
        out_shape=jax.ShapeDtypeStruct((r, n), F32),
        compiler_params=_cparams("parallel"),
        name="norm_mod_matmul",
    )(x, g, shift, scale, w)


def _mm_kernel(a_ref, w_ref, o_ref):
    o_ref[...] = jnp.dot(a_ref[...].astype(BF16), w_ref[...], preferred_element_type=F32)


def matmul(a, w, tm=ROW_TILE):
    r, k = a.shape
    n = w.shape[1]
    assert r % tm == 0
    return pl.pallas_call(
        _mm_kernel,
        grid=(r // tm,),
        in_specs=[pl.BlockSpec((tm, k), lambda i: (i, 0)), pl.BlockSpec((k, n), lambda i: (0, 0))],
        out_specs=pl.BlockSpec((tm, n), lambda i: (i, 0)),
        out_shape=jax.ShapeDtypeStruct((r, n), F32),
        compiler_params=_cparams("parallel"),
        name="matmul",
    )(a, w)


def _mmres_kernel(a_ref, w_ref, x_ref, gate_ref, o_ref):
    y = jnp.dot(a_ref[...].astype(BF16), w_ref[...], preferred_element_type=F32)
    o_ref[...] = x_ref[...] + gate_ref[0] * y


def matmul_residual(a, w, x, gate, mod_index, tm=ROW_TILE):
    r, k = a.shape
    n = w.shape[1]
    assert r % tm == 0
    return pl.pallas_call(
        _mmres_kernel,
        grid=(r // tm,),
        in_specs=[
            pl.BlockSpec((tm, k), lambda i: (i, 0)),
            pl.BlockSpec((k, n), lambda i: (0, 0)),
            pl.BlockSpec((tm, n), lambda i: (i, 0)),
            pl.BlockSpec((1, 1, n), lambda i: (mod_index(i), 0, 0)),
        ],
        out_specs=pl.BlockSpec((tm, n), lambda i: (i, 0)),
        out_shape=jax.ShapeDtypeStruct((r, n), F32),
        compiler_params=_cparams("parallel"),
        name="matmul_residual",
    )(a, w, x, gate)


def _rms_kernel(x_ref, g_ref, o_ref):
    x = x_ref[...]
    ms = jnp.mean(x * x, axis=-1, keepdims=True)
    o_ref[...] = (x * lax.rsqrt(ms + EPS)) * g_ref[...]


def rmsnorm_rows(x, g, tm=ROW_TILE):
    r, k = x.shape
    return pl.pallas_call(
        _rms_kernel,
        grid=(r // tm,),
        in_specs=[pl.BlockSpec((tm, k), lambda i: (i, 0)), pl.BlockSpec((1, k), lambda i: (0, 0))],
        out_specs=pl.BlockSpec((tm, k), lambda i: (i, 0)),
        out_shape=jax.ShapeDtypeStruct((r, k), F32),
        compiler_params=_cparams("parallel"),
        name="rmsnorm_rows",
    )(x, g)


ATTN_KV_CHUNKS = 3
V7X_MXU_DIM = 256
LOG2E = math.log2(math.e)


def _attn_kernel(q_ref, k_ref, v_ref, o_ref):
    q = q_ref[...]
    sk = k_ref.shape[0]
    nch = ATTN_KV_CHUNKS if sk % (ATTN_KV_CHUNKS * V7X_MXU_DIM) == 0 else 1
    ck = sk // nch
    m = l = acc = None
    for c in range(nch):
        kc = k_ref[c * ck:(c + 1) * ck, :]
        vc = v_ref[c * ck:(c + 1) * ck, :]
        s = lax.dot_general(q, kc, (((1,), (1,)), ((), ())), preferred_element_type=F32)
        mc = jnp.max(s, axis=-1, keepdims=True)
        if c == 0:
            m = mc
            p = jnp.exp2(s - m)
            l = jnp.sum(p, axis=-1, keepdims=True)
            acc = jnp.dot(p.astype(BF16), vc, preferred_element_type=F32)
        else:
            m_new = jnp.maximum(m, mc)
            a = jnp.exp2(m - m_new)
            p = jnp.exp2(s - m_new)
            l = a * l + jnp.sum(p, axis=-1, keepdims=True)
            acc = a * acc + jnp.dot(p.astype(BF16), vc, preferred_element_type=F32)
            m = m_new
    o_ref[...] = acc / l


def attention(q, k, v, tq=ATTN_Q_TILE):
    b, sq, hl = q.shape
    sk, hkl = k.shape[1], k.shape[2]
    h, hk = hl // V7X_LANES, hkl // V7X_LANES
    grp = h // hk
    tq = min(tq, sq)
    assert sq % tq == 0
    return pl.pallas_call(
        _attn_kernel,
        grid=(b, h, sq // tq),
        in_specs=[
            pl.BlockSpec((None, tq, V7X_LANES), lambda bi, hi, qi: (bi, qi, hi)),
            pl.BlockSpec((None, sk, V7X_LANES), lambda bi, hi, qi: (bi, 0, hi // grp)),
            pl.BlockSpec((None, sk, V7X_LANES), lambda bi, hi, qi: (bi, 0, hi // grp)),
        ],
        out_specs=pl.BlockSpec((None, tq, V7X_LANES), lambda bi, hi, qi: (bi, qi, hi)),
        out_shape=jax.ShapeDtypeStruct((b, sq, hl), F32),
        compiler_params=_cparams("parallel", "parallel", "parallel"),
        name="attention",
    )(q, k, v)


def _rotate(x, cos, sin, quarter):
    n = x.shape[1]
    lane = lax.broadcasted_iota(jnp.int32, x.shape, 1)
    partner = jnp.where(lane % (2 * quarter) < quarter, pltpu.roll(x, n - quarter, axis=1),
                        pltpu.roll(x, quarter, axis=1))
    return x * cos + partner * sin


def _gqa_prep_kernel(p_ref, g_ref, cos_ref, sin_ref, avg_ref, q_ref, k_ref, v_ref):
    cos, sin = cos_ref[...], sin_ref[...]
    low = lax.broadcasted_iota(jnp.int32, cos.shape, 1) < HEAD_DIM
    qk_blocks = (GQA_Q_HEADS + GQA_KV_HEADS) * HEAD_DIM // V7X_LANES
    q_blocks = GQA_Q_HEADS * HEAD_DIM // V7X_LANES
    for c in range(qk_blocks):
        lanes = slice(c * V7X_LANES, (c + 1) * V7X_LANES)
        x = p_ref[:, lanes]
        ms = jnp.dot(x * x, avg_ref[...], precision=lax.Precision.HIGHEST, preferred_element_type=F32)
        y = _rotate(x * lax.rsqrt(ms + EPS) * g_ref[:, lanes], cos, sin, HEAD_DIM // 4)
        if c < q_blocks:
            y = y * (HEAD_DIM ** -0.5 * LOG2E)
            other = pltpu.roll(y, HEAD_DIM, axis=1)
            if c < q_blocks // 2:
                first, second = jnp.where(low, y, 0.0), jnp.where(low, other, 0.0)
            else:
                first, second = jnp.where(low, 0.0, other), jnp.where(low, 0.0, y)
            q_ref[:, 2 * c * V7X_LANES:(2 * c + 1) * V7X_LANES] = first.astype(BF16)
            q_ref[:, (2 * c + 1) * V7X_LANES:(2 * c + 2) * V7X_LANES] = second.astype(BF16)
        else:
            k_ref[...] = y.astype(BF16)
    v_ref[...] = p_ref[:, qk_blocks * V7X_LANES:(qk_blocks + 1) * V7X_LANES].astype(BF16)


def gqa_prep(pa, gains, cos, sin, seq, tm=ROW_TILE):
    r = pa.shape[0]
    per = seq // tm
    avg = jnp.kron(jnp.eye(V7X_LANES // HEAD_DIM, dtype=F32), jnp.full((HEAD_DIM, HEAD_DIM), 1.0 / HEAD_DIM, F32))
    tab = pl.BlockSpec((tm, V7X_LANES), lambda i: (i % per, 0))
    kv = pl.BlockSpec((tm, V7X_LANES), lambda i: (i, 0))
    return pl.pallas_call(
        _gqa_prep_kernel,
        grid=(r // tm,),
        in_specs=[pl.BlockSpec((tm, GQA_IN), lambda i: (i, 0)), pl.BlockSpec(gains.shape, lambda i: (0, 0)), tab, tab,
                  pl.BlockSpec(avg.shape, lambda i: (0, 0))],
        out_specs=[pl.BlockSpec((tm, GQA_Q_HEADS * V7X_LANES), lambda i: (i, 0)), kv, kv],
        out_shape=[jax.ShapeDtypeStruct((r, GQA_Q_HEADS * V7X_LANES), BF16),
                   jax.ShapeDtypeStruct((r, V7X_LANES), BF16), jax.ShapeDtypeStruct((r, V7X_LANES), BF16)],
        compiler_params=_cparams("parallel"),
        name="gqa_prep",
    )(pa, gains, cos, sin, avg)


def _mla_prep_kernel(q_ref, kv_ref, kr_ref, cos_ref, sin_ref, qo_ref, ko_ref, vo_ref):
    cos, sin = cos_ref[...], sin_ref[...]
    quarter = MLA_ROPE // 4
    kr = _rotate(kr_ref[...], cos, sin, quarter)
    scale = (MLA_NOPE + MLA_ROPE) ** -0.5 * LOG2E
    width = MLA_HEADS * V7X_LANES
    for h in range(MLA_HEADS):
        lanes = slice(h * V7X_LANES, (h + 1) * V7X_LANES)
        qo_ref[:, lanes] = (_rotate(q_ref[:, lanes], cos, sin, quarter) * scale).astype(BF16)
        ko_ref[:, lanes] = (kv_ref[:, lanes] + kr).astype(BF16)
        vo_ref[:, lanes] = kv_ref[:, width + h * V7X_LANES:width + (h + 1) * V7X_LANES].astype(BF16)


def mla_prep(q, kv, kr, cos, sin, seq, tm=ROW_TILE):
    r, width = q.shape
    per = seq // tm
    tab = pl.BlockSpec((tm, V7X_LANES), lambda i: (i % per, 0))
    out = pl.BlockSpec((tm, width), lambda i: (i, 0))
    return pl.pallas_call(
        _mla_prep_kernel,
        grid=(r // tm,),
        in_specs=[out, pl.BlockSpec((tm, 2 * width), lambda i: (i, 0)), pl.BlockSpec((tm, V7X_LANES), lambda i: (i, 0)),
                  tab, tab],
        out_specs=[out, out, out],
        out_shape=[jax.ShapeDtypeStruct((r, width), BF16)] * 3,
        compiler_params=_cparams("parallel"),
        name="mla_prep",
    )(q, kv, kr, cos, sin)


def _topk_rows(s, iota, k):
    n = s.shape[0]
    row = lax.broadcasted_iota(jnp.int32, (k, s.shape[1]), 0)
    vals = jnp.zeros((k, s.shape[1]), F32)
    ids = jnp.zeros((k, s.shape[1]), F32)
    for r in range(k):
        m = jnp.max(s, axis=0, keepdims=True)
        am = jnp.min(jnp.where(s == m, iota, float(n)), axis=0, keepdims=True)
        vals = jnp.where(row == r, m, vals)
        ids = jnp.where(row == r, am, ids)
        s = jnp.where(iota == am, -jnp.inf, s)
    return vals, ids


def _peer_ret_kernel(x_ref, g_ref, sh_ref, sc_ref, wq_ref, keys_ref, h_ref, eidx_ref, gw_ref, hb_ref, e_scr, w_scr):
    h = _norm_mod(x_ref[...], g_ref[...], sh_ref[0], sc_ref[0])
    h_ref[...] = h
    hb_ref[...] = h.astype(BF16)
    tb = x_ref.shape[0]
    iota_n = lax.broadcasted_iota(jnp.int32, (PEER_KEYS, tb), 0).astype(F32)
    half = PEER_TOPK // 2
    sub = lambda n: lax.broadcasted_iota(jnp.int32, (n, tb), 0).astype(F32)
    pos_c = jnp.concatenate([sub(PEER_TOPK)] + [sub(half) + float(a * PEER_TOPK) for a in range(1, half)]
                            + [(sub(half) + float(half)) * float(PEER_TOPK)], axis=0)

    def pair_up(first, second, scale):
        return jnp.concatenate([first[0:1] * scale + second]
                               + [first[a:a + 1] * scale + second[0:half] for a in range(1, half)]
                               + [first[half:] * scale + second[0:1]], axis=0)

    def head(hd, carry):
        q = jnp.dot(hb_ref[...], wq_ref[hd], preferred_element_type=F32)
        tops = []
        for p in range(2):
            qp = q[:, p * PEER_HALF:(p + 1) * PEER_HALF].astype(BF16)
            s = lax.dot_general(keys_ref[hd, p], qp, (((1,), (1,)), ((), ())),
                                preferred_element_type=F32)
            tops.append(_topk_rows(s, iota_n, PEER_TOPK))
        (s1, i1), (s2, i2) = tops
        cand = pair_up(s1, s2, 1.0)
        cidx = pair_up(i1, i2, float(PEER_KEYS))
        row = lax.broadcasted_iota(jnp.int32, (PEER_TOPK, tb), 0)
        sc = jnp.zeros((PEER_TOPK, tb), F32)
        ex = jnp.zeros((PEER_TOPK, tb), F32)
        for r in range(PEER_TOPK):
            m = jnp.max(cand, axis=0, keepdims=True)
            am = jnp.min(jnp.where(cand == m, pos_c, float(PEER_TOPK * PEER_TOPK)), axis=0, keepdims=True)
            hit = pos_c == am
            e = jnp.max(jnp.where(hit, cidx, 0.0), axis=0, keepdims=True)
            sc = jnp.where(row == r, m, sc)
            ex = jnp.where(row == r, e, ex)
            cand = jnp.where(hit, -jnp.inf, cand)
        pexp = jnp.exp(sc - sc[0:1])
        rows = pl.ds(pl.multiple_of(hd * PEER_TOPK, PEER_TOPK), PEER_TOPK)
        w_scr[rows, :] = pexp / jnp.sum(pexp, axis=0, keepdims=True)
        e_scr[rows, :] = ex
        return carry

    lax.fori_loop(0, PEER_HEADS, head, 0)
    gw_ref[...] = w_scr[...].T
    eidx_ref[...] = (e_scr[...].T * float(EXPERT_ROWS)).astype(jnp.int32)


def peer_retrieve(x, g, shift, scale, wq_heads, keys, mod_index, tb=PEER_TILE):
    r, d = x.shape
    nb = r // tb
    return pl.pallas_call(
        _peer_ret_kernel,
        grid=(nb,),
        in_specs=[
            pl.BlockSpec((tb, d), lambda i: (i, 0)),
            pl.BlockSpec((1, d), lambda i: (0, 0)),
            pl.BlockSpec((1, 1, d), lambda i: (mod_index(i), 0, 0)),
            pl.BlockSpec((1, 1, d), lambda i: (mod_index(i), 0, 0)),
            pl.BlockSpec((PEER_HEADS, d, PEER_QDIM), lambda i: (0, 0, 0)),
            pl.BlockSpec((PEER_HEADS, 2, PEER_KEYS, PEER_HALF), lambda i: (0, 0, 0, 0)),
        ],
        out_specs=[
            pl.BlockSpec((tb, d), lambda i: (i, 0)),
            pl.BlockSpec((tb, PEER_PAIRS), lambda i: (i, 0)),
            pl.BlockSpec((tb, PEER_PAIRS), lambda i: (i, 0)),
        ],
        out_shape=[
            jax.ShapeDtypeStruct((r, d), F32),
            jax.ShapeDtypeStruct((r, PEER_PAIRS), jnp.int32),
            jax.ShapeDtypeStruct((r, PEER_PAIRS), F32),
        ],
        scratch_shapes=[pltpu.VMEM((tb, d), BF16), pltpu.VMEM((PEER_PAIRS, tb), F32),
                        pltpu.VMEM((PEER_PAIRS, tb), F32)],
        compiler_params=_cparams("parallel"),
        name="peer_retrieve",
    )(x, g, shift, scale, wq_heads, keys)


def pack_rows_bf16(tab):
    e, d = tab.shape
    t16 = lax.bitcast_convert_type(tab.astype(BF16), jnp.uint16).astype(jnp.uint32)
    word = t16[:, :d // 2] | (t16[:, d // 2:] << 16)
    return lax.bitcast_convert_type(word, jnp.int32).reshape(e * EXPERT_ROWS, V7X_LANES)


def _slab_value_row(r):
    return r // 2 + EXPERT_ROWS * (r % 2)


PEER_GROUP = V7X_SUBLANES
SLAB_ROWS_BF16 = 2 * EXPERT_ROWS
PAIR_LANES = PEER_PAIRS * SLAB_ROWS_BF16


def _slab(tab_ref, row):
    return tab_ref[pl.ds(pl.multiple_of(row, EXPERT_ROWS), EXPERT_ROWS), :]


def _token_rows_bf16(h8, i):
    row = lax.broadcasted_iota(jnp.int32, (SLAB_ROWS_BF16, V7X_LANES), 0)
    hq = jnp.zeros((SLAB_ROWS_BF16, V7X_LANES), F32)
    for r in range(SLAB_ROWS_BF16):
        v = _slab_value_row(r)
        hq = jnp.where(row == r, h8[i:i + 1, v * V7X_LANES:(v + 1) * V7X_LANES], hq)
    return hq.astype(BF16)


def _peer_u_kernel(idx_ref, tab_ref, h_ref, gw_ref, fold_ref, o_ref, s_ref):
    groups = o_ref.shape[0] // PEER_GROUP
    row = lax.broadcasted_iota(jnp.int32, (PEER_GROUP, V7X_LANES), 0)

    def finish(g, sums):
        hi = sums.astype(BF16)
        lo = (sums - hi.astype(F32)).astype(BF16)
        a = (jnp.dot(hi, fold_ref[...], preferred_element_type=F32)
             + jnp.dot(lo, fold_ref[...], preferred_element_type=F32))
        rows = pl.ds(pl.multiple_of(g * PEER_GROUP, PEER_GROUP), PEER_GROUP)
        gelu = 0.5 * a * (1.0 + lax.erf(a * (2.0 ** -0.5)))
        o_ref[rows, :] = gelu * gw_ref[rows, :]

    def group(g, prev):
        finish(jnp.maximum(g - 1, 0), prev)
        sums = jnp.zeros((PEER_GROUP, PAIR_LANES), F32)
        h8 = h_ref[pl.ds(pl.multiple_of(g * PEER_GROUP, PEER_GROUP), PEER_GROUP), :]
        for i in range(PEER_GROUP):
            t = g * PEER_GROUP + i
            hb = _token_rows_bf16(h8, i)
            tok_idx = idx_ref.at[pl.ds(t * PEER_PAIRS, PEER_PAIRS)]
            for j in range(PEER_PAIRS):
                u = pltpu.bitcast(_slab(tab_ref, tok_idx[j]), BF16)
                s_ref[pl.ds((i * PEER_PAIRS + j) * EXPERT_ROWS, EXPERT_ROWS), :] = pltpu.bitcast(u * hb, jnp.int32)
            prod = pltpu.bitcast(s_ref[pl.ds(i * PEER_PAIRS * EXPERT_ROWS, PEER_PAIRS * EXPERT_ROWS), :], BF16)
            pick = jnp.where(row == i, 1.0, 0.0).astype(BF16)
            sums = sums + lax.dot_general(pick, prod, (((1,), (1,)), ((), ())), preferred_element_type=F32)
        return sums

    last = lax.fori_loop(0, groups, group, jnp.zeros((PEER_GROUP, PAIR_LANES), F32))
    finish(groups - 1, last)


def peer_expert_gates(idx4, tab, h, gw, tb=PEER_TILE):
    r = h.shape[0]
    fold = jnp.repeat(jnp.eye(PEER_PAIRS, dtype=BF16), SLAB_ROWS_BF16, axis=0)
    return pl.pallas_call(
        _peer_u_kernel,
        grid=(r // tb,),
        in_specs=[
            pl.BlockSpec((tb * PEER_PAIRS,), lambda i: (i,), memory_space=pltpu.SMEM),
            pl.BlockSpec(tab.shape, lambda i: (0, 0), pipeline_mode=pl.Buffered(1)),
            pl.BlockSpec((tb, h.shape[1]), lambda i: (i, 0)),
            pl.BlockSpec((tb, PEER_PAIRS), lambda i: (i, 0)),
            pl.BlockSpec((PAIR_LANES, PEER_PAIRS), lambda i: (0, 0)),
        ],
        out_specs=pl.BlockSpec((tb, PEER_PAIRS), lambda i: (i, 0)),
        out_shape=jax.ShapeDtypeStruct((r, PEER_PAIRS), F32),
        scratch_shapes=[pltpu.VMEM((PEER_GROUP * PEER_PAIRS * EXPERT_ROWS, V7X_LANES), jnp.int32)],
        compiler_params=_cparams("arbitrary"),
        name="peer_expert_gates",
    )(idx4, tab, h, gw, fold)


def _peer_v_kernel(idx_ref, w_ref, tab_ref, spread_ref, o_ref, g_ref):
    groups = w_ref.shape[0] // PEER_GROUP
    last_slot = PEER_GROUP - 1
    shape = (SLAB_ROWS_BF16, PAIR_LANES)
    out_row = lax.broadcasted_iota(jnp.int32, shape, 0)
    diag = (lax.broadcasted_iota(jnp.int32, shape, 1) % SLAB_ROWS_BF16) == 2 * (out_row % EXPERT_ROWS) + out_row // EXPERT_ROWS
    slot_rows = PEER_PAIRS * EXPERT_ROWS

    @pl.when(pl.program_id(0) == 0)
    def _():
        g_ref[pl.ds(last_slot * slot_rows, slot_rows), :] = jnp.zeros((slot_rows, V7X_LANES), jnp.int32)

    def contract(slot, wrow, t):
        rows = pltpu.bitcast(g_ref[pl.ds(slot * slot_rows, slot_rows), :], BF16)
        wi = jnp.where(diag, jnp.broadcast_to(wrow, shape), 0.0).astype(BF16)
        o_ref[pl.ds(pl.multiple_of(t * SLAB_ROWS_BF16, SLAB_ROWS_BF16), SLAB_ROWS_BF16), :] = jnp.dot(
            wi, rows, preferred_element_type=F32)

    def group(g, prev_wide):
        contract(last_slot, prev_wide[last_slot:], jnp.maximum(g * PEER_GROUP - 1, 0))
        w8 = w_ref[pl.ds(pl.multiple_of(g * PEER_GROUP, PEER_GROUP), PEER_GROUP), :]
        wide = jnp.dot(w8.astype(BF16), spread_ref[...], preferred_element_type=F32)
        for i in range(PEER_GROUP):
            t = g * PEER_GROUP + i
            tok_idx = idx_ref.at[pl.ds(t * PEER_PAIRS, PEER_PAIRS)]
            for j in range(PEER_PAIRS):
                g_ref[pl.ds((i * PEER_PAIRS + j) * EXPERT_ROWS, EXPERT_ROWS), :] = _slab(tab_ref, tok_idx[j])
            if i < last_slot:
                contract(i, wide[i:i + 1], t)
        return wide

    wide = lax.fori_loop(0, groups, group, jnp.zeros((PEER_GROUP, PAIR_LANES), F32))
    contract(last_slot, wide[last_slot:], groups * PEER_GROUP - 1)


def peer_expert_mix(idx4, w, tab, tb=PEER_TILE):
    r = w.shape[0]
    spread = jnp.repeat(jnp.eye(PEER_PAIRS, dtype=BF16), SLAB_ROWS_BF16, axis=1)
    return pl.pallas_call(
        _peer_v_kernel,
        grid=(r // tb,),
        in_specs=[
            pl.BlockSpec((tb * PEER_PAIRS,), lambda i: (i,), memory_space=pltpu.SMEM),
            pl.BlockSpec((tb, PEER_PAIRS), lambda i: (i, 0)),
            pl.BlockSpec(tab.shape, lambda i: (0, 0), pipeline_mode=pl.Buffered(1)),
            pl.BlockSpec((PEER_PAIRS, PAIR_LANES), lambda i: (0, 0)),
        ],
        out_specs=pl.BlockSpec((tb * SLAB_ROWS_BF16, V7X_LANES), lambda i: (i, 0)),
        out_shape=jax.ShapeDtypeStruct((r * SLAB_ROWS_BF16, V7X_LANES), F32),
        scratch_shapes=[pltpu.VMEM((PEER_GROUP * PEER_PAIRS * EXPERT_ROWS, V7X_LANES), jnp.int32)],
        compiler_params=_cparams("arbitrary"),
        name="peer_expert_mix",
    )(idx4, w, tab, spread)


def peer(x, g, shift, scale, wq, keys, u_tab, v_tab, mod_index_peer):
    r, d = x.shape
    wq_heads = wq.astype(BF16).reshape(d, PEER_HEADS, PEER_QDIM).transpose(1, 0, 2)
    h, idx4, gw = peer_retrieve(x, g, shift, scale, wq_heads, keys.astype(BF16), mod_index_peer)
    idx4 = idx4.reshape(r * PEER_PAIRS)
    w = peer_expert_gates(idx4, pack_rows_bf16(u_tab), h, gw)
    return peer_expert_mix(idx4, w, pack_rows_bf16(v_tab)).reshape(r, d)


HG_TILE = 128
HG_SUB = 16
HG_PAIR = 2 * HG_DK


def _hgrn_kernel(q_ref, f_ref, v_ref, lb_ref, o_ref, st_ref, *, rev):
    @pl.when(pl.program_id(2) == 0)
    def _():
        st_ref[...] = jnp.zeros_like(st_ref)

    n = q_ref.shape[0]
    lb = lb_ref[...]
    f = lb + (1.0 - lb) * jax.nn.sigmoid(f_ref[...])
    kk = 1.0 - f
    lf = jnp.log(f)
    t = lax.broadcasted_iota(jnp.int32, (n, n), 0)
    s = lax.broadcasted_iota(jnp.int32, (n, n), 1)
    same = (t // HG_SUB) == (s // HG_SUB)
    mid = (t // HG_SUB) * HG_SUB + (HG_SUB // 2 if rev else HG_SUB // 2 - 1)
    seen = (s >= t) if rev else (s <= t)
    seen_mid = (s >= mid) if rev else (s <= mid)
    one = lambda m: jnp.where(m, 1.0, 0.0).astype(F32)
    hp = lambda a, b: jnp.dot(a, b, precision=lax.Precision.HIGHEST, preferred_element_type=F32)
    cum = hp(one(same & seen), lf)
    ref = hp(one(same & seen_mid), lf)
    last = hp(one(same), lf)
    q = q_ref[...]
    qe = (q * jnp.exp(cum)).astype(BF16)
    qm = (q * jnp.exp(cum - ref)).astype(BF16)
    km = (kk * jnp.exp(ref - cum)).astype(BF16)
    kd = (kk * jnp.exp(last - cum)).astype(BF16)
    dec = jnp.exp(last)
    vb = v_ref[...].astype(BF16)
    ti = lax.broadcasted_iota(jnp.int32, (HG_SUB, HG_SUB), 0)
    si = lax.broadcasted_iota(jnp.int32, (HG_SUB, HG_SUB), 1)
    causal = (si >= ti) if rev else (si <= ti)
    steps = range(n // HG_SUB)
    for c in (reversed(steps) if rev else steps):
        rows = slice(c * HG_SUB, (c + 1) * HG_SUB)
        for h in range(HG_PAIR // HG_DK):
            cols = slice(h * HG_DK, (h + 1) * HG_DK)
            st = st_ref[h]
            att = lax.dot_general(qm[rows, cols], km[rows, cols], (((1,), (1,)), ((), ())),
                                  preferred_element_type=F32)
            att = jnp.where(causal, att, 0.0).astype(BF16)
            o = (lax.dot_general(qe[rows, cols], st.astype(BF16), (((1,), (1,)), ((), ())),
                                 preferred_element_type=F32)
                 + jnp.dot(att, vb[rows, cols], preferred_element_type=F32))
            o_ref[rows, cols] = o
            upd = lax.dot_general(vb[rows, cols], kd[rows, cols], (((0,), (0,)), ((), ())),
                                  preferred_element_type=F32)
            st_ref[h] = st * dec[c * HG_SUB:c * HG_SUB + 1, cols] + upd


def hgrn_scan(pa, lb, rev, batch):
    rows = pa.shape[0]
    per = rows // batch // HG_TILE
    nctx = CTX_LEN // HG_TILE
    col0 = GQA_IN // HG_PAIR
    blocks = HG_W // HG_PAIR

    def tok(b, k):
        if rev:
            k = jnp.where(k < nctx, nctx - 1 - k, per - 1 - (k - nctx))
        return b * per + k

    spec = lambda cb: pl.BlockSpec((HG_TILE, HG_PAIR), lambda b, p, k: (tok(b, k), cb + p))
    d = 1 if rev else 0
    return pl.pallas_call(
        functools.partial(_hgrn_kernel, rev=rev),
        grid=(batch, blocks, per),
        in_specs=[spec(col0), spec(col0 + (1 + d) * blocks), spec(col0 + 3 * blocks),
                  pl.BlockSpec((None, 1, HG_PAIR), lambda b, p, k: (d, 0, p))],
        out_specs=pl.BlockSpec((HG_TILE, HG_PAIR), lambda b, p, k: (tok(b, k), p)),
        out_shape=jax.ShapeDtypeStruct((rows, HG_W), F32),
        scratch_shapes=[pltpu.VMEM((HG_PAIR // HG_DK, HG_DV, HG_DK), F32)],
        compiler_params=_cparams("parallel", "parallel", "arbitrary"),
        name="hgrn_scan_rev" if rev else "hgrn_scan_fwd",
    )(pa, pa, pa, lb)


def _rms_norm(x, g):
    xf = x.astype(F32)
    y = xf * lax.rsqrt(jnp.mean(xf * xf, axis=-1, keepdims=True) + EPS)
    return y * g.astype(F32)


def _hgrn2(pa2, batch, lb, onorm_g):
    lb = lb.reshape(2, 1, HG_W)
    o = hgrn_scan(pa2, lb, False, batch) + hgrn_scan(pa2, lb, True, batch)
    o = _rms_norm(o.reshape(-1, HG_HEADS, HG_DV), onorm_g).reshape(-1, HG_W)
    return o * jax.nn.silu(pa2[:, GQA_IN + 4 * HG_W:])


S5_CHUNK = 64
S5_LANES = S5_GROUPS * S5_STATE


def _s5_kernel(u_ref, bre_ref, bim_ref, cre_ref, cim_ref, are_ref, aim_ref, y_ref, xre, xim, hre, him):
    nb = V7X_SUBLANES // 2
    tiles = u_ref.shape[0] // V7X_SUBLANES
    direction = pl.program_id(0)
    rev = direction == 1

    @pl.when(pl.program_id(1) == 0)
    def _():
        hre[...] = jnp.zeros_like(hre)
        him[...] = jnp.zeros_like(him)

    ub = u_ref[...].astype(BF16)
    xre[...] = jnp.dot(ub, bre_ref[...], preferred_element_type=F32)
    xim[...] = jnp.dot(ub, bim_ref[...], preferred_element_type=F32)
    shape = (V7X_SUBLANES, S5_LANES)
    second = (lax.broadcasted_iota(jnp.int32, shape, 0) // nb) != direction
    are = jnp.broadcast_to(are_ref[...], shape)
    aim = jnp.broadcast_to(aim_ref[...], shape)
    cre = jnp.where(second, are * are - aim * aim, are)
    cim = jnp.where(second, 2.0 * are * aim, aim)

    def step(k, carry):
        pr, pi = carry
        k = jnp.where(rev, tiles - 1 - k, k)
        r = pl.ds(pl.multiple_of(k * V7X_SUBLANES, V7X_SUBLANES), V7X_SUBLANES)
        xr, xi = xre[r, :], xim[r, :]
        sr = jnp.where(second, pltpu.roll(xr, nb, axis=0), 0.0)
        si = jnp.where(second, pltpu.roll(xi, nb, axis=0), 0.0)
        nr = xr + (are * sr - aim * si) + (cre * pr - cim * pi)
        ni = xi + (are * si + aim * sr) + (cre * pi + cim * pr)
        xre[r, :] = nr
        xim[r, :] = ni
        return (jnp.where(second, nr, pltpu.roll(nr, nb, axis=0)),
                jnp.where(second, ni, pltpu.roll(ni, nb, axis=0)))

    hr, hi = lax.fori_loop(0, tiles, step, (hre[...], him[...]))
    hre[...] = hr
    him[...] = hi
    y_ref[...] = (jnp.dot(xre[...].astype(BF16), cre_ref[...], preferred_element_type=F32)
                  - jnp.dot(xim[...].astype(BF16), cim_ref[...], preferred_element_type=F32))


def s5_scan(u, bre, bim, cre, cim, are, aim, nb):
    rows, w = u.shape
    blk = S5_CHUNK * nb
    assert rows % blk == 0 and 2 * nb == V7X_SUBLANES
    nblk, nctx = rows // blk, CTX_LEN // S5_CHUNK
    wspec = lambda shape: pl.BlockSpec((None,) + shape, lambda d, i: (d, 0, 0))

    def chunk(d, i):
        back = jnp.where(i < nctx, nctx - 1 - i, nblk - 1 - (i - nctx))
        return jnp.where(d == 1, back, i)
    return pl.pallas_call(
        _s5_kernel,
        grid=(2, rows // blk),
        in_specs=[
            pl.BlockSpec((blk, w), lambda d, i: (chunk(d, i), 0)),
            wspec((w, S5_LANES)), wspec((w, S5_LANES)), wspec((S5_LANES, w)), wspec((S5_LANES, w)),
            wspec((1, S5_LANES)), wspec((1, S5_LANES)),
        ],
        out_specs=pl.BlockSpec((None, blk, w), lambda d, i: (d, chunk(d, i), 0)),
        out_shape=jax.ShapeDtypeStruct((2, rows, w), F32),
        scratch_shapes=[pltpu.VMEM((blk, S5_LANES), F32), pltpu.VMEM((blk, S5_LANES), F32),
                        pltpu.VMEM((V7X_SUBLANES, S5_LANES), F32), pltpu.VMEM((V7X_SUBLANES, S5_LANES), F32)],
        compiler_params=_cparams("arbitrary", "arbitrary"),
        name="s5_scan",
    )(u, bre, bim, cre, cim, are, aim)


def _s5_core(u, a_re, a_im, log_dt, b_re, b_im, c_re, c_im, d_skip):
    b, s, w = u.shape
    dt = jnp.exp(log_dt)[..., None]
    mag = jnp.exp(a_re * dt)
    abar_re, abar_im = mag * jnp.cos(a_im * dt), mag * jnp.sin(a_im * dt)
    den = a_re * a_re + a_im * a_im
    k_re = ((abar_re - 1.0) * a_re + abar_im * a_im) / den
    k_im = (abar_im * a_re - (abar_re - 1.0) * a_im) / den
    bb_re = k_re[..., None] * b_re - k_im[..., None] * b_im
    bb_im = k_re[..., None] * b_im + k_im[..., None] * b_re
    eye = jnp.eye(S5_GROUPS, dtype=F32)
    bd_in = lambda m: jnp.einsum('dgpc,gh->dgchp', m, eye).reshape(2, w, S5_LANES).astype(BF16)
    bd_out = lambda m: jnp.einsum('dgcp,gh->dgphc', m, eye).reshape(2, S5_LANES, w).astype(BF16)
    y2 = s5_scan(u.transpose(1, 0, 2).reshape(s * b, w), bd_in(bb_re), bd_in(bb_im), bd_out(c_re), bd_out(c_im),
                 abar_re.reshape(2, 1, S5_LANES), abar_im.reshape(2, 1, S5_LANES), b)
    return d_skip * u + (y2[0] + y2[1]).reshape(s, b, w).transpose(1, 0, 2)


def _rope_tables(rows, rot_dim, lead, period):
    axis_dim = rot_dim // 2
    inv = ROPE_THETA ** (-jnp.arange(0, axis_dim, 2, dtype=F32) / axis_dim)
    t = jnp.arange(rows * GRID_W)
    r = (t // GRID_W).astype(F32)[:, None] * inv
    c = (t % GRID_W).astype(F32)[:, None] * inv
    n = t.shape[0]
    tail = period - lead - rot_dim
    cos = jnp.concatenate([jnp.ones((n, lead), F32), jnp.cos(r), jnp.cos(r), jnp.cos(c), jnp.cos(c),
                           jnp.ones((n, tail), F32)], axis=-1)
    sin = jnp.concatenate([jnp.zeros((n, lead), F32), -jnp.sin(r), jnp.sin(r), -jnp.sin(c), jnp.sin(c),
                           jnp.zeros((n, tail), F32)], axis=-1)
    reps = V7X_LANES // period
    cos = jnp.concatenate([jnp.ones((CTX_LEN, V7X_LANES), F32), jnp.tile(cos, (1, reps))], axis=0)
    sin = jnp.concatenate([jnp.zeros((CTX_LEN, V7X_LANES), F32), jnp.tile(sin, (1, reps))], axis=0)
    return cos, sin


def _attend_all(q, k, v, need_ctx):
    o_lat = attention(q[:, CTX_LEN:], k, v)
    if need_ctx:
        o_ctx = attention(q[:, :CTX_LEN], k[:, :CTX_LEN], v[:, :CTX_LEN])
    else:
        o_ctx = jnp.zeros((q.shape[0], CTX_LEN, q.shape[2]), F32)
    return jnp.concatenate([o_ctx, o_lat], axis=1)


def _gqa(pa2, b, qn_g, kn_g, cos, sin, need_ctx):
    s = pa2.shape[0] // b
    grp = GQA_Q_HEADS // GQA_KV_HEADS
    gains = jnp.concatenate([jnp.tile(qn_g, GQA_Q_HEADS), jnp.tile(kn_g, GQA_KV_HEADS)])[None, :]
    q, k, v = gqa_prep(pa2, gains, cos, sin, s)
    o = _attend_all(q.reshape(b, s, -1), k.reshape(b, s, -1), v.reshape(b, s, -1), need_ctx)
    o = o.reshape(b, s, GQA_Q_HEADS, V7X_LANES)
    first = (jnp.arange(GQA_Q_HEADS) < grp)[None, None, :, None]
    o = jnp.where(first, o[..., :HEAD_DIM], o[..., HEAD_DIM:])
    return o.reshape(b, s, GQA_Q_HEADS * HEAD_DIM)


def _mla(p, qa_g, w_qup, kva_g, w_kvup, cos, sin, need_ctx):
    b, s, _ = p.shape
    r0 = MLA_Q_RANK
    r1 = MLA_Q_RANK + MLA_KV_RANK
    blocks = lambda w, lo, hi: jnp.pad(w.reshape(w.shape[0], MLA_HEADS, -1)[..., lo:hi],
                                       ((0, 0), (0, 0), (0, V7X_LANES - (hi - lo)))).reshape(w.shape[0], -1)
    w_q = blocks(w_qup, 0, MLA_NOPE + MLA_ROPE).astype(BF16)
    w_kv = jnp.concatenate([blocks(w_kvup, 0, MLA_NOPE), blocks(w_kvup, MLA_NOPE, MLA_NOPE + MLA_V)],
                           axis=1).astype(BF16)
    zero = jnp.zeros((1, 1, r0), F32)
    q = norm_mod_matmul(p[..., :r0].reshape(b * s, r0), qa_g[None, :], zero, zero, w_q, lambda i: 0)
    zero = jnp.zeros((1, 1, MLA_KV_RANK), F32)
    kv = norm_mod_matmul(p[..., r0:r1].reshape(b * s, MLA_KV_RANK), kva_g[None, :], zero, zero, w_kv, lambda i: 0)
    kr = jnp.pad(p[..., r1:r1 + MLA_ROPE].reshape(b * s, MLA_ROPE),
                 ((0, 0), (MLA_NOPE, V7X_LANES - MLA_NOPE - MLA_ROPE)))
    q, k, v = mla_prep(q, kv, kr, cos, sin, s)
    o = _attend_all(q.reshape(b, s, -1), k.reshape(b, s, -1), v.reshape(b, s, -1), need_ctx)
    return o.reshape(b, s, MLA_HEADS, V7X_LANES)[..., :MLA_V].reshape(b, s, MLA_HEADS * MLA_V)


def kernel(x, c, ctx, c_ctx, ada_w, ada_b, norm1_g, norm2_g, ev_w_in, ev_w_out, gqa_qn_g, gqa_kn_g,
           hg_lb_logits, hg_onorm_g, od_w_in, od_w_out, mla_qa_g, mla_w_qup, mla_kva_g, mla_w_kvup,
           s5_a_re, s5_a_im, s5_log_dt, s5_b_re, s5_b_im, s5_c_re, s5_c_im, s5_d, s5_w_glu, s5_b_glu,
           peer_wq, peer_keys, peer_u, peer_v, final_g):
    b, t, d = x.shape
    s = CTX_LEN + t
    depth = ada_w.shape[0]
    rows = t // GRID_W
    cos_g, sin_g = _rope_tables(rows, HEAD_DIM, 0, HEAD_DIM)
    cos_m, sin_m = _rope_tables(rows, MLA_ROPE, MLA_NOPE, V7X_LANES)
    lb_all = jnp.cumsum(jax.nn.softmax(hg_lb_logits, axis=1), axis=1)

    def mod_index(tile):
        per, nctx = s // tile, CTX_LEN // tile
        return lambda i: jnp.where(i % per < nctx, b, i // per)

    mi_row = mod_index(ROW_TILE)
    mi_peer = mod_index(PEER_TILE)

    xa = jnp.concatenate([ctx, x], axis=1).reshape(b * s, d)
    s_all = jnp.concatenate([jax.nn.silu(c), jax.nn.silu(c_ctx)[None, :]], axis=0)
    s_pad = jnp.concatenate([s_all, jnp.zeros((V7X_SUBLANES - (b + 1) % V7X_SUBLANES, d), F32)], axis=0)

    for layer in range(depth):
        need_ctx = layer < depth - 1
        j = layer // 2
        mod = matmul(s_pad, ada_w[layer].astype(BF16), tm=s_pad.shape[0])[:b + 1] + ada_b[layer]
        mod = [m[:, None, :] for m in jnp.split(mod, 6, axis=-1)]
        if layer % 2 == 0:
            pa2 = norm_mod_matmul(xa, norm1_g[layer][None, :], mod[0], mod[1], ev_w_in[j].astype(BF16), mi_row)
            pa = pa2.reshape(b, s, -1)
            ya = _gqa(pa2, b, gqa_qn_g[j], gqa_kn_g[j], cos_g, sin_g, need_ctx)
            yb = _hgrn2(pa2, b, lb_all[:, j], hg_onorm_g[j]).reshape(b, s, HG_W)
            w_out = ev_w_out[j]
        else:
            pa = norm_mod_matmul(xa, norm1_g[layer][None, :], mod[0], mod[1], od_w_in[j].astype(BF16), mi_row)
            pa = pa.reshape(b, s, -1)
            ya = _mla(pa[..., :MLA_IN], mla_qa_g[j], mla_w_qup[j], mla_kva_g[j], mla_w_kvup[j], cos_m, sin_m, need_ctx)
            y5 = _s5_core(pa[..., MLA_IN:], s5_a_re[j], s5_a_im[j], s5_log_dt[j], s5_b_re[j], s5_b_im[j],
                          s5_c_re[j], s5_c_im[j], s5_d[j])
            z = jax.nn.gelu(y5, approximate=False).reshape(b * s, S5_WIDTH)
            gl = matmul(z, s5_w_glu[j].astype(BF16)) + s5_b_glu[j]
            yb = (z * jax.nn.sigmoid(gl)).reshape(b, s, S5_WIDTH)
            w_out = od_w_out[j]
        y =jnp.concatenate([ya, yb], axis=-1).reshape(b * s, -1)
        xa = matmul_residual(y, w_out.astype(BF16), xa, mod[2], mi_row)
        out = peer(xa, norm2_g[layer][None, :], mod[3], mod[4],
                   peer_wq[layer], peer_keys[layer], peer_u[layer], peer_v[layer], mi_peer)
        gate = jnp.concatenate([jnp.broadcast_to(mod[5][b:b + 1], (b, CTX_LEN, d)),
                                jnp.broadcast_to(mod[5][:b], (b, t, d))], axis=1).reshape(b * s, d)
        xa = xa + gate * out
    xl = xa.reshape(b, s, d)[:, CTX_LEN:].reshape(b * t, d)
    return rmsnorm_rows(xl, final_g[None, :]).reshape(b, t, d)
```

```python
import functools
import math

import jax
import jax.numpy as jnp
from jax import lax
from jax.experimental import pallas as pl
from jax.experimental.pallas import tpu as pltpu

F32 = jnp.float32
BF16 = jnp.bfloat16

D_MODEL = 1024
GRID_W = 64
CTX_LEN = 256
EPS = 1e-6
ROPE_THETA = 10000.0

MIX_HALF = D_MODEL // 2
HEAD_DIM = 64
GQA_Q_HEADS = MIX_HALF // HEAD_DIM
GQA_KV_HEADS = GQA_Q_HEADS // 4
GQA_IN = (GQA_Q_HEADS + 2 * GQA_KV_HEADS) * HEAD_DIM

HG_DK = 128
HG_DV = 128
HG_HEADS = MIX_HALF // HG_DV
HG_W = HG_HEADS * HG_DK
GLA_CHUNK = 64

MLA_HEADS = MIX_HALF // HEAD_DIM
MLA_NOPE = 64
MLA_ROPE = 32
MLA_V = 64
MLA_Q_RANK = 384
MLA_KV_RANK = 256
MLA_IN = MLA_Q_RANK + MLA_KV_RANK + MLA_ROPE

S5_WIDTH = MIX_HALF
S5_GROUP = 16
S5_GROUPS = S5_WIDTH // S5_GROUP
S5_STATE = 64

PEER_HEADS = 8
PEER_KEYS = 128
N_EXPERTS = PEER_KEYS * PEER_KEYS
PEER_TOPK = 16
PEER_QDIM = 256
PEER_HALF = PEER_QDIM // 2
PEER_PAIRS = PEER_HEADS * PEER_TOPK

V7X_LANES = 128
V7X_SUBLANES = 8
V7X_VMEM_BYTES = 64 * 1024 * 1024
VMEM_LIMIT = V7X_VMEM_BYTES - 8 * 1024 * 1024

ROW_TILE = 256
PEER_TILE = 128
ATTN_Q_TILE = 512
EXPERT_WORDS = D_MODEL // 2
EXPERT_ROWS = EXPERT_WORDS // V7X_LANES


def _cparams(*sem):
    return pltpu.CompilerParams(dimension_semantics=sem, vmem_limit_bytes=VMEM_LIMIT)


def _norm_mod(x, g, shift, scale):
    ms = jnp.mean(x * x, axis=-1, keepdims=True)
    h = (x * lax.rsqrt(ms + EPS)) * g
    return h * (1.0 + scale) + shift


def _nmm_kernel(x_ref, g_ref, sh_ref, sc_ref, w_ref, o_ref):
    h = _norm_mod(x_ref[...], g_ref[...], sh_ref[0], sc_ref[0])
    o_ref[...] = jnp.dot(h.astype(BF16), w_ref[...], preferred_element_type=F32)


def norm_mod_matmul(x, g, shift, scale, w, mod_index, tm=ROW_TILE):
    r, k = x.shape
    n = w.shape[1]
    assert r % tm == 0 and w.shape[0] == k
    return pl.pallas_call(
        _nmm_kernel,
        grid=(r // tm,),
        in_specs=[
            pl.BlockSpec((tm, k), lambda i: (i, 0)),
            pl.BlockSpec((1, k), lambda i: (0, 0)),
            pl.BlockSpec((1, 1, k), lambda i: (mod_index(i), 0, 0)),
            pl.BlockSpec((1, 1, k), lambda i: (mod_index(i), 0, 0)),
            pl.BlockSpec((k, n), lambda i: (0, 0)),
        ],
        out_specs=pl.BlockSpec((tm, n), lambda i: (i, 0)),
        out_shape=jax.ShapeDtypeStruct((r, n), F32),
        compiler_params=_cparams("parallel"),
        name="norm_mod_matmul",
    )(x, g, shift, scale, w)


def _mm_kernel(a_ref, w_ref, o_ref):
    o_ref[...] = jnp.dot(a_ref[...].astype(BF16), w_ref[...], preferred_element_type=F32)


def matmul(a, w, tm=ROW_TILE):
    r, k = a.shape
    n = w.shape[1]
    assert r % tm == 0
    return pl.pallas_call(
        _mm_kernel,
        grid=(r // tm,),
        in_specs=[pl.BlockSpec((tm, k), lambda i: (i, 0)), pl.BlockSpec((k, n), lambda i: (0, 0))],
        out_specs=pl.BlockSpec((tm, n), lambda i: (i, 0)),
        out_shape=jax.ShapeDtypeStruct((r, n), F32),
        compiler_params=_cparams("parallel"),
        name="matmul",
    )(a, w)


def _mmres_kernel(a_ref, w_ref, x_ref, gate_ref, o_ref):
    y = jnp.dot(a_ref[...].astype(BF16), w_ref[...], preferred_element_type=F32)
    o_ref[...] = x_ref[...] + gate_ref[0] * y


def matmul_residual(a, w, x, gate, mod_index, tm=ROW_TILE):
    r, k = a.shape
    n = w.shape[1]
    assert r % tm == 0
    return pl.pallas_call(
        _mmres_kernel,
        grid=(r // tm,),
        in_specs=[
            pl.BlockSpec((tm, k), lambda i: (i, 0)),
            pl.BlockSpec((k, n), lambda i: (0, 0)),
            pl.BlockSpec((tm, n), lambda i: (i, 0)),
            pl.BlockSpec((1, 1, n), lambda i: (mod_index(i), 0, 0)),
        ],
        out_specs=pl.BlockSpec((tm, n), lambda i: (i, 0)),
        out_shape=jax.ShapeDtypeStruct((r, n), F32),
        compiler_params=_cparams("parallel"),
        name="matmul_residual",
    )(a, w, x, gate)


def _rms_kernel(x_ref, g_ref, o_ref):
    x = x_ref[...]
    ms = jnp.mean(x * x, axis=-1, keepdims=True)
    o_ref[...] = (x * lax.rsqrt(ms + EPS)) * g_ref[...]


def rmsnorm_rows(x, g, tm=ROW_TILE):
    r, k = x.shape
    return pl.pallas_call(
        _rms_kernel,
        grid=(r // tm,),
        in_specs=[pl.BlockSpec((tm, k), lambda i: (i, 0)), pl.BlockSpec((1, k), lambda i: (0, 0))],
        out_specs=pl.BlockSpec((tm, k), lambda i: (i, 0)),
        out_shape=jax.ShapeDtypeStruct((r, k), F32),
        compiler_params=_cparams("parallel"),
        name="rmsnorm_rows",
    )(x, g)


ATTN_KV_CHUNKS = 3
V7X_MXU_DIM = 256
LOG2E = math.log2(math.e)


def _attn_kernel(q_ref, k_ref, v_ref, o_ref):
    q = q_ref[...]
    sk = k_ref.shape[0]
    nch = ATTN_KV_CHUNKS if sk % (ATTN_KV_CHUNKS * V7X_MXU_DIM) == 0 else 1
    ck = sk // nch
    m = l = acc = None
    for c in range(nch):
        kc = k_ref[c * ck:(c + 1) * ck, :]
        vc = v_ref[c * ck:(c + 1) * ck, :]
        s = lax.dot_general(q, kc, (((1,), (1,)), ((), ())), preferred_element_type=F32)
        mc = jnp.max(s, axis=-1, keepdims=True)
        if c == 0:
            m = mc
            p = jnp.exp2(s - m)
            l = jnp.sum(p, axis=-1, keepdims=True)
            acc = jnp.dot(p.astype(BF16), vc, preferred_element_type=F32)
        else:
            m_new = jnp.maximum(m, mc)
            a = jnp.exp2(m - m_new)
            p = jnp.exp2(s - m_new)
            l = a * l + jnp.sum(p, axis=-1, keepdims=True)
            acc = a * acc + jnp.dot(p.astype(BF16), vc, preferred_element_type=F32)
            m = m_new
    o_ref[...] = acc / l


def attention(q, k, v, tq=ATTN_Q_TILE):
    b, sq, hl = q.shape
    sk, hkl = k.shape[1], k.shape[2]
    h, hk = hl // V7X_LANES, hkl // V7X_LANES
    grp = h // hk
    tq = min(tq, sq)
    assert sq % tq == 0
    return pl.pallas_call(
        _attn_kernel,
        grid=(b, h, sq // tq),
        in_specs=[
            pl.BlockSpec((None, tq, V7X_LANES), lambda bi, hi, qi: (bi, qi, hi)),
            pl.BlockSpec((None, sk, V7X_LANES), lambda bi, hi, qi: (bi, 0, hi // grp)),
            pl.BlockSpec((None, sk, V7X_LANES), lambda bi, hi, qi: (bi, 0, hi // grp)),
        ],
        out_specs=pl.BlockSpec((None, tq, V7X_LANES), lambda bi, hi, qi: (bi, qi, hi)),
        out_shape=jax.ShapeDtypeStruct((b, sq, hl), F32),
        compiler_params=_cparams("parallel", "parallel", "parallel"),
        name="attention",
    )(q, k, v)


def _rotate(x, cos, sin, quarter):
    n = x.shape[1]
    lane = lax.broadcasted_iota(jnp.int32, x.shape, 1)
    partner = jnp.where(lane % (2 * quarter) < quarter, pltpu.roll(x, n - quarter, axis=1),
                        pltpu.roll(x, quarter, axis=1))
    return x * cos + partner * sin


def _gqa_prep_kernel(p_ref, g_ref, cos_ref, sin_ref, avg_ref, q_ref, k_ref, v_ref):
    cos, sin = cos_ref[...], sin_ref[...]
    low = lax.broadcasted_iota(jnp.int32, cos.shape, 1) < HEAD_DIM
    qk_blocks = (GQA_Q_HEADS + GQA_KV_HEADS) * HEAD_DIM // V7X_LANES
    q_blocks = GQA_Q_HEADS * HEAD_DIM // V7X_LANES
    for c in range(qk_blocks):
        lanes = slice(c * V7X_LANES, (c + 1) * V7X_LANES)
        x = p_ref[:, lanes]
        ms = jnp.dot(x * x, avg_ref[...], precision=lax.Precision.HIGHEST, preferred_element_type=F32)
        y = _rotate(x * lax.rsqrt(ms + EPS) * g_ref[:, lanes], cos, sin, HEAD_DIM // 4)
        if c < q_blocks:
            y = y * (HEAD_DIM ** -0.5 * LOG2E)
            other = pltpu.roll(y, HEAD_DIM, axis=1)
            if c < q_blocks // 2:
                first, second = jnp.where(low, y, 0.0), jnp.where(low, other, 0.0)
            else:
                first, second = jnp.where(low, 0.0, other), jnp.where(low, 0.0, y)
            q_ref[:, 2 * c * V7X_LANES:(2 * c + 1) * V7X_LANES] = first.astype(BF16)
            q_ref[:, (2 * c + 1) * V7X_LANES:(2 * c + 2) * V7X_LANES] = second.astype(BF16)
        else:
            k_ref[...] = y.astype(BF16)
    v_ref[...] = p_ref[:, qk_blocks * V7X_LANES:(qk_blocks + 1) * V7X_LANES].astype(BF16)


def gqa_prep(pa, gains, cos, sin, seq, tm=ROW_TILE):
    r = pa.shape[0]
    per = seq // tm
    avg = jnp.kron(jnp.eye(V7X_LANES // HEAD_DIM, dtype=F32), jnp.full((HEAD_DIM, HEAD_DIM), 1.0 / HEAD_DIM, F32))
    tab = pl.BlockSpec((tm, V7X_LANES), lambda i: (i % per, 0))
    kv = pl.BlockSpec((tm, V7X_LANES), lambda i: (i, 0))
    return pl.pallas_call(
        _gqa_prep_kernel,
        grid=(r // tm,),
        in_specs=[pl.BlockSpec((tm, GQA_IN), lambda i: (i, 0)), pl.BlockSpec(gains.shape, lambda i: (0, 0)), tab, tab,
                  pl.BlockSpec(avg.shape, lambda i: (0, 0))],
        out_specs=[pl.BlockSpec((tm, GQA_Q_HEADS * V7X_LANES), lambda i: (i, 0)), kv, kv],
        out_shape=[jax.ShapeDtypeStruct((r, GQA_Q_HEADS * V7X_LANES), BF16),
                   jax.ShapeDtypeStruct((r, V7X_LANES), BF16), jax.ShapeDtypeStruct((r, V7X_LANES), BF16)],
        compiler_params=_cparams("parallel"),
        name="gqa_prep",
    )(pa, gains, cos, sin, avg)


def _mla_prep_kernel(q_ref, kv_ref, kr_ref, cos_ref, sin_ref, qo_ref, ko_ref, vo_ref):
    cos, sin = cos_ref[...], sin_ref[...]
    quarter = MLA_ROPE // 4
    kr = _rotate(kr_ref[...], cos, sin, quarter)
    scale = (MLA_NOPE + MLA_ROPE) ** -0.5 * LOG2E
    width = MLA_HEADS * V7X_LANES
    for h in range(MLA_HEADS):
        lanes = slice(h * V7X_LANES, (h + 1) * V7X_LANES)
        qo_ref[:, lanes] = (_rotate(q_ref[:, lanes], cos, sin, quarter) * scale).astype(BF16)
        ko_ref[:, lanes] = (kv_ref[:, lanes] + kr).astype(BF16)
        vo_ref[:, lanes] = kv_ref[:, width + h * V7X_LANES:width + (h + 1) * V7X_LANES].astype(BF16)


def mla_prep(q, kv, kr, cos, sin, seq, tm=ROW_TILE):
    r, width = q.shape
    per = seq // tm
    tab = pl.BlockSpec((tm, V7X_LANES), lambda i: (i % per, 0))
    out = pl.BlockSpec((tm, width), lambda i: (i, 0))
    return pl.pallas_call(
        _mla_prep_kernel,
        grid=(r // tm,),
        in_specs=[out, pl.BlockSpec((tm, 2 * width), lambda i: (i, 0)), pl.BlockSpec((tm, V7X_LANES), lambda i: (i, 0)),
                  tab, tab],
        out_specs=[out, out, out],
        out_shape=[jax.ShapeDtypeStruct((r, width), BF16)] * 3,
        compiler_params=_cparams("parallel"),
        name="mla_prep",
    )(q, kv, kr, cos, sin)


def _topk_rows(s, iota, k):
    n = s.shape[0]
    row = lax.broadcasted_iota(jnp.int32, (k, s.shape[1]), 0)
    vals = jnp.zeros((k, s.shape[1]), F32)
    ids = jnp.zeros((k, s.shape[1]), F32)
    for r in range(k):
        m = jnp.max(s, axis=0, keepdims=True)
        am = jnp.min(jnp.where(s == m, iota, float(n)), axis=0, keepdims=True)
        vals = jnp.where(row == r, m, vals)
        ids = jnp.where(row == r, am, ids)
        s = jnp.where(iota == am, -jnp.inf, s)
    return vals, ids


def _peer_ret_kernel(x_ref, g_ref, sh_ref, sc_ref, wq_ref, keys_ref, h_ref, eidx_ref, gw_ref, hb_ref, e_scr, w_scr):
    h = _norm_mod(x_ref[...], g_ref[...], sh_ref[0], sc_ref[0])
    h_ref[...] = h
    hb_ref[...] = h.astype(BF16)
    tb = x_ref.shape[0]
    iota_n = lax.broadcasted_iota(jnp.int32, (PEER_KEYS, tb), 0).astype(F32)
    half = PEER_TOPK // 2
    sub = lambda n: lax.broadcasted_iota(jnp.int32, (n, tb), 0).astype(F32)
    pos_c = jnp.concatenate([sub(PEER_TOPK)] + [sub(half) + float(a * PEER_TOPK) for a in range(1, half)]
                            + [(sub(half) + float(half)) * float(PEER_TOPK)], axis=0)

    def pair_up(first, second, scale):
        return jnp.concatenate([first[0:1] * scale + second]
                               + [first[a:a + 1] * scale + second[0:half] for a in range(1, half)]
                               + [first[half:] * scale + second[0:1]], axis=0)

    def head(hd, carry):
        q = jnp.dot(hb_ref[...], wq_ref[hd], preferred_element_type=F32)
        tops = []
        for p in range(2):
            qp = q[:, p * PEER_HALF:(p + 1) * PEER_HALF].astype(BF16)
            s = lax.dot_general(keys_ref[hd, p], qp, (((1,), (1,)), ((), ())),
                                preferred_element_type=F32)
            tops.append(_topk_rows(s, iota_n, PEER_TOPK))
        (s1, i1), (s2, i2) = tops
        cand = pair_up(s1, s2, 1.0)
        cidx = pair_up(i1, i2, float(PEER_KEYS))
        row = lax.broadcasted_iota(jnp.int32, (PEER_TOPK, tb), 0)
        sc = jnp.zeros((PEER_TOPK, tb), F32)
        ex = jnp.zeros((PEER_TOPK, tb), F32)
        for r in range(PEER_TOPK):
            m = jnp.max(cand, axis=0, keepdims=True)
            am = jnp.min(jnp.where(cand == m, pos_c, float(PEER_TOPK * PEER_TOPK)), axis=0, keepdims=True)
            hit = pos_c == am
            e = jnp.max(jnp.where(hit, cidx, 0.0), axis=0, keepdims=True)
            sc = jnp.where(row == r, m, sc)
            ex = jnp.where(row == r, e, ex)
            cand = jnp.where(hit, -jnp.inf, cand)
        pexp = jnp.exp(sc - sc[0:1])
        rows = pl.ds(pl.multiple_of(hd * PEER_TOPK, PEER_TOPK), PEER_TOPK)
        w_scr[rows, :] = pexp / jnp.sum(pexp, axis=0, keepdims=True)
        e_scr[rows, :] = ex
        return carry

    lax.fori_loop(0, PEER_HEADS, head, 0)
    gw_ref[...] = w_scr[...].T
    eidx_ref[...] = (e_scr[...].T * float(EXPERT_ROWS)).astype(jnp.int32)


def peer_retrieve(x, g, shift, scale, wq_heads, keys, mod_index, tb=PEER_TILE):
    r, d = x.shape
    nb = r // tb
    return pl.pallas_call(
        _peer_ret_kernel,
        grid=(nb,),
        in_specs=[
            pl.BlockSpec((tb, d), lambda i: (i, 0)),
            pl.BlockSpec((1, d), lambda i: (0, 0)),
            pl.BlockSpec((1, 1, d), lambda i: (mod_index(i), 0, 0)),
            pl.BlockSpec((1, 1, d), lambda i: (mod_index(i), 0, 0)),
            pl.BlockSpec((PEER_HEADS, d, PEER_QDIM), lambda i: (0, 0, 0)),
            pl.BlockSpec((PEER_HEADS, 2, PEER_KEYS, PEER_HALF), lambda i: (0, 0, 0, 0)),
        ],
        out_specs=[
            pl.BlockSpec((tb, d), lambda i: (i, 0)),
            pl.BlockSpec((tb, PEER_PAIRS), lambda i: (i, 0)),
            pl.BlockSpec((tb, PEER_PAIRS), lambda i: (i, 0)),
        ],
        out_shape=[
            jax.ShapeDtypeStruct((r, d), F32),
            jax.ShapeDtypeStruct((r, PEER_PAIRS), jnp.int32),
            jax.ShapeDtypeStruct((r, PEER_PAIRS), F32),
        ],
        scratch_shapes=[pltpu.VMEM((tb, d), BF16), pltpu.VMEM((PEER_PAIRS, tb), F32),
                        pltpu.VMEM((PEER_PAIRS, tb), F32)],
        compiler_params=_cparams("parallel"),
        name="peer_retrieve",
    )(x, g, shift, scale, wq_heads, keys)


def pack_rows_bf16(tab):
    e, d = tab.shape
    t16 = lax.bitcast_convert_type(tab.astype(BF16), jnp.uint16).astype(jnp.uint32)
    word = t16[:, :d // 2] | (t16[:, d // 2:] << 16)
    return lax.bitcast_convert_type(word, jnp.int32).reshape(e * EXPERT_ROWS, V7X_LANES)


def _slab_value_row(r):
    return r // 2 + EXPERT_ROWS * (r % 2)


PEER_GROUP = V7X_SUBLANES
SLAB_ROWS_BF16 = 2 * EXPERT_ROWS
PAIR_LANES = PEER_PAIRS * SLAB_ROWS_BF16


def _slab(tab_ref, row):
    return tab_ref[pl.ds(pl.multiple_of(row, EXPERT_ROWS), EXPERT_ROWS), :]


def _token_rows_bf16(h8, i):
    row = lax.broadcasted_iota(jnp.int32, (SLAB_ROWS_BF16, V7X_LANES), 0)
    hq = jnp.zeros((SLAB_ROWS_BF16, V7X_LANES), F32)
    for r in range(SLAB_ROWS_BF16):
        v = _slab_value_row(r)
        hq = jnp.where(row == r, h8[i:i + 1, v * V7X_LANES:(v + 1) * V7X_LANES], hq)
    return hq.astype(BF16)


def _peer_u_kernel(idx_ref, tab_ref, h_ref, gw_ref, fold_ref, o_ref, s_ref):
    groups = o_ref.shape[0] // PEER_GROUP
    row = lax.broadcasted_iota(jnp.int32, (PEER_GROUP, V7X_LANES), 0)

    def finish(g, sums):
        hi = sums.astype(BF16)
        lo = (sums - hi.astype(F32)).astype(BF16)
        a = (jnp.dot(hi, fold_ref[...], preferred_element_type=F32)
             + jnp.dot(lo, fold_ref[...], preferred_element_type=F32))
        rows = pl.ds(pl.multiple_of(g * PEER_GROUP, PEER_GROUP), PEER_GROUP)
        gelu = 0.5 * a * (1.0 + lax.erf(a * (2.0 ** -0.5)))
        o_ref[rows, :] = gelu * gw_ref[rows, :]

    def group(g, prev):
        finish(jnp.maximum(g - 1, 0), prev)
        sums = jnp.zeros((PEER_GROUP, PAIR_LANES), F32)
        h8 = h_ref[pl.ds(pl.multiple_of(g * PEER_GROUP, PEER_GROUP), PEER_GROUP), :]
        for i in range(PEER_GROUP):
            t = g * PEER_GROUP + i
            hb = _token_rows_bf16(h8, i)
            tok_idx = idx_ref.at[pl.ds(t * PEER_PAIRS, PEER_PAIRS)]
            for j in range(PEER_PAIRS):
                u = pltpu.bitcast(_slab(tab_ref, tok_idx[j]), BF16)
                s_ref[pl.ds((i * PEER_PAIRS + j) * EXPERT_ROWS, EXPERT_ROWS), :] = pltpu.bitcast(u * hb, jnp.int32)
            prod = pltpu.bitcast(s_ref[pl.ds(i * PEER_PAIRS * EXPERT_ROWS, PEER_PAIRS * EXPERT_ROWS), :], BF16)
            pick = jnp.where(row == i, 1.0, 0.0).astype(BF16)
            sums = sums + lax.dot_general(pick, prod, (((1,), (1,)), ((), ())), preferred_element_type=F32)
        return sums

    last = lax.fori_loop(0, groups, group, jnp.zeros((PEER_GROUP, PAIR_LANES), F32))
    finish(groups - 1, last)


def peer_expert_gates(idx4, tab, h, gw, tb=PEER_TILE):
    r = h.shape[0]
    fold = jnp.repeat(jnp.eye(PEER_PAIRS, dtype=BF16), SLAB_ROWS_BF16, axis=0)
    return pl.pallas_call(
        _peer_u_kernel,
        grid=(r // tb,),
        in_specs=[
            pl.BlockSpec((tb * PEER_PAIRS,), lambda i: (i,), memory_space=pltpu.SMEM),
            pl.BlockSpec(tab.shape, lambda i: (0, 0), pipeline_mode=pl.Buffered(1)),
            pl.BlockSpec((tb, h.shape[1]), lambda i: (i, 0)),
            pl.BlockSpec((tb, PEER_PAIRS), lambda i: (i, 0)),
            pl.BlockSpec((PAIR_LANES, PEER_PAIRS), lambda i: (0, 0)),
        ],
        out_specs=pl.BlockSpec((tb, PEER_PAIRS), lambda i: (i, 0)),
        out_shape=jax.ShapeDtypeStruct((r, PEER_PAIRS), F32),
        scratch_shapes=[pltpu.VMEM((PEER_GROUP * PEER_PAIRS * EXPERT_ROWS, V7X_LANES), jnp.int32)],
        compiler_params=_cparams("arbitrary"),
        name="peer_expert_gates",
    )(idx4, tab, h, gw, fold)


def _peer_v_kernel(idx_ref, w_ref, tab_ref, spread_ref, o_ref, g_ref):
    groups = w_ref.shape[0] // PEER_GROUP
    last_slot = PEER_GROUP - 1
    shape = (SLAB_ROWS_BF16, PAIR_LANES)
    out_row = lax.broadcasted_iota(jnp.int32, shape, 0)
    diag = (lax.broadcasted_iota(jnp.int32, shape, 1) % SLAB_ROWS_BF16) == 2 * (out_row % EXPERT_ROWS) + out_row // EXPERT_ROWS
    slot_rows = PEER_PAIRS * EXPERT_ROWS

    @pl.when(pl.program_id(0) == 0)
    def _():
        g_ref[pl.ds(last_slot * slot_rows, slot_rows), :] = jnp.zeros((slot_rows, V7X_LANES), jnp.int32)

    def contract(slot, wrow, t):
        rows = pltpu.bitcast(g_ref[pl.ds(slot * slot_rows, slot_rows), :], BF16)
        wi = jnp.where(diag, jnp.broadcast_to(wrow, shape), 0.0).astype(BF16)
        o_ref[pl.ds(pl.multiple_of(t * SLAB_ROWS_BF16, SLAB_ROWS_BF16), SLAB_ROWS_BF16), :] = jnp.dot(
            wi, rows, preferred_element_type=F32)

    def group(g, prev_wide):
        contract(last_slot, prev_wide[last_slot:], jnp.maximum(g * PEER_GROUP - 1, 0))
        w8 = w_ref[pl.ds(pl.multiple_of(g * PEER_GROUP, PEER_GROUP), PEER_GROUP), :]
        wide = jnp.dot(w8.astype(BF16), spread_ref[...], preferred_element_type=F32)
        for i in range(PEER_GROUP):
            t = g * PEER_GROUP + i
            tok_idx = idx_ref.at[pl.ds(t * PEER_PAIRS, PEER_PAIRS)]
            for j in range(PEER_PAIRS):
                g_ref[pl.ds((i * PEER_PAIRS + j) * EXPERT_ROWS, EXPERT_ROWS), :] = _slab(tab_ref, tok_idx[j])
            if i < last_slot:
                contract(i, wide[i:i + 1], t)
        return wide

    wide = lax.fori_loop(0, groups, group, jnp.zeros((PEER_GROUP, PAIR_LANES), F32))
    contract(last_slot, wide[last_slot:], groups * PEER_GROUP - 1)


def peer_expert_mix(idx4, w, tab, tb=PEER_TILE):
    r = w.shape[0]
    spread = jnp.repeat(jnp.eye(PEER_PAIRS, dtype=BF16), SLAB_ROWS_BF16, axis=1)
    return pl.pallas_call(
        _peer_v_kernel,
        grid=(r // tb,),
        in_specs=[
            pl.BlockSpec((tb * PEER_PAIRS,), lambda i: (i,), memory_space=pltpu.SMEM),
            pl.BlockSpec((tb, PEER_PAIRS), lambda i: (i, 0)),
            pl.BlockSpec(tab.shape, lambda i: (0, 0), pipeline_mode=pl.Buffered(1)),
            pl.BlockSpec((PEER_PAIRS, PAIR_LANES), lambda i: (0, 0)),
        ],
        out_specs=pl.BlockSpec((tb * SLAB_ROWS_BF16, V7X_LANES), lambda i: (i, 0)),
        out_shape=jax.ShapeDtypeStruct((r * SLAB_ROWS_BF16, V7X_LANES), F32),
        scratch_shapes=[pltpu.VMEM((PEER_GROUP * PEER_PAIRS * EXPERT_ROWS, V7X_LANES), jnp.int32)],
        compiler_params=_cparams("arbitrary"),
        name="peer_expert_mix",
    )(idx4, w, tab, spread)


def _peer_front_kernel(x_ref, g_ref, sh_ref, sc_ref, wq_ref, keys_ref, tab_ref, fold_ref, idx_out_ref, w_out_ref,
                       hb_ref, e_scr, w_scr, h_buf, gw_buf, idx_vmem, idx_smem, s_ref, sem):
    step = pl.program_id(0)
    wslot = step % 2
    rslot = 1 - wslot
    tb = x_ref.shape[0]

    @pl.when(step == 0)
    def _():
        h_buf[1] = jnp.zeros(h_buf.shape[1:], F32)
        gw_buf[1] = jnp.zeros(gw_buf.shape[1:], F32)
        idx_vmem[...] = jnp.zeros(idx_vmem.shape, jnp.int32)
        fill = pltpu.make_async_copy(idx_vmem, idx_smem.at[1], sem)
        fill.start()
        fill.wait()

    h = _norm_mod(x_ref[...], g_ref[...], sh_ref[0], sc_ref[0])
    h_buf[wslot] = h
    hb_ref[...] = h.astype(BF16)
    iota_n = lax.broadcasted_iota(jnp.int32, (PEER_KEYS, tb), 0).astype(F32)
    half = PEER_TOPK // 2
    sub = lambda n: lax.broadcasted_iota(jnp.int32, (n, tb), 0).astype(F32)
    pos_c = jnp.concatenate([sub(PEER_TOPK)] + [sub(half) + float(a * PEER_TOPK) for a in range(1, half)]
                            + [(sub(half) + float(half)) * float(PEER_TOPK)], axis=0)

    def pair_up(first, second, scale):
        return jnp.concatenate([first[0:1] * scale + second]
                               + [first[a:a + 1] * scale + second[0:half] for a in range(1, half)]
                               + [first[half:] * scale + second[0:1]], axis=0)

    def retrieve_head(hd):
        q = jnp.dot(hb_ref[...], wq_ref[hd], preferred_element_type=F32)
        tops = []
        for p in range(2):
            qp = q[:, p * PEER_HALF:(p + 1) * PEER_HALF].astype(BF16)
            s = lax.dot_general(keys_ref[hd, p], qp, (((1,), (1,)), ((), ())),
                                preferred_element_type=F32)
            tops.append(_topk_rows(s, iota_n, PEER_TOPK))
        (s1, i1), (s2, i2) = tops
        cand = pair_up(s1, s2, 1.0)
        cidx = pair_up(i1, i2, float(PEER_KEYS))
        row = lax.broadcasted_iota(jnp.int32, (PEER_TOPK, tb), 0)
        sc = jnp.zeros((PEER_TOPK, tb), F32)
        ex = jnp.zeros((PEER_TOPK, tb), F32)
        for r in range(PEER_TOPK):
            m = jnp.max(cand, axis=0, keepdims=True)
            am = jnp.min(jnp.where(cand == m, pos_c, float(PEER_TOPK * PEER_TOPK)), axis=0, keepdims=True)
            hit = pos_c == am
            e = jnp.max(jnp.where(hit, cidx, 0.0), axis=0, keepdims=True)
            sc = jnp.where(row == r, m, sc)
            ex = jnp.where(row == r, e, ex)
            cand = jnp.where(hit, -jnp.inf, cand)
        pexp = jnp.exp(sc - sc[0:1])
        rows = pl.ds(pl.multiple_of(hd * PEER_TOPK, PEER_TOPK), PEER_TOPK)
        w_scr[rows, :] = pexp / jnp.sum(pexp, axis=0, keepdims=True)
        e_scr[rows, :] = ex

    row8 = lax.broadcasted_iota(jnp.int32, (PEER_GROUP, V7X_LANES), 0)
    h_prev, gw_prev, idx_prev = h_buf.at[rslot], gw_buf.at[rslot], idx_smem.at[rslot]

    def finish(g, sums):
        hi = sums.astype(BF16)
        lo = (sums - hi.astype(F32)).astype(BF16)
        a = (jnp.dot(hi, fold_ref[...], preferred_element_type=F32)
             + jnp.dot(lo, fold_ref[...], preferred_element_type=F32))
        rows = pl.ds(pl.multiple_of(g * PEER_GROUP, PEER_GROUP), PEER_GROUP)
        gelu = 0.5 * a * (1.0 + lax.erf(a * (2.0 ** -0.5)))
        w_out_ref[rows, :] = gelu * gw_prev[rows, :]

    def gate_group(g, prev):
        finish(jnp.maximum(g - 1, 0), prev)
        sums = jnp.zeros((PEER_GROUP, PAIR_LANES), F32)
        h8 = h_prev[pl.ds(pl.multiple_of(g * PEER_GROUP, PEER_GROUP), PEER_GROUP), :]
        for i in range(PEER_GROUP):
            hb = _token_rows_bf16(h8, i)
            tok_idx = idx_prev.at[g * PEER_GROUP + i]
            for j in range(PEER_PAIRS):
                u = pltpu.bitcast(_slab(tab_ref, tok_idx[j]), BF16)
                s_ref[pl.ds((i * PEER_PAIRS + j) * EXPERT_ROWS, EXPERT_ROWS), :] = pltpu.bitcast(u * hb, jnp.int32)
            prod = pltpu.bitcast(s_ref[pl.ds(i * PEER_PAIRS * EXPERT_ROWS, PEER_PAIRS * EXPERT_ROWS), :], BF16)
            pick = jnp.where(row8 == i, 1.0, 0.0).astype(BF16)
            sums = sums + lax.dot_general(pick, prod, (((1,), (1,)), ((), ())), preferred_element_type=F32)
        return sums

    groups_per_head = tb // PEER_GROUP // PEER_HEADS

    def body(hd, sums):
        retrieve_head(hd)
        for k in range(groups_per_head):
            sums = gate_group(hd * groups_per_head + k, sums)
        return sums

    last = lax.fori_loop(0, PEER_HEADS, body, jnp.zeros((PEER_GROUP, PAIR_LANES), F32))
    finish(tb // PEER_GROUP - 1, last)
    gw_buf[wslot] = w_scr[...].T
    idx = (e_scr[...].T * float(EXPERT_ROWS)).astype(jnp.int32)
    idx_out_ref[...] = idx
    idx_vmem[...] = idx
    handoff = pltpu.make_async_copy(idx_vmem, idx_smem.at[wslot], sem)
    handoff.start()
    handoff.wait()


def peer_front(x, g, shift, scale, wq_heads, keys, tab, mod_index, tb=PEER_TILE):
    r, d = x.shape
    nb = r // tb
    fold = jnp.repeat(jnp.eye(PEER_PAIRS, dtype=BF16), SLAB_ROWS_BF16, axis=0)
    cur = lambda i: jnp.minimum(i, nb - 1)
    const = lambda shape: pl.BlockSpec(shape, lambda i: (0,) * len(shape), pipeline_mode=pl.Buffered(1))
    return pl.pallas_call(
        _peer_front_kernel,
        grid=(nb + 1,),
        in_specs=[
            pl.BlockSpec((tb, d), lambda i: (cur(i), 0)),
            pl.BlockSpec((1, d), lambda i: (0, 0)),
            pl.BlockSpec((1, 1, d), lambda i: (mod_index(cur(i)), 0, 0)),
            pl.BlockSpec((1, 1, d), lambda i: (mod_index(cur(i)), 0, 0)),
            const((PEER_HEADS, d, PEER_QDIM)),
            const((PEER_HEADS, 2, PEER_KEYS, PEER_HALF)),
            const(tab.shape),
            const((PAIR_LANES, PEER_PAIRS)),
        ],
        out_specs=[
            pl.BlockSpec((tb, PEER_PAIRS), lambda i: (cur(i), 0)),
            pl.BlockSpec((tb, PEER_PAIRS), lambda i: (jnp.maximum(i - 1, 0), 0)),
        ],
        out_shape=[jax.ShapeDtypeStruct((r, PEER_PAIRS), jnp.int32), jax.ShapeDtypeStruct((r, PEER_PAIRS), F32)],
        scratch_shapes=[
            pltpu.VMEM((tb, d), BF16), pltpu.VMEM((PEER_PAIRS, tb), F32), pltpu.VMEM((PEER_PAIRS, tb), F32),
            pltpu.VMEM((2, tb, d), F32), pltpu.VMEM((2, tb, PEER_PAIRS), F32),
            pltpu.VMEM((tb, PEER_PAIRS), jnp.int32), pltpu.SMEM((2, tb, PEER_PAIRS), jnp.int32),
            pltpu.VMEM((PEER_GROUP * PEER_PAIRS * EXPERT_ROWS, V7X_LANES), jnp.int32),
            pltpu.SemaphoreType.DMA,
        ],
        compiler_params=_cparams("arbitrary"),
        name="peer_front",
    )(x, g, shift, scale, wq_heads, keys, tab, fold)


def peer(x, g, shift, scale, wq, keys, u_tab, v_tab, mod_index_peer):
    r, d = x.shape
    wq_heads = wq.astype(BF16).reshape(d, PEER_HEADS, PEER_QDIM).transpose(1, 0, 2)
    idx4, w = peer_front(x, g, shift, scale, wq_heads, keys.astype(BF16), pack_rows_bf16(u_tab), mod_index_peer)
    return peer_expert_mix(idx4.reshape(r * PEER_PAIRS), w, pack_rows_bf16(v_tab)).reshape(r, d)


HG_TILE = 128
HG_SUB = 16
HG_PAIR = 2 * HG_DK


def _hgrn_kernel(q_ref, f_ref, v_ref, lb_ref, o_ref, st_ref, *, rev):
    @pl.when(pl.program_id(2) == 0)
    def _():
        st_ref[...] = jnp.zeros_like(st_ref)

    n = q_ref.shape[0]
    lb = lb_ref[...]
    f = lb + (1.0 - lb) * jax.nn.sigmoid(f_ref[...])
    kk = 1.0 - f
    lf = jnp.log(f)
    t = lax.broadcasted_iota(jnp.int32, (n, n), 0)
    s = lax.broadcasted_iota(jnp.int32, (n, n), 1)
    same = (t // HG_SUB) == (s // HG_SUB)
    mid = (t // HG_SUB) * HG_SUB + (HG_SUB // 2 if rev else HG_SUB // 2 - 1)
    seen = (s >= t) if rev else (s <= t)
    seen_mid = (s >= mid) if rev else (s <= mid)
    one = lambda m: jnp.where(m, 1.0, 0.0).astype(F32)
    hp = lambda a, b: jnp.dot(a, b, precision=lax.Precision.HIGHEST, preferred_element_type=F32)
    cum = hp(one(same & seen), lf)
    ref = hp(one(same & seen_mid), lf)
    last = hp(one(same), lf)
    q = q_ref[...]
    qe = (q * jnp.exp(cum)).astype(BF16)
    qm = (q * jnp.exp(cum - ref)).astype(BF16)
    km = (kk * jnp.exp(ref - cum)).astype(BF16)
    kd = (kk * jnp.exp(last - cum)).astype(BF16)
    dec = jnp.exp(last)
    vb = v_ref[...].astype(BF16)
    ti = lax.broadcasted_iota(jnp.int32, (HG_SUB, HG_SUB), 0)
    si = lax.broadcasted_iota(jnp.int32, (HG_SUB, HG_SUB), 1)
    causal = (si >= ti) if rev else (si <= ti)
    steps = range(n // HG_SUB)
    for c in (reversed(steps) if rev else steps):
        rows = slice(c * HG_SUB, (c + 1) * HG_SUB)
        for h in range(HG_PAIR // HG_DK):
            cols = slice(h * HG_DK, (h + 1) * HG_DK)
            st = st_ref[h]
            att = lax.dot_general(qm[rows, cols], km[rows, cols], (((1,), (1,)), ((), ())),
                                  preferred_element_type=F32)
            att = jnp.where(causal, att, 0.0).astype(BF16)
            o = (lax.dot_general(qe[rows, cols], st.astype(BF16), (((1,), (1,)), ((), ())),
                                 preferred_element_type=F32)
                 + jnp.dot(att, vb[rows, cols], preferred_element_type=F32))
            o_ref[rows, cols] = o
            upd = lax.dot_general(vb[rows, cols], kd[rows, cols], (((0,), (0,)), ((), ())),
                                  preferred_element_type=F32)
            st_ref[h] = st * dec[c * HG_SUB:c * HG_SUB + 1, cols] + upd


def hgrn_scan(pa, lb, rev, batch):
    rows = pa.shape[0]
    per = rows // batch // HG_TILE
    nctx = CTX_LEN // HG_TILE
    col0 = GQA_IN // HG_PAIR
    blocks = HG_W // HG_PAIR

    def tok(b, k):
        if rev:
            k = jnp.where(k < nctx, nctx - 1 - k, per - 1 - (k - nctx))
        return b * per + k

    spec = lambda cb: pl.BlockSpec((HG_TILE, HG_PAIR), lambda b, p, k: (tok(b, k), cb + p))
    d = 1 if rev else 0
    return pl.pallas_call(
        functools.partial(_hgrn_kernel, rev=rev),
        grid=(batch, blocks, per),
        in_specs=[spec(col0), spec(col0 + (1 + d) * blocks), spec(col0 + 3 * blocks),
                  pl.BlockSpec((None, 1, HG_PAIR), lambda b, p, k: (d, 0, p))],
        out_specs=pl.BlockSpec((HG_TILE, HG_PAIR), lambda b, p, k: (tok(b, k), p)),
        out_shape=jax.ShapeDtypeStruct((rows, HG_W), F32),
        scratch_shapes=[pltpu.VMEM((HG_PAIR // HG_DK, HG_DV, HG_DK), F32)],
        compiler_params=_cparams("parallel", "parallel", "arbitrary"),
        name="hgrn_scan_rev" if rev else "hgrn_scan_fwd",
    )(pa, pa, pa, lb)


def _rms_norm(x, g):
    xf = x.astype(F32)
    y = xf * lax.rsqrt(jnp.mean(xf * xf, axis=-1, keepdims=True) + EPS)
    return y * g.astype(F32)


def _hgrn2(pa2, batch, lb, onorm_g):
    lb = lb.reshape(2, 1, HG_W)
    o = hgrn_scan(pa2, lb, False, batch) + hgrn_scan(pa2, lb, True, batch)
    o = _rms_norm(o.reshape(-1, HG_HEADS, HG_DV), onorm_g).reshape(-1, HG_W)
    return o * jax.nn.silu(pa2[:, GQA_IN + 4 * HG_W:])


S5_CHUNK = 64
S5_LANES = S5_GROUPS * S5_STATE


def _s5_kernel(u_ref, bre_ref, bim_ref, cre_ref, cim_ref, are_ref, aim_ref, y_ref, xre, xim, hre, him):
    nb = V7X_SUBLANES // 2
    tiles = u_ref.shape[0] // V7X_SUBLANES
    direction = pl.program_id(0)
    rev = direction == 1

    @pl.when(pl.program_id(1) == 0)
    def _():
        hre[...] = jnp.zeros_like(hre)
        him[...] = jnp.zeros_like(him)

    ub = u_ref[...].astype(BF16)
    xre[...] = jnp.dot(ub, bre_ref[...], preferred_element_type=F32)
    xim[...] = jnp.dot(ub, bim_ref[...], preferred_element_type=F32)
    shape = (V7X_SUBLANES, S5_LANES)
    second = (lax.broadcasted_iota(jnp.int32, shape, 0) // nb) != direction
    are = jnp.broadcast_to(are_ref[...], shape)
    aim = jnp.broadcast_to(aim_ref[...], shape)
    cre = jnp.where(second, are * are - aim * aim, are)
    cim = jnp.where(second, 2.0 * are * aim, aim)

    def step(k, carry):
        pr, pi = carry
        k = jnp.where(rev, tiles - 1 - k, k)
        r = pl.ds(pl.multiple_of(k * V7X_SUBLANES, V7X_SUBLANES), V7X_SUBLANES)
        xr, xi = xre[r, :], xim[r, :]
        sr = jnp.where(second, pltpu.roll(xr, nb, axis=0), 0.0)
        si = jnp.where(second, pltpu.roll(xi, nb, axis=0), 0.0)
        nr = xr + (are * sr - aim * si) + (cre * pr - cim * pi)
        ni = xi + (are * si + aim * sr) + (cre * pi + cim * pr)
        xre[r, :] = nr
        xim[r, :] = ni
        return (jnp.where(second, nr, pltpu.roll(nr, nb, axis=0)),
                jnp.where(second, ni, pltpu.roll(ni, nb, axis=0)))

    hr, hi = lax.fori_loop(0, tiles, step, (hre[...], him[...]))
    hre[...] = hr
    him[...] = hi
    y_ref[...] = (jnp.dot(xre[...].astype(BF16), cre_ref[...], preferred_element_type=F32)
                  - jnp.dot(xim[...].astype(BF16), cim_ref[...], preferred_element_type=F32))


def s5_scan(u, bre, bim, cre, cim, are, aim, nb):
    rows, w = u.shape
    blk = S5_CHUNK * nb
    assert rows % blk == 0 and 2 * nb == V7X_SUBLANES
    nblk, nctx = rows // blk, CTX_LEN // S5_CHUNK
    wspec = lambda shape: pl.BlockSpec((None,) + shape, lambda d, i: (d, 0, 0))

    def chunk(d, i):
        back = jnp.where(i < nctx, nctx - 1 - i, nblk - 1 - (i - nctx))
        return jnp.where(d == 1, back, i)
    return pl.pallas_call(
        _s5_kernel,
        grid=(2, rows // blk),
        in_specs=[
            pl.BlockSpec((blk, w), lambda d, i: (chunk(d, i), 0)),
            wspec((w, S5_LANES)), wspec((w, S5_LANES)), wspec((S5_LANES, w)), wspec((S5_LANES, w)),
            wspec((1, S5_LANES)), wspec((1, S5_LANES)),
        ],
        out_specs=pl.BlockSpec((None, blk, w), lambda d, i: (d, chunk(d, i), 0)),
        out_shape=jax.ShapeDtypeStruct((2, rows, w), F32),
        scratch_shapes=[pltpu.VMEM((blk, S5_LANES), F32), pltpu.VMEM((blk, S5_LANES), F32),
                        pltpu.VMEM((V7X_SUBLANES, S5_LANES), F32), pltpu.VMEM((V7X_SUBLANES, S5_LANES), F32)],
        compiler_params=_cparams("arbitrary", "arbitrary"),
        name="s5_scan",
    )(u, bre, bim, cre, cim, are, aim)


def _s5_core(u, a_re, a_im, log_dt, b_re, b_im, c_re, c_im, d_skip):
    b, s, w = u.shape
    dt = jnp.exp(log_dt)[..., None]
    mag = jnp.exp(a_re * dt)
    abar_re, abar_im = mag * jnp.cos(a_im * dt), mag * jnp.sin(a_im * dt)
    den = a_re * a_re + a_im * a_im
    k_re = ((abar_re - 1.0) * a_re + abar_im * a_im) / den
    k_im = (abar_im * a_re - (abar_re - 1.0) * a_im) / den
    bb_re = k_re[..., None] * b_re - k_im[..., None] * b_im
    bb_im = k_re[..., None] * b_im + k_im[..., None] * b_re
    eye = jnp.eye(S5_GROUPS, dtype=F32)
    bd_in = lambda m: jnp.einsum('dgpc,gh->dgchp', m, eye).reshape(2, w, S5_LANES).astype(BF16)
    bd_out = lambda m: jnp.einsum('dgcp,gh->dgphc', m, eye).reshape(2, S5_LANES, w).astype(BF16)
    y2 = s5_scan(u.transpose(1, 0, 2).reshape(s * b, w), bd_in(bb_re), bd_in(bb_im), bd_out(c_re), bd_out(c_im),
                 abar_re.reshape(2, 1, S5_LANES), abar_im.reshape(2, 1, S5_LANES), b)
    return d_skip * u + (y2[0] + y2[1]).reshape(s, b, w).transpose(1, 0, 2)


def _rope_tables(rows, rot_dim, lead, period):
    axis_dim = rot_dim // 2
    inv = ROPE_THETA ** (-jnp.arange(0, axis_dim, 2, dtype=F32) / axis_dim)
    t = jnp.arange(rows * GRID_W)
    r = (t // GRID_W).astype(F32)[:, None] * inv
    c = (t % GRID_W).astype(F32)[:, None] * inv
    n = t.shape[0]
    tail = period - lead - rot_dim
    cos = jnp.concatenate([jnp.ones((n, lead), F32), jnp.cos(r), jnp.cos(r), jnp.cos(c), jnp.cos(c),
                           jnp.ones((n, tail), F32)], axis=-1)
    sin = jnp.concatenate([jnp.zeros((n, lead), F32), -jnp.sin(r), jnp.sin(r), -jnp.sin(c), jnp.sin(c),
                           jnp.zeros((n, tail), F32)], axis=-1)
    reps = V7X_LANES // period
    cos = jnp.concatenate([jnp.ones((CTX_LEN, V7X_LANES), F32), jnp.tile(cos, (1, reps))], axis=0)
    sin = jnp.concatenate([jnp.zeros((CTX_LEN, V7X_LANES), F32), jnp.tile(sin, (1, reps))], axis=0)
    return cos, sin


def _attend_all(q, k, v, need_ctx):
    o_lat = attention(q[:, CTX_LEN:], k, v)
    if need_ctx:
        o_ctx = attention(q[:, :CTX_LEN], k[:, :CTX_LEN], v[:, :CTX_LEN])
    else:
        o_ctx = jnp.zeros((q.shape[0], CTX_LEN, q.shape[2]), F32)
    return jnp.concatenate([o_ctx, o_lat], axis=1)


def _gqa(pa2, b, qn_g, kn_g, cos, sin, need_ctx):
    s = pa2.shape[0] // b
    grp = GQA_Q_HEADS // GQA_KV_HEADS
    gains = jnp.concatenate([jnp.tile(qn_g, GQA_Q_HEADS), jnp.tile(kn_g, GQA_KV_HEADS)])[None, :]
    q, k, v = gqa_prep(pa2, gains, cos, sin, s)
    o = _attend_all(q.reshape(b, s, -1), k.reshape(b, s, -1), v.reshape(b, s, -1), need_ctx)
    o = o.reshape(b, s, GQA_Q_HEADS, V7X_LANES)
    first = (jnp.arange(GQA_Q_HEADS) < grp)[None, None, :, None]
    o = jnp.where(first, o[..., :HEAD_DIM], o[..., HEAD_DIM:])
    return o.reshape(b, s, GQA_Q_HEADS * HEAD_DIM)


def _mla(p, qa_g, w_qup, kva_g, w_kvup, cos, sin, need_ctx):
    b, s, _ = p.shape
    r0 = MLA_Q_RANK
    r1 = MLA_Q_RANK + MLA_KV_RANK
    blocks = lambda w, lo, hi: jnp.pad(w.reshape(w.shape[0], MLA_HEADS, -1)[..., lo:hi],
                                       ((0, 0), (0, 0), (0, V7X_LANES - (hi - lo)))).reshape(w.shape[0], -1)
    w_q = blocks(w_qup, 0, MLA_NOPE + MLA_ROPE).astype(BF16)
    w_kv = jnp.concatenate([blocks(w_kvup, 0, MLA_NOPE), blocks(w_kvup, MLA_NOPE, MLA_NOPE + MLA_V)],
                           axis=1).astype(BF16)
    zero = jnp.zeros((1, 1, r0), F32)
    q = norm_mod_matmul(p[..., :r0].reshape(b * s, r0), qa_g[None, :], zero, zero, w_q, lambda i: 0)
    zero = jnp.zeros((1, 1, MLA_KV_RANK), F32)
    kv = norm_mod_matmul(p[..., r0:r1].reshape(b * s, MLA_KV_RANK), kva_g[None, :], zero, zero, w_kv, lambda i: 0)
    kr = jnp.pad(p[..., r1:r1 + MLA_ROPE].reshape(b * s, MLA_ROPE),
                 ((0, 0), (MLA_NOPE, V7X_LANES - MLA_NOPE - MLA_ROPE)))
    q, k, v = mla_prep(q, kv, kr, cos, sin, s)
    o = _attend_all(q.reshape(b, s, -1), k.reshape(b, s, -1), v.reshape(b, s, -1), need_ctx)
    return o.reshape(b, s, MLA_HEADS, V7X_LANES)[..., :MLA_V].reshape(b, s, MLA_HEADS * MLA_V)


def kernel(x, c, ctx, c_ctx, ada_w, ada_b, norm1_g, norm2_g, ev_w_in, ev_w_out, gqa_qn_g, gqa_kn_g,
           hg_lb_logits, hg_onorm_g, od_w_in, od_w_out, mla_qa_g, mla_w_qup, mla_kva_g, mla_w_kvup,
           s5_a_re, s5_a_im, s5_log_dt, s5_b_re, s5_b_im, s5_c_re, s5_c_im, s5_d, s5_w_glu, s5_b_glu,
           peer_wq, peer_keys, peer_u, peer_v, final_g):
    b, t, d = x.shape
    s = CTX_LEN + t
    depth = ada_w.shape[0]
    rows = t // GRID_W
    cos_g, sin_g = _rope_tables(rows, HEAD_DIM, 0, HEAD_DIM)
    cos_m, sin_m = _rope_tables(rows, MLA_ROPE, MLA_NOPE, V7X_LANES)
    lb_all = jnp.cumsum(jax.nn.softmax(hg_lb_logits, axis=1), axis=1)

    def mod_index(tile):
        per, nctx = s // tile, CTX_LEN // tile
        return lambda i: jnp.where(i % per < nctx, b, i // per)

    mi_row = mod_index(ROW_TILE)
    mi_peer = mod_index(PEER_TILE)

    xa = jnp.concatenate([ctx, x], axis=1).reshape(b * s, d)
    s_all = jnp.concatenate([jax.nn.silu(c), jax.nn.silu(c_ctx)[None, :]], axis=0)
    s_pad = jnp.concatenate([s_all, jnp.zeros((V7X_SUBLANES - (b + 1) % V7X_SUBLANES, d), F32)], axis=0)

    for layer in range(depth):
        need_ctx = layer < depth - 1
        j = layer // 2
        mod = matmul(s_pad, ada_w[layer].astype(BF16), tm=s_pad.shape[0])[:b + 1] + ada_b[layer]
        mod = [m[:, None, :] for m in jnp.split(mod, 6, axis=-1)]
        if layer % 2 == 0:
            pa2 = norm_mod_matmul(xa, norm1_g[layer][None, :], mod[0], mod[1], ev_w_in[j].astype(BF16), mi_row)
            pa = pa2.reshape(b, s, -1)
            ya = _gqa(pa2, b, gqa_qn_g[j], gqa_kn_g[j], cos_g, sin_g, need_ctx)
            yb = _hgrn2(pa2, b, lb_all[:, j], hg_onorm_g[j]).reshape(b, s, HG_W)
            w_out = ev_w_out[j]
        else:
            pa = norm_mod_matmul(xa, norm1_g[layer][None, :], mod[0], mod[1], od_w_in[j].astype(BF16), mi_row)
            pa = pa.reshape(b, s, -1)
            ya = _mla(pa[..., :MLA_IN], mla_qa_g[j], mla_w_qup[j], mla_kva_g[j], mla_w_kvup[j], cos_m, sin_m, need_ctx)
            y5 = _s5_core(pa[..., MLA_IN:], s5_a_re[j], s5_a_im[j], s5_log_dt[j], s5_b_re[j], s5_b_im[j],
                          s5_c_re[j], s5_c_im[j], s5_d[j])
            z = jax.nn.gelu(y5, approximate=False).reshape(b * s, S5_WIDTH)
            gl = matmul(z, s5_w_glu[j].astype(BF16)) + s5_b_glu[j]
            yb = (z * jax.nn.sigmoid(gl)).reshape(b, s, S5_WIDTH)
            w_out = od_w_out[j]
        y =jnp.concatenate([ya, yb], axis=-1).reshape(b * s, -1)
        xa = matmul_residual(y, w_out.astype(BF16), xa, mod[2], mi_row)
        out = peer(xa, norm2_g[layer][None, :], mod[3], mod[4],
                   peer_wq[layer], peer_keys[layer], peer_u[layer], peer_v[layer], mi_peer)
        gate = jnp.concatenate([jnp.broadcast_to(mod[5][b:b + 1], (b, CTX_LEN, d)),
                                jnp.broadcast_to(mod[5][:b], (b, t, d))], axis=1).reshape(b * s, d)
        xa = xa + gate * out
    xl = xa.reshape(b, s, d)[:, CTX_LEN:].reshape(b * t, d)
    return rmsnorm_rows(xl, final_g[None, :]).reshape(b, t, d)
```

```python
import functools
import math

import jax
import jax.numpy as jnp
from jax import lax
from jax.experimental import pallas as pl
from jax.experimental.pallas import tpu as pltpu

F32 = jnp.float32
BF16 = jnp.bfloat16

D_MODEL = 1024
GRID_W = 64
CTX_LEN = 256
EPS = 1e-6
ROPE_THETA = 10000.0

MIX_HALF = D_MODEL // 2
HEAD_DIM = 64
GQA_Q_HEADS = MIX_HALF // HEAD_DIM
GQA_KV_HEADS = GQA_Q_HEADS // 4
GQA_IN = (GQA_Q_HEADS + 2 * GQA_KV_HEADS) * HEAD_DIM

HG_DK = 128
HG_DV = 128
HG_HEADS = MIX_HALF // HG_DV
HG_W = HG_HEADS * HG_DK

MLA_HEADS = MIX_HALF // HEAD_DIM
MLA_NOPE = 64
MLA_ROPE = 32
MLA_V = 64
MLA_Q_RANK = 384
MLA_KV_RANK = 256
MLA_IN = MLA_Q_RANK + MLA_KV_RANK + MLA_ROPE

S5_WIDTH = MIX_HALF
S5_GROUP = 16
S5_GROUPS = S5_WIDTH // S5_GROUP
S5_STATE = 64

PEER_HEADS = 8
PEER_KEYS = 128
N_EXPERTS = PEER_KEYS * PEER_KEYS
PEER_TOPK = 16
PEER_QDIM = 256
PEER_HALF = PEER_QDIM // 2
PEER_PAIRS = PEER_HEADS * PEER_TOPK

V7X_LANES = 128
V7X_SUBLANES = 8
V7X_VMEM_BYTES = 64 * 1024 * 1024
VMEM_LIMIT = V7X_VMEM_BYTES - 8 * 1024 * 1024

ROW_TILE = 256
PEER_TILE = 128
ATTN_Q_TILE = 512
EXPERT_WORDS = D_MODEL // 2
EXPERT_ROWS = EXPERT_WORDS // V7X_LANES


def _cparams(*sem):
    return pltpu.CompilerParams(dimension_semantics=sem, vmem_limit_bytes=VMEM_LIMIT)


def _norm_mod(x, g, shift, scale):
    ms = jnp.mean(x * x, axis=-1, keepdims=True)
    h = (x * lax.rsqrt(ms + EPS)) * g
    return h * (1.0 + scale) + shift


def _nmm_kernel(x_ref, g_ref, sh_ref, sc_ref, w_ref, o_ref):
    h = _norm_mod(x_ref[...], g_ref[...], sh_ref[0], sc_ref[0])
    o_ref[...] = jnp.dot(h.astype(BF16), w_ref[...], preferred_element_type=F32)


def norm_mod_matmul(x, g, shift, scale, w, mod_index, tm=ROW_TILE):
    r, k = x.shape
    n = w.shape[1]
    assert r % tm == 0 and w.shape[0] == k
    return pl.pallas_call(
        _nmm_kernel,
        grid=(r // tm,),
        in_specs=[
            pl.BlockSpec((tm, k), lambda i: (i, 0)),
            pl.BlockSpec((1, k), lambda i: (0, 0)),
            pl.BlockSpec((1, 1, k), lambda i: (mod_index(i), 0, 0)),
            pl.BlockSpec((1, 1, k), lambda i: (mod_index(i), 0, 0)),
            pl.BlockSpec((k, n), lambda i: (0, 0)),
        ],
        out_specs=pl.BlockSpec((tm, n), lambda i: (i, 0)),
        out_shape=jax.ShapeDtypeStruct((r, n), F32),
        compiler_params=_cparams("parallel"),
        name="norm_mod_matmul",
    )(x, g, shift, scale, w)


def _mm_kernel(a_ref, w_ref, o_ref):
    o_ref[...] = jnp.dot(a_ref[...].astype(BF16), w_ref[...], preferred_element_type=F32)


def matmul(a, w, tm=ROW_TILE):
    r, k = a.shape
    n = w.shape[1]
    assert r % tm == 0
    return pl.pallas_call(
        _mm_kernel,
        grid=(r // tm,),
        in_specs=[pl.BlockSpec((tm, k), lambda i: (i, 0)), pl.BlockSpec((k, n), lambda i: (0, 0))],
        out_specs=pl.BlockSpec((tm, n), lambda i: (i, 0)),
        out_shape=jax.ShapeDtypeStruct((r, n), F32),
        compiler_params=_cparams("parallel"),
        name="matmul",
    )(a, w)


def _mmres_kernel(a_ref, w_ref, x_ref, gate_ref, o_ref):
    y = jnp.dot(a_ref[...].astype(BF16), w_ref[...], preferred_element_type=F32)
    o_ref[...] = x_ref[...] + gate_ref[0] * y


def matmul_residual(a, w, x, gate, mod_index, tm=ROW_TILE):
    r, k = a.shape
    n = w.shape[1]
    assert r % tm == 0
    return pl.pallas_call(
        _mmres_kernel,
        grid=(r // tm,),
        in_specs=[
            pl.BlockSpec((tm, k), lambda i: (i, 0)),
            pl.BlockSpec((k, n), lambda i: (0, 0)),
            pl.BlockSpec((tm, n), lambda i: (i, 0)),
            pl.BlockSpec((1, 1, n), lambda i: (mod_index(i), 0, 0)),
        ],
        out_specs=pl.BlockSpec((tm, n), lambda i: (i, 0)),
        out_shape=jax.ShapeDtypeStruct((r, n), F32),
        compiler_params=_cparams("parallel"),
        name="matmul_residual",
    )(a, w, x, gate)


def _gated_add_kernel(x_ref, y_ref, gate_ref, o_ref):
    o_ref[...] = x_ref[...] + gate_ref[0] * y_ref[...]


def gated_add(x, y, gate, mod_index, tm=ROW_TILE):
    r, n = x.shape
    row = pl.BlockSpec((tm, n), lambda i: (i, 0))
    return pl.pallas_call(
        _gated_add_kernel,
        grid=(r // tm,),
        in_specs=[row, row, pl.BlockSpec((1, 1, n), lambda i: (mod_index(i), 0, 0))],
        out_specs=row,
        out_shape=jax.ShapeDtypeStruct((r, n), F32),
        compiler_params=_cparams("parallel"),
        name="gated_add",
    )(x, y, gate)


def _rms_kernel(x_ref, g_ref, o_ref):
    x = x_ref[...]
    ms = jnp.mean(x * x, axis=-1, keepdims=True)
    o_ref[...] = (x * lax.rsqrt(ms + EPS)) * g_ref[...]


def rmsnorm_rows(x, g, tm=ROW_TILE):
    r, k = x.shape
    return pl.pallas_call(
        _rms_kernel,
        grid=(r // tm,),
        in_specs=[pl.BlockSpec((tm, k), lambda i: (i, 0)), pl.BlockSpec((1, k), lambda i: (0, 0))],
        out_specs=pl.BlockSpec((tm, k), lambda i: (i, 0)),
        out_shape=jax.ShapeDtypeStruct((r, k), F32),
        compiler_params=_cparams("parallel"),
        name="rmsnorm_rows",
    )(x, g)


ATTN_KV_CHUNKS = 3
V7X_MXU_DIM = 256
LOG2E = math.log2(math.e)


def _attn_kernel(q_ref, k_ref, v_ref, o_ref):
    q = q_ref[...]
    sk = k_ref.shape[0]
    nch = ATTN_KV_CHUNKS if sk % (ATTN_KV_CHUNKS * V7X_MXU_DIM) == 0 else 1
    ck = sk // nch
    m = l = acc = None
    for c in range(nch):
        kc = k_ref[c * ck:(c + 1) * ck, :]
        vc = v_ref[c * ck:(c + 1) * ck, :]
        s = lax.dot_general(q, kc, (((1,), (1,)), ((), ())), preferred_element_type=F32)
        mc = jnp.max(s, axis=-1, keepdims=True)
        if c == 0:
            m = mc
            p = jnp.exp2(s - m)
            l = jnp.sum(p, axis=-1, keepdims=True)
            acc = jnp.dot(p.astype(BF16), vc, preferred_element_type=F32)
        else:
            m_new = jnp.maximum(m, mc)
            a = jnp.exp2(m - m_new)
            p = jnp.exp2(s - m_new)
            l = a * l + jnp.sum(p, axis=-1, keepdims=True)
            acc = a * acc + jnp.dot(p.astype(BF16), vc, preferred_element_type=F32)
            m = m_new
    o_ref[...] = acc / l


def attention(q, k, v, tq=ATTN_Q_TILE):
    b, sq, hl = q.shape
    sk, hkl = k.shape[1], k.shape[2]
    h, hk = hl // V7X_LANES, hkl // V7X_LANES
    grp = h // hk
    tq = min(tq, sq)
    assert sq % tq == 0
    return pl.pallas_call(
        _attn_kernel,
        grid=(b, h, sq // tq),
        in_specs=[
            pl.BlockSpec((None, tq, V7X_LANES), lambda bi, hi, qi: (bi, qi, hi)),
            pl.BlockSpec((None, sk, V7X_LANES), lambda bi, hi, qi: (bi, 0, hi // grp)),
            pl.BlockSpec((None, sk, V7X_LANES), lambda bi, hi, qi: (bi, 0, hi // grp)),
        ],
        out_specs=pl.BlockSpec((None, tq, V7X_LANES), lambda bi, hi, qi: (bi, qi, hi)),
        out_shape=jax.ShapeDtypeStruct((b, sq, hl), F32),
        compiler_params=_cparams("parallel", "parallel", "parallel"),
        name="attention",
    )(q, k, v)


def _rotate(x, cos, sin, quarter):
    n = x.shape[1]
    lane = lax.broadcasted_iota(jnp.int32, x.shape, 1)
    partner = jnp.where(lane % (2 * quarter) < quarter, pltpu.roll(x, n - quarter, axis=1),
                        pltpu.roll(x, quarter, axis=1))
    return x * cos + partner * sin


def _gqa_prep_kernel(p_ref, g_ref, cos_ref, sin_ref, avg_ref, q_ref, k_ref, v_ref):
    cos, sin = cos_ref[...], sin_ref[...]
    low = lax.broadcasted_iota(jnp.int32, cos.shape, 1) < HEAD_DIM
    qk_blocks = (GQA_Q_HEADS + GQA_KV_HEADS) * HEAD_DIM // V7X_LANES
    q_blocks = GQA_Q_HEADS * HEAD_DIM // V7X_LANES
    for c in range(qk_blocks):
        lanes = slice(c * V7X_LANES, (c + 1) * V7X_LANES)
        x = p_ref[:, lanes]
        ms = jnp.dot(x * x, avg_ref[...], precision=lax.Precision.HIGHEST, preferred_element_type=F32)
        y = _rotate(x * lax.rsqrt(ms + EPS) * g_ref[:, lanes], cos, sin, HEAD_DIM // 4)
        if c < q_blocks:
            y = y * (HEAD_DIM ** -0.5 * LOG2E)
            other = pltpu.roll(y, HEAD_DIM, axis=1)
            if c < q_blocks // 2:
                first, second = jnp.where(low, y, 0.0), jnp.where(low, other, 0.0)
            else:
                first, second = jnp.where(low, 0.0, other), jnp.where(low, 0.0, y)
            q_ref[:, 2 * c * V7X_LANES:(2 * c + 1) * V7X_LANES] = first.astype(BF16)
            q_ref[:, (2 * c + 1) * V7X_LANES:(2 * c + 2) * V7X_LANES] = second.astype(BF16)
        else:
            k_ref[...] = y.astype(BF16)
    v_ref[...] = p_ref[:, qk_blocks * V7X_LANES:(qk_blocks + 1) * V7X_LANES].astype(BF16)


def gqa_prep(pa, gains, cos, sin, seq, tm=ROW_TILE):
    r = pa.shape[0]
    per = seq // tm
    avg = jnp.kron(jnp.eye(V7X_LANES // HEAD_DIM, dtype=F32), jnp.full((HEAD_DIM, HEAD_DIM), 1.0 / HEAD_DIM, F32))
    tab = pl.BlockSpec((tm, V7X_LANES), lambda i: (i % per, 0))
    kv = pl.BlockSpec((tm, V7X_LANES), lambda i: (i, 0))
    return pl.pallas_call(
        _gqa_prep_kernel,
        grid=(r // tm,),
        in_specs=[pl.BlockSpec((tm, GQA_IN), lambda i: (i, 0)), pl.BlockSpec(gains.shape, lambda i: (0, 0)), tab, tab,
                  pl.BlockSpec(avg.shape, lambda i: (0, 0))],
        out_specs=[pl.BlockSpec((tm, GQA_Q_HEADS * V7X_LANES), lambda i: (i, 0)), kv, kv],
        out_shape=[jax.ShapeDtypeStruct((r, GQA_Q_HEADS * V7X_LANES), BF16),
                   jax.ShapeDtypeStruct((r, V7X_LANES), BF16), jax.ShapeDtypeStruct((r, V7X_LANES), BF16)],
        compiler_params=_cparams("parallel"),
        name="gqa_prep",
    )(pa, gains, cos, sin, avg)


def _mla_prep_kernel(q_ref, kv_ref, kr_ref, cos_ref, sin_ref, qo_ref, ko_ref, vo_ref):
    cos, sin = cos_ref[...], sin_ref[...]
    quarter = MLA_ROPE // 4
    kr = _rotate(kr_ref[...], cos, sin, quarter)
    scale = (MLA_NOPE + MLA_ROPE) ** -0.5 * LOG2E
    width = MLA_HEADS * V7X_LANES
    for h in range(MLA_HEADS):
        lanes = slice(h * V7X_LANES, (h + 1) * V7X_LANES)
        qo_ref[:, lanes] = (_rotate(q_ref[:, lanes], cos, sin, quarter) * scale).astype(BF16)
        ko_ref[:, lanes] = (kv_ref[:, lanes] + kr).astype(BF16)
        vo_ref[:, lanes] = kv_ref[:, width + h * V7X_LANES:width + (h + 1) * V7X_LANES].astype(BF16)


def mla_prep(q, kv, kr, cos, sin, seq, tm=ROW_TILE):
    r, width = q.shape
    per = seq // tm
    tab = pl.BlockSpec((tm, V7X_LANES), lambda i: (i % per, 0))
    out = pl.BlockSpec((tm, width), lambda i: (i, 0))
    return pl.pallas_call(
        _mla_prep_kernel,
        grid=(r // tm,),
        in_specs=[out, pl.BlockSpec((tm, 2 * width), lambda i: (i, 0)), pl.BlockSpec((tm, V7X_LANES), lambda i: (i, 0)),
                  tab, tab],
        out_specs=[out, out, out],
        out_shape=[jax.ShapeDtypeStruct((r, width), BF16)] * 3,
        compiler_params=_cparams("parallel"),
        name="mla_prep",
    )(q, kv, kr, cos, sin)


def _topk_rows(s, iota, k):
    n = s.shape[0]
    row = lax.broadcasted_iota(jnp.int32, (k, s.shape[1]), 0)
    vals = jnp.zeros((k, s.shape[1]), F32)
    ids = jnp.zeros((k, s.shape[1]), F32)
    for r in range(k):
        m = jnp.max(s, axis=0, keepdims=True)
        am = jnp.min(jnp.where(s == m, iota, float(n)), axis=0, keepdims=True)
        vals = jnp.where(row == r, m, vals)
        ids = jnp.where(row == r, am, ids)
        s = jnp.where(iota == am, -jnp.inf, s)
    return vals, ids


def pack_rows_bf16(tab):
    e, d = tab.shape
    t16 = lax.bitcast_convert_type(tab.astype(BF16), jnp.uint16).astype(jnp.uint32)
    word = t16[:, :d // 2] | (t16[:, d // 2:] << 16)
    return lax.bitcast_convert_type(word, jnp.int32).reshape(e * EXPERT_ROWS, V7X_LANES)


def _slab_value_row(r):
    return r // 2 + EXPERT_ROWS * (r % 2)


PEER_GROUP = V7X_SUBLANES
SLAB_ROWS_BF16 = 2 * EXPERT_ROWS
PAIR_LANES = PEER_PAIRS * SLAB_ROWS_BF16


def _slab(tab_ref, row):
    return tab_ref[pl.ds(pl.multiple_of(row, EXPERT_ROWS), EXPERT_ROWS), :]


def _token_rows_bf16(h8, i):
    row = lax.broadcasted_iota(jnp.int32, (SLAB_ROWS_BF16, V7X_LANES), 0)
    hq = jnp.zeros((SLAB_ROWS_BF16, V7X_LANES), F32)
    for r in range(SLAB_ROWS_BF16):
        v = _slab_value_row(r)
        hq = jnp.where(row == r, h8[i:i + 1, v * V7X_LANES:(v + 1) * V7X_LANES], hq)
    return hq.astype(BF16)


def _peer_v_kernel(idx_ref, w_ref, tab_ref, spread_ref, o_ref, g_ref):
    groups = w_ref.shape[0] // PEER_GROUP
    last_slot = PEER_GROUP - 1
    shape = (SLAB_ROWS_BF16, PAIR_LANES)
    out_row = lax.broadcasted_iota(jnp.int32, shape, 0)
    diag = (lax.broadcasted_iota(jnp.int32, shape, 1) % SLAB_ROWS_BF16) == 2 * (out_row % EXPERT_ROWS) + out_row // EXPERT_ROWS
    slot_rows = PEER_PAIRS * EXPERT_ROWS

    @pl.when(pl.program_id(0) == 0)
    def _():
        g_ref[pl.ds(last_slot * slot_rows, slot_rows), :] = jnp.zeros((slot_rows, V7X_LANES), jnp.int32)

    def contract(slot, wrow, t):
        rows = pltpu.bitcast(g_ref[pl.ds(slot * slot_rows, slot_rows), :], BF16)
        wi = jnp.where(diag, jnp.broadcast_to(wrow, shape), 0.0).astype(BF16)
        o_ref[pl.ds(pl.multiple_of(t * SLAB_ROWS_BF16, SLAB_ROWS_BF16), SLAB_ROWS_BF16), :] = jnp.dot(
            wi, rows, preferred_element_type=F32)

    def group(g, prev_wide):
        contract(last_slot, prev_wide[last_slot:], jnp.maximum(g * PEER_GROUP - 1, 0))
        w8 = w_ref[pl.ds(pl.multiple_of(g * PEER_GROUP, PEER_GROUP), PEER_GROUP), :]
        wide = jnp.dot(w8.astype(BF16), spread_ref[...], preferred_element_type=F32)
        for i in range(PEER_GROUP):
            t = g * PEER_GROUP + i
            tok_idx = idx_ref.at[pl.ds(t * PEER_PAIRS, PEER_PAIRS)]
            for j in range(PEER_PAIRS):
                g_ref[pl.ds((i * PEER_PAIRS + j) * EXPERT_ROWS, EXPERT_ROWS), :] = _slab(tab_ref, tok_idx[j])
            if i < last_slot:
                contract(i, wide[i:i + 1], t)
        return wide

    wide = lax.fori_loop(0, groups, group, jnp.zeros((PEER_GROUP, PAIR_LANES), F32))
    contract(last_slot, wide[last_slot:], groups * PEER_GROUP - 1)


def peer_expert_mix(idx4, w, tab, tb=PEER_TILE):
    r = w.shape[0]
    spread = jnp.repeat(jnp.eye(PEER_PAIRS, dtype=BF16), SLAB_ROWS_BF16, axis=1)
    return pl.pallas_call(
        _peer_v_kernel,
        grid=(r // tb,),
        in_specs=[
            pl.BlockSpec((tb * PEER_PAIRS,), lambda i: (i,), memory_space=pltpu.SMEM),
            pl.BlockSpec((tb, PEER_PAIRS), lambda i: (i, 0)),
            pl.BlockSpec(tab.shape, lambda i: (0, 0), pipeline_mode=pl.Buffered(1)),
            pl.BlockSpec((PEER_PAIRS, PAIR_LANES), lambda i: (0, 0)),
        ],
        out_specs=pl.BlockSpec((tb * SLAB_ROWS_BF16, V7X_LANES), lambda i: (i, 0)),
        out_shape=jax.ShapeDtypeStruct((r * SLAB_ROWS_BF16, V7X_LANES), F32),
        scratch_shapes=[pltpu.VMEM((PEER_GROUP * PEER_PAIRS * EXPERT_ROWS, V7X_LANES), jnp.int32)],
        compiler_params=_cparams("arbitrary"),
        name="peer_expert_mix",
    )(idx4, w, tab, spread)


def _peer_front_kernel(x_ref, g_ref, sh_ref, sc_ref, wq_ref, keys_ref, tab_ref, fold_ref, idx_out_ref, w_out_ref,
                       hb_ref, e_scr, w_scr, h_buf, gw_buf, idx_vmem, idx_smem, s_ref, sem):
    step = pl.program_id(0)
    wslot = step % 2
    rslot = 1 - wslot
    tb = x_ref.shape[0]

    @pl.when(step == 0)
    def _():
        h_buf[1] = jnp.zeros(h_buf.shape[1:], F32)
        gw_buf[1] = jnp.zeros(gw_buf.shape[1:], F32)
        idx_vmem[...] = jnp.zeros(idx_vmem.shape, jnp.int32)
        fill = pltpu.make_async_copy(idx_vmem, idx_smem.at[1], sem)
        fill.start()
        fill.wait()

    h = _norm_mod(x_ref[...], g_ref[...], sh_ref[0], sc_ref[0])
    h_buf[wslot] = h
    hb_ref[...] = h.astype(BF16)
    iota_n = lax.broadcasted_iota(jnp.int32, (PEER_KEYS, tb), 0).astype(F32)
    half = PEER_TOPK // 2
    sub = lambda n: lax.broadcasted_iota(jnp.int32, (n, tb), 0).astype(F32)
    pos_c = jnp.concatenate([sub(PEER_TOPK)] + [sub(half) + float(a * PEER_TOPK) for a in range(1, half)]
                            + [(sub(half) + float(half)) * float(PEER_TOPK)], axis=0)

    def pair_up(first, second, scale):
        return jnp.concatenate([first[0:1] * scale + second]
                               + [first[a:a + 1] * scale + second[0:half] for a in range(1, half)]
                               + [first[half:] * scale + second[0:1]], axis=0)

    def retrieve_stages(hd):
        q = jnp.dot(hb_ref[...], wq_ref[hd], preferred_element_type=F32)
        tops = []

        def first_stage(p):
            qp = q[:, p * PEER_HALF:(p + 1) * PEER_HALF].astype(BF16)
            s = lax.dot_general(keys_ref[hd, p], qp, (((1,), (1,)), ((), ())),
                                preferred_element_type=F32)
            tops.append(_topk_rows(s, iota_n, PEER_TOPK))

        return [functools.partial(first_stage, 0), functools.partial(first_stage, 1),
                functools.partial(second_stage, hd, tops)]

    def second_stage(hd, tops):
        (s1, i1), (s2, i2) = tops
        cand = pair_up(s1, s2, 1.0)
        cidx = pair_up(i1, i2, float(PEER_KEYS))
        row = lax.broadcasted_iota(jnp.int32, (PEER_TOPK, tb), 0)
        sc = jnp.zeros((PEER_TOPK, tb), F32)
        ex = jnp.zeros((PEER_TOPK, tb), F32)
        for r in range(PEER_TOPK):
            m = jnp.max(cand, axis=0, keepdims=True)
            am = jnp.min(jnp.where(cand == m, pos_c, float(PEER_TOPK * PEER_TOPK)), axis=0, keepdims=True)
            hit = pos_c == am
            e = jnp.max(jnp.where(hit, cidx, 0.0), axis=0, keepdims=True)
            sc = jnp.where(row == r, m, sc)
            ex = jnp.where(row == r, e, ex)
            cand = jnp.where(hit, -jnp.inf, cand)
        pexp = jnp.exp(sc - sc[0:1])
        rows = pl.ds(pl.multiple_of(hd * PEER_TOPK, PEER_TOPK), PEER_TOPK)
        w_scr[rows, :] = pexp / jnp.sum(pexp, axis=0, keepdims=True)
        e_scr[rows, :] = ex

    row8 = lax.broadcasted_iota(jnp.int32, (PEER_GROUP, V7X_LANES), 0)
    h_prev, gw_prev, idx_prev = h_buf.at[rslot], gw_buf.at[rslot], idx_smem.at[rslot]

    def finish(g, sums):
        hi = sums.astype(BF16)
        lo = (sums - hi.astype(F32)).astype(BF16)
        a = (jnp.dot(hi, fold_ref[...], preferred_element_type=F32)
             + jnp.dot(lo, fold_ref[...], preferred_element_type=F32))
        rows = pl.ds(pl.multiple_of(g * PEER_GROUP, PEER_GROUP), PEER_GROUP)
        gelu = 0.5 * a * (1.0 + lax.erf(a * (2.0 ** -0.5)))
        w_out_ref[rows, :] = gelu * gw_prev[rows, :]

    def gate_group(g, prev, before_token):
        finish(jnp.maximum(g - 1, 0), prev)
        sums = jnp.zeros((PEER_GROUP, PAIR_LANES), F32)
        h8 = h_prev[pl.ds(pl.multiple_of(g * PEER_GROUP, PEER_GROUP), PEER_GROUP), :]
        for i in range(PEER_GROUP):
            if i in before_token:
                before_token[i]()
            hb = _token_rows_bf16(h8, i)
            tok_idx = idx_prev.at[g * PEER_GROUP + i]
            for j in range(PEER_PAIRS):
                u = pltpu.bitcast(_slab(tab_ref, tok_idx[j]), BF16)
                s_ref[pl.ds((i * PEER_PAIRS + j) * EXPERT_ROWS, EXPERT_ROWS), :] = pltpu.bitcast(u * hb, jnp.int32)
            prod = pltpu.bitcast(s_ref[pl.ds(i * PEER_PAIRS * EXPERT_ROWS, PEER_PAIRS * EXPERT_ROWS), :], BF16)
            pick = jnp.where(row8 == i, 1.0, 0.0).astype(BF16)
            sums = sums + lax.dot_general(pick, prod, (((1,), (1,)), ((), ())), preferred_element_type=F32)
        return sums

    groups_per_head = tb // PEER_GROUP // PEER_HEADS

    def body(hd, sums):
        first_a, first_b, second = retrieve_stages(hd)
        placement = [{0: first_a, PEER_GROUP // 2: first_b}, {0: second}]
        for k in range(groups_per_head):
            sums = gate_group(hd * groups_per_head + k, sums, placement[k] if k < len(placement) else {})
        return sums

    last = lax.fori_loop(0, PEER_HEADS, body, jnp.zeros((PEER_GROUP, PAIR_LANES), F32))
    finish(tb // PEER_GROUP - 1, last)
    gw_buf[wslot] = w_scr[...].T
    idx = (e_scr[...].T * float(EXPERT_ROWS)).astype(jnp.int32)
    idx_out_ref[...] = idx
    idx_vmem[...] = idx
    handoff = pltpu.make_async_copy(idx_vmem, idx_smem.at[wslot], sem)
    handoff.start()
    handoff.wait()


def peer_front(x, g, shift, scale, wq_heads, keys, tab, mod_index, tb=PEER_TILE):
    r, d = x.shape
    nb = r // tb
    fold = jnp.repeat(jnp.eye(PEER_PAIRS, dtype=BF16), SLAB_ROWS_BF16, axis=0)
    cur = lambda i: jnp.minimum(i, nb - 1)
    const = lambda shape: pl.BlockSpec(shape, lambda i: (0,) * len(shape), pipeline_mode=pl.Buffered(1))
    return pl.pallas_call(
        _peer_front_kernel,
        grid=(nb + 1,),
        in_specs=[
            pl.BlockSpec((tb, d), lambda i: (cur(i), 0)),
            pl.BlockSpec((1, d), lambda i: (0, 0)),
            pl.BlockSpec((1, 1, d), lambda i: (mod_index(cur(i)), 0, 0)),
            pl.BlockSpec((1, 1, d), lambda i: (mod_index(cur(i)), 0, 0)),
            const((PEER_HEADS, d, PEER_QDIM)),
            const((PEER_HEADS, 2, PEER_KEYS, PEER_HALF)),
            const(tab.shape),
            const((PAIR_LANES, PEER_PAIRS)),
        ],
        out_specs=[
            pl.BlockSpec((tb, PEER_PAIRS), lambda i: (cur(i), 0)),
            pl.BlockSpec((tb, PEER_PAIRS), lambda i: (jnp.maximum(i - 1, 0), 0)),
        ],
        out_shape=[jax.ShapeDtypeStruct((r, PEER_PAIRS), jnp.int32), jax.ShapeDtypeStruct((r, PEER_PAIRS), F32)],
        scratch_shapes=[
            pltpu.VMEM((tb, d), BF16), pltpu.VMEM((PEER_PAIRS, tb), F32), pltpu.VMEM((PEER_PAIRS, tb), F32),
            pltpu.VMEM((2, tb, d), F32), pltpu.VMEM((2, tb, PEER_PAIRS), F32),
            pltpu.VMEM((tb, PEER_PAIRS), jnp.int32), pltpu.SMEM((2, tb, PEER_PAIRS), jnp.int32),
            pltpu.VMEM((PEER_GROUP * PEER_PAIRS * EXPERT_ROWS, V7X_LANES), jnp.int32),
            pltpu.SemaphoreType.DMA,
        ],
        compiler_params=_cparams("arbitrary"),
        name="peer_front",
    )(x, g, shift, scale, wq_heads, keys, tab, fold)


def peer(x, g, shift, scale, wq, keys, u_tab, v_tab, mod_index_peer):
    r, d = x.shape
    wq_heads = wq.astype(BF16).reshape(d, PEER_HEADS, PEER_QDIM).transpose(1, 0, 2)
    idx4, w = peer_front(x, g, shift, scale, wq_heads, keys.astype(BF16), pack_rows_bf16(u_tab), mod_index_peer)
    return peer_expert_mix(idx4.reshape(r * PEER_PAIRS), w, pack_rows_bf16(v_tab)).reshape(r, d)


HG_TILE = 128
HG_SUB = 16
HG_PAIR = 2 * HG_DK


def _hgrn_kernel(q_ref, f_ref, v_ref, lb_ref, o_ref, st_ref, *, rev):
    @pl.when(pl.program_id(2) == 0)
    def _():
        st_ref[...] = jnp.zeros_like(st_ref)

    n = q_ref.shape[0]
    lb = lb_ref[...]
    f = lb + (1.0 - lb) * jax.nn.sigmoid(f_ref[...])
    kk = 1.0 - f
    lf = jnp.log(f)
    t = lax.broadcasted_iota(jnp.int32, (n, n), 0)
    s = lax.broadcasted_iota(jnp.int32, (n, n), 1)
    same = (t // HG_SUB) == (s // HG_SUB)
    mid = (t // HG_SUB) * HG_SUB + (HG_SUB // 2 if rev else HG_SUB // 2 - 1)
    seen = (s >= t) if rev else (s <= t)
    seen_mid = (s >= mid) if rev else (s <= mid)
    one = lambda m: jnp.where(m, 1.0, 0.0).astype(F32)
    hp = lambda a, b: jnp.dot(a, b, precision=lax.Precision.HIGHEST, preferred_element_type=F32)
    cum = hp(one(same & seen), lf)
    ref = hp(one(same & seen_mid), lf)
    last = hp(one(same), lf)
    q = q_ref[...]
    qe = (q * jnp.exp(cum)).astype(BF16)
    qm = (q * jnp.exp(cum - ref)).astype(BF16)
    km = (kk * jnp.exp(ref - cum)).astype(BF16)
    kd = (kk * jnp.exp(last - cum)).astype(BF16)
    dec = jnp.exp(last)
    vb = v_ref[...].astype(BF16)
    steps = range(n // HG_SUB)
    heads = range(HG_PAIR // HG_DK)
    intra = []
    for h in heads:
        cols = slice(h * HG_DK, (h + 1) * HG_DK)
        att = lax.dot_general(qm[:, cols], km[:, cols], (((1,), (1,)), ((), ())), preferred_element_type=F32)
        att = jnp.where(same & seen, att, 0.0).astype(BF16)
        intra.append(jnp.dot(att, vb[:, cols], preferred_element_type=F32))
    for c in (reversed(steps) if rev else steps):
        rows = slice(c * HG_SUB, (c + 1) * HG_SUB)
        for h in heads:
            cols = slice(h * HG_DK, (h + 1) * HG_DK)
            st = st_ref[h]
            o_ref[rows, cols] = intra[h][rows] + lax.dot_general(
                qe[rows, cols], st.astype(BF16), (((1,), (1,)), ((), ())), preferred_element_type=F32)
            upd = lax.dot_general(vb[rows, cols], kd[rows, cols], (((0,), (0,)), ((), ())),
                                  preferred_element_type=F32)
            st_ref[h] = st * dec[c * HG_SUB:c * HG_SUB + 1, cols] + upd


def hgrn_scan(pa, lb, rev, batch):
    rows = pa.shape[0]
    per = rows // batch // HG_TILE
    nctx = CTX_LEN // HG_TILE
    col0 = GQA_IN // HG_PAIR
    blocks = HG_W // HG_PAIR

    def tok(b, k):
        if rev:
            k = jnp.where(k < nctx, nctx - 1 - k, per - 1 - (k - nctx))
        return b * per + k

    spec = lambda cb: pl.BlockSpec((HG_TILE, HG_PAIR), lambda b, p, k: (tok(b, k), cb + p))
    d = 1 if rev else 0
    return pl.pallas_call(
        functools.partial(_hgrn_kernel, rev=rev),
        grid=(batch, blocks, per),
        in_specs=[spec(col0), spec(col0 + (1 + d) * blocks), spec(col0 + 3 * blocks),
                  pl.BlockSpec((None, 1, HG_PAIR), lambda b, p, k: (d, 0, p))],
        out_specs=pl.BlockSpec((HG_TILE, HG_PAIR), lambda b, p, k: (tok(b, k), p)),
        out_shape=jax.ShapeDtypeStruct((rows, HG_W), F32),
        scratch_shapes=[pltpu.VMEM((HG_PAIR // HG_DK, HG_DV, HG_DK), F32)],
        compiler_params=_cparams("parallel", "parallel", "arbitrary"),
        name="hgrn_scan_rev" if rev else "hgrn_scan_fwd",
    )(pa, pa, pa, lb)


def _rms_norm(x, g):
    xf = x.astype(F32)
    y = xf * lax.rsqrt(jnp.mean(xf * xf, axis=-1, keepdims=True) + EPS)
    return y * g.astype(F32)


def _hgrn2(pa2, batch, lb, onorm_g):
    lb = lb.reshape(2, 1, HG_W)
    o = hgrn_scan(pa2, lb, False, batch) + hgrn_scan(pa2, lb, True, batch)
    o = _rms_norm(o.reshape(-1, HG_HEADS, HG_DV), onorm_g).reshape(-1, HG_W)
    return o * jax.nn.silu(pa2[:, GQA_IN + 4 * HG_W:])


S5_CHUNK = 64
S5_LANES = S5_GROUPS * S5_STATE


def _s5_kernel(u_ref, bre_ref, bim_ref, cre_ref, cim_ref, are_ref, aim_ref, y_ref, xre, xim, hre, him):
    nb = V7X_SUBLANES // 2
    tiles = u_ref.shape[0] // V7X_SUBLANES
    direction = pl.program_id(0)
    rev = direction == 1

    @pl.when(pl.program_id(1) == 0)
    def _():
        hre[...] = jnp.zeros_like(hre)
        him[...] = jnp.zeros_like(him)

    ub = u_ref[...].astype(BF16)
    xre[...] = jnp.dot(ub, bre_ref[...], preferred_element_type=F32)
    xim[...] = jnp.dot(ub, bim_ref[...], preferred_element_type=F32)
    shape = (V7X_SUBLANES, S5_LANES)
    second = (lax.broadcasted_iota(jnp.int32, shape, 0) // nb) != direction
    are = jnp.broadcast_to(are_ref[...], shape)
    aim = jnp.broadcast_to(aim_ref[...], shape)
    cre = jnp.where(second, are * are - aim * aim, are)
    cim = jnp.where(second, 2.0 * are * aim, aim)

    def step(k, carry):
        pr, pi = carry
        k = jnp.where(rev, tiles - 1 - k, k)
        r = pl.ds(pl.multiple_of(k * V7X_SUBLANES, V7X_SUBLANES), V7X_SUBLANES)
        xr, xi = xre[r, :], xim[r, :]
        sr = jnp.where(second, pltpu.roll(xr, nb, axis=0), 0.0)
        si = jnp.where(second, pltpu.roll(xi, nb, axis=0), 0.0)
        nr = xr + (are * sr - aim * si) + (cre * pr - cim * pi)
        ni = xi + (are * si + aim * sr) + (cre * pi + cim * pr)
        xre[r, :] = nr
        xim[r, :] = ni
        return (jnp.where(second, nr, pltpu.roll(nr, nb, axis=0)),
                jnp.where(second, ni, pltpu.roll(ni, nb, axis=0)))

    hr, hi = lax.fori_loop(0, tiles, step, (hre[...], him[...]))
    hre[...] = hr
    him[...] = hi
    y_ref[...] = (jnp.dot(xre[...].astype(BF16), cre_ref[...], preferred_element_type=F32)
                  - jnp.dot(xim[...].astype(BF16), cim_ref[...], preferred_element_type=F32))


def s5_scan(u, bre, bim, cre, cim, are, aim, nb):
    rows, w = u.shape
    blk = S5_CHUNK * nb
    assert rows % blk == 0 and 2 * nb == V7X_SUBLANES
    nblk, nctx = rows // blk, CTX_LEN // S5_CHUNK
    wspec = lambda shape: pl.BlockSpec((None,) + shape, lambda d, i: (d, 0, 0))

    def chunk(d, i):
        back = jnp.where(i < nctx, nctx - 1 - i, nblk - 1 - (i - nctx))
        return jnp.where(d == 1, back, i)
    return pl.pallas_call(
        _s5_kernel,
        grid=(2, rows // blk),
        in_specs=[
            pl.BlockSpec((blk, w), lambda d, i: (chunk(d, i), 0)),
            wspec((w, S5_LANES)), wspec((w, S5_LANES)), wspec((S5_LANES, w)), wspec((S5_LANES, w)),
            wspec((1, S5_LANES)), wspec((1, S5_LANES)),
        ],
        out_specs=pl.BlockSpec((None, blk, w), lambda d, i: (d, chunk(d, i), 0)),
        out_shape=jax.ShapeDtypeStruct((2, rows, w), F32),
        scratch_shapes=[pltpu.VMEM((blk, S5_LANES), F32), pltpu.VMEM((blk, S5_LANES), F32),
                        pltpu.VMEM((V7X_SUBLANES, S5_LANES), F32), pltpu.VMEM((V7X_SUBLANES, S5_LANES), F32)],
        compiler_params=_cparams("arbitrary", "arbitrary"),
        name="s5_scan",
    )(u, bre, bim, cre, cim, are, aim)


def _s5_core(u, a_re, a_im, log_dt, b_re, b_im, c_re, c_im, d_skip):
    b, s, w = u.shape
    dt = jnp.exp(log_dt)[..., None]
    mag = jnp.exp(a_re * dt)
    abar_re, abar_im = mag * jnp.cos(a_im * dt), mag * jnp.sin(a_im * dt)
    den = a_re * a_re + a_im * a_im
    k_re = ((abar_re - 1.0) * a_re + abar_im * a_im) / den
    k_im = (abar_im * a_re - (abar_re - 1.0) * a_im) / den
    bb_re = k_re[..., None] * b_re - k_im[..., None] * b_im
    bb_im = k_re[..., None] * b_im + k_im[..., None] * b_re
    eye = jnp.eye(S5_GROUPS, dtype=F32)
    bd_in = lambda m: jnp.einsum('dgpc,gh->dgchp', m, eye).reshape(2, w, S5_LANES).astype(BF16)
    bd_out = lambda m: jnp.einsum('dgcp,gh->dgphc', m, eye).reshape(2, S5_LANES, w).astype(BF16)
    y2 = s5_scan(u.transpose(1, 0, 2).reshape(s * b, w), bd_in(bb_re), bd_in(bb_im), bd_out(c_re), bd_out(c_im),
                 abar_re.reshape(2, 1, S5_LANES), abar_im.reshape(2, 1, S5_LANES), b)
    return d_skip * u + (y2[0] + y2[1]).reshape(s, b, w).transpose(1, 0, 2)


def _rope_tables(rows, rot_dim, lead, period):
    axis_dim = rot_dim // 2
    inv = ROPE_THETA ** (-jnp.arange(0, axis_dim, 2, dtype=F32) / axis_dim)
    t = jnp.arange(rows * GRID_W)
    r = (t // GRID_W).astype(F32)[:, None] * inv
    c = (t % GRID_W).astype(F32)[:, None] * inv
    n = t.shape[0]
    tail = period - lead - rot_dim
    cos = jnp.concatenate([jnp.ones((n, lead), F32), jnp.cos(r), jnp.cos(r), jnp.cos(c), jnp.cos(c),
                           jnp.ones((n, tail), F32)], axis=-1)
    sin = jnp.concatenate([jnp.zeros((n, lead), F32), -jnp.sin(r), jnp.sin(r), -jnp.sin(c), jnp.sin(c),
                           jnp.zeros((n, tail), F32)], axis=-1)
    reps = V7X_LANES // period
    cos = jnp.concatenate([jnp.ones((CTX_LEN, V7X_LANES), F32), jnp.tile(cos, (1, reps))], axis=0)
    sin = jnp.concatenate([jnp.zeros((CTX_LEN, V7X_LANES), F32), jnp.tile(sin, (1, reps))], axis=0)
    return cos, sin


def _attend_all(q, k, v, need_ctx):
    o_lat = attention(q[:, CTX_LEN:], k, v)
    if need_ctx:
        o_ctx = attention(q[:, :CTX_LEN], k[:, :CTX_LEN], v[:, :CTX_LEN])
    else:
        o_ctx = jnp.zeros((q.shape[0], CTX_LEN, q.shape[2]), F32)
    return jnp.concatenate([o_ctx, o_lat], axis=1)


def _gqa(pa2, b, qn_g, kn_g, cos, sin, need_ctx):
    s = pa2.shape[0] // b
    grp = GQA_Q_HEADS // GQA_KV_HEADS
    gains = jnp.concatenate([jnp.tile(qn_g, GQA_Q_HEADS), jnp.tile(kn_g, GQA_KV_HEADS)])[None, :]
    q, k, v = gqa_prep(pa2, gains, cos, sin, s)
    o = _attend_all(q.reshape(b, s, -1), k.reshape(b, s, -1), v.reshape(b, s, -1), need_ctx)
    o = o.reshape(b, s, GQA_Q_HEADS, V7X_LANES)
    first = (jnp.arange(GQA_Q_HEADS) < grp)[None, None, :, None]
    o = jnp.where(first, o[..., :HEAD_DIM], o[..., HEAD_DIM:])
    return o.reshape(b, s, GQA_Q_HEADS * HEAD_DIM)


def _mla(p, qa_g, w_qup, kva_g, w_kvup, cos, sin, need_ctx):
    b, s, _ = p.shape
    r0 = MLA_Q_RANK
    r1 = MLA_Q_RANK + MLA_KV_RANK
    blocks = lambda w, lo, hi: jnp.pad(w.reshape(w.shape[0], MLA_HEADS, -1)[..., lo:hi],
                                       ((0, 0), (0, 0), (0, V7X_LANES - (hi - lo)))).reshape(w.shape[0], -1)
    w_q = blocks(w_qup, 0, MLA_NOPE + MLA_ROPE).astype(BF16)
    w_kv = jnp.concatenate([blocks(w_kvup, 0, MLA_NOPE), blocks(w_kvup, MLA_NOPE, MLA_NOPE + MLA_V)],
                           axis=1).astype(BF16)
    zero = jnp.zeros((1, 1, r0), F32)
    q = norm_mod_matmul(p[..., :r0].reshape(b * s, r0), qa_g[None, :], zero, zero, w_q, lambda i: 0)
    zero = jnp.zeros((1, 1, MLA_KV_RANK), F32)
    kv = norm_mod_matmul(p[..., r0:r1].reshape(b * s, MLA_KV_RANK), kva_g[None, :], zero, zero, w_kv, lambda i: 0)
    kr = jnp.pad(p[..., r1:r1 + MLA_ROPE].reshape(b * s, MLA_ROPE),
                 ((0, 0), (MLA_NOPE, V7X_LANES - MLA_NOPE - MLA_ROPE)))
    q, k, v = mla_prep(q, kv, kr, cos, sin, s)
    o = _attend_all(q.reshape(b, s, -1), k.reshape(b, s, -1), v.reshape(b, s, -1), need_ctx)
    return o.reshape(b, s, MLA_HEADS, V7X_LANES)[..., :MLA_V].reshape(b, s, MLA_HEADS * MLA_V)


def kernel(x, c, ctx, c_ctx, ada_w, ada_b, norm1_g, norm2_g, ev_w_in, ev_w_out, gqa_qn_g, gqa_kn_g,
           hg_lb_logits, hg_onorm_g, od_w_in, od_w_out, mla_qa_g, mla_w_qup, mla_kva_g, mla_w_kvup,
           s5_a_re, s5_a_im, s5_log_dt, s5_b_re, s5_b_im, s5_c_re, s5_c_im, s5_d, s5_w_glu, s5_b_glu,
           peer_wq, peer_keys, peer_u, peer_v, final_g):
    b, t, d = x.shape
    s = CTX_LEN + t
    depth = ada_w.shape[0]
    rows = t // GRID_W
    cos_g, sin_g = _rope_tables(rows, HEAD_DIM, 0, HEAD_DIM)
    cos_m, sin_m = _rope_tables(rows, MLA_ROPE, MLA_NOPE, V7X_LANES)
    lb_all = jnp.cumsum(jax.nn.softmax(hg_lb_logits, axis=1), axis=1)

    def mod_index(tile):
        per, nctx = s // tile, CTX_LEN // tile
        return lambda i: jnp.where(i % per < nctx, b, i // per)

    mi_row = mod_index(ROW_TILE)
    mi_peer = mod_index(PEER_TILE)

    xa = jnp.concatenate([ctx, x], axis=1).reshape(b * s, d)
    s_all = jnp.concatenate([jax.nn.silu(c), jax.nn.silu(c_ctx)[None, :]], axis=0)
    s_pad = jnp.concatenate([s_all, jnp.zeros((V7X_SUBLANES - (b + 1) % V7X_SUBLANES, d), F32)], axis=0)

    for layer in range(depth):
        need_ctx = layer < depth - 1
        j = layer // 2
        mod = matmul(s_pad, ada_w[layer].astype(BF16), tm=s_pad.shape[0])[:b + 1] + ada_b[layer]
        mod = [m[:, None, :] for m in jnp.split(mod, 6, axis=-1)]
        if layer % 2 == 0:
            pa2 = norm_mod_matmul(xa, norm1_g[layer][None, :], mod[0], mod[1], ev_w_in[j].astype(BF16), mi_row)
            ya = _gqa(pa2, b, gqa_qn_g[j], gqa_kn_g[j], cos_g, sin_g, need_ctx)
            yb = _hgrn2(pa2, b, lb_all[:, j], hg_onorm_g[j]).reshape(b, s, HG_W)
            w_out = ev_w_out[j]
        else:
            pa = norm_mod_matmul(xa, norm1_g[layer][None, :], mod[0], mod[1], od_w_in[j].astype(BF16), mi_row)
            pa = pa.reshape(b, s, -1)
            ya = _mla(pa[..., :MLA_IN], mla_qa_g[j], mla_w_qup[j], mla_kva_g[j], mla_w_kvup[j], cos_m, sin_m, need_ctx)
            y5 = _s5_core(pa[..., MLA_IN:], s5_a_re[j], s5_a_im[j], s5_log_dt[j], s5_b_re[j], s5_b_im[j],
                          s5_c_re[j], s5_c_im[j], s5_d[j])
            z = jax.nn.gelu(y5, approximate=False).reshape(b * s, S5_WIDTH)
            gl = matmul(z, s5_w_glu[j].astype(BF16)) + s5_b_glu[j]
            yb = (z * jax.nn.sigmoid(gl)).reshape(b, s, S5_WIDTH)
            w_out = od_w_out[j]
        y = jnp.concatenate([ya, yb], axis=-1).reshape(b * s, -1)
        xa = matmul_residual(y, w_out.astype(BF16), xa, mod[2], mi_row)
        out = peer(xa, norm2_g[layer][None, :], mod[3], mod[4],
                   peer_wq[layer], peer_keys[layer], peer_u[layer], peer_v[layer], mi_peer)
        xa = gated_add(xa, out, mod[5], mi_row)
    xl = xa.reshape(b, s, d)[:, CTX_LEN:].reshape(b * t, d)
    return rmsnorm_rows(xl, final_g[None, :]).reshape(b, t, d)
```

```python
import functools
import math

import jax
import jax.numpy as jnp
from jax import lax
from jax.experimental import pallas as pl
from jax.experimental.pallas import tpu as pltpu

F32 = jnp.float32
BF16 = jnp.bfloat16

D_MODEL = 1024
GRID_W = 64
CTX_LEN = 256
EPS = 1e-6
ROPE_THETA = 10000.0

MIX_HALF = D_MODEL // 2
HEAD_DIM = 64
GQA_Q_HEADS = MIX_HALF // HEAD_DIM
GQA_KV_HEADS = GQA_Q_HEADS // 4
GQA_IN = (GQA_Q_HEADS + 2 * GQA_KV_HEADS) * HEAD_DIM

HG_DK = 128
HG_DV = 128
HG_HEADS = MIX_HALF // HG_DV
HG_W = HG_HEADS * HG_DK

MLA_HEADS = MIX_HALF // HEAD_DIM
MLA_NOPE = 64
MLA_ROPE = 32
MLA_V = 64
MLA_Q_RANK = 384
MLA_KV_RANK = 256
MLA_IN = MLA_Q_RANK + MLA_KV_RANK + MLA_ROPE

S5_WIDTH = MIX_HALF
S5_GROUP = 16
S5_GROUPS = S5_WIDTH // S5_GROUP
S5_STATE = 64

PEER_HEADS = 8
PEER_KEYS = 128
N_EXPERTS = PEER_KEYS * PEER_KEYS
PEER_TOPK = 16
PEER_QDIM = 256
PEER_HALF = PEER_QDIM // 2
PEER_PAIRS = PEER_HEADS * PEER_TOPK

V7X_LANES = 128
V7X_SUBLANES = 8
V7X_VMEM_BYTES = 64 * 1024 * 1024
VMEM_LIMIT = V7X_VMEM_BYTES - 8 * 1024 * 1024

ROW_TILE = 256
PEER_TILE = 128
ATTN_Q_TILE = 1024
EXPERT_WORDS = D_MODEL // 2
EXPERT_ROWS = EXPERT_WORDS // V7X_LANES


def _cparams(*sem):
    return pltpu.CompilerParams(dimension_semantics=sem, vmem_limit_bytes=VMEM_LIMIT)


def _norm_mod(x, g, shift, scale):
    ms = jnp.mean(x * x, axis=-1, keepdims=True)
    h = (x * lax.rsqrt(ms + EPS)) * g
    return h * (1.0 + scale) + shift


def _nmm_kernel(x_ref, g_ref, sh_ref, sc_ref, w_ref, o_ref):
    h = _norm_mod(x_ref[...], g_ref[...], sh_ref[0], sc_ref[0])
    o_ref[...] = jnp.dot(h.astype(BF16), w_ref[...], preferred_element_type=F32)


def norm_mod_matmul(x, g, shift, scale, w, mod_index, tm=ROW_TILE):
    r, k = x.shape
    n = w.shape[1]
    assert r % tm == 0 and w.shape[0] == k
    return pl.pallas_call(
        _nmm_kernel,
        grid=(r // tm,),
        in_specs=[
            pl.BlockSpec((tm, k), lambda i: (i, 0)),
            pl.BlockSpec((1, k), lambda i: (0, 0)),
            pl.BlockSpec((1, 1, k), lambda i: (mod_index(i), 0, 0)),
            pl.BlockSpec((1, 1, k), lambda i: (mod_index(i), 0, 0)),
            pl.BlockSpec((k, n), lambda i: (0, 0)),
        ],
        out_specs=pl.BlockSpec((tm, n), lambda i: (i, 0)),
        out_shape=jax.ShapeDtypeStruct((r, n), F32),
        compiler_params=_cparams("parallel"),
        name="norm_mod_matmul",
    )(x, g, shift, scale, w)


def _mm_kernel(a_ref, w_ref, o_ref):
    o_ref[...] = jnp.dot(a_ref[...].astype(BF16), w_ref[...], preferred_element_type=F32)


def matmul(a, w, tm=ROW_TILE):
    r, k = a.shape
    n = w.shape[1]
    assert r % tm == 0
    return pl.pallas_call(
        _mm_kernel,
        grid=(r // tm,),
        in_specs=[pl.BlockSpec((tm, k), lambda i: (i, 0)), pl.BlockSpec((k, n), lambda i: (0, 0))],
        out_specs=pl.BlockSpec((tm, n), lambda i: (i, 0)),
        out_shape=jax.ShapeDtypeStruct((r, n), F32),
        compiler_params=_cparams("parallel"),
        name="matmul",
    )(a, w)


def _mmres_kernel(a_ref, w_ref, x_ref, gate_ref, o_ref):
    y = jnp.dot(a_ref[...].astype(BF16), w_ref[...], preferred_element_type=F32)
    o_ref[...] = x_ref[...] + gate_ref[0] * y


def matmul_residual(a, w, x, gate, mod_index, tm=ROW_TILE):
    r, k = a.shape
    n = w.shape[1]
    assert r % tm == 0
    return pl.pallas_call(
        _mmres_kernel,
        grid=(r // tm,),
        in_specs=[
            pl.BlockSpec((tm, k), lambda i: (i, 0)),
            pl.BlockSpec((k, n), lambda i: (0, 0)),
            pl.BlockSpec((tm, n), lambda i: (i, 0)),
            pl.BlockSpec((1, 1, n), lambda i: (mod_index(i), 0, 0)),
        ],
        out_specs=pl.BlockSpec((tm, n), lambda i: (i, 0)),
        out_shape=jax.ShapeDtypeStruct((r, n), F32),
        compiler_params=_cparams("parallel"),
        name="matmul_residual",
    )(a, w, x, gate)


def _rms_kernel(x_ref, g_ref, o_ref):
    x = x_ref[...]
    ms = jnp.mean(x * x, axis=-1, keepdims=True)
    o_ref[...] = (x * lax.rsqrt(ms + EPS)) * g_ref[...]


def rmsnorm_rows(x, g, tm=ROW_TILE):
    r, k = x.shape
    return pl.pallas_call(
        _rms_kernel,
        grid=(r // tm,),
        in_specs=[pl.BlockSpec((tm, k), lambda i: (i, 0)), pl.BlockSpec((1, k), lambda i: (0, 0))],
        out_specs=pl.BlockSpec((tm, k), lambda i: (i, 0)),
        out_shape=jax.ShapeDtypeStruct((r, k), F32),
        compiler_params=_cparams("parallel"),
        name="rmsnorm_rows",
    )(x, g)


ATTN_KV_CHUNKS = 3
V7X_MXU_DIM = 256
LOG2E = math.log2(math.e)


def _attn_kernel(q_ref, k_ref, v_ref, o_ref):
    q = q_ref[...]
    sk = k_ref.shape[0]
    nch = ATTN_KV_CHUNKS if sk % (ATTN_KV_CHUNKS * V7X_MXU_DIM) == 0 else 1
    ck = sk // nch
    m = l = acc = None
    for c in range(nch):
        kc = k_ref[c * ck:(c + 1) * ck, :]
        vc = v_ref[c * ck:(c + 1) * ck, :]
        s = lax.dot_general(q, kc, (((1,), (1,)), ((), ())), preferred_element_type=F32)
        mc = jnp.max(s, axis=-1, keepdims=True)
        if c == 0:
            m = mc
            p = jnp.exp2(s - m)
            l = jnp.sum(p, axis=-1, keepdims=True)
            acc = jnp.dot(p.astype(BF16), vc, preferred_element_type=F32)
        else:
            m_new = jnp.maximum(m, mc)
            a = jnp.exp2(m - m_new)
            p = jnp.exp2(s - m_new)
            l = a * l + jnp.sum(p, axis=-1, keepdims=True)
            acc = a * acc + jnp.dot(p.astype(BF16), vc, preferred_element_type=F32)
            m = m_new
    o_ref[...] = acc / l


def attention(q, k, v, tq=ATTN_Q_TILE):
    b, sq, hl = q.shape
    sk, hkl = k.shape[1], k.shape[2]
    h, hk = hl // V7X_LANES, hkl // V7X_LANES
    grp = h // hk
    tq = min(tq, sq)
    assert sq % tq == 0
    return pl.pallas_call(
        _attn_kernel,
        grid=(b, h, sq // tq),
        in_specs=[
            pl.BlockSpec((None, tq, V7X_LANES), lambda bi, hi, qi: (bi, qi, hi)),
            pl.BlockSpec((None, sk, V7X_LANES), lambda bi, hi, qi: (bi, 0, hi // grp)),
            pl.BlockSpec((None, sk, V7X_LANES), lambda bi, hi, qi: (bi, 0, hi // grp)),
        ],
        out_specs=pl.BlockSpec((None, tq, V7X_LANES), lambda bi, hi, qi: (bi, qi, hi)),
        out_shape=jax.ShapeDtypeStruct((b, sq, hl), F32),
        compiler_params=_cparams("parallel", "parallel", "parallel"),
        name="attention",
    )(q, k, v)


def _rotate(x, cos, sin, quarter):
    n = x.shape[1]
    lane = lax.broadcasted_iota(jnp.int32, x.shape, 1)
    partner = jnp.where(lane % (2 * quarter) < quarter, pltpu.roll(x, n - quarter, axis=1),
                        pltpu.roll(x, quarter, axis=1))
    return x * cos + partner * sin


def _gqa_prep_kernel(p_ref, g_ref, cos_ref, sin_ref, avg_ref, q_ref, k_ref, v_ref):
    cos, sin = cos_ref[...], sin_ref[...]
    low = lax.broadcasted_iota(jnp.int32, cos.shape, 1) < HEAD_DIM
    qk_blocks = (GQA_Q_HEADS + GQA_KV_HEADS) * HEAD_DIM // V7X_LANES
    q_blocks = GQA_Q_HEADS * HEAD_DIM // V7X_LANES
    for c in range(qk_blocks):
        lanes = slice(c * V7X_LANES, (c + 1) * V7X_LANES)
        x = p_ref[:, lanes]
        ms = jnp.dot(x * x, avg_ref[...], precision=lax.Precision.HIGHEST, preferred_element_type=F32)
        y = _rotate(x * lax.rsqrt(ms + EPS) * g_ref[:, lanes], cos, sin, HEAD_DIM // 4)
        if c < q_blocks:
            y = y * (HEAD_DIM ** -0.5 * LOG2E)
            other = pltpu.roll(y, HEAD_DIM, axis=1)
            if c < q_blocks // 2:
                first, second = jnp.where(low, y, 0.0), jnp.where(low, other, 0.0)
            else:
                first, second = jnp.where(low, 0.0, other), jnp.where(low, 0.0, y)
            q_ref[:, 2 * c * V7X_LANES:(2 * c + 1) * V7X_LANES] = first.astype(BF16)
            q_ref[:, (2 * c + 1) * V7X_LANES:(2 * c + 2) * V7X_LANES] = second.astype(BF16)
        else:
            k_ref[...] = y.astype(BF16)
    v_ref[...] = p_ref[:, qk_blocks * V7X_LANES:(qk_blocks + 1) * V7X_LANES].astype(BF16)


def gqa_prep(pa, gains, cos, sin, seq, tm=ROW_TILE):
    r = pa.shape[0]
    per = seq // tm
    avg = jnp.kron(jnp.eye(V7X_LANES // HEAD_DIM, dtype=F32), jnp.full((HEAD_DIM, HEAD_DIM), 1.0 / HEAD_DIM, F32))
    tab = pl.BlockSpec((tm, V7X_LANES), lambda i: (i % per, 0))
    kv = pl.BlockSpec((tm, V7X_LANES), lambda i: (i, 0))
    return pl.pallas_call(
        _gqa_prep_kernel,
        grid=(r // tm,),
        in_specs=[pl.BlockSpec((tm, GQA_IN), lambda i: (i, 0)), pl.BlockSpec(gains.shape, lambda i: (0, 0)), tab, tab,
                  pl.BlockSpec(avg.shape, lambda i: (0, 0))],
        out_specs=[pl.BlockSpec((tm, GQA_Q_HEADS * V7X_LANES), lambda i: (i, 0)), kv, kv],
        out_shape=[jax.ShapeDtypeStruct((r, GQA_Q_HEADS * V7X_LANES), BF16),
                   jax.ShapeDtypeStruct((r, V7X_LANES), BF16), jax.ShapeDtypeStruct((r, V7X_LANES), BF16)],
        compiler_params=_cparams("parallel"),
        name="gqa_prep",
    )(pa, gains, cos, sin, avg)


def _mla_prep_kernel(q_ref, kv_ref, kr_ref, cos_ref, sin_ref, qo_ref, ko_ref, vo_ref):
    cos, sin = cos_ref[...], sin_ref[...]
    quarter = MLA_ROPE // 4
    kr = _rotate(kr_ref[...], cos, sin, quarter)
    scale = (MLA_NOPE + MLA_ROPE) ** -0.5 * LOG2E
    width = MLA_HEADS * V7X_LANES
    for h in range(MLA_HEADS):
        lanes = slice(h * V7X_LANES, (h + 1) * V7X_LANES)
        qo_ref[:, lanes] = (_rotate(q_ref[:, lanes], cos, sin, quarter) * scale).astype(BF16)
        ko_ref[:, lanes] = (kv_ref[:, lanes] + kr).astype(BF16)
        vo_ref[:, lanes] = kv_ref[:, width + h * V7X_LANES:width + (h + 1) * V7X_LANES].astype(BF16)


def mla_prep(q, kv, kr, cos, sin, seq, tm=ROW_TILE):
    r, width = q.shape
    per = seq // tm
    tab = pl.BlockSpec((tm, V7X_LANES), lambda i: (i % per, 0))
    out = pl.BlockSpec((tm, width), lambda i: (i, 0))
    return pl.pallas_call(
        _mla_prep_kernel,
        grid=(r // tm,),
        in_specs=[out, pl.BlockSpec((tm, 2 * width), lambda i: (i, 0)), pl.BlockSpec((tm, V7X_LANES), lambda i: (i, 0)),
                  tab, tab],
        out_specs=[out, out, out],
        out_shape=[jax.ShapeDtypeStruct((r, width), BF16)] * 3,
        compiler_params=_cparams("parallel"),
        name="mla_prep",
    )(q, kv, kr, cos, sin)


def _topk_rows(s, iota, k):
    n = s.shape[0]
    row = lax.broadcasted_iota(jnp.int32, (k, s.shape[1]), 0)
    vals = jnp.zeros((k, s.shape[1]), F32)
    ids = jnp.zeros((k, s.shape[1]), F32)
    for r in range(k):
        m = jnp.max(s, axis=0, keepdims=True)
        am = jnp.min(jnp.where(s == m, iota, float(n)), axis=0, keepdims=True)
        vals = jnp.where(row == r, m, vals)
        ids = jnp.where(row == r, am, ids)
        s = jnp.where(iota == am, -jnp.inf, s)
    return vals, ids


def pack_rows_bf16(tab):
    e, d = tab.shape
    t16 = lax.bitcast_convert_type(tab.astype(BF16), jnp.uint16).astype(jnp.uint32)
    word = t16[:, :d // 2] | (t16[:, d // 2:] << 16)
    return lax.bitcast_convert_type(word, jnp.int32).reshape(e * EXPERT_ROWS, V7X_LANES)


def _slab_value_row(r):
    return r // 2 + EXPERT_ROWS * (r % 2)


PEER_GROUP = V7X_SUBLANES
SLAB_ROWS_BF16 = 2 * EXPERT_ROWS
PAIR_LANES = PEER_PAIRS * SLAB_ROWS_BF16


def _slab(tab_ref, row):
    return tab_ref[pl.ds(pl.multiple_of(row, EXPERT_ROWS), EXPERT_ROWS), :]


def _token_rows_bf16(h8, i):
    row = lax.broadcasted_iota(jnp.int32, (SLAB_ROWS_BF16, V7X_LANES), 0)
    hq = jnp.zeros((SLAB_ROWS_BF16, V7X_LANES), F32)
    for r in range(SLAB_ROWS_BF16):
        v = _slab_value_row(r)
        hq = jnp.where(row == r, h8[i:i + 1, v * V7X_LANES:(v + 1) * V7X_LANES], hq)
    return hq.astype(BF16)


def _peer_v_kernel(idx_ref, w_ref, x_ref, gate_ref, tab_ref, spread_ref, o_ref, g_ref):
    groups = w_ref.shape[0] // PEER_GROUP
    last_slot = PEER_GROUP - 1
    shape = (SLAB_ROWS_BF16, PAIR_LANES)
    out_row = lax.broadcasted_iota(jnp.int32, shape, 0)
    diag = (lax.broadcasted_iota(jnp.int32, shape, 1) % SLAB_ROWS_BF16) == 2 * (out_row % EXPERT_ROWS) + out_row // EXPERT_ROWS
    slot_rows = PEER_PAIRS * EXPERT_ROWS
    gate = gate_ref[0]

    @pl.when(pl.program_id(0) == 0)
    def _():
        g_ref[pl.ds(last_slot * slot_rows, slot_rows), :] = jnp.zeros((slot_rows, V7X_LANES), jnp.int32)

    row8 = lax.broadcasted_iota(jnp.int32, (PEER_GROUP, V7X_LANES), 0)
    blocks = range(SLAB_ROWS_BF16)

    def contract(slot, wrow, tile):
        rows = pltpu.bitcast(g_ref[pl.ds(slot * slot_rows, slot_rows), :], BF16)
        wi = jnp.where(diag, jnp.broadcast_to(wrow, shape), 0.0).astype(BF16)
        out = jnp.dot(wi, rows, preferred_element_type=F32)
        return [jnp.where(row8 == slot, jnp.broadcast_to(out[v:v + 1, :], row8.shape), tile[v]) for v in blocks]

    def flush(g, tile):
        rows = pl.ds(pl.multiple_of(g * PEER_GROUP, PEER_GROUP), PEER_GROUP)
        for v in blocks:
            lanes = slice(v * V7X_LANES, (v + 1) * V7X_LANES)
            o_ref[rows, lanes] = x_ref[rows, lanes] + gate[:, lanes] * tile[v]

    def group(g, carry):
        prev_wide, prev_tile = carry
        flush(jnp.maximum(g - 1, 0), contract(last_slot, prev_wide[last_slot:], list(prev_tile)))
        w8 = w_ref[pl.ds(pl.multiple_of(g * PEER_GROUP, PEER_GROUP), PEER_GROUP), :]
        wide = jnp.dot(w8.astype(BF16), spread_ref[...], preferred_element_type=F32)
        tile = [jnp.zeros(row8.shape, F32) for _ in blocks]
        for i in range(PEER_GROUP):
            tok_idx = idx_ref.at[pl.ds((g * PEER_GROUP + i) * PEER_PAIRS, PEER_PAIRS)]
            for j in range(PEER_PAIRS):
                g_ref[pl.ds((i * PEER_PAIRS + j) * EXPERT_ROWS, EXPERT_ROWS), :] = _slab(tab_ref, tok_idx[j])
            if i < last_slot:
                tile = contract(i, wide[i:i + 1], tile)
        return wide, tuple(tile)

    init = (jnp.zeros((PEER_GROUP, PAIR_LANES), F32), tuple(jnp.zeros(row8.shape, F32) for _ in blocks))
    wide, tile = lax.fori_loop(0, groups, group, init)
    flush(groups - 1, contract(last_slot, wide[last_slot:], list(tile)))


def peer_expert_mix(idx4, w, x, gate, tab, mod_index, tb=PEER_TILE):
    r, d = x.shape
    spread = jnp.repeat(jnp.eye(PEER_PAIRS, dtype=BF16), SLAB_ROWS_BF16, axis=1)
    return pl.pallas_call(
        _peer_v_kernel,
        grid=(r // tb,),
        in_specs=[
            pl.BlockSpec((tb * PEER_PAIRS,), lambda i: (i,), memory_space=pltpu.SMEM),
            pl.BlockSpec((tb, PEER_PAIRS), lambda i: (i, 0)),
            pl.BlockSpec((tb, d), lambda i: (i, 0)),
            pl.BlockSpec((1, 1, d), lambda i: (mod_index(i), 0, 0)),
            pl.BlockSpec(tab.shape, lambda i: (0, 0), pipeline_mode=pl.Buffered(1)),
            pl.BlockSpec((PEER_PAIRS, PAIR_LANES), lambda i: (0, 0)),
        ],
        out_specs=pl.BlockSpec((tb, d), lambda i: (i, 0)),
        out_shape=jax.ShapeDtypeStruct((r, d), F32),
        scratch_shapes=[pltpu.VMEM((PEER_GROUP * PEER_PAIRS * EXPERT_ROWS, V7X_LANES), jnp.int32)],
        compiler_params=_cparams("arbitrary"),
        name="peer_expert_mix",
    )(idx4, w, x, gate, tab, spread)


def _peer_front_kernel(x_ref, g_ref, sh_ref, sc_ref, wq_ref, keys_ref, tab_ref, fold_ref, idx_out_ref, w_out_ref,
                       hb_ref, e_scr, w_scr, h_buf, gw_buf, idx_vmem, idx_smem, s_ref, sem):
    step = pl.program_id(0)
    wslot = step % 2
    rslot = 1 - wslot
    tb = x_ref.shape[0]

    @pl.when(step == 0)
    def _():
        h_buf[1] = jnp.zeros(h_buf.shape[1:], F32)
        gw_buf[1] = jnp.zeros(gw_buf.shape[1:], F32)
        idx_vmem[...] = jnp.zeros(idx_vmem.shape, jnp.int32)
        fill = pltpu.make_async_copy(idx_vmem, idx_smem.at[1], sem)
        fill.start()
        fill.wait()

    h = _norm_mod(x_ref[...], g_ref[...], sh_ref[0], sc_ref[0])
    h_buf[wslot] = h
    hb_ref[...] = h.astype(BF16)
    iota_n = lax.broadcasted_iota(jnp.int32, (PEER_KEYS, tb), 0).astype(F32)
    half = PEER_TOPK // 2
    sub = lambda n: lax.broadcasted_iota(jnp.int32, (n, tb), 0).astype(F32)
    pos_c = jnp.concatenate([sub(PEER_TOPK)] + [sub(half) + float(a * PEER_TOPK) for a in range(1, half)]
                            + [(sub(half) + float(half)) * float(PEER_TOPK)], axis=0)

    def pair_up(first, second, scale):
        return jnp.concatenate([first[0:1] * scale + second]
                               + [first[a:a + 1] * scale + second[0:half] for a in range(1, half)]
                               + [first[half:] * scale + second[0:1]], axis=0)

    def retrieve_stages(hd):
        q = jnp.dot(hb_ref[...], wq_ref[hd], preferred_element_type=F32)
        tops = []

        def first_stage(p):
            qp = q[:, p * PEER_HALF:(p + 1) * PEER_HALF].astype(BF16)
            s = lax.dot_general(keys_ref[hd, p], qp, (((1,), (1,)), ((), ())),
                                preferred_element_type=F32)
            tops.append(_topk_rows(s, iota_n, PEER_TOPK))

        return [functools.partial(first_stage, 0), functools.partial(first_stage, 1),
                functools.partial(second_stage, hd, tops)]

    def second_stage(hd, tops):
        (s1, i1), (s2, i2) = tops
        cand = pair_up(s1, s2, 1.0)
        cidx = pair_up(i1, i2, float(PEER_KEYS))
        row = lax.broadcasted_iota(jnp.int32, (PEER_TOPK, tb), 0)
        sc = jnp.zeros((PEER_TOPK, tb), F32)
        ex = jnp.zeros((PEER_TOPK, tb), F32)
        for r in range(PEER_TOPK):
            m = jnp.max(cand, axis=0, keepdims=True)
            am = jnp.min(jnp.where(cand == m, pos_c, float(PEER_TOPK * PEER_TOPK)), axis=0, keepdims=True)
            hit = pos_c == am
            e = jnp.max(jnp.where(hit, cidx, 0.0), axis=0, keepdims=True)
            sc = jnp.where(row == r, m, sc)
            ex = jnp.where(row == r, e, ex)
            cand = jnp.where(hit, -jnp.inf, cand)
        pexp = jnp.exp(sc - sc[0:1])
        rows = pl.ds(pl.multiple_of(hd * PEER_TOPK, PEER_TOPK), PEER_TOPK)
        w_scr[rows, :] = pexp / jnp.sum(pexp, axis=0, keepdims=True)
        e_scr[rows, :] = ex

    row8 = lax.broadcasted_iota(jnp.int32, (PEER_GROUP, V7X_LANES), 0)
    h_prev, gw_prev, idx_prev = h_buf.at[rslot], gw_buf.at[rslot], idx_smem.at[rslot]

    def finish(g, sums):
        hi = sums.astype(BF16)
        lo = (sums - hi.astype(F32)).astype(BF16)
        a = (jnp.dot(hi, fold_ref[...], preferred_element_type=F32)
             + jnp.dot(lo, fold_ref[...], preferred_element_type=F32))
        rows = pl.ds(pl.multiple_of(g * PEER_GROUP, PEER_GROUP), PEER_GROUP)
        gelu = 0.5 * a * (1.0 + lax.erf(a * (2.0 ** -0.5)))
        w_out_ref[rows, :] = gelu * gw_prev[rows, :]

    def gate_group(g, prev, before_token):
        finish(jnp.maximum(g - 1, 0), prev)
        sums = jnp.zeros((PEER_GROUP, PAIR_LANES), F32)
        h8 = h_prev[pl.ds(pl.multiple_of(g * PEER_GROUP, PEER_GROUP), PEER_GROUP), :]
        for i in range(PEER_GROUP):
            if i in before_token:
                before_token[i]()
            hb = _token_rows_bf16(h8, i)
            tok_idx = idx_prev.at[g * PEER_GROUP + i]
            for j in range(PEER_PAIRS):
                u = pltpu.bitcast(_slab(tab_ref, tok_idx[j]), BF16)
                s_ref[pl.ds((i * PEER_PAIRS + j) * EXPERT_ROWS, EXPERT_ROWS), :] = pltpu.bitcast(u * hb, jnp.int32)
            prod = pltpu.bitcast(s_ref[pl.ds(i * PEER_PAIRS * EXPERT_ROWS, PEER_PAIRS * EXPERT_ROWS), :], BF16)
            pick = jnp.where(row8 == i, 1.0, 0.0).astype(BF16)
            sums = sums + lax.dot_general(pick, prod, (((1,), (1,)), ((), ())), preferred_element_type=F32)
        return sums

    groups_per_head = tb // PEER_GROUP // PEER_HEADS

    def body(hd, sums):
        first_a, first_b, second = retrieve_stages(hd)
        placement = [{0: first_a, PEER_GROUP // 2: first_b}, {0: second}]
        for k in range(groups_per_head):
            sums = gate_group(hd * groups_per_head + k, sums, placement[k] if k < len(placement) else {})
        return sums

    last = lax.fori_loop(0, PEER_HEADS, body, jnp.zeros((PEER_GROUP, PAIR_LANES), F32))
    finish(tb // PEER_GROUP - 1, last)
    gw_buf[wslot] = w_scr[...].T
    idx = (e_scr[...].T * float(EXPERT_ROWS)).astype(jnp.int32)
    idx_out_ref[...] = idx
    idx_vmem[...] = idx
    handoff = pltpu.make_async_copy(idx_vmem, idx_smem.at[wslot], sem)
    handoff.start()
    handoff.wait()


def peer_front(x, g, shift, scale, wq_heads, keys, tab, mod_index, tb=PEER_TILE):
    r, d = x.shape
    nb = r // tb
    fold = jnp.repeat(jnp.eye(PEER_PAIRS, dtype=BF16), SLAB_ROWS_BF16, axis=0)
    cur = lambda i: jnp.minimum(i, nb - 1)
    const = lambda shape: pl.BlockSpec(shape, lambda i: (0,) * len(shape), pipeline_mode=pl.Buffered(1))
    return pl.pallas_call(
        _peer_front_kernel,
        grid=(nb + 1,),
        in_specs=[
            pl.BlockSpec((tb, d), lambda i: (cur(i), 0)),
            pl.BlockSpec((1, d), lambda i: (0, 0)),
            pl.BlockSpec((1, 1, d), lambda i: (mod_index(cur(i)), 0, 0)),
            pl.BlockSpec((1, 1, d), lambda i: (mod_index(cur(i)), 0, 0)),
            const((PEER_HEADS, d, PEER_QDIM)),
            const((PEER_HEADS, 2, PEER_KEYS, PEER_HALF)),
            const(tab.shape),
            const((PAIR_LANES, PEER_PAIRS)),
        ],
        out_specs=[
            pl.BlockSpec((tb, PEER_PAIRS), lambda i: (cur(i), 0)),
            pl.BlockSpec((tb, PEER_PAIRS), lambda i: (jnp.maximum(i - 1, 0), 0)),
        ],
        out_shape=[jax.ShapeDtypeStruct((r, PEER_PAIRS), jnp.int32), jax.ShapeDtypeStruct((r, PEER_PAIRS), F32)],
        scratch_shapes=[
            pltpu.VMEM((tb, d), BF16), pltpu.VMEM((PEER_PAIRS, tb), F32), pltpu.VMEM((PEER_PAIRS, tb), F32),
            pltpu.VMEM((2, tb, d), F32), pltpu.VMEM((2, tb, PEER_PAIRS), F32),
            pltpu.VMEM((tb, PEER_PAIRS), jnp.int32), pltpu.SMEM((2, tb, PEER_PAIRS), jnp.int32),
            pltpu.VMEM((PEER_GROUP * PEER_PAIRS * EXPERT_ROWS, V7X_LANES), jnp.int32),
            pltpu.SemaphoreType.DMA,
        ],
        compiler_params=_cparams("arbitrary"),
        name="peer_front",
    )(x, g, shift, scale, wq_heads, keys, tab, fold)


def peer(x, g, shift, scale, gate, wq, keys, u_tab, v_tab, mod_index_peer):
    r, d = x.shape
    wq_heads = wq.astype(BF16).reshape(d, PEER_HEADS, PEER_QDIM).transpose(1, 0, 2)
    idx4, w = peer_front(x, g, shift, scale, wq_heads, keys.astype(BF16), pack_rows_bf16(u_tab), mod_index_peer)
    return peer_expert_mix(idx4.reshape(r * PEER_PAIRS), w, x, gate, pack_rows_bf16(v_tab), mod_index_peer)


HG_TILE = 128
HG_SUB = 16
HG_PAIR = 2 * HG_DK


def _hgrn_kernel(q_ref, f_ref, v_ref, lb_ref, o_ref, st_ref, *, rev):
    @pl.when(pl.program_id(2) == 0)
    def _():
        st_ref[...] = jnp.zeros_like(st_ref)

    n = q_ref.shape[0]
    lb = lb_ref[...]
    f = lb + (1.0 - lb) * jax.nn.sigmoid(f_ref[...])
    kk = 1.0 - f
    lf = jnp.log(f)
    t = lax.broadcasted_iota(jnp.int32, (n, n), 0)
    s = lax.broadcasted_iota(jnp.int32, (n, n), 1)
    same = (t // HG_SUB) == (s // HG_SUB)
    mid = (t // HG_SUB) * HG_SUB + (HG_SUB // 2 if rev else HG_SUB // 2 - 1)
    seen = (s >= t) if rev else (s <= t)
    seen_mid = (s >= mid) if rev else (s <= mid)
    one = lambda m: jnp.where(m, 1.0, 0.0).astype(F32)
    hp = lambda a, b: jnp.dot(a, b, precision=lax.Precision.HIGHEST, preferred_element_type=F32)
    cum = hp(one(same & seen), lf)
    ref = hp(one(same & seen_mid), lf)
    last = hp(one(same), lf)
    q = q_ref[...]
    qe = (q * jnp.exp(cum)).astype(BF16)
    qm = (q * jnp.exp(cum - ref)).astype(BF16)
    km = (kk * jnp.exp(ref - cum)).astype(BF16)
    kd = (kk * jnp.exp(last - cum)).astype(BF16)
    dec = jnp.exp(last)
    vb = v_ref[...].astype(BF16)
    steps = range(n // HG_SUB)
    heads = range(HG_PAIR // HG_DK)
    intra = []
    for h in heads:
        cols = slice(h * HG_DK, (h + 1) * HG_DK)
        att = lax.dot_general(qm[:, cols], km[:, cols], (((1,), (1,)), ((), ())), preferred_element_type=F32)
        att = jnp.where(same & seen, att, 0.0).astype(BF16)
        intra.append(jnp.dot(att, vb[:, cols], preferred_element_type=F32))
    for c in (reversed(steps) if rev else steps):
        rows = slice(c * HG_SUB, (c + 1) * HG_SUB)
        for h in heads:
            cols = slice(h * HG_DK, (h + 1) * HG_DK)
            st = st_ref[h]
            o_ref[rows, cols] = intra[h][rows] + lax.dot_general(
                qe[rows, cols], st.astype(BF16), (((1,), (1,)), ((), ())), preferred_element_type=F32)
            upd = lax.dot_general(vb[rows, cols], kd[rows, cols], (((0,), (0,)), ((), ())),
                                  preferred_element_type=F32)
            st_ref[h] = st * dec[c * HG_SUB:c * HG_SUB + 1, cols] + upd


def hgrn_scan(pa, lb, rev, batch):
    rows = pa.shape[0]
    per = rows // batch // HG_TILE
    nctx = CTX_LEN // HG_TILE
    col0 = GQA_IN // HG_PAIR
    blocks = HG_W // HG_PAIR

    def tok(b, k):
        if rev:
            k = jnp.where(k < nctx, nctx - 1 - k, per - 1 - (k - nctx))
        return b * per + k

    spec = lambda cb: pl.BlockSpec((HG_TILE, HG_PAIR), lambda b, p, k: (tok(b, k), cb + p))
    d = 1 if rev else 0
    return pl.pallas_call(
        functools.partial(_hgrn_kernel, rev=rev),
        grid=(batch, blocks, per),
        in_specs=[spec(col0), spec(col0 + (1 + d) * blocks), spec(col0 + 3 * blocks),
                  pl.BlockSpec((None, 1, HG_PAIR), lambda b, p, k: (d, 0, p))],
        out_specs=pl.BlockSpec((HG_TILE, HG_PAIR), lambda b, p, k: (tok(b, k), p)),
        out_shape=jax.ShapeDtypeStruct((rows, HG_W), F32),
        scratch_shapes=[pltpu.VMEM((HG_PAIR // HG_DK, HG_DV, HG_DK), F32)],
        compiler_params=_cparams("parallel", "parallel", "arbitrary"),
        name="hgrn_scan_rev" if rev else "hgrn_scan_fwd",
    )(pa, pa, pa, lb)


def _rms_norm(x, g):
    xf = x.astype(F32)
    y = xf * lax.rsqrt(jnp.mean(xf * xf, axis=-1, keepdims=True) + EPS)
    return y * g.astype(F32)


def _hgrn2(pa2, batch, lb, onorm_g):
    lb = lb.reshape(2, 1, HG_W)
    o = hgrn_scan(pa2, lb, False, batch) + hgrn_scan(pa2, lb, True, batch)
    o = _rms_norm(o.reshape(-1, HG_HEADS, HG_DV), onorm_g).reshape(-1, HG_W)
    return o * jax.nn.silu(pa2[:, GQA_IN + 4 * HG_W:])


S5_CHUNK = 64
S5_LANES = S5_GROUPS * S5_STATE


def _s5_kernel(u_ref, bre_ref, bim_ref, cre_ref, cim_ref, are_ref, aim_ref, y_ref, xre, xim, hre, him):
    nb = V7X_SUBLANES // 2
    tiles = u_ref.shape[0] // V7X_SUBLANES
    direction = pl.program_id(0)
    rev = direction == 1

    @pl.when(pl.program_id(1) == 0)
    def _():
        hre[...] = jnp.zeros_like(hre)
        him[...] = jnp.zeros_like(him)

    ub = u_ref[...].astype(BF16)
    xre[...] = jnp.dot(ub, bre_ref[...], preferred_element_type=F32)
    xim[...] = jnp.dot(ub, bim_ref[...], preferred_element_type=F32)
    shape = (V7X_SUBLANES, S5_LANES)
    second = (lax.broadcasted_iota(jnp.int32, shape, 0) // nb) != direction
    are = jnp.broadcast_to(are_ref[...], shape)
    aim = jnp.broadcast_to(aim_ref[...], shape)
    cre = jnp.where(second, are * are - aim * aim, are)
    cim = jnp.where(second, 2.0 * are * aim, aim)

    def step(k, carry):
        pr, pi = carry
        k = jnp.where(rev, tiles - 1 - k, k)
        r = pl.ds(pl.multiple_of(k * V7X_SUBLANES, V7X_SUBLANES), V7X_SUBLANES)
        xr, xi = xre[r, :], xim[r, :]
        sr = jnp.where(second, pltpu.roll(xr, nb, axis=0), 0.0)
        si = jnp.where(second, pltpu.roll(xi, nb, axis=0), 0.0)
        nr = xr + (are * sr - aim * si) + (cre * pr - cim * pi)
        ni = xi + (are * si + aim * sr) + (cre * pi + cim * pr)
        xre[r, :] = nr
        xim[r, :] = ni
        return (jnp.where(second, nr, pltpu.roll(nr, nb, axis=0)),
                jnp.where(second, ni, pltpu.roll(ni, nb, axis=0)))

    hr, hi = lax.fori_loop(0, tiles, step, (hre[...], him[...]))
    hre[...] = hr
    him[...] = hi
    y_ref[...] = (jnp.dot(xre[...].astype(BF16), cre_ref[...], preferred_element_type=F32)
                  - jnp.dot(xim[...].astype(BF16), cim_ref[...], preferred_element_type=F32))


def s5_scan(u, bre, bim, cre, cim, are, aim, nb):
    rows, w = u.shape
    blk = S5_CHUNK * nb
    assert rows % blk == 0 and 2 * nb == V7X_SUBLANES
    nblk, nctx = rows // blk, CTX_LEN // S5_CHUNK
    wspec = lambda shape: pl.BlockSpec((None,) + shape, lambda d, i: (d, 0, 0))

    def chunk(d, i):
        back = jnp.where(i < nctx, nctx - 1 - i, nblk - 1 - (i - nctx))
        return jnp.where(d == 1, back, i)
    return pl.pallas_call(
        _s5_kernel,
        grid=(2, rows // blk),
        in_specs=[
            pl.BlockSpec((blk, w), lambda d, i: (chunk(d, i), 0)),
            wspec((w, S5_LANES)), wspec((w, S5_LANES)), wspec((S5_LANES, w)), wspec((S5_LANES, w)),
            wspec((1, S5_LANES)), wspec((1, S5_LANES)),
        ],
        out_specs=pl.BlockSpec((None, blk, w), lambda d, i: (d, chunk(d, i), 0)),
        out_shape=jax.ShapeDtypeStruct((2, rows, w), F32),
        scratch_shapes=[pltpu.VMEM((blk, S5_LANES), F32), pltpu.VMEM((blk, S5_LANES), F32),
                        pltpu.VMEM((V7X_SUBLANES, S5_LANES), F32), pltpu.VMEM((V7X_SUBLANES, S5_LANES), F32)],
        compiler_params=_cparams("arbitrary", "arbitrary"),
        name="s5_scan",
    )(u, bre, bim, cre, cim, are, aim)


def _s5_core(u, a_re, a_im, log_dt, b_re, b_im, c_re, c_im, d_skip):
    b, s, w = u.shape
    dt = jnp.exp(log_dt)[..., None]
    mag = jnp.exp(a_re * dt)
    abar_re, abar_im = mag * jnp.cos(a_im * dt), mag * jnp.sin(a_im * dt)
    den = a_re * a_re + a_im * a_im
    k_re = ((abar_re - 1.0) * a_re + abar_im * a_im) / den
    k_im = (abar_im * a_re - (abar_re - 1.0) * a_im) / den
    bb_re = k_re[..., None] * b_re - k_im[..., None] * b_im
    bb_im = k_re[..., None] * b_im + k_im[..., None] * b_re
    eye = jnp.eye(S5_GROUPS, dtype=F32)
    bd_in = lambda m: jnp.einsum('dgpc,gh->dgchp', m, eye).reshape(2, w, S5_LANES).astype(BF16)
    bd_out = lambda m: jnp.einsum('dgcp,gh->dgphc', m, eye).reshape(2, S5_LANES, w).astype(BF16)
    y2 = s5_scan(u.transpose(1, 0, 2).reshape(s * b, w), bd_in(bb_re), bd_in(bb_im), bd_out(c_re), bd_out(c_im),
                 abar_re.reshape(2, 1, S5_LANES), abar_im.reshape(2, 1, S5_LANES), b)
    return d_skip * u + (y2[0] + y2[1]).reshape(s, b, w).transpose(1, 0, 2)


def _rope_tables(rows, rot_dim, lead, period):
    axis_dim = rot_dim // 2
    inv = ROPE_THETA ** (-jnp.arange(0, axis_dim, 2, dtype=F32) / axis_dim)
    t = jnp.arange(rows * GRID_W)
    r = (t // GRID_W).astype(F32)[:, None] * inv
    c = (t % GRID_W).astype(F32)[:, None] * inv
    n = t.shape[0]
    tail = period - lead - rot_dim
    cos = jnp.concatenate([jnp.ones((n, lead), F32), jnp.cos(r), jnp.cos(r), jnp.cos(c), jnp.cos(c),
                           jnp.ones((n, tail), F32)], axis=-1)
    sin = jnp.concatenate([jnp.zeros((n, lead), F32), -jnp.sin(r), jnp.sin(r), -jnp.sin(c), jnp.sin(c),
                           jnp.zeros((n, tail), F32)], axis=-1)
    reps = V7X_LANES // period
    cos = jnp.concatenate([jnp.ones((CTX_LEN, V7X_LANES), F32), jnp.tile(cos, (1, reps))], axis=0)
    sin = jnp.concatenate([jnp.zeros((CTX_LEN, V7X_LANES), F32), jnp.tile(sin, (1, reps))], axis=0)
    return cos, sin


def _attend_all(q, k, v, need_ctx):
    o_lat = attention(q[:, CTX_LEN:], k, v)
    if need_ctx:
        o_ctx = attention(q[:, :CTX_LEN], k[:, :CTX_LEN], v[:, :CTX_LEN])
    else:
        o_ctx = jnp.zeros((q.shape[0], CTX_LEN, q.shape[2]), F32)
    return jnp.concatenate([o_ctx, o_lat], axis=1)


def _gqa(pa2, b, qn_g, kn_g, cos, sin, need_ctx):
    s = pa2.shape[0] // b
    grp = GQA_Q_HEADS // GQA_KV_HEADS
    gains = jnp.concatenate([jnp.tile(qn_g, GQA_Q_HEADS), jnp.tile(kn_g, GQA_KV_HEADS)])[None, :]
    q, k, v = gqa_prep(pa2, gains, cos, sin, s)
    o = _attend_all(q.reshape(b, s, -1), k.reshape(b, s, -1), v.reshape(b, s, -1), need_ctx)
    o = o.reshape(b, s, GQA_Q_HEADS, V7X_LANES)
    first = (jnp.arange(GQA_Q_HEADS) < grp)[None, None, :, None]
    o = jnp.where(first, o[..., :HEAD_DIM], o[..., HEAD_DIM:])
    return o.reshape(b, s, GQA_Q_HEADS * HEAD_DIM)


def _mla(p, qa_g, w_qup, kva_g, w_kvup, cos, sin, need_ctx):
    b, s, _ = p.shape
    r0 = MLA_Q_RANK
    r1 = MLA_Q_RANK + MLA_KV_RANK
    blocks = lambda w, lo, hi: jnp.pad(w.reshape(w.shape[0], MLA_HEADS, -1)[..., lo:hi],
                                       ((0, 0), (0, 0), (0, V7X_LANES - (hi - lo)))).reshape(w.shape[0], -1)
    w_q = blocks(w_qup, 0, MLA_NOPE + MLA_ROPE).astype(BF16)
    w_kv = jnp.concatenate([blocks(w_kvup, 0, MLA_NOPE), blocks(w_kvup, MLA_NOPE, MLA_NOPE + MLA_V)],
                           axis=1).astype(BF16)
    zero = jnp.zeros((1, 1, r0), F32)
    q = norm_mod_matmul(p[..., :r0].reshape(b * s, r0), qa_g[None, :], zero, zero, w_q, lambda i: 0)
    zero = jnp.zeros((1, 1, MLA_KV_RANK), F32)
    kv = norm_mod_matmul(p[..., r0:r1].reshape(b * s, MLA_KV_RANK), kva_g[None, :], zero, zero, w_kv, lambda i: 0)
    kr = jnp.pad(p[..., r1:r1 + MLA_ROPE].reshape(b * s, MLA_ROPE),
                 ((0, 0), (MLA_NOPE, V7X_LANES - MLA_NOPE - MLA_ROPE)))
    q, k, v = mla_prep(q, kv, kr, cos, sin, s)
    o = _attend_all(q.reshape(b, s, -1), k.reshape(b, s, -1), v.reshape(b, s, -1), need_ctx)
    return o.reshape(b, s, MLA_HEADS, V7X_LANES)[..., :MLA_V].reshape(b, s, MLA_HEADS * MLA_V)


def kernel(x, c, ctx, c_ctx, ada_w, ada_b, norm1_g, norm2_g, ev_w_in, ev_w_out, gqa_qn_g, gqa_kn_g,
           hg_lb_logits, hg_onorm_g, od_w_in, od_w_out, mla_qa_g, mla_w_qup, mla_kva_g, mla_w_kvup,
           s5_a_re, s5_a_im, s5_log_dt, s5_b_re, s5_b_im, s5_c_re, s5_c_im, s5_d, s5_w_glu, s5_b_glu,
           peer_wq, peer_keys, peer_u, peer_v, final_g):
    b, t, d = x.shape
    s = CTX_LEN + t
    depth = ada_w.shape[0]
    rows = t // GRID_W
    cos_g, sin_g = _rope_tables(rows, HEAD_DIM, 0, HEAD_DIM)
    cos_m, sin_m = _rope_tables(rows, MLA_ROPE, MLA_NOPE, V7X_LANES)
    lb_all = jnp.cumsum(jax.nn.softmax(hg_lb_logits, axis=1), axis=1)

    def mod_index(tile):
        per, nctx = s // tile, CTX_LEN // tile
        return lambda i: jnp.where(i % per < nctx, b, i // per)

    mi_row = mod_index(ROW_TILE)
    mi_peer = mod_index(PEER_TILE)

    xa = jnp.concatenate([ctx, x], axis=1).reshape(b * s, d)
    s_all = jnp.concatenate([jax.nn.silu(c), jax.nn.silu(c_ctx)[None, :]], axis=0)
    s_pad = jnp.concatenate([s_all, jnp.zeros((V7X_SUBLANES - (b + 1) % V7X_SUBLANES, d), F32)], axis=0)

    for layer in range(depth):
        need_ctx = layer < depth - 1
        j = layer // 2
        mod = matmul(s_pad, ada_w[layer].astype(BF16), tm=s_pad.shape[0])[:b + 1] + ada_b[layer]
        mod = [m[:, None, :] for m in jnp.split(mod, 6, axis=-1)]
        if layer % 2 == 0:
            pa2 = norm_mod_matmul(xa, norm1_g[layer][None, :], mod[0], mod[1], ev_w_in[j].astype(BF16), mi_row)
            ya = _gqa(pa2, b, gqa_qn_g[j], gqa_kn_g[j], cos_g, sin_g, need_ctx)
            yb = _hgrn2(pa2, b, lb_all[:, j], hg_onorm_g[j]).reshape(b, s, HG_W)
            w_out = ev_w_out[j]
        else:
            pa = norm_mod_matmul(xa, norm1_g[layer][None, :], mod[0], mod[1], od_w_in[j].astype(BF16), mi_row)
            pa = pa.reshape(b, s, -1)
            ya = _mla(pa[..., :MLA_IN], mla_qa_g[j], mla_w_qup[j], mla_kva_g[j], mla_w_kvup[j], cos_m, sin_m, need_ctx)
            y5 = _s5_core(pa[..., MLA_IN:], s5_a_re[j], s5_a_im[j], s5_log_dt[j], s5_b_re[j], s5_b_im[j],
                          s5_c_re[j], s5_c_im[j], s5_d[j])
            z = jax.nn.gelu(y5, approximate=False).reshape(b * s, S5_WIDTH)
            gl = matmul(z, s5_w_glu[j].astype(BF16)) + s5_b_glu[j]
            yb = (z * jax.nn.sigmoid(gl)).reshape(b, s, S5_WIDTH)
            w_out = od_w_out[j]
        y = jnp.concatenate([ya, yb], axis=-1).reshape(b * s, -1)
        xa = matmul_residual(y, w_out.astype(BF16), xa, mod[2], mi_row)
        xa = peer(xa, norm2_g[layer][None, :], mod[3], mod[4], mod[5],
                  peer_wq[layer], peer_keys[layer], peer_u[layer], peer_v[layer], mi_peer)
    xl = xa.reshape(b, s, d)[:, CTX_LEN:].reshape(b * t, d)
    return rmsnorm_rows(xl, final_g[None, :]).reshape(b, t, d)
```

```python
import functools
import math

import jax
import jax.numpy as jnp
from jax import lax
from jax.experimental import pallas as pl
from jax.experimental.pallas import tpu as pltpu

F32 = jnp.float32
BF16 = jnp.bfloat16

D_MODEL = 1024
GRID_W = 64
CTX_LEN = 256
EPS = 1e-6
ROPE_THETA = 10000.0

MIX_HALF = D_MODEL // 2
HEAD_DIM = 64
GQA_Q_HEADS = MIX_HALF // HEAD_DIM
GQA_KV_HEADS = GQA_Q_HEADS // 4
GQA_IN = (GQA_Q_HEADS + 2 * GQA_KV_HEADS) * HEAD_DIM

HG_DK = 128
HG_DV = 128
HG_HEADS = MIX_HALF // HG_DV
HG_W = HG_HEADS * HG_DK

MLA_HEADS = MIX_HALF // HEAD_DIM
MLA_NOPE = 64
MLA_ROPE = 32
MLA_V = 64
MLA_Q_RANK = 384
MLA_KV_RANK = 256
MLA_IN = MLA_Q_RANK + MLA_KV_RANK + MLA_ROPE

S5_WIDTH = MIX_HALF
S5_GROUP = 16
S5_GROUPS = S5_WIDTH // S5_GROUP
S5_STATE = 64

PEER_HEADS = 8
PEER_KEYS = 128
N_EXPERTS = PEER_KEYS * PEER_KEYS
PEER_TOPK = 16
PEER_QDIM = 256
PEER_HALF = PEER_QDIM // 2
PEER_PAIRS = PEER_HEADS * PEER_TOPK

V7X_LANES = 128
V7X_SUBLANES = 8
V7X_VMEM_BYTES = 64 * 1024 * 1024
VMEM_LIMIT = V7X_VMEM_BYTES - 8 * 1024 * 1024

ROW_TILE = 256
PEER_TILE = 128
ATTN_Q_TILE = 1024
EXPERT_WORDS = D_MODEL // 2
EXPERT_ROWS = EXPERT_WORDS // V7X_LANES


def _cparams(*sem):
    return pltpu.CompilerParams(dimension_semantics=sem, vmem_limit_bytes=VMEM_LIMIT)


def _norm_mod(x, g, shift, scale):
    ms = jnp.mean(x * x, axis=-1, keepdims=True)
    h = (x * lax.rsqrt(ms + EPS)) * g
    return h * (1.0 + scale) + shift


def _nmm_kernel(x_ref, g_ref, sh_ref, sc_ref, w_ref, o_ref):
    h = _norm_mod(x_ref[...], g_ref[...], sh_ref[0], sc_ref[0])
    o_ref[...] = jnp.dot(h.astype(BF16), w_ref[...], preferred_element_type=F32)


def norm_mod_matmul(x, g, shift, scale, w, mod_index, tm=ROW_TILE):
    r, k = x.shape
    n = w.shape[1]
    assert r % tm == 0 and w.shape[0] == k
    return pl.pallas_call(
        _nmm_kernel,
        grid=(r // tm,),
        in_specs=[
            pl.BlockSpec((tm, k), lambda i: (i, 0)),
            pl.BlockSpec((1, k), lambda i: (0, 0)),
            pl.BlockSpec((1, 1, k), lambda i: (mod_index(i), 0, 0)),
            pl.BlockSpec((1, 1, k), lambda i: (mod_index(i), 0, 0)),
            pl.BlockSpec((k, n), lambda i: (0, 0)),
        ],
        out_specs=pl.BlockSpec((tm, n), lambda i: (i, 0)),
        out_shape=jax.ShapeDtypeStruct((r, n), F32),
        compiler_params=_cparams("parallel"),
        name="norm_mod_matmul",
    )(x, g, shift, scale, w)


def _mm_kernel(a_ref, w_ref, o_ref):
    o_ref[...] = jnp.dot(a_ref[...].astype(BF16), w_ref[...], preferred_element_type=F32)


def matmul(a, w, tm=ROW_TILE):
    r, k = a.shape
    n = w.shape[1]
    assert r % tm == 0
    return pl.pallas_call(
        _mm_kernel,
        grid=(r // tm,),
        in_specs=[pl.BlockSpec((tm, k), lambda i: (i, 0)), pl.BlockSpec((k, n), lambda i: (0, 0))],
        out_specs=pl.BlockSpec((tm, n), lambda i: (i, 0)),
        out_shape=jax.ShapeDtypeStruct((r, n), F32),
        compiler_params=_cparams("parallel"),
        name="matmul",
    )(a, w)


def _mmres_kernel(a_ref, b_ref, w_ref, x_ref, gate_ref, o_ref):
    ka = a_ref.shape[1]
    y = (jnp.dot(a_ref[...].astype(BF16), w_ref[:ka, :], preferred_element_type=F32)
         + jnp.dot(b_ref[...].astype(BF16), w_ref[ka:, :], preferred_element_type=F32))
    o_ref[...] = x_ref[...] + gate_ref[0] * y


def matmul_residual(a, b, w, x, gate, mod_index, tm=ROW_TILE):
    r, ka = a.shape
    kb = b.shape[1]
    k, n = w.shape
    assert r % tm == 0 and ka + kb == k
    return pl.pallas_call(
        _mmres_kernel,
        grid=(r // tm,),
        in_specs=[
            pl.BlockSpec((tm, ka), lambda i: (i, 0)),
            pl.BlockSpec((tm, kb), lambda i: (i, 0)),
            pl.BlockSpec((k, n), lambda i: (0, 0)),
            pl.BlockSpec((tm, n), lambda i: (i, 0)),
            pl.BlockSpec((1, 1, n), lambda i: (mod_index(i), 0, 0)),
        ],
        out_specs=pl.BlockSpec((tm, n), lambda i: (i, 0)),
        out_shape=jax.ShapeDtypeStruct((r, n), F32),
        compiler_params=_cparams("parallel"),
        name="matmul_residual",
    )(a, b, w, x, gate)


def _rms_kernel(x_ref, g_ref, o_ref):
    x = x_ref[...]
    ms = jnp.mean(x * x, axis=-1, keepdims=True)
    o_ref[...] = (x * lax.rsqrt(ms + EPS)) * g_ref[...]


def rmsnorm_rows(x, g, tm=ROW_TILE):
    r, k = x.shape
    return pl.pallas_call(
        _rms_kernel,
        grid=(r // tm,),
        in_specs=[pl.BlockSpec((tm, k), lambda i: (i, 0)), pl.BlockSpec((1, k), lambda i: (0, 0))],
        out_specs=pl.BlockSpec((tm, k), lambda i: (i, 0)),
        out_shape=jax.ShapeDtypeStruct((r, k), F32),
        compiler_params=_cparams("parallel"),
        name="rmsnorm_rows",
    )(x, g)


ATTN_KV_CHUNKS = 3
V7X_MXU_DIM = 256
LOG2E = math.log2(math.e)


def _attn_kernel(q_ref, k_ref, v_ref, o_ref):
    q = q_ref[...]
    sk = k_ref.shape[0]
    nch = ATTN_KV_CHUNKS if sk % (ATTN_KV_CHUNKS * V7X_MXU_DIM) == 0 else 1
    ck = sk // nch
    m = l = acc = None
    for c in range(nch):
        kc = k_ref[c * ck:(c + 1) * ck, :]
        vc = v_ref[c * ck:(c + 1) * ck, :]
        s = lax.dot_general(q, kc, (((1,), (1,)), ((), ())), preferred_element_type=F32)
        mc = jnp.max(s, axis=-1, keepdims=True)
        if c == 0:
            m = mc
            p = jnp.exp2(s - m)
            l = jnp.sum(p, axis=-1, keepdims=True)
            acc = jnp.dot(p.astype(BF16), vc, preferred_element_type=F32)
        else:
            m_new = jnp.maximum(m, mc)
            a = jnp.exp2(m - m_new)
            p = jnp.exp2(s - m_new)
            l = a * l + jnp.sum(p, axis=-1, keepdims=True)
            acc = a * acc + jnp.dot(p.astype(BF16), vc, preferred_element_type=F32)
            m = m_new
    o_ref[...] = acc / l


def attention(q, k, v, tq=ATTN_Q_TILE):
    b, sq, hl = q.shape
    sk, hkl = k.shape[1], k.shape[2]
    h, hk = hl // V7X_LANES, hkl // V7X_LANES
    grp = h // hk
    tq = min(tq, sq)
    assert sq % tq == 0
    return pl.pallas_call(
        _attn_kernel,
        grid=(b, h, sq // tq),
        in_specs=[
            pl.BlockSpec((None, tq, V7X_LANES), lambda bi, hi, qi: (bi, qi, hi)),
            pl.BlockSpec((None, sk, V7X_LANES), lambda bi, hi, qi: (bi, 0, hi // grp)),
            pl.BlockSpec((None, sk, V7X_LANES), lambda bi, hi, qi: (bi, 0, hi // grp)),
        ],
        out_specs=pl.BlockSpec((None, tq, V7X_LANES), lambda bi, hi, qi: (bi, qi, hi)),
        out_shape=jax.ShapeDtypeStruct((b, sq, hl), F32),
        compiler_params=_cparams("parallel", "parallel", "parallel"),
        name="attention",
    )(q, k, v)


def _rotate(x, cos, sin, quarter):
    n = x.shape[1]
    lane = lax.broadcasted_iota(jnp.int32, x.shape, 1)
    partner = jnp.where(lane % (2 * quarter) < quarter, pltpu.roll(x, n - quarter, axis=1),
                        pltpu.roll(x, quarter, axis=1))
    return x * cos + partner * sin


def _gqa_prep_kernel(p_ref, g_ref, cos_ref, sin_ref, avg_ref, q_ref, k_ref, v_ref):
    cos, sin = cos_ref[...], sin_ref[...]
    low = lax.broadcasted_iota(jnp.int32, cos.shape, 1) < HEAD_DIM
    qk_blocks = (GQA_Q_HEADS + GQA_KV_HEADS) * HEAD_DIM // V7X_LANES
    q_blocks = GQA_Q_HEADS * HEAD_DIM // V7X_LANES
    for c in range(qk_blocks):
        lanes = slice(c * V7X_LANES, (c + 1) * V7X_LANES)
        x = p_ref[:, lanes]
        ms = jnp.dot(x * x, avg_ref[...], precision=lax.Precision.HIGHEST, preferred_element_type=F32)
        y = _rotate(x * lax.rsqrt(ms + EPS) * g_ref[:, lanes], cos, sin, HEAD_DIM // 4)
        if c < q_blocks:
            y = y * (HEAD_DIM ** -0.5 * LOG2E)
            other = pltpu.roll(y, HEAD_DIM, axis=1)
            if c < q_blocks // 2:
                first, second = jnp.where(low, y, 0.0), jnp.where(low, other, 0.0)
            else:
                first, second = jnp.where(low, 0.0, other), jnp.where(low, 0.0, y)
            q_ref[:, 2 * c * V7X_LANES:(2 * c + 1) * V7X_LANES] = first.astype(BF16)
            q_ref[:, (2 * c + 1) * V7X_LANES:(2 * c + 2) * V7X_LANES] = second.astype(BF16)
        else:
            k_ref[...] = y.astype(BF16)
    v_ref[...] = p_ref[:, qk_blocks * V7X_LANES:(qk_blocks + 1) * V7X_LANES].astype(BF16)


def gqa_prep(pa, gains, cos, sin, seq, tm=ROW_TILE):
    r = pa.shape[0]
    per = seq // tm
    avg = jnp.kron(jnp.eye(V7X_LANES // HEAD_DIM, dtype=F32), jnp.full((HEAD_DIM, HEAD_DIM), 1.0 / HEAD_DIM, F32))
    tab = pl.BlockSpec((tm, V7X_LANES), lambda i: (i % per, 0))
    kv = pl.BlockSpec((tm, V7X_LANES), lambda i: (i, 0))
    return pl.pallas_call(
        _gqa_prep_kernel,
        grid=(r // tm,),
        in_specs=[pl.BlockSpec((tm, GQA_IN), lambda i: (i, 0)), pl.BlockSpec(gains.shape, lambda i: (0, 0)), tab, tab,
                  pl.BlockSpec(avg.shape, lambda i: (0, 0))],
        out_specs=[pl.BlockSpec((tm, GQA_Q_HEADS * V7X_LANES), lambda i: (i, 0)), kv, kv],
        out_shape=[jax.ShapeDtypeStruct((r, GQA_Q_HEADS * V7X_LANES), BF16),
                   jax.ShapeDtypeStruct((r, V7X_LANES), BF16), jax.ShapeDtypeStruct((r, V7X_LANES), BF16)],
        compiler_params=_cparams("parallel"),
        name="gqa_prep",
    )(pa, gains, cos, sin, avg)


def _mla_prep_kernel(q_ref, kv_ref, kr_ref, cos_ref, sin_ref, qo_ref, ko_ref, vo_ref):
    cos, sin = cos_ref[...], sin_ref[...]
    quarter = MLA_ROPE // 4
    kr = _rotate(kr_ref[...], cos, sin, quarter)
    scale = (MLA_NOPE + MLA_ROPE) ** -0.5 * LOG2E
    width = MLA_HEADS * V7X_LANES
    for h in range(MLA_HEADS):
        lanes = slice(h * V7X_LANES, (h + 1) * V7X_LANES)
        qo_ref[:, lanes] = (_rotate(q_ref[:, lanes], cos, sin, quarter) * scale).astype(BF16)
        ko_ref[:, lanes] = (kv_ref[:, lanes] + kr).astype(BF16)
        vo_ref[:, lanes] = kv_ref[:, width + h * V7X_LANES:width + (h + 1) * V7X_LANES].astype(BF16)


def mla_prep(q, kv, kr, cos, sin, seq, tm=ROW_TILE):
    r, width = q.shape
    per = seq // tm
    tab = pl.BlockSpec((tm, V7X_LANES), lambda i: (i % per, 0))
    out = pl.BlockSpec((tm, width), lambda i: (i, 0))
    return pl.pallas_call(
        _mla_prep_kernel,
        grid=(r // tm,),
        in_specs=[out, pl.BlockSpec((tm, 2 * width), lambda i: (i, 0)), pl.BlockSpec((tm, V7X_LANES), lambda i: (i, 0)),
                  tab, tab],
        out_specs=[out, out, out],
        out_shape=[jax.ShapeDtypeStruct((r, width), BF16)] * 3,
        compiler_params=_cparams("parallel"),
        name="mla_prep",
    )(q, kv, kr, cos, sin)


def _topk_rows(s, iota, k):
    n = s.shape[0]
    row = lax.broadcasted_iota(jnp.int32, (k, s.shape[1]), 0)
    vals = jnp.zeros((k, s.shape[1]), F32)
    ids = jnp.zeros((k, s.shape[1]), F32)
    for r in range(k):
        m = jnp.max(s, axis=0, keepdims=True)
        am = jnp.min(jnp.where(s == m, iota, float(n)), axis=0, keepdims=True)
        vals = jnp.where(row == r, m, vals)
        ids = jnp.where(row == r, am, ids)
        s = jnp.where(iota == am, -jnp.inf, s)
    return vals, ids


def pack_rows_bf16(tab):
    e, d = tab.shape
    t16 = lax.bitcast_convert_type(tab.astype(BF16), jnp.uint16).astype(jnp.uint32)
    word = t16[:, :d // 2] | (t16[:, d // 2:] << 16)
    return lax.bitcast_convert_type(word, jnp.int32).reshape(e * EXPERT_ROWS, V7X_LANES)


def _slab_value_row(r):
    return r // 2 + EXPERT_ROWS * (r % 2)


PEER_GROUP = V7X_SUBLANES
SLAB_ROWS_BF16 = 2 * EXPERT_ROWS
PAIR_LANES = PEER_PAIRS * SLAB_ROWS_BF16


def _slab(tab_ref, row):
    return tab_ref[pl.ds(pl.multiple_of(row, EXPERT_ROWS), EXPERT_ROWS), :]


def _token_rows_bf16(h8, i):
    row = lax.broadcasted_iota(jnp.int32, (SLAB_ROWS_BF16, V7X_LANES), 0)
    hq = jnp.zeros((SLAB_ROWS_BF16, V7X_LANES), F32)
    for r in range(SLAB_ROWS_BF16):
        v = _slab_value_row(r)
        hq = jnp.where(row == r, h8[i:i + 1, v * V7X_LANES:(v + 1) * V7X_LANES], hq)
    return hq.astype(BF16)


def _peer_v_kernel(idx_ref, w_ref, x_ref, gate_ref, tab_ref, spread_ref, o_ref, g_ref):
    groups = w_ref.shape[0] // PEER_GROUP
    last_slot = PEER_GROUP - 1
    shape = (SLAB_ROWS_BF16, PAIR_LANES)
    out_row = lax.broadcasted_iota(jnp.int32, shape, 0)
    diag = (lax.broadcasted_iota(jnp.int32, shape, 1) % SLAB_ROWS_BF16) == 2 * (out_row % EXPERT_ROWS) + out_row // EXPERT_ROWS
    slot_rows = PEER_PAIRS * EXPERT_ROWS
    gate = gate_ref[0]

    @pl.when(pl.program_id(0) == 0)
    def _():
        g_ref[pl.ds(last_slot * slot_rows, slot_rows), :] = jnp.zeros((slot_rows, V7X_LANES), jnp.int32)

    row8 = lax.broadcasted_iota(jnp.int32, (PEER_GROUP, V7X_LANES), 0)
    blocks = range(SLAB_ROWS_BF16)

    def contract(slot, wrow, tile):
        rows = pltpu.bitcast(g_ref[pl.ds(slot * slot_rows, slot_rows), :], BF16)
        wi = jnp.where(diag, jnp.broadcast_to(wrow, shape), 0.0).astype(BF16)
        out = jnp.dot(wi, rows, preferred_element_type=F32)
        return [jnp.where(row8 == slot, jnp.broadcast_to(out[v:v + 1, :], row8.shape), tile[v]) for v in blocks]

    def flush(g, tile):
        rows = pl.ds(pl.multiple_of(g * PEER_GROUP, PEER_GROUP), PEER_GROUP)
        for v in blocks:
            lanes = slice(v * V7X_LANES, (v + 1) * V7X_LANES)
            o_ref[rows, lanes] = x_ref[rows, lanes] + gate[:, lanes] * tile[v]

    def group(g, carry):
        prev_wide, prev_tile = carry
        flush(jnp.maximum(g - 1, 0), contract(last_slot, prev_wide[last_slot:], list(prev_tile)))
        w8 = w_ref[pl.ds(pl.multiple_of(g * PEER_GROUP, PEER_GROUP), PEER_GROUP), :]
        wide = jnp.dot(w8.astype(BF16), spread_ref[...], preferred_element_type=F32)
        tile = [jnp.zeros(row8.shape, F32) for _ in blocks]
        for i in range(PEER_GROUP):
            tok_idx = idx_ref.at[pl.ds((g * PEER_GROUP + i) * PEER_PAIRS, PEER_PAIRS)]
            for j in range(PEER_PAIRS):
                g_ref[pl.ds((i * PEER_PAIRS + j) * EXPERT_ROWS, EXPERT_ROWS), :] = _slab(tab_ref, tok_idx[j])
            if i < last_slot:
                tile = contract(i, wide[i:i + 1], tile)
        return wide, tuple(tile)

    init = (jnp.zeros((PEER_GROUP, PAIR_LANES), F32), tuple(jnp.zeros(row8.shape, F32) for _ in blocks))
    wide, tile = lax.fori_loop(0, groups, group, init)
    flush(groups - 1, contract(last_slot, wide[last_slot:], list(tile)))


def peer_expert_mix(idx4, w, x, gate, tab, mod_index, tb=PEER_TILE):
    r, d = x.shape
    spread = jnp.repeat(jnp.eye(PEER_PAIRS, dtype=BF16), SLAB_ROWS_BF16, axis=1)
    return pl.pallas_call(
        _peer_v_kernel,
        grid=(r // tb,),
        in_specs=[
            pl.BlockSpec((tb * PEER_PAIRS,), lambda i: (i,), memory_space=pltpu.SMEM),
            pl.BlockSpec((tb, PEER_PAIRS), lambda i: (i, 0)),
            pl.BlockSpec((tb, d), lambda i: (i, 0)),
            pl.BlockSpec((1, 1, d), lambda i: (mod_index(i), 0, 0)),
            pl.BlockSpec(tab.shape, lambda i: (0, 0), pipeline_mode=pl.Buffered(1)),
            pl.BlockSpec((PEER_PAIRS, PAIR_LANES), lambda i: (0, 0)),
        ],
        out_specs=pl.BlockSpec((tb, d), lambda i: (i, 0)),
        out_shape=jax.ShapeDtypeStruct((r, d), F32),
        scratch_shapes=[pltpu.VMEM((PEER_GROUP * PEER_PAIRS * EXPERT_ROWS, V7X_LANES), jnp.int32)],
        compiler_params=_cparams("arbitrary"),
        name="peer_expert_mix",
    )(idx4, w, x, gate, tab, spread)


def _peer_front_kernel(x_ref, g_ref, sh_ref, sc_ref, wq_ref, keys_ref, tab_ref, fold_ref, idx_out_ref, w_out_ref,
                       hb_ref, e_scr, w_scr, h_buf, gw_buf, idx_vmem, idx_smem, s_ref, sem):
    step = pl.program_id(0)
    wslot = step % 2
    rslot = 1 - wslot
    tb = x_ref.shape[0]

    @pl.when(step == 0)
    def _():
        h_buf[1] = jnp.zeros(h_buf.shape[1:], F32)
        gw_buf[1] = jnp.zeros(gw_buf.shape[1:], F32)
        idx_vmem[...] = jnp.zeros(idx_vmem.shape, jnp.int32)
        fill = pltpu.make_async_copy(idx_vmem, idx_smem.at[1], sem)
        fill.start()
        fill.wait()

    h = _norm_mod(x_ref[...], g_ref[...], sh_ref[0], sc_ref[0])
    h_buf[wslot] = h
    hb_ref[...] = h.astype(BF16)
    iota_n = lax.broadcasted_iota(jnp.int32, (PEER_KEYS, tb), 0).astype(F32)
    half = PEER_TOPK // 2
    sub = lambda n: lax.broadcasted_iota(jnp.int32, (n, tb), 0).astype(F32)
    pos_c = jnp.concatenate([sub(PEER_TOPK)] + [sub(half) + float(a * PEER_TOPK) for a in range(1, half)]
                            + [(sub(half) + float(half)) * float(PEER_TOPK)], axis=0)

    def pair_up(first, second, scale):
        return jnp.concatenate([first[0:1] * scale + second]
                               + [first[a:a + 1] * scale + second[0:half] for a in range(1, half)]
                               + [first[half:] * scale + second[0:1]], axis=0)

    def retrieve_stages(hd):
        q = jnp.dot(hb_ref[...], wq_ref[hd], preferred_element_type=F32)
        tops = []

        def first_stage(p):
            qp = q[:, p * PEER_HALF:(p + 1) * PEER_HALF].astype(BF16)
            s = lax.dot_general(keys_ref[hd, p], qp, (((1,), (1,)), ((), ())),
                                preferred_element_type=F32)
            tops.append(_topk_rows(s, iota_n, PEER_TOPK))

        return [functools.partial(first_stage, 0), functools.partial(first_stage, 1),
                functools.partial(second_stage, hd, tops)]

    def second_stage(hd, tops):
        (s1, i1), (s2, i2) = tops
        cand = pair_up(s1, s2, 1.0)
        cidx = pair_up(i1, i2, float(PEER_KEYS))
        row = lax.broadcasted_iota(jnp.int32, (PEER_TOPK, tb), 0)
        sc = jnp.zeros((PEER_TOPK, tb), F32)
        ex = jnp.zeros((PEER_TOPK, tb), F32)
        for r in range(PEER_TOPK):
            m = jnp.max(cand, axis=0, keepdims=True)
            am = jnp.min(jnp.where(cand == m, pos_c, float(PEER_TOPK * PEER_TOPK)), axis=0, keepdims=True)
            hit = pos_c == am
            e = jnp.max(jnp.where(hit, cidx, 0.0), axis=0, keepdims=True)
            sc = jnp.where(row == r, m, sc)
            ex = jnp.where(row == r, e, ex)
            cand = jnp.where(hit, -jnp.inf, cand)
        pexp = jnp.exp(sc - sc[0:1])
        rows = pl.ds(pl.multiple_of(hd * PEER_TOPK, PEER_TOPK), PEER_TOPK)
        w_scr[rows, :] = pexp / jnp.sum(pexp, axis=0, keepdims=True)
        e_scr[rows, :] = ex

    row8 = lax.broadcasted_iota(jnp.int32, (PEER_GROUP, V7X_LANES), 0)
    h_prev, gw_prev, idx_prev = h_buf.at[rslot], gw_buf.at[rslot], idx_smem.at[rslot]

    def finish(g, sums):
        hi = sums.astype(BF16)
        lo = (sums - hi.astype(F32)).astype(BF16)
        a = (jnp.dot(hi, fold_ref[...], preferred_element_type=F32)
             + jnp.dot(lo, fold_ref[...], preferred_element_type=F32))
        rows = pl.ds(pl.multiple_of(g * PEER_GROUP, PEER_GROUP), PEER_GROUP)
        gelu = 0.5 * a * (1.0 + lax.erf(a * (2.0 ** -0.5)))
        w_out_ref[rows, :] = gelu * gw_prev[rows, :]

    def gate_group(g, prev, before_token):
        finish(jnp.maximum(g - 1, 0), prev)
        sums = jnp.zeros((PEER_GROUP, PAIR_LANES), F32)
        h8 = h_prev[pl.ds(pl.multiple_of(g * PEER_GROUP, PEER_GROUP), PEER_GROUP), :]
        for i in range(PEER_GROUP):
            if i in before_token:
                before_token[i]()
            hb = _token_rows_bf16(h8, i)
            tok_idx = idx_prev.at[g * PEER_GROUP + i]
            for j in range(PEER_PAIRS):
                u = pltpu.bitcast(_slab(tab_ref, tok_idx[j]), BF16)
                s_ref[pl.ds((i * PEER_PAIRS + j) * EXPERT_ROWS, EXPERT_ROWS), :] = pltpu.bitcast(u * hb, jnp.int32)
            prod = pltpu.bitcast(s_ref[pl.ds(i * PEER_PAIRS * EXPERT_ROWS, PEER_PAIRS * EXPERT_ROWS), :], BF16)
            pick = jnp.where(row8 == i, 1.0, 0.0).astype(BF16)
            sums = sums + lax.dot_general(pick, prod, (((1,), (1,)), ((), ())), preferred_element_type=F32)
        return sums

    groups_per_head = tb // PEER_GROUP // PEER_HEADS

    def body(hd, sums):
        first_a, first_b, second = retrieve_stages(hd)
        placement = [{0: first_a, PEER_GROUP // 2: first_b}, {0: second}]
        for k in range(groups_per_head):
            sums = gate_group(hd * groups_per_head + k, sums, placement[k] if k < len(placement) else {})
        return sums

    last = lax.fori_loop(0, PEER_HEADS, body, jnp.zeros((PEER_GROUP, PAIR_LANES), F32))
    finish(tb // PEER_GROUP - 1, last)
    gw_buf[wslot] = w_scr[...].T
    idx = (e_scr[...].T * float(EXPERT_ROWS)).astype(jnp.int32)
    idx_out_ref[...] = idx
    idx_vmem[...] = idx
    handoff = pltpu.make_async_copy(idx_vmem, idx_smem.at[wslot], sem)
    handoff.start()
    handoff.wait()


def peer_front(x, g, shift, scale, wq_heads, keys, tab, mod_index, tb=PEER_TILE):
    r, d = x.shape
    nb = r // tb
    fold = jnp.repeat(jnp.eye(PEER_PAIRS, dtype=BF16), SLAB_ROWS_BF16, axis=0)
    cur = lambda i: jnp.minimum(i, nb - 1)
    const = lambda shape: pl.BlockSpec(shape, lambda i: (0,) * len(shape), pipeline_mode=pl.Buffered(1))
    return pl.pallas_call(
        _peer_front_kernel,
        grid=(nb + 1,),
        in_specs=[
            pl.BlockSpec((tb, d), lambda i: (cur(i), 0)),
            pl.BlockSpec((1, d), lambda i: (0, 0)),
            pl.BlockSpec((1, 1, d), lambda i: (mod_index(cur(i)), 0, 0)),
            pl.BlockSpec((1, 1, d), lambda i: (mod_index(cur(i)), 0, 0)),
            const((PEER_HEADS, d, PEER_QDIM)),
            const((PEER_HEADS, 2, PEER_KEYS, PEER_HALF)),
            const(tab.shape),
            const((PAIR_LANES, PEER_PAIRS)),
        ],
        out_specs=[
            pl.BlockSpec((tb, PEER_PAIRS), lambda i: (cur(i), 0)),
            pl.BlockSpec((tb, PEER_PAIRS), lambda i: (jnp.maximum(i - 1, 0), 0)),
        ],
        out_shape=[jax.ShapeDtypeStruct((r, PEER_PAIRS), jnp.int32), jax.ShapeDtypeStruct((r, PEER_PAIRS), F32)],
        scratch_shapes=[
            pltpu.VMEM((tb, d), BF16), pltpu.VMEM((PEER_PAIRS, tb), F32), pltpu.VMEM((PEER_PAIRS, tb), F32),
            pltpu.VMEM((2, tb, d), F32), pltpu.VMEM((2, tb, PEER_PAIRS), F32),
            pltpu.VMEM((tb, PEER_PAIRS), jnp.int32), pltpu.SMEM((2, tb, PEER_PAIRS), jnp.int32),
            pltpu.VMEM((PEER_GROUP * PEER_PAIRS * EXPERT_ROWS, V7X_LANES), jnp.int32),
            pltpu.SemaphoreType.DMA,
        ],
        compiler_params=_cparams("arbitrary"),
        name="peer_front",
    )(x, g, shift, scale, wq_heads, keys, tab, fold)


def peer(x, g, shift, scale, gate, wq, keys, u_tab, v_tab, mod_index_peer):
    r, d = x.shape
    wq_heads = wq.astype(BF16).reshape(d, PEER_HEADS, PEER_QDIM).transpose(1, 0, 2)
    idx4, w = peer_front(x, g, shift, scale, wq_heads, keys.astype(BF16), pack_rows_bf16(u_tab), mod_index_peer)
    return peer_expert_mix(idx4.reshape(r * PEER_PAIRS), w, x, gate, pack_rows_bf16(v_tab), mod_index_peer)


HG_TILE = 128
HG_SUB = 16
HG_PAIR = 2 * HG_DK


def _hgrn_kernel(q_ref, f_ref, v_ref, lb_ref, o_ref, st_ref, *, rev):
    @pl.when(pl.program_id(2) == 0)
    def _():
        st_ref[...] = jnp.zeros_like(st_ref)

    n = q_ref.shape[0]
    lb = lb_ref[...]
    f = lb + (1.0 - lb) * jax.nn.sigmoid(f_ref[...])
    kk = 1.0 - f
    lf = jnp.log(f)
    t = lax.broadcasted_iota(jnp.int32, (n, n), 0)
    s = lax.broadcasted_iota(jnp.int32, (n, n), 1)
    same = (t // HG_SUB) == (s // HG_SUB)
    mid = (t // HG_SUB) * HG_SUB + (HG_SUB // 2 if rev else HG_SUB // 2 - 1)
    seen = (s >= t) if rev else (s <= t)
    seen_mid = (s >= mid) if rev else (s <= mid)
    one = lambda m: jnp.where(m, 1.0, 0.0).astype(F32)
    hp = lambda a, b: jnp.dot(a, b, precision=lax.Precision.HIGHEST, preferred_element_type=F32)
    cum = hp(one(same & seen), lf)
    ref = hp(one(same & seen_mid), lf)
    last = hp(one(same), lf)
    q = q_ref[...]
    qe = (q * jnp.exp(cum)).astype(BF16)
    qm = (q * jnp.exp(cum - ref)).astype(BF16)
    km = (kk * jnp.exp(ref - cum)).astype(BF16)
    kd = (kk * jnp.exp(last - cum)).astype(BF16)
    dec = jnp.exp(last)
    vb = v_ref[...].astype(BF16)
    steps = range(n // HG_SUB)
    heads = range(HG_PAIR // HG_DK)
    intra = []
    for h in heads:
        cols = slice(h * HG_DK, (h + 1) * HG_DK)
        att = lax.dot_general(qm[:, cols], km[:, cols], (((1,), (1,)), ((), ())), preferred_element_type=F32)
        att = jnp.where(same & seen, att, 0.0).astype(BF16)
        intra.append(jnp.dot(att, vb[:, cols], preferred_element_type=F32))
    for c in (reversed(steps) if rev else steps):
        rows = slice(c * HG_SUB, (c + 1) * HG_SUB)
        for h in heads:
            cols = slice(h * HG_DK, (h + 1) * HG_DK)
            st = st_ref[h]
            o_ref[rows, cols] = intra[h][rows] + lax.dot_general(
                qe[rows, cols], st.astype(BF16), (((1,), (1,)), ((), ())), preferred_element_type=F32)
            upd = lax.dot_general(vb[rows, cols], kd[rows, cols], (((0,), (0,)), ((), ())),
                                  preferred_element_type=F32)
            st_ref[h] = st * dec[c * HG_SUB:c * HG_SUB + 1, cols] + upd


def hgrn_scan(pa, lb, rev, batch):
    rows = pa.shape[0]
    per = rows // batch // HG_TILE
    nctx = CTX_LEN // HG_TILE
    col0 = GQA_IN // HG_PAIR
    blocks = HG_W // HG_PAIR

    def tok(b, k):
        if rev:
            k = jnp.where(k < nctx, nctx - 1 - k, per - 1 - (k - nctx))
        return b * per + k

    spec = lambda cb: pl.BlockSpec((HG_TILE, HG_PAIR), lambda b, p, k: (tok(b, k), cb + p))
    d = 1 if rev else 0
    return pl.pallas_call(
        functools.partial(_hgrn_kernel, rev=rev),
        grid=(batch, blocks, per),
        in_specs=[spec(col0), spec(col0 + (1 + d) * blocks), spec(col0 + 3 * blocks),
                  pl.BlockSpec((None, 1, HG_PAIR), lambda b, p, k: (d, 0, p))],
        out_specs=pl.BlockSpec((HG_TILE, HG_PAIR), lambda b, p, k: (tok(b, k), p)),
        out_shape=jax.ShapeDtypeStruct((rows, HG_W), F32),
        scratch_shapes=[pltpu.VMEM((HG_PAIR // HG_DK, HG_DV, HG_DK), F32)],
        compiler_params=_cparams("parallel", "parallel", "arbitrary"),
        name="hgrn_scan_rev" if rev else "hgrn_scan_fwd",
    )(pa, pa, pa, lb)


def _rms_norm(x, g):
    xf = x.astype(F32)
    y = xf * lax.rsqrt(jnp.mean(xf * xf, axis=-1, keepdims=True) + EPS)
    return y * g.astype(F32)


def _hgrn2(pa2, batch, lb, onorm_g):
    lb = lb.reshape(2, 1, HG_W)
    o = hgrn_scan(pa2, lb, False, batch) + hgrn_scan(pa2, lb, True, batch)
    o = _rms_norm(o.reshape(-1, HG_HEADS, HG_DV), onorm_g).reshape(-1, HG_W)
    return o * jax.nn.silu(pa2[:, GQA_IN + 4 * HG_W:])


S5_CHUNK = 64
S5_LANES = S5_GROUPS * S5_STATE


def _s5_kernel(u_ref, bre_ref, bim_ref, cre_ref, cim_ref, are_ref, aim_ref, y_ref, xre, xim, hre, him):
    nb = V7X_SUBLANES // 2
    tiles = u_ref.shape[0] // V7X_SUBLANES
    direction = pl.program_id(0)
    rev = direction == 1

    @pl.when(pl.program_id(1) == 0)
    def _():
        hre[...] = jnp.zeros_like(hre)
        him[...] = jnp.zeros_like(him)

    ub = u_ref[...].astype(BF16)
    tile = V7X_MXU_DIM
    fan = (S5_LANES // tile) // (u_ref.shape[1] // tile)
    for nt in range(S5_LANES // tile):
        cols = slice(nt * tile, (nt + 1) * tile)
        src = ub[:, (nt // fan) * tile:(nt // fan + 1) * tile]
        xre[:, cols] = jnp.dot(src, bre_ref[nt], preferred_element_type=F32)
        xim[:, cols] = jnp.dot(src, bim_ref[nt], preferred_element_type=F32)
    shape = (V7X_SUBLANES, S5_LANES)
    second = (lax.broadcasted_iota(jnp.int32, shape, 0) // nb) != direction
    are = jnp.broadcast_to(are_ref[...], shape)
    aim = jnp.broadcast_to(aim_ref[...], shape)
    cre = jnp.where(second, are * are - aim * aim, are)
    cim = jnp.where(second, 2.0 * are * aim, aim)

    def step(k, carry):
        pr, pi = carry
        k = jnp.where(rev, tiles - 1 - k, k)
        r = pl.ds(pl.multiple_of(k * V7X_SUBLANES, V7X_SUBLANES), V7X_SUBLANES)
        xr, xi = xre[r, :], xim[r, :]
        sr = jnp.where(second, pltpu.roll(xr, nb, axis=0), 0.0)
        si = jnp.where(second, pltpu.roll(xi, nb, axis=0), 0.0)
        nr = xr + (are * sr - aim * si) + (cre * pr - cim * pi)
        ni = xi + (are * si + aim * sr) + (cre * pi + cim * pr)
        xre[r, :] = nr
        xim[r, :] = ni
        return (jnp.where(second, nr, pltpu.roll(nr, nb, axis=0)),
                jnp.where(second, ni, pltpu.roll(ni, nb, axis=0)))

    hr, hi = lax.fori_loop(0, tiles, step, (hre[...], him[...]))
    hre[...] = hr
    him[...] = hi
    for mt in range(u_ref.shape[1] // tile):
        acc = jnp.zeros((u_ref.shape[0], tile), F32)
        for kt in range(mt * fan, (mt + 1) * fan):
            cols = slice(kt * tile, (kt + 1) * tile)
            acc = acc + (jnp.dot(xre[:, cols].astype(BF16), cre_ref[kt], preferred_element_type=F32)
                         - jnp.dot(xim[:, cols].astype(BF16), cim_ref[kt], preferred_element_type=F32))
        y_ref[:, mt * tile:(mt + 1) * tile] = acc


def s5_scan(u, bre, bim, cre, cim, are, aim, nb):
    rows, w = u.shape
    blk = S5_CHUNK * nb
    assert rows % blk == 0 and 2 * nb == V7X_SUBLANES
    nblk, nctx = rows // blk, CTX_LEN // S5_CHUNK
    wspec = lambda shape: pl.BlockSpec((None,) + shape, lambda d, i: (d, 0, 0))
    tiles = pl.BlockSpec((None,) + bre.shape[1:], lambda d, i: (d, 0, 0, 0))

    def chunk(d, i):
        back = jnp.where(i < nctx, nctx - 1 - i, nblk - 1 - (i - nctx))
        return jnp.where(d == 1, back, i)
    return pl.pallas_call(
        _s5_kernel,
        grid=(2, rows // blk),
        in_specs=[
            pl.BlockSpec((blk, w), lambda d, i: (chunk(d, i), 0)),
            tiles, tiles, tiles, tiles,
            wspec((1, S5_LANES)), wspec((1, S5_LANES)),
        ],
        out_specs=pl.BlockSpec((None, blk, w), lambda d, i: (d, chunk(d, i), 0)),
        out_shape=jax.ShapeDtypeStruct((2, rows, w), F32),
        scratch_shapes=[pltpu.VMEM((blk, S5_LANES), F32), pltpu.VMEM((blk, S5_LANES), F32),
                        pltpu.VMEM((V7X_SUBLANES, S5_LANES), F32), pltpu.VMEM((V7X_SUBLANES, S5_LANES), F32)],
        compiler_params=_cparams("arbitrary", "arbitrary"),
        name="s5_scan",
    )(u, bre, bim, cre, cim, are, aim)


def _s5_core(u, a_re, a_im, log_dt, b_re, b_im, c_re, c_im, d_skip):
    b, s, w = u.shape
    dt = jnp.exp(log_dt)[..., None]
    mag = jnp.exp(a_re * dt)
    abar_re, abar_im = mag * jnp.cos(a_im * dt), mag * jnp.sin(a_im * dt)
    den = a_re * a_re + a_im * a_im
    k_re = ((abar_re - 1.0) * a_re + abar_im * a_im) / den
    k_im = (abar_im * a_re - (abar_re - 1.0) * a_im) / den
    bb_re = k_re[..., None] * b_re - k_im[..., None] * b_im
    bb_im = k_re[..., None] * b_im + k_im[..., None] * b_re
    eye = jnp.eye(S5_GROUPS, dtype=F32)
    tile, n_tiles = V7X_MXU_DIM, S5_LANES // V7X_MXU_DIM
    fan = n_tiles // (w // tile)
    cut = lambda m, r, c: m[:, r * tile:(r + 1) * tile, c * tile:(c + 1) * tile]

    def bd_in(m):
        full = jnp.einsum('dgpc,gh->dgchp', m, eye).reshape(2, w, S5_LANES).astype(BF16)
        return jnp.stack([cut(full, nt // fan, nt) for nt in range(n_tiles)], axis=1)

    def bd_out(m):
        full = jnp.einsum('dgcp,gh->dgphc', m, eye).reshape(2, S5_LANES, w).astype(BF16)
        return jnp.stack([cut(full, kt, kt // fan) for kt in range(n_tiles)], axis=1)
    y2 = s5_scan(u.transpose(1, 0, 2).reshape(s * b, w), bd_in(bb_re), bd_in(bb_im), bd_out(c_re), bd_out(c_im),
                 abar_re.reshape(2, 1, S5_LANES), abar_im.reshape(2, 1, S5_LANES), b)
    return d_skip * u + (y2[0] + y2[1]).reshape(s, b, w).transpose(1, 0, 2)


def _rope_tables(rows, rot_dim, lead, period):
    axis_dim = rot_dim // 2
    inv = ROPE_THETA ** (-jnp.arange(0, axis_dim, 2, dtype=F32) / axis_dim)
    t = jnp.arange(rows * GRID_W)
    r = (t // GRID_W).astype(F32)[:, None] * inv
    c = (t % GRID_W).astype(F32)[:, None] * inv
    n = t.shape[0]
    tail = period - lead - rot_dim
    cos = jnp.concatenate([jnp.ones((n, lead), F32), jnp.cos(r), jnp.cos(r), jnp.cos(c), jnp.cos(c),
                           jnp.ones((n, tail), F32)], axis=-1)
    sin = jnp.concatenate([jnp.zeros((n, lead), F32), -jnp.sin(r), jnp.sin(r), -jnp.sin(c), jnp.sin(c),
                           jnp.zeros((n, tail), F32)], axis=-1)
    reps = V7X_LANES // period
    cos = jnp.concatenate([jnp.ones((CTX_LEN, V7X_LANES), F32), jnp.tile(cos, (1, reps))], axis=0)
    sin = jnp.concatenate([jnp.zeros((CTX_LEN, V7X_LANES), F32), jnp.tile(sin, (1, reps))], axis=0)
    return cos, sin


def _attend_all(q, k, v, need_ctx):
    o_lat = attention(q[:, CTX_LEN:], k, v)
    if need_ctx:
        o_ctx = attention(q[:, :CTX_LEN], k[:, :CTX_LEN], v[:, :CTX_LEN])
    else:
        o_ctx = jnp.zeros((q.shape[0], CTX_LEN, q.shape[2]), F32)
    return jnp.concatenate([o_ctx, o_lat], axis=1)


def _gqa(pa2, b, qn_g, kn_g, cos, sin, need_ctx):
    s = pa2.shape[0] // b
    grp = GQA_Q_HEADS // GQA_KV_HEADS
    gains = jnp.concatenate([jnp.tile(qn_g, GQA_Q_HEADS), jnp.tile(kn_g, GQA_KV_HEADS)])[None, :]
    q, k, v = gqa_prep(pa2, gains, cos, sin, s)
    o = _attend_all(q.reshape(b, s, -1), k.reshape(b, s, -1), v.reshape(b, s, -1), need_ctx)
    o = o.reshape(b, s, GQA_Q_HEADS, V7X_LANES)
    first = (jnp.arange(GQA_Q_HEADS) < grp)[None, None, :, None]
    o = jnp.where(first, o[..., :HEAD_DIM], o[..., HEAD_DIM:])
    return o.reshape(b, s, GQA_Q_HEADS * HEAD_DIM)


def _mla(p, qa_g, w_qup, kva_g, w_kvup, cos, sin, need_ctx):
    b, s, _ = p.shape
    r0 = MLA_Q_RANK
    r1 = MLA_Q_RANK + MLA_KV_RANK
    blocks = lambda w, lo, hi: jnp.pad(w.reshape(w.shape[0], MLA_HEADS, -1)[..., lo:hi],
                                       ((0, 0), (0, 0), (0, V7X_LANES - (hi - lo)))).reshape(w.shape[0], -1)
    w_q = blocks(w_qup, 0, MLA_NOPE + MLA_ROPE).astype(BF16)
    w_kv = jnp.concatenate([blocks(w_kvup, 0, MLA_NOPE), blocks(w_kvup, MLA_NOPE, MLA_NOPE + MLA_V)],
                           axis=1).astype(BF16)
    zero = jnp.zeros((1, 1, r0), F32)
    q = norm_mod_matmul(p[..., :r0].reshape(b * s, r0), qa_g[None, :], zero, zero, w_q, lambda i: 0)
    zero = jnp.zeros((1, 1, MLA_KV_RANK), F32)
    kv = norm_mod_matmul(p[..., r0:r1].reshape(b * s, MLA_KV_RANK), kva_g[None, :], zero, zero, w_kv, lambda i: 0)
    kr = jnp.pad(p[..., r1:r1 + MLA_ROPE].reshape(b * s, MLA_ROPE),
                 ((0, 0), (MLA_NOPE, V7X_LANES - MLA_NOPE - MLA_ROPE)))
    q, k, v = mla_prep(q, kv, kr, cos, sin, s)
    o = _attend_all(q.reshape(b, s, -1), k.reshape(b, s, -1), v.reshape(b, s, -1), need_ctx)
    return o.reshape(b, s, MLA_HEADS, V7X_LANES)[..., :MLA_V].reshape(b, s, MLA_HEADS * MLA_V)


def kernel(x, c, ctx, c_ctx, ada_w, ada_b, norm1_g, norm2_g, ev_w_in, ev_w_out, gqa_qn_g, gqa_kn_g,
           hg_lb_logits, hg_onorm_g, od_w_in, od_w_out, mla_qa_g, mla_w_qup, mla_kva_g, mla_w_kvup,
           s5_a_re, s5_a_im, s5_log_dt, s5_b_re, s5_b_im, s5_c_re, s5_c_im, s5_d, s5_w_glu, s5_b_glu,
           peer_wq, peer_keys, peer_u, peer_v, final_g):
    b, t, d = x.shape
    s = CTX_LEN + t
    depth = ada_w.shape[0]
    rows = t // GRID_W
    cos_g, sin_g = _rope_tables(rows, HEAD_DIM, 0, HEAD_DIM)
    cos_m, sin_m = _rope_tables(rows, MLA_ROPE, MLA_NOPE, V7X_LANES)
    lb_all = jnp.cumsum(jax.nn.softmax(hg_lb_logits, axis=1), axis=1)

    def mod_index(tile):
        per, nctx = s // tile, CTX_LEN // tile
        return lambda i: jnp.where(i % per < nctx, b, i // per)

    mi_row = mod_index(ROW_TILE)
    mi_peer = mod_index(PEER_TILE)

    xa = jnp.concatenate([ctx, x], axis=1).reshape(b * s, d)
    s_all = jnp.concatenate([jax.nn.silu(c), jax.nn.silu(c_ctx)[None, :]], axis=0)
    s_pad = jnp.concatenate([s_all, jnp.zeros((V7X_SUBLANES - (b + 1) % V7X_SUBLANES, d), F32)], axis=0)

    for layer in range(depth):
        need_ctx = layer < depth - 1
        j = layer // 2
        mod = matmul(s_pad, ada_w[layer].astype(BF16), tm=s_pad.shape[0])[:b + 1] + ada_b[layer]
        mod = [m[:, None, :] for m in jnp.split(mod, 6, axis=-1)]
        if layer % 2 == 0:
            pa2 = norm_mod_matmul(xa, norm1_g[layer][None, :], mod[0], mod[1], ev_w_in[j].astype(BF16), mi_row)
            ya = _gqa(pa2, b, gqa_qn_g[j], gqa_kn_g[j], cos_g, sin_g, need_ctx)
            yb = _hgrn2(pa2, b, lb_all[:, j], hg_onorm_g[j]).reshape(b, s, HG_W)
            w_out = ev_w_out[j]
        else:
            pa = norm_mod_matmul(xa, norm1_g[layer][None, :], mod[0], mod[1], od_w_in[j].astype(BF16), mi_row)
            pa = pa.reshape(b, s, -1)
            ya = _mla(pa[..., :MLA_IN], mla_qa_g[j], mla_w_qup[j], mla_kva_g[j], mla_w_kvup[j], cos_m, sin_m, need_ctx)
            y5 = _s5_core(pa[..., MLA_IN:], s5_a_re[j], s5_a_im[j], s5_log_dt[j], s5_b_re[j], s5_b_im[j],
                          s5_c_re[j], s5_c_im[j], s5_d[j])
            z = jax.nn.gelu(y5, approximate=False).reshape(b * s, S5_WIDTH)
            gl = matmul(z, s5_w_glu[j].astype(BF16)) + s5_b_glu[j]
            yb = (z * jax.nn.sigmoid(gl)).reshape(b, s, S5_WIDTH)
            w_out = od_w_out[j]
        xa = matmul_residual(ya.reshape(b * s, -1), yb.reshape(b * s, -1), w_out.astype(BF16), xa, mod[2], mi_row)
        xa = peer(xa, norm2_g[layer][None, :], mod[3], mod[4], mod[5],
                  peer_wq[layer], peer_keys[layer], peer_u[layer], peer_v[layer], mi_peer)
    xl = xa.reshape(b, s, d)[:, CTX_LEN:].reshape(b * t, d)
    return rmsnorm_rows(xl, final_g[None, :]).reshape(b, t, d)
```

```python
import functools
import math

import jax
import jax.numpy as jnp
from jax import lax
from jax.experimental import pallas as pl
from jax.experimental.pallas import tpu as pltpu

F32 = jnp.float32
BF16 = jnp.bfloat16

D_MODEL = 1024
GRID_W = 64
CTX_LEN = 256
EPS = 1e-6
ROPE_THETA = 10000.0

MIX_HALF = D_MODEL // 2
HEAD_DIM = 64
GQA_Q_HEADS = MIX_HALF // HEAD_DIM
GQA_KV_HEADS = GQA_Q_HEADS // 4
GQA_IN = (GQA_Q_HEADS + 2 * GQA_KV_HEADS) * HEAD_DIM

HG_DK = 128
HG_DV = 128
HG_HEADS = MIX_HALF // HG_DV
HG_W = HG_HEADS * HG_DK

MLA_HEADS = MIX_HALF // HEAD_DIM
MLA_NOPE = 64
MLA_ROPE = 32
MLA_V = 64
MLA_Q_RANK = 384
MLA_KV_RANK = 256
MLA_IN = MLA_Q_RANK + MLA_KV_RANK + MLA_ROPE

S5_WIDTH = MIX_HALF
S5_GROUP = 16
S5_GROUPS = S5_WIDTH // S5_GROUP
S5_STATE = 64

PEER_HEADS = 8
PEER_KEYS = 128
N_EXPERTS = PEER_KEYS * PEER_KEYS
PEER_TOPK = 16
PEER_QDIM = 256
PEER_HALF = PEER_QDIM // 2
PEER_PAIRS = PEER_HEADS * PEER_TOPK

V7X_LANES = 128
V7X_SUBLANES = 8
V7X_VMEM_BYTES = 64 * 1024 * 1024
VMEM_LIMIT = V7X_VMEM_BYTES - 8 * 1024 * 1024

ROW_TILE = 256
PEER_TILE = 128
ATTN_Q_TILE = 1024
EXPERT_WORDS = D_MODEL // 2
EXPERT_ROWS = EXPERT_WORDS // V7X_LANES


def _cparams(*sem):
    return pltpu.CompilerParams(dimension_semantics=sem, vmem_limit_bytes=VMEM_LIMIT)


def _norm_mod(x, g, shift, scale):
    ms = jnp.mean(x * x, axis=-1, keepdims=True)
    h = (x * lax.rsqrt(ms + EPS)) * g
    return h * (1.0 + scale) + shift


def _nmm_kernel(x_ref, g_ref, sh_ref, sc_ref, w_ref, o_ref):
    h = _norm_mod(x_ref[...], g_ref[...], sh_ref[0], sc_ref[0])
    o_ref[...] = jnp.dot(h.astype(BF16), w_ref[...], preferred_element_type=F32)


def norm_mod_matmul(x, g, shift, scale, w, mod_index, tm=ROW_TILE):
    r, k = x.shape
    n = w.shape[1]
    assert r % tm == 0 and w.shape[0] == k
    return pl.pallas_call(
        _nmm_kernel,
        grid=(r // tm,),
        in_specs=[
            pl.BlockSpec((tm, k), lambda i: (i, 0)),
            pl.BlockSpec((1, k), lambda i: (0, 0)),
            pl.BlockSpec((1, 1, k), lambda i: (mod_index(i), 0, 0)),
            pl.BlockSpec((1, 1, k), lambda i: (mod_index(i), 0, 0)),
            pl.BlockSpec((k, n), lambda i: (0, 0)),
        ],
        out_specs=pl.BlockSpec((tm, n), lambda i: (i, 0)),
        out_shape=jax.ShapeDtypeStruct((r, n), F32),
        compiler_params=_cparams("parallel"),
        name="norm_mod_matmul",
    )(x, g, shift, scale, w)


def _mm_kernel(a_ref, w_ref, o_ref):
    o_ref[...] = jnp.dot(a_ref[...].astype(BF16), w_ref[...], preferred_element_type=F32)


def matmul(a, w, tm=ROW_TILE):
    r, k = a.shape
    n = w.shape[1]
    assert r % tm == 0
    return pl.pallas_call(
        _mm_kernel,
        grid=(r // tm,),
        in_specs=[pl.BlockSpec((tm, k), lambda i: (i, 0)), pl.BlockSpec((k, n), lambda i: (0, 0))],
        out_specs=pl.BlockSpec((tm, n), lambda i: (i, 0)),
        out_shape=jax.ShapeDtypeStruct((r, n), F32),
        compiler_params=_cparams("parallel"),
        name="matmul",
    )(a, w)


def _mmres_kernel(a_ref, b_ref, w_ref, x_ref, gate_ref, o_ref):
    ka = a_ref.shape[1]
    y = (jnp.dot(a_ref[...].astype(BF16), w_ref[:ka, :], preferred_element_type=F32)
         + jnp.dot(b_ref[...].astype(BF16), w_ref[ka:, :], preferred_element_type=F32))
    o_ref[...] = x_ref[...] + gate_ref[0] * y


def matmul_residual(a, b, w, x, gate, mod_index, tm=ROW_TILE):
    r, ka = a.shape
    kb = b.shape[1]
    k, n = w.shape
    assert r % tm == 0 and ka + kb == k
    return pl.pallas_call(
        _mmres_kernel,
        grid=(r // tm,),
        in_specs=[
            pl.BlockSpec((tm, ka), lambda i: (i, 0)),
            pl.BlockSpec((tm, kb), lambda i: (i, 0)),
            pl.BlockSpec((k, n), lambda i: (0, 0)),
            pl.BlockSpec((tm, n), lambda i: (i, 0)),
            pl.BlockSpec((1, 1, n), lambda i: (mod_index(i), 0, 0)),
        ],
        out_specs=pl.BlockSpec((tm, n), lambda i: (i, 0)),
        out_shape=jax.ShapeDtypeStruct((r, n), F32),
        compiler_params=_cparams("parallel"),
        name="matmul_residual",
    )(a, b, w, x, gate)


def _rms_kernel(x_ref, g_ref, o_ref):
    x = x_ref[...]
    ms = jnp.mean(x * x, axis=-1, keepdims=True)
    o_ref[...] = (x * lax.rsqrt(ms + EPS)) * g_ref[...]


def rmsnorm_rows(x, g, tm=ROW_TILE):
    r, k = x.shape
    return pl.pallas_call(
        _rms_kernel,
        grid=(r // tm,),
        in_specs=[pl.BlockSpec((tm, k), lambda i: (i, 0)), pl.BlockSpec((1, k), lambda i: (0, 0))],
        out_specs=pl.BlockSpec((tm, k), lambda i: (i, 0)),
        out_shape=jax.ShapeDtypeStruct((r, k), F32),
        compiler_params=_cparams("parallel"),
        name="rmsnorm_rows",
    )(x, g)


ATTN_KV_CHUNKS = 3
V7X_MXU_DIM = 256
LOG2E = math.log2(math.e)


def _attn_kernel(q_ref, k_ref, v_ref, o_ref):
    q = q_ref[...]
    sk = k_ref.shape[0]
    nch = ATTN_KV_CHUNKS if sk % (ATTN_KV_CHUNKS * V7X_MXU_DIM) == 0 else 1
    ck = sk // nch
    m = l = acc = None
    for c in range(nch):
        kc = k_ref[c * ck:(c + 1) * ck, :]
        vc = v_ref[c * ck:(c + 1) * ck, :]
        s = lax.dot_general(q, kc, (((1,), (1,)), ((), ())), preferred_element_type=F32)
        mc = jnp.max(s, axis=-1, keepdims=True)
        if c == 0:
            m = mc
            p = jnp.exp2(s - m)
            l = jnp.sum(p, axis=-1, keepdims=True)
            acc = jnp.dot(p.astype(BF16), vc, preferred_element_type=F32)
        else:
            m_new = jnp.maximum(m, mc)
            a = jnp.exp2(m - m_new)
            p = jnp.exp2(s - m_new)
            l = a * l + jnp.sum(p, axis=-1, keepdims=True)
            acc = a * acc + jnp.dot(p.astype(BF16), vc, preferred_element_type=F32)
            m = m_new
    o_ref[...] = acc / l


def attention(q, k, v, tq=ATTN_Q_TILE):
    b, sq, hl = q.shape
    sk, hkl = k.shape[1], k.shape[2]
    h, hk = hl // V7X_LANES, hkl // V7X_LANES
    grp = h // hk
    tq = min(tq, sq)
    assert sq % tq == 0
    return pl.pallas_call(
        _attn_kernel,
        grid=(b, h, sq // tq),
        in_specs=[
            pl.BlockSpec((None, tq, V7X_LANES), lambda bi, hi, qi: (bi, qi, hi)),
            pl.BlockSpec((None, sk, V7X_LANES), lambda bi, hi, qi: (bi, 0, hi // grp)),
            pl.BlockSpec((None, sk, V7X_LANES), lambda bi, hi, qi: (bi, 0, hi // grp)),
        ],
        out_specs=pl.BlockSpec((None, tq, V7X_LANES), lambda bi, hi, qi: (bi, qi, hi)),
        out_shape=jax.ShapeDtypeStruct((b, sq, hl), F32),
        compiler_params=_cparams("parallel", "parallel", "parallel"),
        name="attention",
    )(q, k, v)


def _rotate(x, cos, sin, quarter):
    n = x.shape[1]
    lane = lax.broadcasted_iota(jnp.int32, x.shape, 1)
    partner = jnp.where(lane % (2 * quarter) < quarter, pltpu.roll(x, n - quarter, axis=1),
                        pltpu.roll(x, quarter, axis=1))
    return x * cos + partner * sin


def _gqa_prep_kernel(p_ref, g_ref, cos_ref, sin_ref, avg_ref, q_ref, k_ref, v_ref):
    cos, sin = cos_ref[...], sin_ref[...]
    low = lax.broadcasted_iota(jnp.int32, cos.shape, 1) < HEAD_DIM
    qk_blocks = (GQA_Q_HEADS + GQA_KV_HEADS) * HEAD_DIM // V7X_LANES
    q_blocks = GQA_Q_HEADS * HEAD_DIM // V7X_LANES
    for c in range(qk_blocks):
        lanes = slice(c * V7X_LANES, (c + 1) * V7X_LANES)
        x = p_ref[:, lanes]
        ms = jnp.dot(x * x, avg_ref[...], precision=lax.Precision.HIGHEST, preferred_element_type=F32)
        y = _rotate(x * lax.rsqrt(ms + EPS) * g_ref[:, lanes], cos, sin, HEAD_DIM // 4)
        if c < q_blocks:
            y = y * (HEAD_DIM ** -0.5 * LOG2E)
            other = pltpu.roll(y, HEAD_DIM, axis=1)
            if c < q_blocks // 2:
                first, second = jnp.where(low, y, 0.0), jnp.where(low, other, 0.0)
            else:
                first, second = jnp.where(low, 0.0, other), jnp.where(low, 0.0, y)
            q_ref[:, 2 * c * V7X_LANES:(2 * c + 1) * V7X_LANES] = first.astype(BF16)
            q_ref[:, (2 * c + 1) * V7X_LANES:(2 * c + 2) * V7X_LANES] = second.astype(BF16)
        else:
            k_ref[...] = y.astype(BF16)
    v_ref[...] = p_ref[:, qk_blocks * V7X_LANES:(qk_blocks + 1) * V7X_LANES].astype(BF16)


def gqa_prep(pa, gains, cos, sin, seq, tm=ROW_TILE):
    r = pa.shape[0]
    per = seq // tm
    avg = jnp.kron(jnp.eye(V7X_LANES // HEAD_DIM, dtype=F32), jnp.full((HEAD_DIM, HEAD_DIM), 1.0 / HEAD_DIM, F32))
    tab = pl.BlockSpec((tm, V7X_LANES), lambda i: (i % per, 0))
    kv = pl.BlockSpec((tm, V7X_LANES), lambda i: (i, 0))
    return pl.pallas_call(
        _gqa_prep_kernel,
        grid=(r // tm,),
        in_specs=[pl.BlockSpec((tm, GQA_IN), lambda i: (i, 0)), pl.BlockSpec(gains.shape, lambda i: (0, 0)), tab, tab,
                  pl.BlockSpec(avg.shape, lambda i: (0, 0))],
        out_specs=[pl.BlockSpec((tm, GQA_Q_HEADS * V7X_LANES), lambda i: (i, 0)), kv, kv],
        out_shape=[jax.ShapeDtypeStruct((r, GQA_Q_HEADS * V7X_LANES), BF16),
                   jax.ShapeDtypeStruct((r, V7X_LANES), BF16), jax.ShapeDtypeStruct((r, V7X_LANES), BF16)],
        compiler_params=_cparams("parallel"),
        name="gqa_prep",
    )(pa, gains, cos, sin, avg)


def _mla_prep_kernel(q_ref, kv_ref, kr_ref, cos_ref, sin_ref, qo_ref, ko_ref, vo_ref):
    cos, sin = cos_ref[...], sin_ref[...]
    quarter = MLA_ROPE // 4
    kr = _rotate(kr_ref[...], cos, sin, quarter)
    scale = (MLA_NOPE + MLA_ROPE) ** -0.5 * LOG2E
    width = MLA_HEADS * V7X_LANES
    for h in range(MLA_HEADS):
        lanes = slice(h * V7X_LANES, (h + 1) * V7X_LANES)
        qo_ref[:, lanes] = (_rotate(q_ref[:, lanes], cos, sin, quarter) * scale).astype(BF16)
        ko_ref[:, lanes] = (kv_ref[:, lanes] + kr).astype(BF16)
        vo_ref[:, lanes] = kv_ref[:, width + h * V7X_LANES:width + (h + 1) * V7X_LANES].astype(BF16)


def mla_prep(q, kv, kr, cos, sin, seq, tm=ROW_TILE):
    r, width = q.shape
    per = seq // tm
    tab = pl.BlockSpec((tm, V7X_LANES), lambda i: (i % per, 0))
    out = pl.BlockSpec((tm, width), lambda i: (i, 0))
    return pl.pallas_call(
        _mla_prep_kernel,
        grid=(r // tm,),
        in_specs=[out, pl.BlockSpec((tm, 2 * width), lambda i: (i, 0)), pl.BlockSpec((tm, V7X_LANES), lambda i: (i, 0)),
                  tab, tab],
        out_specs=[out, out, out],
        out_shape=[jax.ShapeDtypeStruct((r, width), BF16)] * 3,
        compiler_params=_cparams("parallel"),
        name="mla_prep",
    )(q, kv, kr, cos, sin)


def _topk_rows(s, iota, k):
    n = s.shape[0]
    row = lax.broadcasted_iota(jnp.int32, (k, s.shape[1]), 0)
    vals = jnp.zeros((k, s.shape[1]), F32)
    ids = jnp.zeros((k, s.shape[1]), F32)
    for r in range(k):
        m = jnp.max(s, axis=0, keepdims=True)
        am = jnp.min(jnp.where(s == m, iota, float(n)), axis=0, keepdims=True)
        vals = jnp.where(row == r, m, vals)
        ids = jnp.where(row == r, am, ids)
        s = jnp.where(iota == am, -jnp.inf, s)
    return vals, ids


def pack_rows_bf16(tab):
    e, d = tab.shape
    t16 = lax.bitcast_convert_type(tab.astype(BF16), jnp.uint16).astype(jnp.uint32)
    word = t16[:, :d // 2] | (t16[:, d // 2:] << 16)
    return lax.bitcast_convert_type(word, jnp.int32).reshape(e * EXPERT_ROWS, V7X_LANES)


def _slab_value_row(r):
    return r // 2 + EXPERT_ROWS * (r % 2)


PEER_GROUP = V7X_SUBLANES
SLAB_ROWS_BF16 = 2 * EXPERT_ROWS
PAIR_LANES = PEER_PAIRS * SLAB_ROWS_BF16


def _slab(tab_ref, row):
    return tab_ref[pl.ds(pl.multiple_of(row, EXPERT_ROWS), EXPERT_ROWS), :]


def _token_rows_bf16(h8, i):
    row = lax.broadcasted_iota(jnp.int32, (SLAB_ROWS_BF16, V7X_LANES), 0)
    hq = jnp.zeros((SLAB_ROWS_BF16, V7X_LANES), F32)
    for r in range(SLAB_ROWS_BF16):
        v = _slab_value_row(r)
        hq = jnp.where(row == r, h8[i:i + 1, v * V7X_LANES:(v + 1) * V7X_LANES], hq)
    return hq.astype(BF16)


def _peer_v_kernel(idx_ref, w_ref, x_ref, gate_ref, tab_ref, spread_ref, o_ref, g_ref):
    groups = w_ref.shape[0] // PEER_GROUP
    last_slot = PEER_GROUP - 1
    shape = (SLAB_ROWS_BF16, PAIR_LANES)
    out_row = lax.broadcasted_iota(jnp.int32, shape, 0)
    diag = (lax.broadcasted_iota(jnp.int32, shape, 1) % SLAB_ROWS_BF16) == 2 * (out_row % EXPERT_ROWS) + out_row // EXPERT_ROWS
    slot_rows = PEER_PAIRS * EXPERT_ROWS
    gate = gate_ref[0]

    @pl.when(pl.program_id(0) == 0)
    def _():
        g_ref[pl.ds(last_slot * slot_rows, slot_rows), :] = jnp.zeros((slot_rows, V7X_LANES), jnp.int32)

    row8 = lax.broadcasted_iota(jnp.int32, (PEER_GROUP, V7X_LANES), 0)
    blocks = range(SLAB_ROWS_BF16)

    def contract(slot, wrow, tile):
        rows = pltpu.bitcast(g_ref[pl.ds(slot * slot_rows, slot_rows), :], BF16)
        wi = jnp.where(diag, jnp.broadcast_to(wrow, shape), 0.0).astype(BF16)
        out = jnp.dot(wi, rows, preferred_element_type=F32)
        return [jnp.where(row8 == slot, jnp.broadcast_to(out[v:v + 1, :], row8.shape), tile[v]) for v in blocks]

    def flush(g, tile):
        rows = pl.ds(pl.multiple_of(g * PEER_GROUP, PEER_GROUP), PEER_GROUP)
        for v in blocks:
            lanes = slice(v * V7X_LANES, (v + 1) * V7X_LANES)
            o_ref[rows, lanes] = x_ref[rows, lanes] + gate[:, lanes] * tile[v]

    def group(g, carry):
        prev_wide, prev_tile = carry
        flush(jnp.maximum(g - 1, 0), contract(last_slot, prev_wide[last_slot:], list(prev_tile)))
        w8 = w_ref[pl.ds(pl.multiple_of(g * PEER_GROUP, PEER_GROUP), PEER_GROUP), :]
        wide = jnp.dot(w8.astype(BF16), spread_ref[...], preferred_element_type=F32)
        tile = [jnp.zeros(row8.shape, F32) for _ in blocks]
        for i in range(PEER_GROUP):
            tok_idx = idx_ref.at[pl.ds((g * PEER_GROUP + i) * PEER_PAIRS, PEER_PAIRS)]
            for j in range(PEER_PAIRS):
                g_ref[pl.ds((i * PEER_PAIRS + j) * EXPERT_ROWS, EXPERT_ROWS), :] = _slab(tab_ref, tok_idx[j])
            if i < last_slot:
                tile = contract(i, wide[i:i + 1], tile)
        return wide, tuple(tile)

    init = (jnp.zeros((PEER_GROUP, PAIR_LANES), F32), tuple(jnp.zeros(row8.shape, F32) for _ in blocks))
    wide, tile = lax.fori_loop(0, groups, group, init)
    flush(groups - 1, contract(last_slot, wide[last_slot:], list(tile)))


def peer_expert_mix(idx4, w, x, gate, tab, mod_index, tb=PEER_TILE):
    r, d = x.shape
    spread = jnp.repeat(jnp.eye(PEER_PAIRS, dtype=BF16), SLAB_ROWS_BF16, axis=1)
    return pl.pallas_call(
        _peer_v_kernel,
        grid=(r // tb,),
        in_specs=[
            pl.BlockSpec((tb * PEER_PAIRS,), lambda i: (i,), memory_space=pltpu.SMEM),
            pl.BlockSpec((tb, PEER_PAIRS), lambda i: (i, 0)),
            pl.BlockSpec((tb, d), lambda i: (i, 0)),
            pl.BlockSpec((1, 1, d), lambda i: (mod_index(i), 0, 0)),
            pl.BlockSpec(tab.shape, lambda i: (0, 0), pipeline_mode=pl.Buffered(1)),
            pl.BlockSpec((PEER_PAIRS, PAIR_LANES), lambda i: (0, 0)),
        ],
        out_specs=pl.BlockSpec((tb, d), lambda i: (i, 0)),
        out_shape=jax.ShapeDtypeStruct((r, d), F32),
        scratch_shapes=[pltpu.VMEM((PEER_GROUP * PEER_PAIRS * EXPERT_ROWS, V7X_LANES), jnp.int32)],
        compiler_params=_cparams("arbitrary"),
        name="peer_expert_mix",
    )(idx4, w, x, gate, tab, spread)


def _peer_front_kernel(x_ref, g_ref, sh_ref, sc_ref, wq_ref, keys_ref, tab_ref, fold_ref, idx_out_ref, w_out_ref,
                       hb_ref, e_scr, w_scr, h_buf, gw_buf, idx_vmem, idx_smem, s_ref, sem):
    step = pl.program_id(0)
    wslot = step % 2
    rslot = 1 - wslot
    tb = x_ref.shape[0]

    @pl.when(step == 0)
    def _():
        h_buf[1] = jnp.zeros(h_buf.shape[1:], F32)
        gw_buf[1] = jnp.zeros(gw_buf.shape[1:], F32)
        idx_vmem[...] = jnp.zeros(idx_vmem.shape, jnp.int32)
        fill = pltpu.make_async_copy(idx_vmem, idx_smem.at[1], sem)
        fill.start()
        fill.wait()

    h = _norm_mod(x_ref[...], g_ref[...], sh_ref[0], sc_ref[0])
    h_buf[wslot] = h
    hb_ref[...] = h.astype(BF16)
    iota_n = lax.broadcasted_iota(jnp.int32, (PEER_KEYS, tb), 0).astype(F32)
    half = PEER_TOPK // 2
    sub = lambda n: lax.broadcasted_iota(jnp.int32, (n, tb), 0).astype(F32)
    pos_c = jnp.concatenate([sub(PEER_TOPK)] + [sub(half) + float(a * PEER_TOPK) for a in range(1, half)]
                            + [(sub(half) + float(half)) * float(PEER_TOPK)], axis=0)

    def pair_up(first, second, scale):
        return jnp.concatenate([first[0:1] * scale + second]
                               + [first[a:a + 1] * scale + second[0:half] for a in range(1, half)]
                               + [first[half:] * scale + second[0:1]], axis=0)

    def retrieve_stages(hd):
        q = jnp.dot(hb_ref[...], wq_ref[hd], preferred_element_type=F32)
        tops = []

        def first_stage(p):
            qp = q[:, p * PEER_HALF:(p + 1) * PEER_HALF].astype(BF16)
            s = lax.dot_general(keys_ref[hd, p], qp, (((1,), (1,)), ((), ())),
                                preferred_element_type=F32)
            tops.append(_topk_rows(s, iota_n, PEER_TOPK))

        return [functools.partial(first_stage, 0), functools.partial(first_stage, 1),
                functools.partial(second_stage, hd, tops)]

    def second_stage(hd, tops):
        (s1, i1), (s2, i2) = tops
        cand = pair_up(s1, s2, 1.0)
        cidx = pair_up(i1, i2, float(PEER_KEYS))
        row = lax.broadcasted_iota(jnp.int32, (PEER_TOPK, tb), 0)
        sc = jnp.zeros((PEER_TOPK, tb), F32)
        ex = jnp.zeros((PEER_TOPK, tb), F32)
        for r in range(PEER_TOPK):
            m = jnp.max(cand, axis=0, keepdims=True)
            am = jnp.min(jnp.where(cand == m, pos_c, float(PEER_TOPK * PEER_TOPK)), axis=0, keepdims=True)
            hit = pos_c == am
            e = jnp.max(jnp.where(hit, cidx, 0.0), axis=0, keepdims=True)
            sc = jnp.where(row == r, m, sc)
            ex = jnp.where(row == r, e, ex)
            cand = jnp.where(hit, -jnp.inf, cand)
        pexp = jnp.exp(sc - sc[0:1])
        rows = pl.ds(pl.multiple_of(hd * PEER_TOPK, PEER_TOPK), PEER_TOPK)
        w_scr[rows, :] = pexp / jnp.sum(pexp, axis=0, keepdims=True)
        e_scr[rows, :] = ex

    row8 = lax.broadcasted_iota(jnp.int32, (PEER_GROUP, V7X_LANES), 0)
    h_prev, gw_prev, idx_prev = h_buf.at[rslot], gw_buf.at[rslot], idx_smem.at[rslot]

    def finish(g, sums):
        hi = sums.astype(BF16)
        lo = (sums - hi.astype(F32)).astype(BF16)
        a = (jnp.dot(hi, fold_ref[...], preferred_element_type=F32)
             + jnp.dot(lo, fold_ref[...], preferred_element_type=F32))
        rows = pl.ds(pl.multiple_of(g * PEER_GROUP, PEER_GROUP), PEER_GROUP)
        gelu = 0.5 * a * (1.0 + lax.erf(a * (2.0 ** -0.5)))
        w_out_ref[rows, :] = gelu * gw_prev[rows, :]

    def gate_group(g, prev, before_token):
        finish(jnp.maximum(g - 1, 0), prev)
        sums = jnp.zeros((PEER_GROUP, PAIR_LANES), F32)
        h8 = h_prev[pl.ds(pl.multiple_of(g * PEER_GROUP, PEER_GROUP), PEER_GROUP), :]
        for i in range(PEER_GROUP):
            if i in before_token:
                before_token[i]()
            hb = _token_rows_bf16(h8, i)
            tok_idx = idx_prev.at[g * PEER_GROUP + i]
            for j in range(PEER_PAIRS):
                u = pltpu.bitcast(_slab(tab_ref, tok_idx[j]), BF16)
                s_ref[pl.ds((i * PEER_PAIRS + j) * EXPERT_ROWS, EXPERT_ROWS), :] = pltpu.bitcast(u * hb, jnp.int32)
            prod = pltpu.bitcast(s_ref[pl.ds(i * PEER_PAIRS * EXPERT_ROWS, PEER_PAIRS * EXPERT_ROWS), :], BF16)
            pick = jnp.where(row8 == i, 1.0, 0.0).astype(BF16)
            sums = sums + lax.dot_general(pick, prod, (((1,), (1,)), ((), ())), preferred_element_type=F32)
        return sums

    groups_per_head = tb // PEER_GROUP // PEER_HEADS

    def body(hd, sums):
        first_a, first_b, second = retrieve_stages(hd)
        placement = [{0: first_a, PEER_GROUP // 2: first_b}, {0: second}]
        for k in range(groups_per_head):
            sums = gate_group(hd * groups_per_head + k, sums, placement[k] if k < len(placement) else {})
        return sums

    last = lax.fori_loop(0, PEER_HEADS, body, jnp.zeros((PEER_GROUP, PAIR_LANES), F32))
    finish(tb // PEER_GROUP - 1, last)
    gw_buf[wslot] = w_scr[...].T
    idx = (e_scr[...].T * float(EXPERT_ROWS)).astype(jnp.int32)
    idx_out_ref[...] = idx
    idx_vmem[...] = idx
    handoff = pltpu.make_async_copy(idx_vmem, idx_smem.at[wslot], sem)
    handoff.start()
    handoff.wait()


def peer_front(x, g, shift, scale, wq_heads, keys, tab, mod_index, tb=PEER_TILE):
    r, d = x.shape
    nb = r // tb
    fold = jnp.repeat(jnp.eye(PEER_PAIRS, dtype=BF16), SLAB_ROWS_BF16, axis=0)
    cur = lambda i: jnp.minimum(i, nb - 1)
    const = lambda shape: pl.BlockSpec(shape, lambda i: (0,) * len(shape), pipeline_mode=pl.Buffered(1))
    return pl.pallas_call(
        _peer_front_kernel,
        grid=(nb + 1,),
        in_specs=[
            pl.BlockSpec((tb, d), lambda i: (cur(i), 0)),
            pl.BlockSpec((1, d), lambda i: (0, 0)),
            pl.BlockSpec((1, 1, d), lambda i: (mod_index(cur(i)), 0, 0)),
            pl.BlockSpec((1, 1, d), lambda i: (mod_index(cur(i)), 0, 0)),
            const((PEER_HEADS, d, PEER_QDIM)),
            const((PEER_HEADS, 2, PEER_KEYS, PEER_HALF)),
            const(tab.shape),
            const((PAIR_LANES, PEER_PAIRS)),
        ],
        out_specs=[
            pl.BlockSpec((tb, PEER_PAIRS), lambda i: (cur(i), 0)),
            pl.BlockSpec((tb, PEER_PAIRS), lambda i: (jnp.maximum(i - 1, 0), 0)),
        ],
        out_shape=[jax.ShapeDtypeStruct((r, PEER_PAIRS), jnp.int32), jax.ShapeDtypeStruct((r, PEER_PAIRS), F32)],
        scratch_shapes=[
            pltpu.VMEM((tb, d), BF16), pltpu.VMEM((PEER_PAIRS, tb), F32), pltpu.VMEM((PEER_PAIRS, tb), F32),
            pltpu.VMEM((2, tb, d), F32), pltpu.VMEM((2, tb, PEER_PAIRS), F32),
            pltpu.VMEM((tb, PEER_PAIRS), jnp.int32), pltpu.SMEM((2, tb, PEER_PAIRS), jnp.int32),
            pltpu.VMEM((PEER_GROUP * PEER_PAIRS * EXPERT_ROWS, V7X_LANES), jnp.int32),
            pltpu.SemaphoreType.DMA,
        ],
        compiler_params=_cparams("arbitrary"),
        name="peer_front",
    )(x, g, shift, scale, wq_heads, keys, tab, fold)


def peer(x, g, shift, scale, gate, wq, keys, u_tab, v_tab, mod_index_peer):
    r, d = x.shape
    wq_heads = wq.astype(BF16).reshape(d, PEER_HEADS, PEER_QDIM).transpose(1, 0, 2)
    idx4, w = peer_front(x, g, shift, scale, wq_heads, keys.astype(BF16), pack_rows_bf16(u_tab), mod_index_peer)
    return peer_expert_mix(idx4.reshape(r * PEER_PAIRS), w, x, gate, pack_rows_bf16(v_tab), mod_index_peer)


HG_TILE = 128
HG_SUB = 16
HG_PAIR = 2 * HG_DK


def _hgrn_kernel(*refs, rev):
    blocks = HG_W // HG_PAIR
    q_refs, f_refs, v_refs = refs[:blocks], refs[blocks:2 * blocks], refs[2 * blocks:3 * blocks]
    lb_ref, o_ref, st_ref = refs[3 * blocks:]
    wide = lambda rs: jnp.concatenate([r[...] for r in rs], axis=1)

    @pl.when(pl.program_id(1) == 0)
    def _():
        st_ref[...] = jnp.zeros_like(st_ref)

    n = o_ref.shape[0]
    lb = lb_ref[...]
    f = lb + (1.0 - lb) * jax.nn.sigmoid(wide(f_refs))
    kk = 1.0 - f
    lf = jnp.log(f)
    t = lax.broadcasted_iota(jnp.int32, (n, n), 0)
    s = lax.broadcasted_iota(jnp.int32, (n, n), 1)
    same = (t // HG_SUB) == (s // HG_SUB)
    mid = (t // HG_SUB) * HG_SUB + (HG_SUB // 2 if rev else HG_SUB // 2 - 1)
    seen = (s >= t) if rev else (s <= t)
    seen_mid = (s >= mid) if rev else (s <= mid)
    one = lambda m: jnp.where(m, 1.0, 0.0).astype(F32)
    hp = lambda a, b: jnp.dot(a, b, precision=lax.Precision.HIGHEST, preferred_element_type=F32)
    cum = hp(one(same & seen), lf)
    ref = hp(one(same & seen_mid), lf)
    last = hp(one(same), lf)
    q = wide(q_refs)
    qe = (q * jnp.exp(cum)).astype(BF16)
    qm = (q * jnp.exp(cum - ref)).astype(BF16)
    km = (kk * jnp.exp(ref - cum)).astype(BF16)
    kd = (kk * jnp.exp(last - cum)).astype(BF16)
    dec = jnp.exp(last)
    vb = wide(v_refs).astype(BF16)
    steps = range(n // HG_SUB)
    heads = range(HG_HEADS)
    intra = []
    for h in heads:
        cols = slice(h * HG_DK, (h + 1) * HG_DK)
        att = lax.dot_general(qm[:, cols], km[:, cols], (((1,), (1,)), ((), ())), preferred_element_type=F32)
        att = jnp.where(same & seen, att, 0.0).astype(BF16)
        intra.append(jnp.dot(att, vb[:, cols], preferred_element_type=F32))
    for c in (reversed(steps) if rev else steps):
        rows = slice(c * HG_SUB, (c + 1) * HG_SUB)
        for h in heads:
            cols = slice(h * HG_DK, (h + 1) * HG_DK)
            st = st_ref[h]
            o_ref[rows, cols] = intra[h][rows] + lax.dot_general(
                qe[rows, cols], st.astype(BF16), (((1,), (1,)), ((), ())), preferred_element_type=F32)
            upd = lax.dot_general(vb[rows, cols], kd[rows, cols], (((0,), (0,)), ((), ())),
                                  preferred_element_type=F32)
            st_ref[h] = st * dec[c * HG_SUB:c * HG_SUB + 1, cols] + upd


def hgrn_scan(pa, lb, rev, batch):
    rows = pa.shape[0]
    per = rows // batch // HG_TILE
    nctx = CTX_LEN // HG_TILE
    col0 = GQA_IN // HG_PAIR
    blocks = HG_W // HG_PAIR

    def tok(b, k):
        if rev:
            k = jnp.where(k < nctx, nctx - 1 - k, per - 1 - (k - nctx))
        return b * per + k

    specs = lambda cb: [pl.BlockSpec((HG_TILE, HG_PAIR), lambda b, k, c=cb + p: (tok(b, k), c)) for p in range(blocks)]
    d = 1 if rev else 0
    return pl.pallas_call(
        functools.partial(_hgrn_kernel, rev=rev),
        grid=(batch, per),
        in_specs=specs(col0) + specs(col0 + (1 + d) * blocks) + specs(col0 + 3 * blocks)
        + [pl.BlockSpec((None, 1, HG_W), lambda b, k: (d, 0, 0))],
        out_specs=pl.BlockSpec((HG_TILE, HG_W), lambda b, k: (tok(b, k), 0)),
        out_shape=jax.ShapeDtypeStruct((rows, HG_W), F32),
        scratch_shapes=[pltpu.VMEM((HG_HEADS, HG_DV, HG_DK), F32)],
        compiler_params=_cparams("parallel", "arbitrary"),
        name="hgrn_scan_rev" if rev else "hgrn_scan_fwd",
    )(*([pa] * (3 * blocks)), lb)


def _rms_norm(x, g):
    xf = x.astype(F32)
    y = xf * lax.rsqrt(jnp.mean(xf * xf, axis=-1, keepdims=True) + EPS)
    return y * g.astype(F32)


def _hgrn2(pa2, batch, lb, onorm_g):
    lb = lb.reshape(2, 1, HG_W)
    o = hgrn_scan(pa2, lb, False, batch) + hgrn_scan(pa2, lb, True, batch)
    o = _rms_norm(o.reshape(-1, HG_HEADS, HG_DV), onorm_g).reshape(-1, HG_W)
    return o * jax.nn.silu(pa2[:, GQA_IN + 4 * HG_W:])


S5_CHUNK = 64
S5_LANES = S5_GROUPS * S5_STATE


def _s5_kernel(u_ref, bre_ref, bim_ref, cre_ref, cim_ref, are_ref, aim_ref, y_ref, xre, xim, hre, him):
    nb = V7X_SUBLANES // 2
    tiles = u_ref.shape[0] // V7X_SUBLANES
    direction = pl.program_id(0)
    rev = direction == 1

    @pl.when(pl.program_id(1) == 0)
    def _():
        hre[...] = jnp.zeros_like(hre)
        him[...] = jnp.zeros_like(him)

    ub = u_ref[...].astype(BF16)
    tile = V7X_MXU_DIM
    fan = (S5_LANES // tile) // (u_ref.shape[1] // tile)
    for nt in range(S5_LANES // tile):
        cols = slice(nt * tile, (nt + 1) * tile)
        src = ub[:, (nt // fan) * tile:(nt // fan + 1) * tile]
        xre[:, cols] = jnp.dot(src, bre_ref[nt], preferred_element_type=F32)
        xim[:, cols] = jnp.dot(src, bim_ref[nt], preferred_element_type=F32)
    shape = (V7X_SUBLANES, S5_LANES)
    second = (lax.broadcasted_iota(jnp.int32, shape, 0) // nb) != direction
    are = jnp.broadcast_to(are_ref[...], shape)
    aim = jnp.broadcast_to(aim_ref[...], shape)
    cre = jnp.where(second, are * are - aim * aim, are)
    cim = jnp.where(second, 2.0 * are * aim, aim)

    def step(k, carry):
        pr, pi = carry
        k = jnp.where(rev, tiles - 1 - k, k)
        r = pl.ds(pl.multiple_of(k * V7X_SUBLANES, V7X_SUBLANES), V7X_SUBLANES)
        xr, xi = xre[r, :], xim[r, :]
        sr = jnp.where(second, pltpu.roll(xr, nb, axis=0), 0.0)
        si = jnp.where(second, pltpu.roll(xi, nb, axis=0), 0.0)
        nr = xr + (are * sr - aim * si) + (cre * pr - cim * pi)
        ni = xi + (are * si + aim * sr) + (cre * pi + cim * pr)
        xre[r, :] = nr
        xim[r, :] = ni
        return (jnp.where(second, nr, pltpu.roll(nr, nb, axis=0)),
                jnp.where(second, ni, pltpu.roll(ni, nb, axis=0)))

    hr, hi = lax.fori_loop(0, tiles, step, (hre[...], him[...]))
    hre[...] = hr
    him[...] = hi
    for mt in range(u_ref.shape[1] // tile):
        acc = jnp.zeros((u_ref.shape[0], tile), F32)
        for kt in range(mt * fan, (mt + 1) * fan):
            cols = slice(kt * tile, (kt + 1) * tile)
            acc = acc + (jnp.dot(xre[:, cols].astype(BF16), cre_ref[kt], preferred_element_type=F32)
                         - jnp.dot(xim[:, cols].astype(BF16), cim_ref[kt], preferred_element_type=F32))
        y_ref[:, mt * tile:(mt + 1) * tile] = acc


def s5_scan(u, bre, bim, cre, cim, are, aim, nb):
    rows, w = u.shape
    blk = S5_CHUNK * nb
    assert rows % blk == 0 and 2 * nb == V7X_SUBLANES
    nblk, nctx = rows // blk, CTX_LEN // S5_CHUNK
    wspec = lambda shape: pl.BlockSpec((None,) + shape, lambda d, i: (d, 0, 0))
    tiles = pl.BlockSpec((None,) + bre.shape[1:], lambda d, i: (d, 0, 0, 0))

    def chunk(d, i):
        back = jnp.where(i < nctx, nctx - 1 - i, nblk - 1 - (i - nctx))
        return jnp.where(d == 1, back, i)
    return pl.pallas_call(
        _s5_kernel,
        grid=(2, rows // blk),
        in_specs=[
            pl.BlockSpec((blk, w), lambda d, i: (chunk(d, i), 0)),
            tiles, tiles, tiles, tiles,
            wspec((1, S5_LANES)), wspec((1, S5_LANES)),
        ],
        out_specs=pl.BlockSpec((None, blk, w), lambda d, i: (d, chunk(d, i), 0)),
        out_shape=jax.ShapeDtypeStruct((2, rows, w), F32),
        scratch_shapes=[pltpu.VMEM((blk, S5_LANES), F32), pltpu.VMEM((blk, S5_LANES), F32),
                        pltpu.VMEM((V7X_SUBLANES, S5_LANES), F32), pltpu.VMEM((V7X_SUBLANES, S5_LANES), F32)],
        compiler_params=_cparams("arbitrary", "arbitrary"),
        name="s5_scan",
    )(u, bre, bim, cre, cim, are, aim)


def _s5_core(u, a_re, a_im, log_dt, b_re, b_im, c_re, c_im, d_skip):
    b, s, w = u.shape
    dt = jnp.exp(log_dt)[..., None]
    mag = jnp.exp(a_re * dt)
    abar_re, abar_im = mag * jnp.cos(a_im * dt), mag * jnp.sin(a_im * dt)
    den = a_re * a_re + a_im * a_im
    k_re = ((abar_re - 1.0) * a_re + abar_im * a_im) / den
    k_im = (abar_im * a_re - (abar_re - 1.0) * a_im) / den
    bb_re = k_re[..., None] * b_re - k_im[..., None] * b_im
    bb_im = k_re[..., None] * b_im + k_im[..., None] * b_re
    eye = jnp.eye(S5_GROUPS, dtype=F32)
    tile, n_tiles = V7X_MXU_DIM, S5_LANES // V7X_MXU_DIM
    fan = n_tiles // (w // tile)
    cut = lambda m, r, c: m[:, r * tile:(r + 1) * tile, c * tile:(c + 1) * tile]

    def bd_in(m):
        full = jnp.einsum('dgpc,gh->dgchp', m, eye).reshape(2, w, S5_LANES).astype(BF16)
        return jnp.stack([cut(full, nt // fan, nt) for nt in range(n_tiles)], axis=1)

    def bd_out(m):
        full = jnp.einsum('dgcp,gh->dgphc', m, eye).reshape(2, S5_LANES, w).astype(BF16)
        return jnp.stack([cut(full, kt, kt // fan) for kt in range(n_tiles)], axis=1)
    y2 = s5_scan(u.transpose(1, 0, 2).reshape(s * b, w), bd_in(bb_re), bd_in(bb_im), bd_out(c_re), bd_out(c_im),
                 abar_re.reshape(2, 1, S5_LANES), abar_im.reshape(2, 1, S5_LANES), b)
    return d_skip * u + (y2[0] + y2[1]).reshape(s, b, w).transpose(1, 0, 2)


def _rope_tables(rows, rot_dim, lead, period):
    axis_dim = rot_dim // 2
    inv = ROPE_THETA ** (-jnp.arange(0, axis_dim, 2, dtype=F32) / axis_dim)
    t = jnp.arange(rows * GRID_W)
    r = (t // GRID_W).astype(F32)[:, None] * inv
    c = (t % GRID_W).astype(F32)[:, None] * inv
    n = t.shape[0]
    tail = period - lead - rot_dim
    cos = jnp.concatenate([jnp.ones((n, lead), F32), jnp.cos(r), jnp.cos(r), jnp.cos(c), jnp.cos(c),
                           jnp.ones((n, tail), F32)], axis=-1)
    sin = jnp.concatenate([jnp.zeros((n, lead), F32), -jnp.sin(r), jnp.sin(r), -jnp.sin(c), jnp.sin(c),
                           jnp.zeros((n, tail), F32)], axis=-1)
    reps = V7X_LANES // period
    cos = jnp.concatenate([jnp.ones((CTX_LEN, V7X_LANES), F32), jnp.tile(cos, (1, reps))], axis=0)
    sin = jnp.concatenate([jnp.zeros((CTX_LEN, V7X_LANES), F32), jnp.tile(sin, (1, reps))], axis=0)
    return cos, sin


def _attend_all(q, k, v, need_ctx):
    o_lat = attention(q[:, CTX_LEN:], k, v)
    if need_ctx:
        o_ctx = attention(q[:, :CTX_LEN], k[:, :CTX_LEN], v[:, :CTX_LEN])
    else:
        o_ctx = jnp.zeros((q.shape[0], CTX_LEN, q.shape[2]), F32)
    return jnp.concatenate([o_ctx, o_lat], axis=1)


def _gqa(pa2, b, qn_g, kn_g, cos, sin, need_ctx):
    s = pa2.shape[0] // b
    grp = GQA_Q_HEADS // GQA_KV_HEADS
    gains = jnp.concatenate([jnp.tile(qn_g, GQA_Q_HEADS), jnp.tile(kn_g, GQA_KV_HEADS)])[None, :]
    q, k, v = gqa_prep(pa2, gains, cos, sin, s)
    o = _attend_all(q.reshape(b, s, -1), k.reshape(b, s, -1), v.reshape(b, s, -1), need_ctx)
    o = o.reshape(b, s, GQA_Q_HEADS, V7X_LANES)
    first = (jnp.arange(GQA_Q_HEADS) < grp)[None, None, :, None]
    o = jnp.where(first, o[..., :HEAD_DIM], o[..., HEAD_DIM:])
    return o.reshape(b, s, GQA_Q_HEADS * HEAD_DIM)


def _mla(p, qa_g, w_qup, kva_g, w_kvup, cos, sin, need_ctx):
    b, s, _ = p.shape
    r0 = MLA_Q_RANK
    r1 = MLA_Q_RANK + MLA_KV_RANK
    blocks = lambda w, lo, hi: jnp.pad(w.reshape(w.shape[0], MLA_HEADS, -1)[..., lo:hi],
                                       ((0, 0), (0, 0), (0, V7X_LANES - (hi - lo)))).reshape(w.shape[0], -1)
    w_q = blocks(w_qup, 0, MLA_NOPE + MLA_ROPE).astype(BF16)
    w_kv = jnp.concatenate([blocks(w_kvup, 0, MLA_NOPE), blocks(w_kvup, MLA_NOPE, MLA_NOPE + MLA_V)],
                           axis=1).astype(BF16)
    zero = jnp.zeros((1, 1, r0), F32)
    q = norm_mod_matmul(p[..., :r0].reshape(b * s, r0), qa_g[None, :], zero, zero, w_q, lambda i: 0)
    zero = jnp.zeros((1, 1, MLA_KV_RANK), F32)
    kv = norm_mod_matmul(p[..., r0:r1].reshape(b * s, MLA_KV_RANK), kva_g[None, :], zero, zero, w_kv, lambda i: 0)
    kr = jnp.pad(p[..., r1:r1 + MLA_ROPE].reshape(b * s, MLA_ROPE),
                 ((0, 0), (MLA_NOPE, V7X_LANES - MLA_NOPE - MLA_ROPE)))
    q, k, v = mla_prep(q, kv, kr, cos, sin, s)
    o = _attend_all(q.reshape(b, s, -1), k.reshape(b, s, -1), v.reshape(b, s, -1), need_ctx)
    return o.reshape(b, s, MLA_HEADS, V7X_LANES)[..., :MLA_V].reshape(b, s, MLA_HEADS * MLA_V)


def kernel(x, c, ctx, c_ctx, ada_w, ada_b, norm1_g, norm2_g, ev_w_in, ev_w_out, gqa_qn_g, gqa_kn_g,
           hg_lb_logits, hg_onorm_g, od_w_in, od_w_out, mla_qa_g, mla_w_qup, mla_kva_g, mla_w_kvup,
           s5_a_re, s5_a_im, s5_log_dt, s5_b_re, s5_b_im, s5_c_re, s5_c_im, s5_d, s5_w_glu, s5_b_glu,
           peer_wq, peer_keys, peer_u, peer_v, final_g):
    b, t, d = x.shape
    s = CTX_LEN + t
    depth = ada_w.shape[0]
    rows = t // GRID_W
    cos_g, sin_g = _rope_tables(rows, HEAD_DIM, 0, HEAD_DIM)
    cos_m, sin_m = _rope_tables(rows, MLA_ROPE, MLA_NOPE, V7X_LANES)
    lb_all = jnp.cumsum(jax.nn.softmax(hg_lb_logits, axis=1), axis=1)

    def mod_index(tile):
        per, nctx = s // tile, CTX_LEN // tile
        return lambda i: jnp.where(i % per < nctx, b, i // per)

    mi_row = mod_index(ROW_TILE)
    mi_peer = mod_index(PEER_TILE)

    xa = jnp.concatenate([ctx, x], axis=1).reshape(b * s, d)
    s_all = jnp.concatenate([jax.nn.silu(c), jax.nn.silu(c_ctx)[None, :]], axis=0)
    s_pad = jnp.concatenate([s_all, jnp.zeros((V7X_SUBLANES - (b + 1) % V7X_SUBLANES, d), F32)], axis=0)

    for layer in range(depth):
        need_ctx = layer < depth - 1
        j = layer // 2
        mod = matmul(s_pad, ada_w[layer].astype(BF16), tm=s_pad.shape[0])[:b + 1] + ada_b[layer]
        mod = [m[:, None, :] for m in jnp.split(mod, 6, axis=-1)]
        if layer % 2 == 0:
            pa2 = norm_mod_matmul(xa, norm1_g[layer][None, :], mod[0], mod[1], ev_w_in[j].astype(BF16), mi_row)
            ya = _gqa(pa2, b, gqa_qn_g[j], gqa_kn_g[j], cos_g, sin_g, need_ctx)
            yb = _hgrn2(pa2, b, lb_all[:, j], hg_onorm_g[j]).reshape(b, s, HG_W)
            w_out = ev_w_out[j]
        else:
            pa = norm_mod_matmul(xa, norm1_g[layer][None, :], mod[0], mod[1], od_w_in[j].astype(BF16), mi_row)
            pa = pa.reshape(b, s, -1)
            ya = _mla(pa[..., :MLA_IN], mla_qa_g[j], mla_w_qup[j], mla_kva_g[j], mla_w_kvup[j], cos_m, sin_m, need_ctx)
            y5 = _s5_core(pa[..., MLA_IN:], s5_a_re[j], s5_a_im[j], s5_log_dt[j], s5_b_re[j], s5_b_im[j],
                          s5_c_re[j], s5_c_im[j], s5_d[j])
            z = jax.nn.gelu(y5, approximate=False).reshape(b * s, S5_WIDTH)
            gl = matmul(z, s5_w_glu[j].astype(BF16)) + s5_b_glu[j]
            yb = (z * jax.nn.sigmoid(gl)).reshape(b, s, S5_WIDTH)
            w_out = od_w_out[j]
        xa = matmul_residual(ya.reshape(b * s, -1), yb.reshape(b * s, -1), w_out.astype(BF16), xa, mod[2], mi_row)
        xa = peer(xa, norm2_g[layer][None, :], mod[3], mod[4], mod[5],
                  peer_wq[layer], peer_keys[layer], peer_u[layer], peer_v[layer], mi_peer)
    xl = xa.reshape(b, s, d)[:, CTX_LEN:].reshape(b * t, d)
    return rmsnorm_rows(xl, final_g[None, :]).reshape(b, t, d)
```

```python
import functools
import math

import jax
import jax.numpy as jnp
from jax import lax
from jax.experimental import pallas as pl
from jax.experimental.pallas import tpu as pltpu

F32 = jnp.float32
BF16 = jnp.bfloat16

D_MODEL = 1024
GRID_W = 64
CTX_LEN = 256
EPS = 1e-6
ROPE_THETA = 10000.0

MIX_HALF = D_MODEL // 2
HEAD_DIM = 64
GQA_Q_HEADS = MIX_HALF // HEAD_DIM
GQA_KV_HEADS = GQA_Q_HEADS // 4
GQA_IN = (GQA_Q_HEADS + 2 * GQA_KV_HEADS) * HEAD_DIM

HG_DK = 128
HG_DV = 128
HG_HEADS = MIX_HALF // HG_DV
HG_W = HG_HEADS * HG_DK

MLA_HEADS = MIX_HALF // HEAD_DIM
MLA_NOPE = 64
MLA_ROPE = 32
MLA_V = 64
MLA_Q_RANK = 384
MLA_KV_RANK = 256
MLA_IN = MLA_Q_RANK + MLA_KV_RANK + MLA_ROPE

S5_WIDTH = MIX_HALF
S5_GROUP = 16
S5_GROUPS = S5_WIDTH // S5_GROUP
S5_STATE = 64

PEER_HEADS = 8
PEER_KEYS = 128
N_EXPERTS = PEER_KEYS * PEER_KEYS
PEER_TOPK = 16
PEER_QDIM = 256
PEER_HALF = PEER_QDIM // 2
PEER_PAIRS = PEER_HEADS * PEER_TOPK

V7X_LANES = 128
V7X_SUBLANES = 8
V7X_VMEM_BYTES = 64 * 1024 * 1024
VMEM_LIMIT = V7X_VMEM_BYTES - 8 * 1024 * 1024

ROW_TILE = 256
PEER_TILE = 128
ATTN_Q_TILE = 1024
EXPERT_WORDS = D_MODEL // 2
EXPERT_ROWS = EXPERT_WORDS // V7X_LANES


def _cparams(*sem):
    return pltpu.CompilerParams(dimension_semantics=sem, vmem_limit_bytes=VMEM_LIMIT)


def _norm_mod(x, g, shift, scale):
    ms = jnp.mean(x * x, axis=-1, keepdims=True)
    h = (x * lax.rsqrt(ms + EPS)) * g
    return h * (1.0 + scale) + shift


def _nmm_kernel(x_ref, g_ref, sh_ref, sc_ref, w_ref, o_ref):
    h = _norm_mod(x_ref[...], g_ref[...], sh_ref[0], sc_ref[0])
    o_ref[...] = jnp.dot(h.astype(BF16), w_ref[...], preferred_element_type=F32)


def norm_mod_matmul(x, g, shift, scale, w, mod_index, tm=ROW_TILE):
    r, k = x.shape
    n = w.shape[1]
    assert r % tm == 0 and w.shape[0] == k
    return pl.pallas_call(
        _nmm_kernel,
        grid=(r // tm,),
        in_specs=[
            pl.BlockSpec((tm, k), lambda i: (i, 0)),
            pl.BlockSpec((1, k), lambda i: (0, 0)),
            pl.BlockSpec((1, 1, k), lambda i: (mod_index(i), 0, 0)),
            pl.BlockSpec((1, 1, k), lambda i: (mod_index(i), 0, 0)),
            pl.BlockSpec((k, n), lambda i: (0, 0)),
        ],
        out_specs=pl.BlockSpec((tm, n), lambda i: (i, 0)),
        out_shape=jax.ShapeDtypeStruct((r, n), F32),
        compiler_params=_cparams("parallel"),
        name="norm_mod_matmul",
    )(x, g, shift, scale, w)


def _mm_kernel(a_ref, w_ref, o_ref):
    o_ref[...] = jnp.dot(a_ref[...].astype(BF16), w_ref[...], preferred_element_type=F32)


def matmul(a, w, tm=ROW_TILE):
    r, k = a.shape
    n = w.shape[1]
    assert r % tm == 0
    return pl.pallas_call(
        _mm_kernel,
        grid=(r // tm,),
        in_specs=[pl.BlockSpec((tm, k), lambda i: (i, 0)), pl.BlockSpec((k, n), lambda i: (0, 0))],
        out_specs=pl.BlockSpec((tm, n), lambda i: (i, 0)),
        out_shape=jax.ShapeDtypeStruct((r, n), F32),
        compiler_params=_cparams("parallel"),
        name="matmul",
    )(a, w)


def _mmres_kernel(a_ref, b_ref, w_ref, x_ref, gate_ref, o_ref):
    ka = a_ref.shape[1]
    y = (jnp.dot(a_ref[...].astype(BF16), w_ref[:ka, :], preferred_element_type=F32)
         + jnp.dot(b_ref[...].astype(BF16), w_ref[ka:, :], preferred_element_type=F32))
    o_ref[...] = x_ref[...] + gate_ref[0] * y


def matmul_residual(a, b, w, x, gate, mod_index, tm=ROW_TILE):
    r, ka = a.shape
    kb = b.shape[1]
    k, n = w.shape
    assert r % tm == 0 and ka + kb == k
    return pl.pallas_call(
        _mmres_kernel,
        grid=(r // tm,),
        in_specs=[
            pl.BlockSpec((tm, ka), lambda i: (i, 0)),
            pl.BlockSpec((tm, kb), lambda i: (i, 0)),
            pl.BlockSpec((k, n), lambda i: (0, 0)),
            pl.BlockSpec((tm, n), lambda i: (i, 0)),
            pl.BlockSpec((1, 1, n), lambda i: (mod_index(i), 0, 0)),
        ],
        out_specs=pl.BlockSpec((tm, n), lambda i: (i, 0)),
        out_shape=jax.ShapeDtypeStruct((r, n), F32),
        compiler_params=_cparams("parallel"),
        name="matmul_residual",
    )(a, b, w, x, gate)


def _rms_kernel(x_ref, g_ref, o_ref):
    x = x_ref[...]
    ms = jnp.mean(x * x, axis=-1, keepdims=True)
    o_ref[...] = (x * lax.rsqrt(ms + EPS)) * g_ref[...]


def rmsnorm_rows(x, g, tm=ROW_TILE):
    r, k = x.shape
    return pl.pallas_call(
        _rms_kernel,
        grid=(r // tm,),
        in_specs=[pl.BlockSpec((tm, k), lambda i: (i, 0)), pl.BlockSpec((1, k), lambda i: (0, 0))],
        out_specs=pl.BlockSpec((tm, k), lambda i: (i, 0)),
        out_shape=jax.ShapeDtypeStruct((r, k), F32),
        compiler_params=_cparams("parallel"),
        name="rmsnorm_rows",
    )(x, g)


ATTN_KV_CHUNKS = 3
V7X_MXU_DIM = 256
LOG2E = math.log2(math.e)


def _attn_kernel(q_ref, k_ref, v_ref, o_ref):
    q = q_ref[...]
    sk = k_ref.shape[0]
    nch = ATTN_KV_CHUNKS if sk % (ATTN_KV_CHUNKS * V7X_MXU_DIM) == 0 else 1
    ck = sk // nch
    m = l = acc = None
    for c in range(nch):
        kc = k_ref[c * ck:(c + 1) * ck, :]
        vc = v_ref[c * ck:(c + 1) * ck, :]
        s = lax.dot_general(q, kc, (((1,), (1,)), ((), ())), preferred_element_type=F32)
        mc = jnp.max(s, axis=-1, keepdims=True)
        if c == 0:
            m = mc
            p = jnp.exp2(s - m)
            l = jnp.sum(p, axis=-1, keepdims=True)
            acc = jnp.dot(p.astype(BF16), vc, preferred_element_type=F32)
        else:
            m_new = jnp.maximum(m, mc)
            a = jnp.exp2(m - m_new)
            p = jnp.exp2(s - m_new)
            l = a * l + jnp.sum(p, axis=-1, keepdims=True)
            acc = a * acc + jnp.dot(p.astype(BF16), vc, preferred_element_type=F32)
            m = m_new
    o_ref[...] = acc / l


def attention(q, k, v, tq=ATTN_Q_TILE):
    b, sq, hl = q.shape
    sk, hkl = k.shape[1], k.shape[2]
    h, hk = hl // V7X_LANES, hkl // V7X_LANES
    grp = h // hk
    tq = min(tq, sq)
    assert sq % tq == 0
    return pl.pallas_call(
        _attn_kernel,
        grid=(b, h, sq // tq),
        in_specs=[
            pl.BlockSpec((None, tq, V7X_LANES), lambda bi, hi, qi: (bi, qi, hi)),
            pl.BlockSpec((None, sk, V7X_LANES), lambda bi, hi, qi: (bi, 0, hi // grp)),
            pl.BlockSpec((None, sk, V7X_LANES), lambda bi, hi, qi: (bi, 0, hi // grp)),
        ],
        out_specs=pl.BlockSpec((None, tq, V7X_LANES), lambda bi, hi, qi: (bi, qi, hi)),
        out_shape=jax.ShapeDtypeStruct((b, sq, hl), F32),
        compiler_params=_cparams("parallel", "parallel", "parallel"),
        name="attention",
    )(q, k, v)


def _rotate(x, cos, sin, quarter):
    n = x.shape[1]
    lane = lax.broadcasted_iota(jnp.int32, x.shape, 1)
    partner = jnp.where(lane % (2 * quarter) < quarter, pltpu.roll(x, n - quarter, axis=1),
                        pltpu.roll(x, quarter, axis=1))
    return x * cos + partner * sin


def _gqa_prep_kernel(p_ref, g_ref, cos_ref, sin_ref, avg_ref, q_ref, k_ref, v_ref):
    cos, sin = cos_ref[...], sin_ref[...]
    low = lax.broadcasted_iota(jnp.int32, cos.shape, 1) < HEAD_DIM
    qk_blocks = (GQA_Q_HEADS + GQA_KV_HEADS) * HEAD_DIM // V7X_LANES
    q_blocks = GQA_Q_HEADS * HEAD_DIM // V7X_LANES
    for c in range(qk_blocks):
        lanes = slice(c * V7X_LANES, (c + 1) * V7X_LANES)
        x = p_ref[:, lanes]
        ms = jnp.dot(x * x, avg_ref[...], precision=lax.Precision.HIGHEST, preferred_element_type=F32)
        y = _rotate(x * lax.rsqrt(ms + EPS) * g_ref[:, lanes], cos, sin, HEAD_DIM // 4)
        if c < q_blocks:
            y = y * (HEAD_DIM ** -0.5 * LOG2E)
            other = pltpu.roll(y, HEAD_DIM, axis=1)
            if c < q_blocks // 2:
                first, second = jnp.where(low, y, 0.0), jnp.where(low, other, 0.0)
            else:
                first, second = jnp.where(low, 0.0, other), jnp.where(low, 0.0, y)
            q_ref[:, 2 * c * V7X_LANES:(2 * c + 1) * V7X_LANES] = first.astype(BF16)
            q_ref[:, (2 * c + 1) * V7X_LANES:(2 * c + 2) * V7X_LANES] = second.astype(BF16)
        else:
            k_ref[...] = y.astype(BF16)
    v_ref[...] = p_ref[:, qk_blocks * V7X_LANES:(qk_blocks + 1) * V7X_LANES].astype(BF16)


def gqa_prep(pa, gains, cos, sin, seq, tm=ROW_TILE):
    r = pa.shape[0]
    per = seq // tm
    avg = jnp.kron(jnp.eye(V7X_LANES // HEAD_DIM, dtype=F32), jnp.full((HEAD_DIM, HEAD_DIM), 1.0 / HEAD_DIM, F32))
    tab = pl.BlockSpec((tm, V7X_LANES), lambda i: (i % per, 0))
    kv = pl.BlockSpec((tm, V7X_LANES), lambda i: (i, 0))
    return pl.pallas_call(
        _gqa_prep_kernel,
        grid=(r // tm,),
        in_specs=[pl.BlockSpec((tm, GQA_IN), lambda i: (i, 0)), pl.BlockSpec(gains.shape, lambda i: (0, 0)), tab, tab,
                  pl.BlockSpec(avg.shape, lambda i: (0, 0))],
        out_specs=[pl.BlockSpec((tm, GQA_Q_HEADS * V7X_LANES), lambda i: (i, 0)), kv, kv],
        out_shape=[jax.ShapeDtypeStruct((r, GQA_Q_HEADS * V7X_LANES), BF16),
                   jax.ShapeDtypeStruct((r, V7X_LANES), BF16), jax.ShapeDtypeStruct((r, V7X_LANES), BF16)],
        compiler_params=_cparams("parallel"),
        name="gqa_prep",
    )(pa, gains, cos, sin, avg)


def _mla_prep_kernel(q_ref, kv_ref, kr_ref, cos_ref, sin_ref, qo_ref, ko_ref, vo_ref):
    cos, sin = cos_ref[...], sin_ref[...]
    quarter = MLA_ROPE // 4
    kr = _rotate(kr_ref[...], cos, sin, quarter)
    scale = (MLA_NOPE + MLA_ROPE) ** -0.5 * LOG2E
    width = MLA_HEADS * V7X_LANES
    for h in range(MLA_HEADS):
        lanes = slice(h * V7X_LANES, (h + 1) * V7X_LANES)
        qo_ref[:, lanes] = (_rotate(q_ref[:, lanes], cos, sin, quarter) * scale).astype(BF16)
        ko_ref[:, lanes] = (kv_ref[:, lanes] + kr).astype(BF16)
        vo_ref[:, lanes] = kv_ref[:, width + h * V7X_LANES:width + (h + 1) * V7X_LANES].astype(BF16)


def mla_prep(q, kv, kr, cos, sin, seq, tm=ROW_TILE):
    r, width = q.shape
    per = seq // tm
    tab = pl.BlockSpec((tm, V7X_LANES), lambda i: (i % per, 0))
    out = pl.BlockSpec((tm, width), lambda i: (i, 0))
    return pl.pallas_call(
        _mla_prep_kernel,
        grid=(r // tm,),
        in_specs=[out, pl.BlockSpec((tm, 2 * width), lambda i: (i, 0)), pl.BlockSpec((tm, V7X_LANES), lambda i: (i, 0)),
                  tab, tab],
        out_specs=[out, out, out],
        out_shape=[jax.ShapeDtypeStruct((r, width), BF16)] * 3,
        compiler_params=_cparams("parallel"),
        name="mla_prep",
    )(q, kv, kr, cos, sin)


def _topk_rows(s, iota, k):
    n = s.shape[0]
    row = lax.broadcasted_iota(jnp.int32, (k, s.shape[1]), 0)
    vals = jnp.zeros((k, s.shape[1]), F32)
    ids = jnp.zeros((k, s.shape[1]), F32)
    for r in range(k):
        m = jnp.max(s, axis=0, keepdims=True)
        am = jnp.min(jnp.where(s == m, iota, float(n)), axis=0, keepdims=True)
        vals = jnp.where(row == r, m, vals)
        ids = jnp.where(row == r, am, ids)
        s = jnp.where(iota == am, -jnp.inf, s)
    return vals, ids


def pack_rows_bf16(tab):
    e, d = tab.shape
    t16 = lax.bitcast_convert_type(tab.astype(BF16), jnp.uint16).astype(jnp.uint32)
    word = t16[:, :d // 2] | (t16[:, d // 2:] << 16)
    return lax.bitcast_convert_type(word, jnp.int32).reshape(e * EXPERT_ROWS, V7X_LANES)


def _slab_value_row(r):
    return r // 2 + EXPERT_ROWS * (r % 2)


PEER_GROUP = V7X_SUBLANES
SLAB_ROWS_BF16 = 2 * EXPERT_ROWS
PAIR_LANES = PEER_PAIRS * SLAB_ROWS_BF16


def _slab(tab_ref, row):
    return tab_ref[pl.ds(pl.multiple_of(row, EXPERT_ROWS), EXPERT_ROWS), :]


def _token_rows_bf16(h8, i):
    row = lax.broadcasted_iota(jnp.int32, (SLAB_ROWS_BF16, V7X_LANES), 0)
    hq = jnp.zeros((SLAB_ROWS_BF16, V7X_LANES), F32)
    for r in range(SLAB_ROWS_BF16):
        v = _slab_value_row(r)
        hq = jnp.where(row == r, h8[i:i + 1, v * V7X_LANES:(v + 1) * V7X_LANES], hq)
    return hq.astype(BF16)


def _peer_v_kernel(idx_ref, w_ref, x_ref, gate_ref, tab_ref, spread_ref, o_ref, g_ref):
    groups = w_ref.shape[0] // PEER_GROUP
    last_slot = PEER_GROUP - 1
    shape = (SLAB_ROWS_BF16, PAIR_LANES)
    out_row = lax.broadcasted_iota(jnp.int32, shape, 0)
    diag = (lax.broadcasted_iota(jnp.int32, shape, 1) % SLAB_ROWS_BF16) == 2 * (out_row % EXPERT_ROWS) + out_row // EXPERT_ROWS
    slot_rows = PEER_PAIRS * EXPERT_ROWS
    gate = gate_ref[0]

    @pl.when(pl.program_id(0) == 0)
    def _():
        g_ref[pl.ds(last_slot * slot_rows, slot_rows), :] = jnp.zeros((slot_rows, V7X_LANES), jnp.int32)

    row8 = lax.broadcasted_iota(jnp.int32, (PEER_GROUP, V7X_LANES), 0)
    blocks = range(SLAB_ROWS_BF16)

    def contract(slot, wrow, tile):
        rows = pltpu.bitcast(g_ref[pl.ds(slot * slot_rows, slot_rows), :], BF16)
        wi = jnp.where(diag, jnp.broadcast_to(wrow, shape), 0.0).astype(BF16)
        out = jnp.dot(wi, rows, preferred_element_type=F32)
        return [jnp.where(row8 == slot, jnp.broadcast_to(out[v:v + 1, :], row8.shape), tile[v]) for v in blocks]

    def flush(g, tile):
        rows = pl.ds(pl.multiple_of(g * PEER_GROUP, PEER_GROUP), PEER_GROUP)
        for v in blocks:
            lanes = slice(v * V7X_LANES, (v + 1) * V7X_LANES)
            o_ref[rows, lanes] = x_ref[rows, lanes] + gate[:, lanes] * tile[v]

    def group(g, carry):
        prev_wide, prev_tile = carry
        flush(jnp.maximum(g - 1, 0), contract(last_slot, prev_wide[last_slot:], list(prev_tile)))
        w8 = w_ref[pl.ds(pl.multiple_of(g * PEER_GROUP, PEER_GROUP), PEER_GROUP), :]
        wide = jnp.dot(w8.astype(BF16), spread_ref[...], preferred_element_type=F32)
        tile = [jnp.zeros(row8.shape, F32) for _ in blocks]
        for i in range(PEER_GROUP):
            tok_idx = idx_ref.at[pl.ds((g * PEER_GROUP + i) * PEER_PAIRS, PEER_PAIRS)]
            for j in range(PEER_PAIRS):
                g_ref[pl.ds((i * PEER_PAIRS + j) * EXPERT_ROWS, EXPERT_ROWS), :] = _slab(tab_ref, tok_idx[j])
            if i < last_slot:
                tile = contract(i, wide[i:i + 1], tile)
        return wide, tuple(tile)

    init = (jnp.zeros((PEER_GROUP, PAIR_LANES), F32), tuple(jnp.zeros(row8.shape, F32) for _ in blocks))
    wide, tile = lax.fori_loop(0, groups, group, init)
    flush(groups - 1, contract(last_slot, wide[last_slot:], list(tile)))


def peer_expert_mix(idx4, w, x, gate, tab, mod_index, tb=PEER_TILE):
    r, d = x.shape
    spread = jnp.repeat(jnp.eye(PEER_PAIRS, dtype=BF16), SLAB_ROWS_BF16, axis=1)
    return pl.pallas_call(
        _peer_v_kernel,
        grid=(r // tb,),
        in_specs=[
            pl.BlockSpec((tb * PEER_PAIRS,), lambda i: (i,), memory_space=pltpu.SMEM),
            pl.BlockSpec((tb, PEER_PAIRS), lambda i: (i, 0)),
            pl.BlockSpec((tb, d), lambda i: (i, 0)),
            pl.BlockSpec((1, 1, d), lambda i: (mod_index(i), 0, 0)),
            pl.BlockSpec(tab.shape, lambda i: (0, 0), pipeline_mode=pl.Buffered(1)),
            pl.BlockSpec((PEER_PAIRS, PAIR_LANES), lambda i: (0, 0)),
        ],
        out_specs=pl.BlockSpec((tb, d), lambda i: (i, 0)),
        out_shape=jax.ShapeDtypeStruct((r, d), F32),
        scratch_shapes=[pltpu.VMEM((PEER_GROUP * PEER_PAIRS * EXPERT_ROWS, V7X_LANES), jnp.int32)],
        compiler_params=_cparams("arbitrary"),
        name="peer_expert_mix",
    )(idx4, w, x, gate, tab, spread)


def _peer_front_kernel(x_ref, g_ref, sh_ref, sc_ref, wq_ref, keys_ref, tab_ref, fold_ref, idx_out_ref, w_out_ref,
                       hb_ref, e_scr, w_scr, h_buf, gw_buf, idx_vmem, idx_smem, s_ref, sem):
    step = pl.program_id(0)
    wslot = step % 2
    rslot = 1 - wslot
    tb = x_ref.shape[0]

    @pl.when(step == 0)
    def _():
        h_buf[1] = jnp.zeros(h_buf.shape[1:], F32)
        gw_buf[1] = jnp.zeros(gw_buf.shape[1:], F32)
        idx_vmem[...] = jnp.zeros(idx_vmem.shape, jnp.int32)
        fill = pltpu.make_async_copy(idx_vmem, idx_smem.at[1], sem)
        fill.start()
        fill.wait()

    h = _norm_mod(x_ref[...], g_ref[...], sh_ref[0], sc_ref[0])
    h_buf[wslot] = h
    hb_ref[...] = h.astype(BF16)
    iota_n = lax.broadcasted_iota(jnp.int32, (PEER_KEYS, tb), 0).astype(F32)
    half = PEER_TOPK // 2
    sub = lambda n: lax.broadcasted_iota(jnp.int32, (n, tb), 0).astype(F32)
    pos_c = jnp.concatenate([sub(PEER_TOPK)] + [sub(half) + float(a * PEER_TOPK) for a in range(1, half)]
                            + [(sub(half) + float(half)) * float(PEER_TOPK)], axis=0)

    def pair_up(first, second, scale):
        return jnp.concatenate([first[0:1] * scale + second]
                               + [first[a:a + 1] * scale + second[0:half] for a in range(1, half)]
                               + [first[half:] * scale + second[0:1]], axis=0)

    def retrieve_stages(hd):
        q = jnp.dot(hb_ref[...], wq_ref[hd], preferred_element_type=F32)
        tops = []

        def first_stage(p):
            qp = q[:, p * PEER_HALF:(p + 1) * PEER_HALF].astype(BF16)
            s = lax.dot_general(keys_ref[hd, p], qp, (((1,), (1,)), ((), ())),
                                preferred_element_type=F32)
            tops.append(_topk_rows(s, iota_n, PEER_TOPK))

        return [functools.partial(first_stage, 0), functools.partial(first_stage, 1),
                functools.partial(second_stage, hd, tops)]

    def second_stage(hd, tops):
        (s1, i1), (s2, i2) = tops
        cand = pair_up(s1, s2, 1.0)
        cidx = pair_up(i1, i2, float(PEER_KEYS))
        row = lax.broadcasted_iota(jnp.int32, (PEER_TOPK, tb), 0)
        sc = jnp.zeros((PEER_TOPK, tb), F32)
        ex = jnp.zeros((PEER_TOPK, tb), F32)
        for r in range(PEER_TOPK):
            m = jnp.max(cand, axis=0, keepdims=True)
            am = jnp.min(jnp.where(cand == m, pos_c, float(PEER_TOPK * PEER_TOPK)), axis=0, keepdims=True)
            hit = pos_c == am
            e = jnp.max(jnp.where(hit, cidx, 0.0), axis=0, keepdims=True)
            sc = jnp.where(row == r, m, sc)
            ex = jnp.where(row == r, e, ex)
            cand = jnp.where(hit, -jnp.inf, cand)
        pexp = jnp.exp(sc - sc[0:1])
        rows = pl.ds(pl.multiple_of(hd * PEER_TOPK, PEER_TOPK), PEER_TOPK)
        w_scr[rows, :] = pexp / jnp.sum(pexp, axis=0, keepdims=True)
        e_scr[rows, :] = ex

    row8 = lax.broadcasted_iota(jnp.int32, (PEER_GROUP, V7X_LANES), 0)
    h_prev, gw_prev, idx_prev = h_buf.at[rslot], gw_buf.at[rslot], idx_smem.at[rslot]

    def finish(g, sums):
        hi = sums.astype(BF16)
        lo = (sums - hi.astype(F32)).astype(BF16)
        a = (jnp.dot(hi, fold_ref[...], preferred_element_type=F32)
             + jnp.dot(lo, fold_ref[...], preferred_element_type=F32))
        rows = pl.ds(pl.multiple_of(g * PEER_GROUP, PEER_GROUP), PEER_GROUP)
        gelu = 0.5 * a * (1.0 + lax.erf(a * (2.0 ** -0.5)))
        w_out_ref[rows, :] = gelu * gw_prev[rows, :]

    def gate_group(g, prev, before_token):
        finish(jnp.maximum(g - 1, 0), prev)
        sums = jnp.zeros((PEER_GROUP, PAIR_LANES), F32)
        h8 = h_prev[pl.ds(pl.multiple_of(g * PEER_GROUP, PEER_GROUP), PEER_GROUP), :]
        for i in range(PEER_GROUP):
            if i in before_token:
                before_token[i]()
            hb = _token_rows_bf16(h8, i)
            tok_idx = idx_prev.at[g * PEER_GROUP + i]
            for j in range(PEER_PAIRS):
                u = pltpu.bitcast(_slab(tab_ref, tok_idx[j]), BF16)
                s_ref[pl.ds((i * PEER_PAIRS + j) * EXPERT_ROWS, EXPERT_ROWS), :] = pltpu.bitcast(u * hb, jnp.int32)
            prod = pltpu.bitcast(s_ref[pl.ds(i * PEER_PAIRS * EXPERT_ROWS, PEER_PAIRS * EXPERT_ROWS), :], BF16)
            pick = jnp.where(row8 == i, 1.0, 0.0).astype(BF16)
            sums = sums + lax.dot_general(pick, prod, (((1,), (1,)), ((), ())), preferred_element_type=F32)
        return sums

    groups_per_head = tb // PEER_GROUP // PEER_HEADS

    def body(hd, sums):
        first_a, first_b, second = retrieve_stages(hd)
        placement = [{0: first_a, PEER_GROUP // 2: first_b}, {0: second}]
        for k in range(groups_per_head):
            sums = gate_group(hd * groups_per_head + k, sums, placement[k] if k < len(placement) else {})
        return sums

    last = lax.fori_loop(0, PEER_HEADS, body, jnp.zeros((PEER_GROUP, PAIR_LANES), F32))
    finish(tb // PEER_GROUP - 1, last)
    gw_buf[wslot] = w_scr[...].T
    idx = (e_scr[...].T * float(EXPERT_ROWS)).astype(jnp.int32)
    idx_out_ref[...] = idx
    idx_vmem[...] = idx
    handoff = pltpu.make_async_copy(idx_vmem, idx_smem.at[wslot], sem)
    handoff.start()
    handoff.wait()


def peer_front(x, g, shift, scale, wq_heads, keys, tab, mod_index, tb=PEER_TILE):
    r, d = x.shape
    nb = r // tb
    fold = jnp.repeat(jnp.eye(PEER_PAIRS, dtype=BF16), SLAB_ROWS_BF16, axis=0)
    cur = lambda i: jnp.minimum(i, nb - 1)
    const = lambda shape: pl.BlockSpec(shape, lambda i: (0,) * len(shape), pipeline_mode=pl.Buffered(1))
    return pl.pallas_call(
        _peer_front_kernel,
        grid=(nb + 1,),
        in_specs=[
            pl.BlockSpec((tb, d), lambda i: (cur(i), 0)),
            pl.BlockSpec((1, d), lambda i: (0, 0)),
            pl.BlockSpec((1, 1, d), lambda i: (mod_index(cur(i)), 0, 0)),
            pl.BlockSpec((1, 1, d), lambda i: (mod_index(cur(i)), 0, 0)),
            const((PEER_HEADS, d, PEER_QDIM)),
            const((PEER_HEADS, 2, PEER_KEYS, PEER_HALF)),
            const(tab.shape),
            const((PAIR_LANES, PEER_PAIRS)),
        ],
        out_specs=[
            pl.BlockSpec((tb, PEER_PAIRS), lambda i: (cur(i), 0)),
            pl.BlockSpec((tb, PEER_PAIRS), lambda i: (jnp.maximum(i - 1, 0), 0)),
        ],
        out_shape=[jax.ShapeDtypeStruct((r, PEER_PAIRS), jnp.int32), jax.ShapeDtypeStruct((r, PEER_PAIRS), F32)],
        scratch_shapes=[
            pltpu.VMEM((tb, d), BF16), pltpu.VMEM((PEER_PAIRS, tb), F32), pltpu.VMEM((PEER_PAIRS, tb), F32),
            pltpu.VMEM((2, tb, d), F32), pltpu.VMEM((2, tb, PEER_PAIRS), F32),
            pltpu.VMEM((tb, PEER_PAIRS), jnp.int32), pltpu.SMEM((2, tb, PEER_PAIRS), jnp.int32),
            pltpu.VMEM((PEER_GROUP * PEER_PAIRS * EXPERT_ROWS, V7X_LANES), jnp.int32),
            pltpu.SemaphoreType.DMA,
        ],
        compiler_params=_cparams("arbitrary"),
        name="peer_front",
    )(x, g, shift, scale, wq_heads, keys, tab, fold)


def peer(x, g, shift, scale, gate, wq, keys, u_tab, v_tab, mod_index_peer):
    r, d = x.shape
    wq_heads = wq.astype(BF16).reshape(d, PEER_HEADS, PEER_QDIM).transpose(1, 0, 2)
    idx4, w = peer_front(x, g, shift, scale, wq_heads, keys.astype(BF16), pack_rows_bf16(u_tab), mod_index_peer)
    return peer_expert_mix(idx4.reshape(r * PEER_PAIRS), w, x, gate, pack_rows_bf16(v_tab), mod_index_peer)


HG_TILE = 256
HG_SUB = 16
HG_PAIR = 2 * HG_DK


def _hgrn_kernel(*refs, rev):
    blocks = HG_W // HG_PAIR
    q_refs, f_refs, v_refs = refs[:blocks], refs[blocks:2 * blocks], refs[2 * blocks:3 * blocks]
    lb_ref, o_ref, st_ref = refs[3 * blocks:]
    wide = lambda rs: jnp.concatenate([r[...] for r in rs], axis=1)

    @pl.when(pl.program_id(1) == 0)
    def _():
        st_ref[...] = jnp.zeros_like(st_ref)

    n = o_ref.shape[0]
    lb = lb_ref[...]
    f = lb + (1.0 - lb) * jax.nn.sigmoid(wide(f_refs))
    kk = 1.0 - f
    lf = jnp.log(f)
    t = lax.broadcasted_iota(jnp.int32, (n, n), 0)
    s = lax.broadcasted_iota(jnp.int32, (n, n), 1)
    same = (t // HG_SUB) == (s // HG_SUB)
    mid = (t // HG_SUB) * HG_SUB + (HG_SUB // 2 if rev else HG_SUB // 2 - 1)
    seen = (s >= t) if rev else (s <= t)
    seen_mid = (s >= mid) if rev else (s <= mid)
    one = lambda m: jnp.where(m, 1.0, 0.0).astype(F32)
    hp = lambda a, b: jnp.dot(a, b, precision=lax.Precision.HIGHEST, preferred_element_type=F32)
    cum = hp(one(same & seen), lf)
    ref = hp(one(same & seen_mid), lf)
    last = hp(one(same), lf)
    q = wide(q_refs)
    qe = (q * jnp.exp(cum)).astype(BF16)
    qm = (q * jnp.exp(cum - ref)).astype(BF16)
    km = (kk * jnp.exp(ref - cum)).astype(BF16)
    kd = (kk * jnp.exp(last - cum)).astype(BF16)
    dec = jnp.exp(last)
    vb = wide(v_refs).astype(BF16)
    steps = range(n // HG_SUB)
    heads = range(HG_HEADS)
    intra = []
    for h in heads:
        cols = slice(h * HG_DK, (h + 1) * HG_DK)
        att = lax.dot_general(qm[:, cols], km[:, cols], (((1,), (1,)), ((), ())), preferred_element_type=F32)
        att = jnp.where(same & seen, att, 0.0).astype(BF16)
        intra.append(jnp.dot(att, vb[:, cols], preferred_element_type=F32))
    for c in (reversed(steps) if rev else steps):
        rows = slice(c * HG_SUB, (c + 1) * HG_SUB)
        for h in heads:
            cols = slice(h * HG_DK, (h + 1) * HG_DK)
            st = st_ref[h]
            o_ref[rows, cols] = intra[h][rows] + lax.dot_general(
                qe[rows, cols], st.astype(BF16), (((1,), (1,)), ((), ())), preferred_element_type=F32)
            upd = lax.dot_general(vb[rows, cols], kd[rows, cols], (((0,), (0,)), ((), ())),
                                  preferred_element_type=F32)
            st_ref[h] = st * dec[c * HG_SUB:c * HG_SUB + 1, cols] + upd


def hgrn_scan(pa, lb, rev, batch):
    rows = pa.shape[0]
    per = rows // batch // HG_TILE
    nctx = CTX_LEN // HG_TILE
    col0 = GQA_IN // HG_PAIR
    blocks = HG_W // HG_PAIR

    def tok(b, k):
        if rev:
            k = jnp.where(k < nctx, nctx - 1 - k, per - 1 - (k - nctx))
        return b * per + k

    specs = lambda cb: [pl.BlockSpec((HG_TILE, HG_PAIR), lambda b, k, c=cb + p: (tok(b, k), c)) for p in range(blocks)]
    d = 1 if rev else 0
    return pl.pallas_call(
        functools.partial(_hgrn_kernel, rev=rev),
        grid=(batch, per),
        in_specs=specs(col0) + specs(col0 + (1 + d) * blocks) + specs(col0 + 3 * blocks)
        + [pl.BlockSpec((None, 1, HG_W), lambda b, k: (d, 0, 0))],
        out_specs=pl.BlockSpec((HG_TILE, HG_W), lambda b, k: (tok(b, k), 0)),
        out_shape=jax.ShapeDtypeStruct((rows, HG_W), F32),
        scratch_shapes=[pltpu.VMEM((HG_HEADS, HG_DV, HG_DK), F32)],
        compiler_params=_cparams("parallel", "arbitrary"),
        name="hgrn_scan_rev" if rev else "hgrn_scan_fwd",
    )(*([pa] * (3 * blocks)), lb)


def _rms_norm(x, g):
    xf = x.astype(F32)
    y = xf * lax.rsqrt(jnp.mean(xf * xf, axis=-1, keepdims=True) + EPS)
    return y * g.astype(F32)


def _hgrn2(pa2, batch, lb, onorm_g):
    lb = lb.reshape(2, 1, HG_W)
    o = hgrn_scan(pa2, lb, False, batch) + hgrn_scan(pa2, lb, True, batch)
    o = _rms_norm(o.reshape(-1, HG_HEADS, HG_DV), onorm_g).reshape(-1, HG_W)
    return o * jax.nn.silu(pa2[:, GQA_IN + 4 * HG_W:])


S5_CHUNK = 128
S5_LANES = S5_GROUPS * S5_STATE


def _s5_kernel(u_ref, bre_ref, bim_ref, cre_ref, cim_ref, are_ref, aim_ref, y_ref, xre, xim, hre, him):
    nb = V7X_SUBLANES // 2
    tiles = u_ref.shape[0] // V7X_SUBLANES
    direction = pl.program_id(0)
    rev = direction == 1

    @pl.when(pl.program_id(1) == 0)
    def _():
        hre[...] = jnp.zeros_like(hre)
        him[...] = jnp.zeros_like(him)

    ub = u_ref[...].astype(BF16)
    tile = V7X_MXU_DIM
    fan = (S5_LANES // tile) // (u_ref.shape[1] // tile)
    for nt in range(S5_LANES // tile):
        cols = slice(nt * tile, (nt + 1) * tile)
        src = ub[:, (nt // fan) * tile:(nt // fan + 1) * tile]
        xre[:, cols] = jnp.dot(src, bre_ref[nt], preferred_element_type=F32)
        xim[:, cols] = jnp.dot(src, bim_ref[nt], preferred_element_type=F32)
    shape = (V7X_SUBLANES, S5_LANES)
    second = (lax.broadcasted_iota(jnp.int32, shape, 0) // nb) != direction
    are = jnp.broadcast_to(are_ref[...], shape)
    aim = jnp.broadcast_to(aim_ref[...], shape)
    cre = jnp.where(second, are * are - aim * aim, are)
    cim = jnp.where(second, 2.0 * are * aim, aim)

    def step(k, carry):
        pr, pi = carry
        k = jnp.where(rev, tiles - 1 - k, k)
        r = pl.ds(pl.multiple_of(k * V7X_SUBLANES, V7X_SUBLANES), V7X_SUBLANES)
        xr, xi = xre[r, :], xim[r, :]
        sr = jnp.where(second, pltpu.roll(xr, nb, axis=0), 0.0)
        si = jnp.where(second, pltpu.roll(xi, nb, axis=0), 0.0)
        nr = xr + (are * sr - aim * si) + (cre * pr - cim * pi)
        ni = xi + (are * si + aim * sr) + (cre * pi + cim * pr)
        xre[r, :] = nr
        xim[r, :] = ni
        return (jnp.where(second, nr, pltpu.roll(nr, nb, axis=0)),
                jnp.where(second, ni, pltpu.roll(ni, nb, axis=0)))

    hr, hi = lax.fori_loop(0, tiles, step, (hre[...], him[...]))
    hre[...] = hr
    him[...] = hi
    for mt in range(u_ref.shape[1] // tile):
        acc = jnp.zeros((u_ref.shape[0], tile), F32)
        for kt in range(mt * fan, (mt + 1) * fan):
            cols = slice(kt * tile, (kt + 1) * tile)
            acc = acc + (jnp.dot(xre[:, cols].astype(BF16), cre_ref[kt], preferred_element_type=F32)
                         - jnp.dot(xim[:, cols].astype(BF16), cim_ref[kt], preferred_element_type=F32))
        y_ref[:, mt * tile:(mt + 1) * tile] = acc


def s5_scan(u, bre, bim, cre, cim, are, aim, nb):
    rows, w = u.shape
    blk = S5_CHUNK * nb
    assert rows % blk == 0 and 2 * nb == V7X_SUBLANES
    nblk, nctx = rows // blk, CTX_LEN // S5_CHUNK
    wspec = lambda shape: pl.BlockSpec((None,) + shape, lambda d, i: (d, 0, 0))
    tiles = pl.BlockSpec((None,) + bre.shape[1:], lambda d, i: (d, 0, 0, 0))

    def chunk(d, i):
        back = jnp.where(i < nctx, nctx - 1 - i, nblk - 1 - (i - nctx))
        return jnp.where(d == 1, back, i)
    return pl.pallas_call(
        _s5_kernel,
        grid=(2, rows // blk),
        in_specs=[
            pl.BlockSpec((blk, w), lambda d, i: (chunk(d, i), 0)),
            tiles, tiles, tiles, tiles,
            wspec((1, S5_LANES)), wspec((1, S5_LANES)),
        ],
        out_specs=pl.BlockSpec((None, blk, w), lambda d, i: (d, chunk(d, i), 0)),
        out_shape=jax.ShapeDtypeStruct((2, rows, w), F32),
        scratch_shapes=[pltpu.VMEM((blk, S5_LANES), F32), pltpu.VMEM((blk, S5_LANES), F32),
                        pltpu.VMEM((V7X_SUBLANES, S5_LANES), F32), pltpu.VMEM((V7X_SUBLANES, S5_LANES), F32)],
        compiler_params=_cparams("arbitrary", "arbitrary"),
        name="s5_scan",
    )(u, bre, bim, cre, cim, are, aim)


def _s5_core(u, a_re, a_im, log_dt, b_re, b_im, c_re, c_im, d_skip):
    b, s, w = u.shape
    dt = jnp.exp(log_dt)[..., None]
    mag = jnp.exp(a_re * dt)
    abar_re, abar_im = mag * jnp.cos(a_im * dt), mag * jnp.sin(a_im * dt)
    den = a_re * a_re + a_im * a_im
    k_re = ((abar_re - 1.0) * a_re + abar_im * a_im) / den
    k_im = (abar_im * a_re - (abar_re - 1.0) * a_im) / den
    bb_re = k_re[..., None] * b_re - k_im[..., None] * b_im
    bb_im = k_re[..., None] * b_im + k_im[..., None] * b_re
    eye = jnp.eye(S5_GROUPS, dtype=F32)
    tile, n_tiles = V7X_MXU_DIM, S5_LANES // V7X_MXU_DIM
    fan = n_tiles // (w // tile)
    cut = lambda m, r, c: m[:, r * tile:(r + 1) * tile, c * tile:(c + 1) * tile]

    def bd_in(m):
        full = jnp.einsum('dgpc,gh->dgchp', m, eye).reshape(2, w, S5_LANES).astype(BF16)
        return jnp.stack([cut(full, nt // fan, nt) for nt in range(n_tiles)], axis=1)

    def bd_out(m):
        full = jnp.einsum('dgcp,gh->dgphc', m, eye).reshape(2, S5_LANES, w).astype(BF16)
        return jnp.stack([cut(full, kt, kt // fan) for kt in range(n_tiles)], axis=1)
    y2 = s5_scan(u.transpose(1, 0, 2).reshape(s * b, w), bd_in(bb_re), bd_in(bb_im), bd_out(c_re), bd_out(c_im),
                 abar_re.reshape(2, 1, S5_LANES), abar_im.reshape(2, 1, S5_LANES), b)
    return d_skip * u + (y2[0] + y2[1]).reshape(s, b, w).transpose(1, 0, 2)


def _rope_tables(rows, rot_dim, lead, period):
    axis_dim = rot_dim // 2
    inv = ROPE_THETA ** (-jnp.arange(0, axis_dim, 2, dtype=F32) / axis_dim)
    t = jnp.arange(rows * GRID_W)
    r = (t // GRID_W).astype(F32)[:, None] * inv
    c = (t % GRID_W).astype(F32)[:, None] * inv
    n = t.shape[0]
    tail = period - lead - rot_dim
    cos = jnp.concatenate([jnp.ones((n, lead), F32), jnp.cos(r), jnp.cos(r), jnp.cos(c), jnp.cos(c),
                           jnp.ones((n, tail), F32)], axis=-1)
    sin = jnp.concatenate([jnp.zeros((n, lead), F32), -jnp.sin(r), jnp.sin(r), -jnp.sin(c), jnp.sin(c),
                           jnp.zeros((n, tail), F32)], axis=-1)
    reps = V7X_LANES // period
    cos = jnp.concatenate([jnp.ones((CTX_LEN, V7X_LANES), F32), jnp.tile(cos, (1, reps))], axis=0)
    sin = jnp.concatenate([jnp.zeros((CTX_LEN, V7X_LANES), F32), jnp.tile(sin, (1, reps))], axis=0)
    return cos, sin


def _attend_all(q, k, v, need_ctx):
    o_lat = attention(q[:, CTX_LEN:], k, v)
    if need_ctx:
        o_ctx = attention(q[:, :CTX_LEN], k[:, :CTX_LEN], v[:, :CTX_LEN])
    else:
        o_ctx = jnp.zeros((q.shape[0], CTX_LEN, q.shape[2]), F32)
    return jnp.concatenate([o_ctx, o_lat], axis=1)


def _gqa(pa2, b, qn_g, kn_g, cos, sin, need_ctx):
    s = pa2.shape[0] // b
    grp = GQA_Q_HEADS // GQA_KV_HEADS
    gains = jnp.concatenate([jnp.tile(qn_g, GQA_Q_HEADS), jnp.tile(kn_g, GQA_KV_HEADS)])[None, :]
    q, k, v = gqa_prep(pa2, gains, cos, sin, s)
    o = _attend_all(q.reshape(b, s, -1), k.reshape(b, s, -1), v.reshape(b, s, -1), need_ctx)
    o = o.reshape(b, s, GQA_Q_HEADS, V7X_LANES)
    first = (jnp.arange(GQA_Q_HEADS) < grp)[None, None, :, None]
    o = jnp.where(first, o[..., :HEAD_DIM], o[..., HEAD_DIM:])
    return o.reshape(b, s, GQA_Q_HEADS * HEAD_DIM)


def _mla(p, qa_g, w_qup, kva_g, w_kvup, cos, sin, need_ctx):
    b, s, _ = p.shape
    r0 = MLA_Q_RANK
    r1 = MLA_Q_RANK + MLA_KV_RANK
    blocks = lambda w, lo, hi: jnp.pad(w.reshape(w.shape[0], MLA_HEADS, -1)[..., lo:hi],
                                       ((0, 0), (0, 0), (0, V7X_LANES - (hi - lo)))).reshape(w.shape[0], -1)
    w_q = blocks(w_qup, 0, MLA_NOPE + MLA_ROPE).astype(BF16)
    w_kv = jnp.concatenate([blocks(w_kvup, 0, MLA_NOPE), blocks(w_kvup, MLA_NOPE, MLA_NOPE + MLA_V)],
                           axis=1).astype(BF16)
    zero = jnp.zeros((1, 1, r0), F32)
    q = norm_mod_matmul(p[..., :r0].reshape(b * s, r0), qa_g[None, :], zero, zero, w_q, lambda i: 0)
    zero = jnp.zeros((1, 1, MLA_KV_RANK), F32)
    kv = norm_mod_matmul(p[..., r0:r1].reshape(b * s, MLA_KV_RANK), kva_g[None, :], zero, zero, w_kv, lambda i: 0)
    kr = jnp.pad(p[..., r1:r1 + MLA_ROPE].reshape(b * s, MLA_ROPE),
                 ((0, 0), (MLA_NOPE, V7X_LANES - MLA_NOPE - MLA_ROPE)))
    q, k, v = mla_prep(q, kv, kr, cos, sin, s)
    o = _attend_all(q.reshape(b, s, -1), k.reshape(b, s, -1), v.reshape(b, s, -1), need_ctx)
    return o.reshape(b, s, MLA_HEADS, V7X_LANES)[..., :MLA_V].reshape(b, s, MLA_HEADS * MLA_V)


def kernel(x, c, ctx, c_ctx, ada_w, ada_b, norm1_g, norm2_g, ev_w_in, ev_w_out, gqa_qn_g, gqa_kn_g,
           hg_lb_logits, hg_onorm_g, od_w_in, od_w_out, mla_qa_g, mla_w_qup, mla_kva_g, mla_w_kvup,
           s5_a_re, s5_a_im, s5_log_dt, s5_b_re, s5_b_im, s5_c_re, s5_c_im, s5_d, s5_w_glu, s5_b_glu,
           peer_wq, peer_keys, peer_u, peer_v, final_g):
    b, t, d = x.shape
    s = CTX_LEN + t
    depth = ada_w.shape[0]
    rows = t // GRID_W
    cos_g, sin_g = _rope_tables(rows, HEAD_DIM, 0, HEAD_DIM)
    cos_m, sin_m = _rope_tables(rows, MLA_ROPE, MLA_NOPE, V7X_LANES)
    lb_all = jnp.cumsum(jax.nn.softmax(hg_lb_logits, axis=1), axis=1)

    def mod_index(tile):
        per, nctx = s // tile, CTX_LEN // tile
        return lambda i: jnp.where(i % per < nctx, b, i // per)

    mi_row = mod_index(ROW_TILE)
    mi_peer = mod_index(PEER_TILE)

    xa = jnp.concatenate([ctx, x], axis=1).reshape(b * s, d)
    s_all = jnp.concatenate([jax.nn.silu(c), jax.nn.silu(c_ctx)[None, :]], axis=0)
    s_pad = jnp.concatenate([s_all, jnp.zeros((V7X_SUBLANES - (b + 1) % V7X_SUBLANES, d), F32)], axis=0)

    for layer in range(depth):
        need_ctx = layer < depth - 1
        j = layer // 2
        mod = matmul(s_pad, ada_w[layer].astype(BF16), tm=s_pad.shape[0])[:b + 1] + ada_b[layer]
        mod = [m[:, None, :] for m in jnp.split(mod, 6, axis=-1)]
        if layer % 2 == 0:
            pa2 = norm_mod_matmul(xa, norm1_g[layer][None, :], mod[0], mod[1], ev_w_in[j].astype(BF16), mi_row)
            ya = _gqa(pa2, b, gqa_qn_g[j], gqa_kn_g[j], cos_g, sin_g, need_ctx)
            yb = _hgrn2(pa2, b, lb_all[:, j], hg_onorm_g[j]).reshape(b, s, HG_W)
            w_out = ev_w_out[j]
        else:
            pa = norm_mod_matmul(xa, norm1_g[layer][None, :], mod[0], mod[1], od_w_in[j].astype(BF16), mi_row)
            pa = pa.reshape(b, s, -1)
            ya = _mla(pa[..., :MLA_IN], mla_qa_g[j], mla_w_qup[j], mla_kva_g[j], mla_w_kvup[j], cos_m, sin_m, need_ctx)
            y5 = _s5_core(pa[..., MLA_IN:], s5_a_re[j], s5_a_im[j], s5_log_dt[j], s5_b_re[j], s5_b_im[j],
                          s5_c_re[j], s5_c_im[j], s5_d[j])
            z = jax.nn.gelu(y5, approximate=False).reshape(b * s, S5_WIDTH)
            gl = matmul(z, s5_w_glu[j].astype(BF16)) + s5_b_glu[j]
            yb = (z * jax.nn.sigmoid(gl)).reshape(b, s, S5_WIDTH)
            w_out = od_w_out[j]
        xa = matmul_residual(ya.reshape(b * s, -1), yb.reshape(b * s, -1), w_out.astype(BF16), xa, mod[2], mi_row)
        xa = peer(xa, norm2_g[layer][None, :], mod[3], mod[4], mod[5],
                  peer_wq[layer], peer_keys[layer], peer_u[layer], peer_v[layer], mi_peer)
    xl = xa.reshape(b, s, d)[:, CTX_LEN:].reshape(b * t, d)
    return rmsnorm_rows(xl, final_g[None, :]).reshape(b, t, d)
```

```python
import functools
import math

import jax
import jax.numpy as jnp
from jax import lax
from jax.experimental import pallas as pl
from jax.experimental.pallas import tpu as pltpu

F32 = jnp.float32
BF16 = jnp.bfloat16

D_MODEL = 1024
GRID_W = 64
CTX_LEN = 256
EPS = 1e-6
ROPE_THETA = 10000.0

MIX_HALF = D_MODEL // 2
HEAD_DIM = 64
GQA_Q_HEADS = MIX_HALF // HEAD_DIM
GQA_KV_HEADS = GQA_Q_HEADS // 4
GQA_IN = (GQA_Q_HEADS + 2 * GQA_KV_HEADS) * HEAD_DIM

HG_DK = 128
HG_DV = 128
HG_HEADS = MIX_HALF // HG_DV
HG_W = HG_HEADS * HG_DK

MLA_HEADS = MIX_HALF // HEAD_DIM
MLA_NOPE = 64
MLA_ROPE = 32
MLA_V = 64
MLA_Q_RANK = 384
MLA_KV_RANK = 256
MLA_IN = MLA_Q_RANK + MLA_KV_RANK + MLA_ROPE

S5_WIDTH = MIX_HALF
S5_GROUP = 16
S5_GROUPS = S5_WIDTH // S5_GROUP
S5_STATE = 64

PEER_HEADS = 8
PEER_KEYS = 128
N_EXPERTS = PEER_KEYS * PEER_KEYS
PEER_TOPK = 16
PEER_QDIM = 256
PEER_HALF = PEER_QDIM // 2
PEER_PAIRS = PEER_HEADS * PEER_TOPK

V7X_LANES = 128
V7X_SUBLANES = 8
V7X_VMEM_BYTES = 64 * 1024 * 1024
VMEM_LIMIT = V7X_VMEM_BYTES - 8 * 1024 * 1024

ROW_TILE = 256
PEER_TILE = 128
ATTN_Q_TILE = 1024
EXPERT_WORDS = D_MODEL // 2
EXPERT_ROWS = EXPERT_WORDS // V7X_LANES


def _cparams(*sem):
    return pltpu.CompilerParams(dimension_semantics=sem, vmem_limit_bytes=VMEM_LIMIT)


def _norm_mod(x, g, shift, scale):
    ms = jnp.mean(x * x, axis=-1, keepdims=True)
    h = (x * lax.rsqrt(ms + EPS)) * g
    return h * (1.0 + scale) + shift


def _nmm_kernel(x_ref, g_ref, sh_ref, sc_ref, w_ref, o_ref):
    h = _norm_mod(x_ref[...], g_ref[...], sh_ref[0], sc_ref[0])
    o_ref[...] = jnp.dot(h.astype(BF16), w_ref[...], preferred_element_type=F32)


def norm_mod_matmul(x, g, shift, scale, w, mod_index, tm=ROW_TILE):
    r, k = x.shape
    n = w.shape[1]
    assert r % tm == 0 and w.shape[0] == k
    return pl.pallas_call(
        _nmm_kernel,
        grid=(r // tm,),
        in_specs=[
            pl.BlockSpec((tm, k), lambda i: (i, 0)),
            pl.BlockSpec((1, k), lambda i: (0, 0)),
            pl.BlockSpec((1, 1, k), lambda i: (mod_index(i), 0, 0)),
            pl.BlockSpec((1, 1, k), lambda i: (mod_index(i), 0, 0)),
            pl.BlockSpec((k, n), lambda i: (0, 0)),
        ],
        out_specs=pl.BlockSpec((tm, n), lambda i: (i, 0)),
        out_shape=jax.ShapeDtypeStruct((r, n), F32),
        compiler_params=_cparams("parallel"),
        name="norm_mod_matmul",
    )(x, g, shift, scale, w)


def _mm_kernel(a_ref, w_ref, o_ref):
    o_ref[...] = jnp.dot(a_ref[...].astype(BF16), w_ref[...], preferred_element_type=F32)


def matmul(a, w, tm=ROW_TILE):
    r, k = a.shape
    n = w.shape[1]
    assert r % tm == 0
    return pl.pallas_call(
        _mm_kernel,
        grid=(r // tm,),
        in_specs=[pl.BlockSpec((tm, k), lambda i: (i, 0)), pl.BlockSpec((k, n), lambda i: (0, 0))],
        out_specs=pl.BlockSpec((tm, n), lambda i: (i, 0)),
        out_shape=jax.ShapeDtypeStruct((r, n), F32),
        compiler_params=_cparams("parallel"),
        name="matmul",
    )(a, w)


def _mmres_kernel(a_ref, b_ref, w_ref, x_ref, gate_ref, o_ref):
    ka = a_ref.shape[1]
    y = (jnp.dot(a_ref[...].astype(BF16), w_ref[:ka, :], preferred_element_type=F32)
         + jnp.dot(b_ref[...].astype(BF16), w_ref[ka:, :], preferred_element_type=F32))
    o_ref[...] = x_ref[...] + gate_ref[0] * y


def matmul_residual(a, b, w, x, gate, mod_index, tm=ROW_TILE):
    r, ka = a.shape
    kb = b.shape[1]
    k, n = w.shape
    assert r % tm == 0 and ka + kb == k
    return pl.pallas_call(
        _mmres_kernel,
        grid=(r // tm,),
        in_specs=[
            pl.BlockSpec((tm, ka), lambda i: (i, 0)),
            pl.BlockSpec((tm, kb), lambda i: (i, 0)),
            pl.BlockSpec((k, n), lambda i: (0, 0)),
            pl.BlockSpec((tm, n), lambda i: (i, 0)),
            pl.BlockSpec((1, 1, n), lambda i: (mod_index(i), 0, 0)),
        ],
        out_specs=pl.BlockSpec((tm, n), lambda i: (i, 0)),
        out_shape=jax.ShapeDtypeStruct((r, n), F32),
        compiler_params=_cparams("parallel"),
        name="matmul_residual",
    )(a, b, w, x, gate)


def _rms_kernel(x_ref, g_ref, o_ref):
    x = x_ref[...]
    ms = jnp.mean(x * x, axis=-1, keepdims=True)
    o_ref[...] = (x * lax.rsqrt(ms + EPS)) * g_ref[...]


def rmsnorm_rows(x, g, tm=ROW_TILE):
    r, k = x.shape
    return pl.pallas_call(
        _rms_kernel,
        grid=(r // tm,),
        in_specs=[pl.BlockSpec((tm, k), lambda i: (i, 0)), pl.BlockSpec((1, k), lambda i: (0, 0))],
        out_specs=pl.BlockSpec((tm, k), lambda i: (i, 0)),
        out_shape=jax.ShapeDtypeStruct((r, k), F32),
        compiler_params=_cparams("parallel"),
        name="rmsnorm_rows",
    )(x, g)


ATTN_KV_CHUNKS = 3
V7X_MXU_DIM = 256
LOG2E = math.log2(math.e)


def _attn_kernel(q_ref, k_ref, v_ref, o_ref):
    q = q_ref[...]
    sk = k_ref.shape[0]
    nch = ATTN_KV_CHUNKS if sk % (ATTN_KV_CHUNKS * V7X_MXU_DIM) == 0 else 1
    ck = sk // nch
    m = l = acc = None
    for c in range(nch):
        kc = k_ref[c * ck:(c + 1) * ck, :]
        vc = v_ref[c * ck:(c + 1) * ck, :]
        s = lax.dot_general(q, kc, (((1,), (1,)), ((), ())), preferred_element_type=F32)
        mc = jnp.max(s, axis=-1, keepdims=True)
        if c == 0:
            m = mc
            p = jnp.exp2(s - m)
            l = jnp.sum(p, axis=-1, keepdims=True)
            acc = jnp.dot(p.astype(BF16), vc, preferred_element_type=F32)
        else:
            m_new = jnp.maximum(m, mc)
            a = jnp.exp2(m - m_new)
            p = jnp.exp2(s - m_new)
            l = a * l + jnp.sum(p, axis=-1, keepdims=True)
            acc = a * acc + jnp.dot(p.astype(BF16), vc, preferred_element_type=F32)
            m = m_new
    o_ref[...] = acc / l


def attention(q, k, v, tq=ATTN_Q_TILE):
    b, sq, hl = q.shape
    sk, hkl = k.shape[1], k.shape[2]
    h, hk = hl // V7X_LANES, hkl // V7X_LANES
    grp = h // hk
    tq = min(tq, sq)
    assert sq % tq == 0
    return pl.pallas_call(
        _attn_kernel,
        grid=(b, h, sq // tq),
        in_specs=[
            pl.BlockSpec((None, tq, V7X_LANES), lambda bi, hi, qi: (bi, qi, hi)),
            pl.BlockSpec((None, sk, V7X_LANES), lambda bi, hi, qi: (bi, 0, hi // grp)),
            pl.BlockSpec((None, sk, V7X_LANES), lambda bi, hi, qi: (bi, 0, hi // grp)),
        ],
        out_specs=pl.BlockSpec((None, tq, V7X_LANES), lambda bi, hi, qi: (bi, qi, hi)),
        out_shape=jax.ShapeDtypeStruct((b, sq, hl), F32),
        compiler_params=_cparams("parallel", "parallel", "parallel"),
        name="attention",
    )(q, k, v)


def _rotate(x, cos, sin, quarter):
    n = x.shape[1]
    lane = lax.broadcasted_iota(jnp.int32, x.shape, 1)
    partner = jnp.where(lane % (2 * quarter) < quarter, pltpu.roll(x, n - quarter, axis=1),
                        pltpu.roll(x, quarter, axis=1))
    return x * cos + partner * sin


def _gqa_prep_kernel(p_ref, g_ref, cos_ref, sin_ref, avg_ref, q_ref, k_ref, v_ref):
    cos, sin = cos_ref[...], sin_ref[...]
    low = lax.broadcasted_iota(jnp.int32, cos.shape, 1) < HEAD_DIM
    qk_blocks = (GQA_Q_HEADS + GQA_KV_HEADS) * HEAD_DIM // V7X_LANES
    q_blocks = GQA_Q_HEADS * HEAD_DIM // V7X_LANES
    for c in range(qk_blocks):
        lanes = slice(c * V7X_LANES, (c + 1) * V7X_LANES)
        x = p_ref[:, lanes]
        ms = jnp.dot(x * x, avg_ref[...], precision=lax.Precision.HIGHEST, preferred_element_type=F32)
        y = _rotate(x * lax.rsqrt(ms + EPS) * g_ref[:, lanes], cos, sin, HEAD_DIM // 4)
        if c < q_blocks:
            y = y * (HEAD_DIM ** -0.5 * LOG2E)
            other = pltpu.roll(y, HEAD_DIM, axis=1)
            if c < q_blocks // 2:
                first, second = jnp.where(low, y, 0.0), jnp.where(low, other, 0.0)
            else:
                first, second = jnp.where(low, 0.0, other), jnp.where(low, 0.0, y)
            q_ref[:, 2 * c * V7X_LANES:(2 * c + 1) * V7X_LANES] = first.astype(BF16)
            q_ref[:, (2 * c + 1) * V7X_LANES:(2 * c + 2) * V7X_LANES] = second.astype(BF16)
        else:
            k_ref[...] = y.astype(BF16)
    v_ref[...] = p_ref[:, qk_blocks * V7X_LANES:(qk_blocks + 1) * V7X_LANES].astype(BF16)


def gqa_prep(pa, gains, cos, sin, seq, tm=ROW_TILE):
    r = pa.shape[0]
    per = seq // tm
    avg = jnp.kron(jnp.eye(V7X_LANES // HEAD_DIM, dtype=F32), jnp.full((HEAD_DIM, HEAD_DIM), 1.0 / HEAD_DIM, F32))
    tab = pl.BlockSpec((tm, V7X_LANES), lambda i: (i % per, 0))
    kv = pl.BlockSpec((tm, V7X_LANES), lambda i: (i, 0))
    return pl.pallas_call(
        _gqa_prep_kernel,
        grid=(r // tm,),
        in_specs=[pl.BlockSpec((tm, GQA_IN), lambda i: (i, 0)), pl.BlockSpec(gains.shape, lambda i: (0, 0)), tab, tab,
                  pl.BlockSpec(avg.shape, lambda i: (0, 0))],
        out_specs=[pl.BlockSpec((tm, GQA_Q_HEADS * V7X_LANES), lambda i: (i, 0)), kv, kv],
        out_shape=[jax.ShapeDtypeStruct((r, GQA_Q_HEADS * V7X_LANES), BF16),
                   jax.ShapeDtypeStruct((r, V7X_LANES), BF16), jax.ShapeDtypeStruct((r, V7X_LANES), BF16)],
        compiler_params=_cparams("parallel"),
        name="gqa_prep",
    )(pa, gains, cos, sin, avg)


def _mla_prep_kernel(q_ref, kv_ref, kr_ref, cos_ref, sin_ref, qo_ref, ko_ref, vo_ref):
    cos, sin = cos_ref[...], sin_ref[...]
    quarter = MLA_ROPE // 4
    kr = _rotate(kr_ref[...], cos, sin, quarter)
    scale = (MLA_NOPE + MLA_ROPE) ** -0.5 * LOG2E
    width = MLA_HEADS * V7X_LANES
    for h in range(MLA_HEADS):
        lanes = slice(h * V7X_LANES, (h + 1) * V7X_LANES)
        qo_ref[:, lanes] = (_rotate(q_ref[:, lanes], cos, sin, quarter) * scale).astype(BF16)
        ko_ref[:, lanes] = (kv_ref[:, lanes] + kr).astype(BF16)
        vo_ref[:, lanes] = kv_ref[:, width + h * V7X_LANES:width + (h + 1) * V7X_LANES].astype(BF16)


def mla_prep(q, kv, kr, cos, sin, seq, tm=ROW_TILE):
    r, width = q.shape
    per = seq // tm
    tab = pl.BlockSpec((tm, V7X_LANES), lambda i: (i % per, 0))
    out = pl.BlockSpec((tm, width), lambda i: (i, 0))
    return pl.pallas_call(
        _mla_prep_kernel,
        grid=(r // tm,),
        in_specs=[out, pl.BlockSpec((tm, 2 * width), lambda i: (i, 0)), pl.BlockSpec((tm, V7X_LANES), lambda i: (i, 0)),
                  tab, tab],
        out_specs=[out, out, out],
        out_shape=[jax.ShapeDtypeStruct((r, width), BF16)] * 3,
        compiler_params=_cparams("parallel"),
        name="mla_prep",
    )(q, kv, kr, cos, sin)


def _topk_rows(s, iota, k):
    n = s.shape[0]
    row = lax.broadcasted_iota(jnp.int32, (k, s.shape[1]), 0)
    vals = jnp.zeros((k, s.shape[1]), F32)
    ids = jnp.zeros((k, s.shape[1]), F32)
    for r in range(k):
        m = jnp.max(s, axis=0, keepdims=True)
        am = jnp.min(jnp.where(s == m, iota, float(n)), axis=0, keepdims=True)
        vals = jnp.where(row == r, m, vals)
        ids = jnp.where(row == r, am, ids)
        s = jnp.where(iota == am, -jnp.inf, s)
    return vals, ids


def pack_rows_bf16(tab):
    e, d = tab.shape
    t16 = lax.bitcast_convert_type(tab.astype(BF16), jnp.uint16).astype(jnp.uint32)
    word = t16[:, :d // 2] | (t16[:, d // 2:] << 16)
    return lax.bitcast_convert_type(word, jnp.int32).reshape(e * EXPERT_ROWS, V7X_LANES)


def _slab_value_row(r):
    return r // 2 + EXPERT_ROWS * (r % 2)


PEER_GROUP = V7X_SUBLANES
SLAB_ROWS_BF16 = 2 * EXPERT_ROWS
PAIR_LANES = PEER_PAIRS * SLAB_ROWS_BF16


def _slab(tab_ref, row):
    return tab_ref[pl.ds(pl.multiple_of(row, EXPERT_ROWS), EXPERT_ROWS), :]


def _token_rows_bf16(h8, i):
    row = lax.broadcasted_iota(jnp.int32, (SLAB_ROWS_BF16, V7X_LANES), 0)
    hq = jnp.zeros((SLAB_ROWS_BF16, V7X_LANES), F32)
    for r in range(SLAB_ROWS_BF16):
        v = _slab_value_row(r)
        hq = jnp.where(row == r, h8[i:i + 1, v * V7X_LANES:(v + 1) * V7X_LANES], hq)
    return hq.astype(BF16)


def _peer_v_kernel(idx_ref, w_ref, x_ref, gate_ref, tab_ref, spread_ref, o_ref, g_ref):
    groups = w_ref.shape[0] // PEER_GROUP
    last_slot = PEER_GROUP - 1
    shape = (SLAB_ROWS_BF16, PAIR_LANES)
    out_row = lax.broadcasted_iota(jnp.int32, shape, 0)
    diag = (lax.broadcasted_iota(jnp.int32, shape, 1) % SLAB_ROWS_BF16) == 2 * (out_row % EXPERT_ROWS) + out_row // EXPERT_ROWS
    slot_rows = PEER_PAIRS * EXPERT_ROWS
    gate = gate_ref[0]

    @pl.when(pl.program_id(0) == 0)
    def _():
        g_ref[pl.ds(last_slot * slot_rows, slot_rows), :] = jnp.zeros((slot_rows, V7X_LANES), jnp.int32)

    row8 = lax.broadcasted_iota(jnp.int32, (PEER_GROUP, V7X_LANES), 0)
    blocks = range(SLAB_ROWS_BF16)

    def contract(slot, wrow, tile):
        rows = pltpu.bitcast(g_ref[pl.ds(slot * slot_rows, slot_rows), :], BF16)
        wi = jnp.where(diag, jnp.broadcast_to(wrow, shape), 0.0).astype(BF16)
        out = jnp.dot(wi, rows, preferred_element_type=F32)
        return [jnp.where(row8 == slot, jnp.broadcast_to(out[v:v + 1, :], row8.shape), tile[v]) for v in blocks]

    def flush(g, tile):
        rows = pl.ds(pl.multiple_of(g * PEER_GROUP, PEER_GROUP), PEER_GROUP)
        for v in blocks:
            lanes = slice(v * V7X_LANES, (v + 1) * V7X_LANES)
            o_ref[rows, lanes] = x_ref[rows, lanes] + gate[:, lanes] * tile[v]

    def group(g, carry):
        prev_wide, prev_tile = carry
        flush(jnp.maximum(g - 1, 0), contract(last_slot, prev_wide[last_slot:], list(prev_tile)))
        w8 = w_ref[pl.ds(pl.multiple_of(g * PEER_GROUP, PEER_GROUP), PEER_GROUP), :]
        wide = jnp.dot(w8.astype(BF16), spread_ref[...], preferred_element_type=F32)
        tile = [jnp.zeros(row8.shape, F32) for _ in blocks]
        for i in range(PEER_GROUP):
            tok_idx = idx_ref.at[pl.ds((g * PEER_GROUP + i) * PEER_PAIRS, PEER_PAIRS)]
            for j in range(PEER_PAIRS):
                g_ref[pl.ds((i * PEER_PAIRS + j) * EXPERT_ROWS, EXPERT_ROWS), :] = _slab(tab_ref, tok_idx[j])
            if i < last_slot:
                tile = contract(i, wide[i:i + 1], tile)
        return wide, tuple(tile)

    init = (jnp.zeros((PEER_GROUP, PAIR_LANES), F32), tuple(jnp.zeros(row8.shape, F32) for _ in blocks))
    wide, tile = lax.fori_loop(0, groups, group, init)
    flush(groups - 1, contract(last_slot, wide[last_slot:], list(tile)))


def peer_expert_mix(idx4, w, x, gate, tab, mod_index, tb=PEER_TILE):
    r, d = x.shape
    spread = jnp.repeat(jnp.eye(PEER_PAIRS, dtype=BF16), SLAB_ROWS_BF16, axis=1)
    return pl.pallas_call(
        _peer_v_kernel,
        grid=(r // tb,),
        in_specs=[
            pl.BlockSpec((tb * PEER_PAIRS,), lambda i: (i,), memory_space=pltpu.SMEM),
            pl.BlockSpec((tb, PEER_PAIRS), lambda i: (i, 0)),
            pl.BlockSpec((tb, d), lambda i: (i, 0)),
            pl.BlockSpec((1, 1, d), lambda i: (mod_index(i), 0, 0)),
            pl.BlockSpec(tab.shape, lambda i: (0, 0), pipeline_mode=pl.Buffered(1)),
            pl.BlockSpec((PEER_PAIRS, PAIR_LANES), lambda i: (0, 0)),
        ],
        out_specs=pl.BlockSpec((tb, d), lambda i: (i, 0)),
        out_shape=jax.ShapeDtypeStruct((r, d), F32),
        scratch_shapes=[pltpu.VMEM((PEER_GROUP * PEER_PAIRS * EXPERT_ROWS, V7X_LANES), jnp.int32)],
        compiler_params=_cparams("arbitrary"),
        name="peer_expert_mix",
    )(idx4, w, x, gate, tab, spread)


TOPK_B_BEFORE_TOKEN = 2
TOPK_C_BEFORE_TOKEN = 5


def _peer_front_kernel(x_ref, g_ref, sh_ref, sc_ref, wq_ref, keys_ref, tab_ref, fold_ref, idx_out_ref, w_out_ref,
                       hb_ref, e_scr, w_scr, h_buf, gw_buf, idx_vmem, idx_smem, s_ref, sem):
    step = pl.program_id(0)
    wslot = step % 2
    rslot = 1 - wslot
    tb = x_ref.shape[0]

    @pl.when(step == 0)
    def _():
        h_buf[1] = jnp.zeros(h_buf.shape[1:], F32)
        gw_buf[1] = jnp.zeros(gw_buf.shape[1:], F32)
        idx_vmem[...] = jnp.zeros(idx_vmem.shape, jnp.int32)
        fill = pltpu.make_async_copy(idx_vmem, idx_smem.at[1], sem)
        fill.start()
        fill.wait()

    h = _norm_mod(x_ref[...], g_ref[...], sh_ref[0], sc_ref[0])
    h_buf[wslot] = h
    hb_ref[...] = h.astype(BF16)
    iota_n = lax.broadcasted_iota(jnp.int32, (PEER_KEYS, tb), 0).astype(F32)
    half = PEER_TOPK // 2
    sub = lambda n: lax.broadcasted_iota(jnp.int32, (n, tb), 0).astype(F32)
    pos_c = jnp.concatenate([sub(PEER_TOPK)] + [sub(half) + float(a * PEER_TOPK) for a in range(1, half)]
                            + [(sub(half) + float(half)) * float(PEER_TOPK)], axis=0)

    def pair_up(first, second, scale):
        return jnp.concatenate([first[0:1] * scale + second]
                               + [first[a:a + 1] * scale + second[0:half] for a in range(1, half)]
                               + [first[half:] * scale + second[0:1]], axis=0)

    def retrieve_stages(hd):
        q = jnp.dot(hb_ref[...], wq_ref[hd], preferred_element_type=F32)
        tops = []

        def first_stage(p):
            qp = q[:, p * PEER_HALF:(p + 1) * PEER_HALF].astype(BF16)
            s = lax.dot_general(keys_ref[hd, p], qp, (((1,), (1,)), ((), ())),
                                preferred_element_type=F32)
            tops.append(_topk_rows(s, iota_n, PEER_TOPK))

        return [functools.partial(first_stage, 0), functools.partial(first_stage, 1),
                functools.partial(second_stage, hd, tops)]

    def second_stage(hd, tops):
        (s1, i1), (s2, i2) = tops
        cand = pair_up(s1, s2, 1.0)
        cidx = pair_up(i1, i2, float(PEER_KEYS))
        row = lax.broadcasted_iota(jnp.int32, (PEER_TOPK, tb), 0)
        sc = jnp.zeros((PEER_TOPK, tb), F32)
        ex = jnp.zeros((PEER_TOPK, tb), F32)
        for r in range(PEER_TOPK):
            m = jnp.max(cand, axis=0, keepdims=True)
            am = jnp.min(jnp.where(cand == m, pos_c, float(PEER_TOPK * PEER_TOPK)), axis=0, keepdims=True)
            hit = pos_c == am
            e = jnp.max(jnp.where(hit, cidx, 0.0), axis=0, keepdims=True)
            sc = jnp.where(row == r, m, sc)
            ex = jnp.where(row == r, e, ex)
            cand = jnp.where(hit, -jnp.inf, cand)
        pexp = jnp.exp(sc - sc[0:1])
        rows = pl.ds(pl.multiple_of(hd * PEER_TOPK, PEER_TOPK), PEER_TOPK)
        w_scr[rows, :] = pexp / jnp.sum(pexp, axis=0, keepdims=True)
        e_scr[rows, :] = ex

    row8 = lax.broadcasted_iota(jnp.int32, (PEER_GROUP, V7X_LANES), 0)
    h_prev, gw_prev, idx_prev = h_buf.at[rslot], gw_buf.at[rslot], idx_smem.at[rslot]

    def finish(g, sums):
        hi = sums.astype(BF16)
        lo = (sums - hi.astype(F32)).astype(BF16)
        a = (jnp.dot(hi, fold_ref[...], preferred_element_type=F32)
             + jnp.dot(lo, fold_ref[...], preferred_element_type=F32))
        rows = pl.ds(pl.multiple_of(g * PEER_GROUP, PEER_GROUP), PEER_GROUP)
        gelu = 0.5 * a * (1.0 + lax.erf(a * (2.0 ** -0.5)))
        w_out_ref[rows, :] = gelu * gw_prev[rows, :]

    def gate_group(g, prev, before_token):
        finish(jnp.maximum(g - 1, 0), prev)
        sums = jnp.zeros((PEER_GROUP, PAIR_LANES), F32)
        h8 = h_prev[pl.ds(pl.multiple_of(g * PEER_GROUP, PEER_GROUP), PEER_GROUP), :]
        for i in range(PEER_GROUP):
            if i in before_token:
                before_token[i]()
            hb = _token_rows_bf16(h8, i)
            tok_idx = idx_prev.at[g * PEER_GROUP + i]
            for j in range(PEER_PAIRS):
                u = pltpu.bitcast(_slab(tab_ref, tok_idx[j]), BF16)
                s_ref[pl.ds((i * PEER_PAIRS + j) * EXPERT_ROWS, EXPERT_ROWS), :] = pltpu.bitcast(u * hb, jnp.int32)
            prod = pltpu.bitcast(s_ref[pl.ds(i * PEER_PAIRS * EXPERT_ROWS, PEER_PAIRS * EXPERT_ROWS), :], BF16)
            pick = jnp.where(row8 == i, 1.0, 0.0).astype(BF16)
            sums = sums + lax.dot_general(pick, prod, (((1,), (1,)), ((), ())), preferred_element_type=F32)
        return sums

    groups_per_head = tb // PEER_GROUP // PEER_HEADS

    def body(hd, sums):
        first_a, first_b, second = retrieve_stages(hd)
        placement = [{0: first_a, TOPK_B_BEFORE_TOKEN: first_b, TOPK_C_BEFORE_TOKEN: second}]
        for k in range(groups_per_head):
            sums = gate_group(hd * groups_per_head + k, sums, placement[k] if k < len(placement) else {})
        return sums

    last = lax.fori_loop(0, PEER_HEADS, body, jnp.zeros((PEER_GROUP, PAIR_LANES), F32))
    finish(tb // PEER_GROUP - 1, last)
    gw_buf[wslot] = w_scr[...].T
    idx = (e_scr[...].T * float(EXPERT_ROWS)).astype(jnp.int32)
    idx_out_ref[...] = idx
    idx_vmem[...] = idx
    handoff = pltpu.make_async_copy(idx_vmem, idx_smem.at[wslot], sem)
    handoff.start()
    handoff.wait()


def peer_front(x, g, shift, scale, wq_heads, keys, tab, mod_index, tb=PEER_TILE):
    r, d = x.shape
    nb = r // tb
    fold = jnp.repeat(jnp.eye(PEER_PAIRS, dtype=BF16), SLAB_ROWS_BF16, axis=0)
    cur = lambda i: jnp.minimum(i, nb - 1)
    const = lambda shape: pl.BlockSpec(shape, lambda i: (0,) * len(shape), pipeline_mode=pl.Buffered(1))
    return pl.pallas_call(
        _peer_front_kernel,
        grid=(nb + 1,),
        in_specs=[
            pl.BlockSpec((tb, d), lambda i: (cur(i), 0)),
            pl.BlockSpec((1, d), lambda i: (0, 0)),
            pl.BlockSpec((1, 1, d), lambda i: (mod_index(cur(i)), 0, 0)),
            pl.BlockSpec((1, 1, d), lambda i: (mod_index(cur(i)), 0, 0)),
            const((PEER_HEADS, d, PEER_QDIM)),
            const((PEER_HEADS, 2, PEER_KEYS, PEER_HALF)),
            const(tab.shape),
            const((PAIR_LANES, PEER_PAIRS)),
        ],
        out_specs=[
            pl.BlockSpec((tb, PEER_PAIRS), lambda i: (cur(i), 0)),
            pl.BlockSpec((tb, PEER_PAIRS), lambda i: (jnp.maximum(i - 1, 0), 0)),
        ],
        out_shape=[jax.ShapeDtypeStruct((r, PEER_PAIRS), jnp.int32), jax.ShapeDtypeStruct((r, PEER_PAIRS), F32)],
        scratch_shapes=[
            pltpu.VMEM((tb, d), BF16), pltpu.VMEM((PEER_PAIRS, tb), F32), pltpu.VMEM((PEER_PAIRS, tb), F32),
            pltpu.VMEM((2, tb, d), F32), pltpu.VMEM((2, tb, PEER_PAIRS), F32),
            pltpu.VMEM((tb, PEER_PAIRS), jnp.int32), pltpu.SMEM((2, tb, PEER_PAIRS), jnp.int32),
            pltpu.VMEM((PEER_GROUP * PEER_PAIRS * EXPERT_ROWS, V7X_LANES), jnp.int32),
            pltpu.SemaphoreType.DMA,
        ],
        compiler_params=_cparams("arbitrary"),
        name="peer_front",
    )(x, g, shift, scale, wq_heads, keys, tab, fold)


def peer(x, g, shift, scale, gate, wq, keys, u_tab, v_tab, mod_index_peer):
    r, d = x.shape
    wq_heads = wq.astype(BF16).reshape(d, PEER_HEADS, PEER_QDIM).transpose(1, 0, 2)
    idx4, w = peer_front(x, g, shift, scale, wq_heads, keys.astype(BF16), pack_rows_bf16(u_tab), mod_index_peer)
    return peer_expert_mix(idx4.reshape(r * PEER_PAIRS), w, x, gate, pack_rows_bf16(v_tab), mod_index_peer)


HG_TILE = 256
HG_SUB = 16
HG_PAIR = 2 * HG_DK


def _hgrn_kernel(*refs, rev):
    blocks = HG_W // HG_PAIR
    q_refs, f_refs, v_refs = refs[:blocks], refs[blocks:2 * blocks], refs[2 * blocks:3 * blocks]
    lb_ref, o_ref, st_ref = refs[3 * blocks:]
    wide = lambda rs: jnp.concatenate([r[...] for r in rs], axis=1)

    @pl.when(pl.program_id(1) == 0)
    def _():
        st_ref[...] = jnp.zeros_like(st_ref)

    n = o_ref.shape[0]
    lb = lb_ref[...]
    f = lb + (1.0 - lb) * jax.nn.sigmoid(wide(f_refs))
    kk = 1.0 - f
    lf = jnp.log(f)
    t = lax.broadcasted_iota(jnp.int32, (n, n), 0)
    s = lax.broadcasted_iota(jnp.int32, (n, n), 1)
    same = (t // HG_SUB) == (s // HG_SUB)
    mid = (t // HG_SUB) * HG_SUB + (HG_SUB // 2 if rev else HG_SUB // 2 - 1)
    seen = (s >= t) if rev else (s <= t)
    seen_mid = (s >= mid) if rev else (s <= mid)
    one = lambda m: jnp.where(m, 1.0, 0.0).astype(F32)
    hp = lambda a, b: jnp.dot(a, b, precision=lax.Precision.HIGHEST, preferred_element_type=F32)
    cum = hp(one(same & seen), lf)
    ref = hp(one(same & seen_mid), lf)
    last = hp(one(same), lf)
    q = wide(q_refs)
    qe = (q * jnp.exp(cum)).astype(BF16)
    qm = (q * jnp.exp(cum - ref)).astype(BF16)
    km = (kk * jnp.exp(ref - cum)).astype(BF16)
    kd = (kk * jnp.exp(last - cum)).astype(BF16)
    dec = jnp.exp(last)
    vb = wide(v_refs).astype(BF16)
    steps = range(n // HG_SUB)
    heads = range(HG_HEADS)
    intra = []
    for h in heads:
        cols = slice(h * HG_DK, (h + 1) * HG_DK)
        att = lax.dot_general(qm[:, cols], km[:, cols], (((1,), (1,)), ((), ())), preferred_element_type=F32)
        att = jnp.where(same & seen, att, 0.0).astype(BF16)
        intra.append(jnp.dot(att, vb[:, cols], preferred_element_type=F32))
    for c in (reversed(steps) if rev else steps):
        rows = slice(c * HG_SUB, (c + 1) * HG_SUB)
        for h in heads:
            cols = slice(h * HG_DK, (h + 1) * HG_DK)
            st = st_ref[h]
            o_ref[rows, cols] = intra[h][rows] + lax.dot_general(
                qe[rows, cols], st.astype(BF16), (((1,), (1,)), ((), ())), preferred_element_type=F32)
            upd = lax.dot_general(vb[rows, cols], kd[rows, cols], (((0,), (0,)), ((), ())),
                                  preferred_element_type=F32)
            st_ref[h] = st * dec[c * HG_SUB:c * HG_SUB + 1, cols] + upd


def hgrn_scan(pa, lb, rev, batch):
    rows = pa.shape[0]
    per = rows // batch // HG_TILE
    nctx = CTX_LEN // HG_TILE
    col0 = GQA_IN // HG_PAIR
    blocks = HG_W // HG_PAIR

    def tok(b, k):
        if rev:
            k = jnp.where(k < nctx, nctx - 1 - k, per - 1 - (k - nctx))
        return b * per + k

    specs = lambda cb: [pl.BlockSpec((HG_TILE, HG_PAIR), lambda b, k, c=cb + p: (tok(b, k), c)) for p in range(blocks)]
    d = 1 if rev else 0
    return pl.pallas_call(
        functools.partial(_hgrn_kernel, rev=rev),
        grid=(batch, per),
        in_specs=specs(col0) + specs(col0 + (1 + d) * blocks) + specs(col0 + 3 * blocks)
        + [pl.BlockSpec((None, 1, HG_W), lambda b, k: (d, 0, 0))],
        out_specs=pl.BlockSpec((HG_TILE, HG_W), lambda b, k: (tok(b, k), 0)),
        out_shape=jax.ShapeDtypeStruct((rows, HG_W), F32),
        scratch_shapes=[pltpu.VMEM((HG_HEADS, HG_DV, HG_DK), F32)],
        compiler_params=_cparams("parallel", "arbitrary"),
        name="hgrn_scan_rev" if rev else "hgrn_scan_fwd",
    )(*([pa] * (3 * blocks)), lb)


def _rms_norm(x, g):
    xf = x.astype(F32)
    y = xf * lax.rsqrt(jnp.mean(xf * xf, axis=-1, keepdims=True) + EPS)
    return y * g.astype(F32)


def _hgrn2(pa2, batch, lb, onorm_g):
    lb = lb.reshape(2, 1, HG_W)
    o = hgrn_scan(pa2, lb, False, batch) + hgrn_scan(pa2, lb, True, batch)
    o = _rms_norm(o.reshape(-1, HG_HEADS, HG_DV), onorm_g).reshape(-1, HG_W)
    return o * jax.nn.silu(pa2[:, GQA_IN + 4 * HG_W:])


S5_CHUNK = 128
S5_LANES = S5_GROUPS * S5_STATE


def _s5_kernel(u_ref, bre_ref, bim_ref, cre_ref, cim_ref, are_ref, aim_ref, y_ref, xre, xim, hre, him):
    nb = V7X_SUBLANES // 2
    tiles = u_ref.shape[0] // V7X_SUBLANES
    direction = pl.program_id(0)
    rev = direction == 1

    @pl.when(pl.program_id(1) == 0)
    def _():
        hre[...] = jnp.zeros_like(hre)
        him[...] = jnp.zeros_like(him)

    ub = u_ref[...].astype(BF16)
    tile = V7X_MXU_DIM
    fan = (S5_LANES // tile) // (u_ref.shape[1] // tile)
    for nt in range(S5_LANES // tile):
        cols = slice(nt * tile, (nt + 1) * tile)
        src = ub[:, (nt // fan) * tile:(nt // fan + 1) * tile]
        xre[:, cols] = jnp.dot(src, bre_ref[nt], preferred_element_type=F32)
        xim[:, cols] = jnp.dot(src, bim_ref[nt], preferred_element_type=F32)
    shape = (V7X_SUBLANES, S5_LANES)
    second = (lax.broadcasted_iota(jnp.int32, shape, 0) // nb) != direction
    are = jnp.broadcast_to(are_ref[...], shape)
    aim = jnp.broadcast_to(aim_ref[...], shape)
    cre = jnp.where(second, are * are - aim * aim, are)
    cim = jnp.where(second, 2.0 * are * aim, aim)

    def step(k, carry):
        pr, pi = carry
        k = jnp.where(rev, tiles - 1 - k, k)
        r = pl.ds(pl.multiple_of(k * V7X_SUBLANES, V7X_SUBLANES), V7X_SUBLANES)
        xr, xi = xre[r, :], xim[r, :]
        sr = jnp.where(second, pltpu.roll(xr, nb, axis=0), 0.0)
        si = jnp.where(second, pltpu.roll(xi, nb, axis=0), 0.0)
        nr = xr + (are * sr - aim * si) + (cre * pr - cim * pi)
        ni = xi + (are * si + aim * sr) + (cre * pi + cim * pr)
        xre[r, :] = nr
        xim[r, :] = ni
        return (jnp.where(second, nr, pltpu.roll(nr, nb, axis=0)),
                jnp.where(second, ni, pltpu.roll(ni, nb, axis=0)))

    hr, hi = lax.fori_loop(0, tiles, step, (hre[...], him[...]))
    hre[...] = hr
    him[...] = hi
    for mt in range(u_ref.shape[1] // tile):
        acc = jnp.zeros((u_ref.shape[0], tile), F32)
        for kt in range(mt * fan, (mt + 1) * fan):
            cols = slice(kt * tile, (kt + 1) * tile)
            acc = acc + (jnp.dot(xre[:, cols].astype(BF16), cre_ref[kt], preferred_element_type=F32)
                         - jnp.dot(xim[:, cols].astype(BF16), cim_ref[kt], preferred_element_type=F32))
        y_ref[:, mt * tile:(mt + 1) * tile] = acc


def s5_scan(u, bre, bim, cre, cim, are, aim, nb):
    rows, w = u.shape
    blk = S5_CHUNK * nb
    assert rows % blk == 0 and 2 * nb == V7X_SUBLANES
    nblk, nctx = rows // blk, CTX_LEN // S5_CHUNK
    wspec = lambda shape: pl.BlockSpec((None,) + shape, lambda d, i: (d, 0, 0))
    tiles = pl.BlockSpec((None,) + bre.shape[1:], lambda d, i: (d, 0, 0, 0))

    def chunk(d, i):
        back = jnp.where(i < nctx, nctx - 1 - i, nblk - 1 - (i - nctx))
        return jnp.where(d == 1, back, i)
    return pl.pallas_call(
        _s5_kernel,
        grid=(2, rows // blk),
        in_specs=[
            pl.BlockSpec((blk, w), lambda d, i: (chunk(d, i), 0)),
            tiles, tiles, tiles, tiles,
            wspec((1, S5_LANES)), wspec((1, S5_LANES)),
        ],
        out_specs=pl.BlockSpec((None, blk, w), lambda d, i: (d, chunk(d, i), 0)),
        out_shape=jax.ShapeDtypeStruct((2, rows, w), F32),
        scratch_shapes=[pltpu.VMEM((blk, S5_LANES), F32), pltpu.VMEM((blk, S5_LANES), F32),
                        pltpu.VMEM((V7X_SUBLANES, S5_LANES), F32), pltpu.VMEM((V7X_SUBLANES, S5_LANES), F32)],
        compiler_params=_cparams("arbitrary", "arbitrary"),
        name="s5_scan",
    )(u, bre, bim, cre, cim, are, aim)


def _s5_core(u, a_re, a_im, log_dt, b_re, b_im, c_re, c_im, d_skip):
    b, s, w = u.shape
    dt = jnp.exp(log_dt)[..., None]
    mag = jnp.exp(a_re * dt)
    abar_re, abar_im = mag * jnp.cos(a_im * dt), mag * jnp.sin(a_im * dt)
    den = a_re * a_re + a_im * a_im
    k_re = ((abar_re - 1.0) * a_re + abar_im * a_im) / den
    k_im = (abar_im * a_re - (abar_re - 1.0) * a_im) / den
    bb_re = k_re[..., None] * b_re - k_im[..., None] * b_im
    bb_im = k_re[..., None] * b_im + k_im[..., None] * b_re
    eye = jnp.eye(S5_GROUPS, dtype=F32)
    tile, n_tiles = V7X_MXU_DIM, S5_LANES // V7X_MXU_DIM
    fan = n_tiles // (w // tile)
    cut = lambda m, r, c: m[:, r * tile:(r + 1) * tile, c * tile:(c + 1) * tile]

    def bd_in(m):
        full = jnp.einsum('dgpc,gh->dgchp', m, eye).reshape(2, w, S5_LANES).astype(BF16)
        return jnp.stack([cut(full, nt // fan, nt) for nt in range(n_tiles)], axis=1)

    def bd_out(m):
        full = jnp.einsum('dgcp,gh->dgphc', m, eye).reshape(2, S5_LANES, w).astype(BF16)
        return jnp.stack([cut(full, kt, kt // fan) for kt in range(n_tiles)], axis=1)
    y2 = s5_scan(u.transpose(1, 0, 2).reshape(s * b, w), bd_in(bb_re), bd_in(bb_im), bd_out(c_re), bd_out(c_im),
                 abar_re.reshape(2, 1, S5_LANES), abar_im.reshape(2, 1, S5_LANES), b)
    return d_skip * u + (y2[0] + y2[1]).reshape(s, b, w).transpose(1, 0, 2)


def _rope_tables(rows, rot_dim, lead, period):
    axis_dim = rot_dim // 2
    inv = ROPE_THETA ** (-jnp.arange(0, axis_dim, 2, dtype=F32) / axis_dim)
    t = jnp.arange(rows * GRID_W)
    r = (t // GRID_W).astype(F32)[:, None] * inv
    c = (t % GRID_W).astype(F32)[:, None] * inv
    n = t.shape[0]
    tail = period - lead - rot_dim
    cos = jnp.concatenate([jnp.ones((n, lead), F32), jnp.cos(r), jnp.cos(r), jnp.cos(c), jnp.cos(c),
                           jnp.ones((n, tail), F32)], axis=-1)
    sin = jnp.concatenate([jnp.zeros((n, lead), F32), -jnp.sin(r), jnp.sin(r), -jnp.sin(c), jnp.sin(c),
                           jnp.zeros((n, tail), F32)], axis=-1)
    reps = V7X_LANES // period
    cos = jnp.concatenate([jnp.ones((CTX_LEN, V7X_LANES), F32), jnp.tile(cos, (1, reps))], axis=0)
    sin = jnp.concatenate([jnp.zeros((CTX_LEN, V7X_LANES), F32), jnp.tile(sin, (1, reps))], axis=0)
    return cos, sin


def _attend_all(q, k, v, need_ctx):
    o_lat = attention(q[:, CTX_LEN:], k, v)
    if need_ctx:
        o_ctx = attention(q[:, :CTX_LEN], k[:, :CTX_LEN], v[:, :CTX_LEN])
    else:
        o_ctx = jnp.zeros((q.shape[0], CTX_LEN, q.shape[2]), F32)
    return jnp.concatenate([o_ctx, o_lat], axis=1)


def _gqa(pa2, b, qn_g, kn_g, cos, sin, need_ctx):
    s = pa2.shape[0] // b
    grp = GQA_Q_HEADS // GQA_KV_HEADS
    gains = jnp.concatenate([jnp.tile(qn_g, GQA_Q_HEADS), jnp.tile(kn_g, GQA_KV_HEADS)])[None, :]
    q, k, v = gqa_prep(pa2, gains, cos, sin, s)
    o = _attend_all(q.reshape(b, s, -1), k.reshape(b, s, -1), v.reshape(b, s, -1), need_ctx)
    o = o.reshape(b, s, GQA_Q_HEADS, V7X_LANES)
    first = (jnp.arange(GQA_Q_HEADS) < grp)[None, None, :, None]
    o = jnp.where(first, o[..., :HEAD_DIM], o[..., HEAD_DIM:])
    return o.reshape(b, s, GQA_Q_HEADS * HEAD_DIM)


def _mla(p, qa_g, w_qup, kva_g, w_kvup, cos, sin, need_ctx):
    b, s, _ = p.shape
    r0 = MLA_Q_RANK
    r1 = MLA_Q_RANK + MLA_KV_RANK
    blocks = lambda w, lo, hi: jnp.pad(w.reshape(w.shape[0], MLA_HEADS, -1)[..., lo:hi],
                                       ((0, 0), (0, 0), (0, V7X_LANES - (hi - lo)))).reshape(w.shape[0], -1)
    w_q = blocks(w_qup, 0, MLA_NOPE + MLA_ROPE).astype(BF16)
    w_kv = jnp.concatenate([blocks(w_kvup, 0, MLA_NOPE), blocks(w_kvup, MLA_NOPE, MLA_NOPE + MLA_V)],
                           axis=1).astype(BF16)
    zero = jnp.zeros((1, 1, r0), F32)
    q = norm_mod_matmul(p[..., :r0].reshape(b * s, r0), qa_g[None, :], zero, zero, w_q, lambda i: 0)
    zero = jnp.zeros((1, 1, MLA_KV_RANK), F32)
    kv = norm_mod_matmul(p[..., r0:r1].reshape(b * s, MLA_KV_RANK), kva_g[None, :], zero, zero, w_kv, lambda i: 0)
    kr = jnp.pad(p[..., r1:r1 + MLA_ROPE].reshape(b * s, MLA_ROPE),
                 ((0, 0), (MLA_NOPE, V7X_LANES - MLA_NOPE - MLA_ROPE)))
    q, k, v = mla_prep(q, kv, kr, cos, sin, s)
    o = _attend_all(q.reshape(b, s, -1), k.reshape(b, s, -1), v.reshape(b, s, -1), need_ctx)
    return o.reshape(b, s, MLA_HEADS, V7X_LANES)[..., :MLA_V].reshape(b, s, MLA_HEADS * MLA_V)


def kernel(x, c, ctx, c_ctx, ada_w, ada_b, norm1_g, norm2_g, ev_w_in, ev_w_out, gqa_qn_g, gqa_kn_g,
           hg_lb_logits, hg_onorm_g, od_w_in, od_w_out, mla_qa_g, mla_w_qup, mla_kva_g, mla_w_kvup,
           s5_a_re, s5_a_im, s5_log_dt, s5_b_re, s5_b_im, s5_c_re, s5_c_im, s5_d, s5_w_glu, s5_b_glu,
           peer_wq, peer_keys, peer_u, peer_v, final_g):
    b, t, d = x.shape
    s = CTX_LEN + t
    depth = ada_w.shape[0]
    rows = t // GRID_W
    cos_g, sin_g = _rope_tables(rows, HEAD_DIM, 0, HEAD_DIM)
    cos_m, sin_m = _rope_tables(rows, MLA_ROPE, MLA_NOPE, V7X_LANES)
    lb_all = jnp.cumsum(jax.nn.softmax(hg_lb_logits, axis=1), axis=1)

    def mod_index(tile):
        per, nctx = s // tile, CTX_LEN // tile
        return lambda i: jnp.where(i % per < nctx, b, i // per)

    mi_row = mod_index(ROW_TILE)
    mi_peer = mod_index(PEER_TILE)

    xa = jnp.concatenate([ctx, x], axis=1).reshape(b * s, d)
    s_all = jnp.concatenate([jax.nn.silu(c), jax.nn.silu(c_ctx)[None, :]], axis=0)
    s_pad = jnp.concatenate([s_all, jnp.zeros((V7X_SUBLANES - (b + 1) % V7X_SUBLANES, d), F32)], axis=0)

    for layer in range(depth):
        need_ctx = layer < depth - 1
        j = layer // 2
        mod = matmul(s_pad, ada_w[layer].astype(BF16), tm=s_pad.shape[0])[:b + 1] + ada_b[layer]
        mod = [m[:, None, :] for m in jnp.split(mod, 6, axis=-1)]
        if layer % 2 == 0:
            pa2 = norm_mod_matmul(xa, norm1_g[layer][None, :], mod[0], mod[1], ev_w_in[j].astype(BF16), mi_row)
            ya = _gqa(pa2, b, gqa_qn_g[j], gqa_kn_g[j], cos_g, sin_g, need_ctx)
            yb = _hgrn2(pa2, b, lb_all[:, j], hg_onorm_g[j]).reshape(b, s, HG_W)
            w_out = ev_w_out[j]
        else:
            pa = norm_mod_matmul(xa, norm1_g[layer][None, :], mod[0], mod[1], od_w_in[j].astype(BF16), mi_row)
            pa = pa.reshape(b, s, -1)
            ya = _mla(pa[..., :MLA_IN], mla_qa_g[j], mla_w_qup[j], mla_kva_g[j], mla_w_kvup[j], cos_m, sin_m, need_ctx)
            y5 = _s5_core(pa[..., MLA_IN:], s5_a_re[j], s5_a_im[j], s5_log_dt[j], s5_b_re[j], s5_b_im[j],
                          s5_c_re[j], s5_c_im[j], s5_d[j])
            z = jax.nn.gelu(y5, approximate=False).reshape(b * s, S5_WIDTH)
            gl = matmul(z, s5_w_glu[j].astype(BF16)) + s5_b_glu[j]
            yb = (z * jax.nn.sigmoid(gl)).reshape(b, s, S5_WIDTH)
            w_out = od_w_out[j]
        xa = matmul_residual(ya.reshape(b * s, -1), yb.reshape(b * s, -1), w_out.astype(BF16), xa, mod[2], mi_row)
        xa = peer(xa, norm2_g[layer][None, :], mod[3], mod[4], mod[5],
                  peer_wq[layer], peer_keys[layer], peer_u[layer], peer_v[layer], mi_peer)
    xl = xa.reshape(b, s, d)[:, CTX_LEN:].reshape(b * t, d)
    return rmsnorm_rows(xl, final_g[None, :]).reshape(b, t, d)
```

```python
import functools
import math

import jax
import jax.numpy as jnp
from jax import lax
from jax.experimental import pallas as pl
from jax.experimental.pallas import tpu as pltpu

F32 = jnp.float32
BF16 = jnp.bfloat16

D_MODEL = 1024
GRID_W = 64
CTX_LEN = 256
EPS = 1e-6
ROPE_THETA = 10000.0

MIX_HALF = D_MODEL // 2
HEAD_DIM = 64
GQA_Q_HEADS = MIX_HALF // HEAD_DIM
GQA_KV_HEADS = GQA_Q_HEADS // 4
GQA_IN = (GQA_Q_HEADS + 2 * GQA_KV_HEADS) * HEAD_DIM

HG_DK = 128
HG_DV = 128
HG_HEADS = MIX_HALF // HG_DV
HG_W = HG_HEADS * HG_DK

MLA_HEADS = MIX_HALF // HEAD_DIM
MLA_NOPE = 64
MLA_ROPE = 32
MLA_V = 64
MLA_Q_RANK = 384
MLA_KV_RANK = 256
MLA_IN = MLA_Q_RANK + MLA_KV_RANK + MLA_ROPE

S5_WIDTH = MIX_HALF
S5_GROUP = 16
S5_GROUPS = S5_WIDTH // S5_GROUP
S5_STATE = 64

PEER_HEADS = 8
PEER_KEYS = 128
N_EXPERTS = PEER_KEYS * PEER_KEYS
PEER_TOPK = 16
PEER_QDIM = 256
PEER_HALF = PEER_QDIM // 2
PEER_PAIRS = PEER_HEADS * PEER_TOPK

V7X_LANES = 128
V7X_SUBLANES = 8
V7X_VMEM_BYTES = 64 * 1024 * 1024
VMEM_LIMIT = V7X_VMEM_BYTES - 8 * 1024 * 1024

ROW_TILE = 256
PEER_TILE = 128
ATTN_Q_TILE = 1024
EXPERT_WORDS = D_MODEL // 2
EXPERT_ROWS = EXPERT_WORDS // V7X_LANES


def _cparams(*sem):
    return pltpu.CompilerParams(dimension_semantics=sem, vmem_limit_bytes=VMEM_LIMIT)


def _norm_mod(x, g, shift, scale):
    ms = jnp.mean(x * x, axis=-1, keepdims=True)
    h = (x * lax.rsqrt(ms + EPS)) * g
    return h * (1.0 + scale) + shift


def _nmm_kernel(x_ref, g_ref, sh_ref, sc_ref, w_ref, o_ref):
    h = _norm_mod(x_ref[...], g_ref[...], sh_ref[0], sc_ref[0])
    o_ref[...] = jnp.dot(h.astype(BF16), w_ref[...], preferred_element_type=F32)


def norm_mod_matmul(x, g, shift, scale, w, mod_index, tm=ROW_TILE):
    r, k = x.shape
    n = w.shape[1]
    assert r % tm == 0 and w.shape[0] == k
    return pl.pallas_call(
        _nmm_kernel,
        grid=(r // tm,),
        in_specs=[
            pl.BlockSpec((tm, k), lambda i: (i, 0)),
            pl.BlockSpec((1, k), lambda i: (0, 0)),
            pl.BlockSpec((1, 1, k), lambda i: (mod_index(i), 0, 0)),
            pl.BlockSpec((1, 1, k), lambda i: (mod_index(i), 0, 0)),
            pl.BlockSpec((k, n), lambda i: (0, 0)),
        ],
        out_specs=pl.BlockSpec((tm, n), lambda i: (i, 0)),
        out_shape=jax.ShapeDtypeStruct((r, n), F32),
        compiler_params=_cparams("parallel"),
        name="norm_mod_matmul",
    )(x, g, shift, scale, w)


def _mm_kernel(a_ref, w_ref, o_ref):
    o_ref[...] = jnp.dot(a_ref[...].astype(BF16), w_ref[...], preferred_element_type=F32)


def matmul(a, w, tm=ROW_TILE):
    r, k = a.shape
    n = w.shape[1]
    assert r % tm == 0
    return pl.pallas_call(
        _mm_kernel,
        grid=(r // tm,),
        in_specs=[pl.BlockSpec((tm, k), lambda i: (i, 0)), pl.BlockSpec((k, n), lambda i: (0, 0))],
        out_specs=pl.BlockSpec((tm, n), lambda i: (i, 0)),
        out_shape=jax.ShapeDtypeStruct((r, n), F32),
        compiler_params=_cparams("parallel"),
        name="matmul",
    )(a, w)


def _mmres_kernel(a_ref, b_ref, w_ref, x_ref, gate_ref, o_ref):
    ka = a_ref.shape[1]
    y = (jnp.dot(a_ref[...].astype(BF16), w_ref[:ka, :], preferred_element_type=F32)
         + jnp.dot(b_ref[...].astype(BF16), w_ref[ka:, :], preferred_element_type=F32))
    o_ref[...] = x_ref[...] + gate_ref[0] * y


def matmul_residual(a, b, w, x, gate, mod_index, tm=ROW_TILE):
    r, ka = a.shape
    kb = b.shape[1]
    k, n = w.shape
    assert r % tm == 0 and ka + kb == k
    return pl.pallas_call(
        _mmres_kernel,
        grid=(r // tm,),
        in_specs=[
            pl.BlockSpec((tm, ka), lambda i: (i, 0)),
            pl.BlockSpec((tm, kb), lambda i: (i, 0)),
            pl.BlockSpec((k, n), lambda i: (0, 0)),
            pl.BlockSpec((tm, n), lambda i: (i, 0)),
            pl.BlockSpec((1, 1, n), lambda i: (mod_index(i), 0, 0)),
        ],
        out_specs=pl.BlockSpec((tm, n), lambda i: (i, 0)),
        out_shape=jax.ShapeDtypeStruct((r, n), F32),
        compiler_params=_cparams("parallel"),
        name="matmul_residual",
    )(a, b, w, x, gate)


def _rms_kernel(x_ref, g_ref, o_ref):
    x = x_ref[...]
    ms = jnp.mean(x * x, axis=-1, keepdims=True)
    o_ref[...] = (x * lax.rsqrt(ms + EPS)) * g_ref[...]


def rmsnorm_rows(x, g, tm=ROW_TILE):
    r, k = x.shape
    return pl.pallas_call(
        _rms_kernel,
        grid=(r // tm,),
        in_specs=[pl.BlockSpec((tm, k), lambda i: (i, 0)), pl.BlockSpec((1, k), lambda i: (0, 0))],
        out_specs=pl.BlockSpec((tm, k), lambda i: (i, 0)),
        out_shape=jax.ShapeDtypeStruct((r, k), F32),
        compiler_params=_cparams("parallel"),
        name="rmsnorm_rows",
    )(x, g)


ATTN_KV_CHUNKS = 3
V7X_MXU_DIM = 256
LOG2E = math.log2(math.e)


def _attn_kernel(q_ref, k_ref, v_ref, o_ref):
    q = q_ref[...]
    sk = k_ref.shape[0]
    nch = ATTN_KV_CHUNKS if sk % (ATTN_KV_CHUNKS * V7X_MXU_DIM) == 0 else 1
    ck = sk // nch
    m = l = acc = None
    for c in range(nch):
        kc = k_ref[c * ck:(c + 1) * ck, :]
        vc = v_ref[c * ck:(c + 1) * ck, :]
        s = lax.dot_general(q, kc, (((1,), (1,)), ((), ())), preferred_element_type=F32)
        mc = jnp.max(s, axis=-1, keepdims=True)
        if c == 0:
            m = mc
            p = jnp.exp2(s - m)
            l = jnp.sum(p, axis=-1, keepdims=True)
            acc = jnp.dot(p.astype(BF16), vc, preferred_element_type=F32)
        else:
            m_new = jnp.maximum(m, mc)
            a = jnp.exp2(m - m_new)
            p = jnp.exp2(s - m_new)
            l = a * l + jnp.sum(p, axis=-1, keepdims=True)
            acc = a * acc + jnp.dot(p.astype(BF16), vc, preferred_element_type=F32)
            m = m_new
    o_ref[...] = acc / l


def attention(q, k, v, tq=ATTN_Q_TILE):
    b, sq, hl = q.shape
    sk, hkl = k.shape[1], k.shape[2]
    h, hk = hl // V7X_LANES, hkl // V7X_LANES
    grp = h // hk
    tq = min(tq, sq)
    assert sq % tq == 0
    return pl.pallas_call(
        _attn_kernel,
        grid=(b, h, sq // tq),
        in_specs=[
            pl.BlockSpec((None, tq, V7X_LANES), lambda bi, hi, qi: (bi, qi, hi)),
            pl.BlockSpec((None, sk, V7X_LANES), lambda bi, hi, qi: (bi, 0, hi // grp)),
            pl.BlockSpec((None, sk, V7X_LANES), lambda bi, hi, qi: (bi, 0, hi // grp)),
        ],
        out_specs=pl.BlockSpec((None, tq, V7X_LANES), lambda bi, hi, qi: (bi, qi, hi)),
        out_shape=jax.ShapeDtypeStruct((b, sq, hl), F32),
        compiler_params=_cparams("parallel", "parallel", "parallel"),
        name="attention",
    )(q, k, v)


def _rotate(x, cos, sin, quarter):
    n = x.shape[1]
    lane = lax.broadcasted_iota(jnp.int32, x.shape, 1)
    partner = jnp.where(lane % (2 * quarter) < quarter, pltpu.roll(x, n - quarter, axis=1),
                        pltpu.roll(x, quarter, axis=1))
    return x * cos + partner * sin


def _gqa_prep_kernel(p_ref, g_ref, cos_ref, sin_ref, avg_ref, q_ref, k_ref, v_ref):
    cos, sin = cos_ref[...], sin_ref[...]
    low = lax.broadcasted_iota(jnp.int32, cos.shape, 1) < HEAD_DIM
    qk_blocks = (GQA_Q_HEADS + GQA_KV_HEADS) * HEAD_DIM // V7X_LANES
    q_blocks = GQA_Q_HEADS * HEAD_DIM // V7X_LANES
    for c in range(qk_blocks):
        lanes = slice(c * V7X_LANES, (c + 1) * V7X_LANES)
        x = p_ref[:, lanes]
        ms = jnp.dot(x * x, avg_ref[...], precision=lax.Precision.HIGHEST, preferred_element_type=F32)
        y = _rotate(x * lax.rsqrt(ms + EPS) * g_ref[:, lanes], cos, sin, HEAD_DIM // 4)
        if c < q_blocks:
            y = y * (HEAD_DIM ** -0.5 * LOG2E)
            other = pltpu.roll(y, HEAD_DIM, axis=1)
            if c < q_blocks // 2:
                first, second = jnp.where(low, y, 0.0), jnp.where(low, other, 0.0)
            else:
                first, second = jnp.where(low, 0.0, other), jnp.where(low, 0.0, y)
            q_ref[:, 2 * c * V7X_LANES:(2 * c + 1) * V7X_LANES] = first.astype(BF16)
            q_ref[:, (2 * c + 1) * V7X_LANES:(2 * c + 2) * V7X_LANES] = second.astype(BF16)
        else:
            k_ref[...] = y.astype(BF16)
    v_ref[...] = p_ref[:, qk_blocks * V7X_LANES:(qk_blocks + 1) * V7X_LANES].astype(BF16)


def gqa_prep(pa, gains, cos, sin, seq, tm=ROW_TILE):
    r = pa.shape[0]
    per = seq // tm
    avg = jnp.kron(jnp.eye(V7X_LANES // HEAD_DIM, dtype=F32), jnp.full((HEAD_DIM, HEAD_DIM), 1.0 / HEAD_DIM, F32))
    tab = pl.BlockSpec((tm, V7X_LANES), lambda i: (i % per, 0))
    kv = pl.BlockSpec((tm, V7X_LANES), lambda i: (i, 0))
    return pl.pallas_call(
        _gqa_prep_kernel,
        grid=(r // tm,),
        in_specs=[pl.BlockSpec((tm, GQA_IN), lambda i: (i, 0)), pl.BlockSpec(gains.shape, lambda i: (0, 0)), tab, tab,
                  pl.BlockSpec(avg.shape, lambda i: (0, 0))],
        out_specs=[pl.BlockSpec((tm, GQA_Q_HEADS * V7X_LANES), lambda i: (i, 0)), kv, kv],
        out_shape=[jax.ShapeDtypeStruct((r, GQA_Q_HEADS * V7X_LANES), BF16),
                   jax.ShapeDtypeStruct((r, V7X_LANES), BF16), jax.ShapeDtypeStruct((r, V7X_LANES), BF16)],
        compiler_params=_cparams("parallel"),
        name="gqa_prep",
    )(pa, gains, cos, sin, avg)


def _mla_prep_kernel(q_ref, kv_ref, kr_ref, cos_ref, sin_ref, qo_ref, ko_ref, vo_ref):
    cos, sin = cos_ref[...], sin_ref[...]
    quarter = MLA_ROPE // 4
    kr = _rotate(kr_ref[...], cos, sin, quarter)
    scale = (MLA_NOPE + MLA_ROPE) ** -0.5 * LOG2E
    width = MLA_HEADS * V7X_LANES
    for h in range(MLA_HEADS):
        lanes = slice(h * V7X_LANES, (h + 1) * V7X_LANES)
        qo_ref[:, lanes] = (_rotate(q_ref[:, lanes], cos, sin, quarter) * scale).astype(BF16)
        ko_ref[:, lanes] = (kv_ref[:, lanes] + kr).astype(BF16)
        vo_ref[:, lanes] = kv_ref[:, width + h * V7X_LANES:width + (h + 1) * V7X_LANES].astype(BF16)


def mla_prep(q, kv, kr, cos, sin, seq, tm=ROW_TILE):
    r, width = q.shape
    per = seq // tm
    tab = pl.BlockSpec((tm, V7X_LANES), lambda i: (i % per, 0))
    out = pl.BlockSpec((tm, width), lambda i: (i, 0))
    return pl.pallas_call(
        _mla_prep_kernel,
        grid=(r // tm,),
        in_specs=[out, pl.BlockSpec((tm, 2 * width), lambda i: (i, 0)), pl.BlockSpec((tm, V7X_LANES), lambda i: (i, 0)),
                  tab, tab],
        out_specs=[out, out, out],
        out_shape=[jax.ShapeDtypeStruct((r, width), BF16)] * 3,
        compiler_params=_cparams("parallel"),
        name="mla_prep",
    )(q, kv, kr, cos, sin)


def _topk_rows(s, iota, k):
    n = s.shape[0]
    row = lax.broadcasted_iota(jnp.int32, (k, s.shape[1]), 0)
    vals = jnp.zeros((k, s.shape[1]), F32)
    ids = jnp.zeros((k, s.shape[1]), F32)
    for r in range(k):
        m = jnp.max(s, axis=0, keepdims=True)
        am = jnp.min(jnp.where(s == m, iota, float(n)), axis=0, keepdims=True)
        vals = jnp.where(row == r, m, vals)
        ids = jnp.where(row == r, am, ids)
        s = jnp.where(iota == am, -jnp.inf, s)
    return vals, ids


def pack_rows_bf16(tab):
    e, d = tab.shape
    t16 = lax.bitcast_convert_type(tab.astype(BF16), jnp.uint16).astype(jnp.uint32)
    word = t16[:, :d // 2] | (t16[:, d // 2:] << 16)
    return lax.bitcast_convert_type(word, jnp.int32).reshape(e * EXPERT_ROWS, V7X_LANES)


def _slab_value_row(r):
    return r // 2 + EXPERT_ROWS * (r % 2)


PEER_GROUP = V7X_SUBLANES
SLAB_ROWS_BF16 = 2 * EXPERT_ROWS
PAIR_LANES = PEER_PAIRS * SLAB_ROWS_BF16


def _slab(tab_ref, row):
    return tab_ref[pl.ds(pl.multiple_of(row, EXPERT_ROWS), EXPERT_ROWS), :]


def _token_rows_bf16(h8, i):
    row = lax.broadcasted_iota(jnp.int32, (SLAB_ROWS_BF16, V7X_LANES), 0)
    hq = jnp.zeros((SLAB_ROWS_BF16, V7X_LANES), F32)
    for r in range(SLAB_ROWS_BF16):
        v = _slab_value_row(r)
        hq = jnp.where(row == r, h8[i:i + 1, v * V7X_LANES:(v + 1) * V7X_LANES], hq)
    return hq.astype(BF16)


def _peer_v_kernel(idx_ref, w_ref, x_ref, gate_ref, tab_ref, spread_ref, o_ref, g_ref):
    groups = w_ref.shape[0] // PEER_GROUP
    last_slot = PEER_GROUP - 1
    shape = (SLAB_ROWS_BF16, PAIR_LANES)
    out_row = lax.broadcasted_iota(jnp.int32, shape, 0)
    diag = (lax.broadcasted_iota(jnp.int32, shape, 1) % SLAB_ROWS_BF16) == 2 * (out_row % EXPERT_ROWS) + out_row // EXPERT_ROWS
    slot_rows = PEER_PAIRS * EXPERT_ROWS
    gate = gate_ref[0]

    @pl.when(pl.program_id(0) == 0)
    def _():
        g_ref[pl.ds(last_slot * slot_rows, slot_rows), :] = jnp.zeros((slot_rows, V7X_LANES), jnp.int32)

    row8 = lax.broadcasted_iota(jnp.int32, (PEER_GROUP, V7X_LANES), 0)
    blocks = range(SLAB_ROWS_BF16)

    def contract(slot, wrow, tile):
        rows = pltpu.bitcast(g_ref[pl.ds(slot * slot_rows, slot_rows), :], BF16)
        wi = jnp.where(diag, jnp.broadcast_to(wrow, shape), 0.0).astype(BF16)
        out = jnp.dot(wi, rows, preferred_element_type=F32)
        return [jnp.where(row8 == slot, jnp.broadcast_to(out[v:v + 1, :], row8.shape), tile[v]) for v in blocks]

    def flush(g, tile):
        rows = pl.ds(pl.multiple_of(g * PEER_GROUP, PEER_GROUP), PEER_GROUP)
        for v in blocks:
            lanes = slice(v * V7X_LANES, (v + 1) * V7X_LANES)
            o_ref[rows, lanes] = x_ref[rows, lanes] + gate[:, lanes] * tile[v]

    def group(g, carry):
        prev_wide, prev_tile = carry
        flush(jnp.maximum(g - 1, 0), contract(last_slot, prev_wide[last_slot:], list(prev_tile)))
        w8 = w_ref[pl.ds(pl.multiple_of(g * PEER_GROUP, PEER_GROUP), PEER_GROUP), :]
        wide = jnp.dot(w8.astype(BF16), spread_ref[...], preferred_element_type=F32)
        tile = [jnp.zeros(row8.shape, F32) for _ in blocks]
        for i in range(PEER_GROUP):
            tok_idx = idx_ref.at[pl.ds((g * PEER_GROUP + i) * PEER_PAIRS, PEER_PAIRS)]
            for j in range(PEER_PAIRS):
                g_ref[pl.ds((i * PEER_PAIRS + j) * EXPERT_ROWS, EXPERT_ROWS), :] = _slab(tab_ref, tok_idx[j])
            if i < last_slot:
                tile = contract(i, wide[i:i + 1], tile)
        return wide, tuple(tile)

    init = (jnp.zeros((PEER_GROUP, PAIR_LANES), F32), tuple(jnp.zeros(row8.shape, F32) for _ in blocks))
    wide, tile = lax.fori_loop(0, groups, group, init)
    flush(groups - 1, contract(last_slot, wide[last_slot:], list(tile)))


def peer_expert_mix(idx4, w, x, gate, tab, mod_index, tb=PEER_TILE):
    r, d = x.shape
    spread = jnp.repeat(jnp.eye(PEER_PAIRS, dtype=BF16), SLAB_ROWS_BF16, axis=1)
    return pl.pallas_call(
        _peer_v_kernel,
        grid=(r // tb,),
        in_specs=[
            pl.BlockSpec((tb * PEER_PAIRS,), lambda i: (i,), memory_space=pltpu.SMEM),
            pl.BlockSpec((tb, PEER_PAIRS), lambda i: (i, 0)),
            pl.BlockSpec((tb, d), lambda i: (i, 0)),
            pl.BlockSpec((1, 1, d), lambda i: (mod_index(i), 0, 0)),
            pl.BlockSpec(tab.shape, lambda i: (0, 0), pipeline_mode=pl.Buffered(1)),
            pl.BlockSpec((PEER_PAIRS, PAIR_LANES), lambda i: (0, 0)),
        ],
        out_specs=pl.BlockSpec((tb, d), lambda i: (i, 0)),
        out_shape=jax.ShapeDtypeStruct((r, d), F32),
        scratch_shapes=[pltpu.VMEM((PEER_GROUP * PEER_PAIRS * EXPERT_ROWS, V7X_LANES), jnp.int32)],
        compiler_params=_cparams("arbitrary"),
        name="peer_expert_mix",
    )(idx4, w, x, gate, tab, spread)


TOPK_B_BEFORE_TOKEN = 2
TOPK_C_BEFORE_TOKEN = 5


def _peer_front_kernel(x_ref, g_ref, sh_ref, sc_ref, wq_ref, keys_ref, tab_ref, fold_ref, idx_out_ref, w_out_ref,
                       hb_ref, e_scr, w_scr, h_buf, gw_buf, idx_vmem, idx_smem, s_ref, sem):
    step = pl.program_id(0)
    wslot = step % 2
    rslot = 1 - wslot
    tb = x_ref.shape[0]

    @pl.when(step == 0)
    def _():
        h_buf[1] = jnp.zeros(h_buf.shape[1:], F32)
        gw_buf[1] = jnp.zeros(gw_buf.shape[1:], F32)
        idx_vmem[...] = jnp.zeros(idx_vmem.shape, jnp.int32)
        fill = pltpu.make_async_copy(idx_vmem, idx_smem.at[1], sem)
        fill.start()
        fill.wait()

    h = _norm_mod(x_ref[...], g_ref[...], sh_ref[0], sc_ref[0])
    h_buf[wslot] = h
    hb_ref[...] = h.astype(BF16)
    iota_n = lax.broadcasted_iota(jnp.int32, (PEER_KEYS, tb), 0).astype(F32)
    half = PEER_TOPK // 2
    sub = lambda n: lax.broadcasted_iota(jnp.int32, (n, tb), 0).astype(F32)
    pos_c = jnp.concatenate([sub(PEER_TOPK)] + [sub(half) + float(a * PEER_TOPK) for a in range(1, half)]
                            + [(sub(half) + float(half)) * float(PEER_TOPK)], axis=0)

    def pair_up(first, second, scale):
        return jnp.concatenate([first[0:1] * scale + second]
                               + [first[a:a + 1] * scale + second[0:half] for a in range(1, half)]
                               + [first[half:] * scale + second[0:1]], axis=0)

    def retrieve_stages(hd):
        q = jnp.dot(hb_ref[...], wq_ref[hd], preferred_element_type=F32)
        tops = []

        def first_stage(p):
            qp = q[:, p * PEER_HALF:(p + 1) * PEER_HALF].astype(BF16)
            s = lax.dot_general(keys_ref[hd, p], qp, (((1,), (1,)), ((), ())),
                                preferred_element_type=F32)
            tops.append(_topk_rows(s, iota_n, PEER_TOPK))

        return [functools.partial(first_stage, 0), functools.partial(first_stage, 1),
                functools.partial(second_stage, hd, tops)]

    def second_stage(hd, tops):
        (s1, i1), (s2, i2) = tops
        cand = pair_up(s1, s2, 1.0)
        cidx = pair_up(i1, i2, float(PEER_KEYS))
        row = lax.broadcasted_iota(jnp.int32, (PEER_TOPK, tb), 0)
        sc = jnp.zeros((PEER_TOPK, tb), F32)
        ex = jnp.zeros((PEER_TOPK, tb), F32)
        for r in range(PEER_TOPK):
            m = jnp.max(cand, axis=0, keepdims=True)
            am = jnp.min(jnp.where(cand == m, pos_c, float(PEER_TOPK * PEER_TOPK)), axis=0, keepdims=True)
            hit = pos_c == am
            e = jnp.max(jnp.where(hit, cidx, 0.0), axis=0, keepdims=True)
            sc = jnp.where(row == r, m, sc)
            ex = jnp.where(row == r, e, ex)
            cand = jnp.where(hit, -jnp.inf, cand)
        pexp = jnp.exp(sc - sc[0:1])
        rows = pl.ds(pl.multiple_of(hd * PEER_TOPK, PEER_TOPK), PEER_TOPK)
        w_scr[rows, :] = pexp / jnp.sum(pexp, axis=0, keepdims=True)
        e_scr[rows, :] = ex

    row8 = lax.broadcasted_iota(jnp.int32, (PEER_GROUP, V7X_LANES), 0)
    h_prev, gw_prev, idx_prev = h_buf.at[rslot], gw_buf.at[rslot], idx_smem.at[rslot]

    def finish(g, sums):
        hi = sums.astype(BF16)
        lo = (sums - hi.astype(F32)).astype(BF16)
        a = (jnp.dot(hi, fold_ref[...], preferred_element_type=F32)
             + jnp.dot(lo, fold_ref[...], preferred_element_type=F32))
        rows = pl.ds(pl.multiple_of(g * PEER_GROUP, PEER_GROUP), PEER_GROUP)
        gelu = 0.5 * a * (1.0 + lax.erf(a * (2.0 ** -0.5)))
        w_out_ref[rows, :] = gelu * gw_prev[rows, :]

    def gate_group(g, prev, before_token):
        finish(jnp.maximum(g - 1, 0), prev)
        sums = jnp.zeros((PEER_GROUP, PAIR_LANES), F32)
        h8 = h_prev[pl.ds(pl.multiple_of(g * PEER_GROUP, PEER_GROUP), PEER_GROUP), :]
        for i in range(PEER_GROUP):
            if i in before_token:
                before_token[i]()
            hb = _token_rows_bf16(h8, i)
            tok_idx = idx_prev.at[g * PEER_GROUP + i]
            for j in range(PEER_PAIRS):
                u = pltpu.bitcast(_slab(tab_ref, tok_idx[j]), BF16)
                s_ref[pl.ds((i * PEER_PAIRS + j) * EXPERT_ROWS, EXPERT_ROWS), :] = pltpu.bitcast(u * hb, jnp.int32)
            prod = pltpu.bitcast(s_ref[pl.ds(i * PEER_PAIRS * EXPERT_ROWS, PEER_PAIRS * EXPERT_ROWS), :], BF16)
            pick = jnp.where(row8 == i, 1.0, 0.0).astype(BF16)
            sums = sums + lax.dot_general(pick, prod, (((1,), (1,)), ((), ())), preferred_element_type=F32)
        return sums

    groups_per_head = tb // PEER_GROUP // PEER_HEADS

    def body(hd, sums):
        first_a, first_b, second = retrieve_stages(hd)
        placement = [{0: first_a, TOPK_B_BEFORE_TOKEN: first_b, TOPK_C_BEFORE_TOKEN: second}]
        for k in range(groups_per_head):
            sums = gate_group(hd * groups_per_head + k, sums, placement[k] if k < len(placement) else {})
        return sums

    last = lax.fori_loop(0, PEER_HEADS, body, jnp.zeros((PEER_GROUP, PAIR_LANES), F32))
    finish(tb // PEER_GROUP - 1, last)
    gw_buf[wslot] = w_scr[...].T
    idx = (e_scr[...].T * float(EXPERT_ROWS)).astype(jnp.int32)
    idx_out_ref[...] = idx
    idx_vmem[...] = idx
    handoff = pltpu.make_async_copy(idx_vmem, idx_smem.at[wslot], sem)
    handoff.start()
    handoff.wait()


def peer_front(x, g, shift, scale, wq_heads, keys, tab, mod_index, tb=PEER_TILE):
    r, d = x.shape
    nb = r // tb
    fold = jnp.repeat(jnp.eye(PEER_PAIRS, dtype=BF16), SLAB_ROWS_BF16, axis=0)
    cur = lambda i: jnp.minimum(i, nb - 1)
    const = lambda shape: pl.BlockSpec(shape, lambda i: (0,) * len(shape), pipeline_mode=pl.Buffered(1))
    return pl.pallas_call(
        _peer_front_kernel,
        grid=(nb + 1,),
        in_specs=[
            pl.BlockSpec((tb, d), lambda i: (cur(i), 0)),
            pl.BlockSpec((1, d), lambda i: (0, 0)),
            pl.BlockSpec((1, 1, d), lambda i: (mod_index(cur(i)), 0, 0)),
            pl.BlockSpec((1, 1, d), lambda i: (mod_index(cur(i)), 0, 0)),
            const((PEER_HEADS, d, PEER_QDIM)),
            const((PEER_HEADS, 2, PEER_KEYS, PEER_HALF)),
            const(tab.shape),
            const((PAIR_LANES, PEER_PAIRS)),
        ],
        out_specs=[
            pl.BlockSpec((tb, PEER_PAIRS), lambda i: (cur(i), 0)),
            pl.BlockSpec((tb, PEER_PAIRS), lambda i: (jnp.maximum(i - 1, 0), 0)),
        ],
        out_shape=[jax.ShapeDtypeStruct((r, PEER_PAIRS), jnp.int32), jax.ShapeDtypeStruct((r, PEER_PAIRS), F32)],
        scratch_shapes=[
            pltpu.VMEM((tb, d), BF16), pltpu.VMEM((PEER_PAIRS, tb), F32), pltpu.VMEM((PEER_PAIRS, tb), F32),
            pltpu.VMEM((2, tb, d), F32), pltpu.VMEM((2, tb, PEER_PAIRS), F32),
            pltpu.VMEM((tb, PEER_PAIRS), jnp.int32), pltpu.SMEM((2, tb, PEER_PAIRS), jnp.int32),
            pltpu.VMEM((PEER_GROUP * PEER_PAIRS * EXPERT_ROWS, V7X_LANES), jnp.int32),
            pltpu.SemaphoreType.DMA,
        ],
        compiler_params=_cparams("arbitrary"),
        name="peer_front",
    )(x, g, shift, scale, wq_heads, keys, tab, fold)


def peer(x, g, shift, scale, gate, wq, keys, u_tab, v_tab, mod_index_peer):
    r, d = x.shape
    wq_heads = wq.astype(BF16).reshape(d, PEER_HEADS, PEER_QDIM).transpose(1, 0, 2)
    idx4, w = peer_front(x, g, shift, scale, wq_heads, keys.astype(BF16), pack_rows_bf16(u_tab), mod_index_peer)
    return peer_expert_mix(idx4.reshape(r * PEER_PAIRS), w, x, gate, pack_rows_bf16(v_tab), mod_index_peer)


HG_TILE = 256
HG_SUB = 16
HG_PAIR = 2 * HG_DK


def _hgrn_kernel(*refs, rev):
    blocks = HG_W // HG_PAIR
    q_refs, f_refs, v_refs = refs[:blocks], refs[blocks:2 * blocks], refs[2 * blocks:3 * blocks]
    lb_ref, o_ref, st_ref = refs[3 * blocks:]
    wide = lambda rs: jnp.concatenate([r[...] for r in rs], axis=1)

    @pl.when(pl.program_id(1) == 0)
    def _():
        st_ref[...] = jnp.zeros_like(st_ref)

    n = o_ref.shape[0]
    lb = lb_ref[...]
    f = lb + (1.0 - lb) * jax.nn.sigmoid(wide(f_refs))
    kk = 1.0 - f
    lf = jnp.log(f)
    t = lax.broadcasted_iota(jnp.int32, (n, n), 0)
    s = lax.broadcasted_iota(jnp.int32, (n, n), 1)
    same = (t // HG_SUB) == (s // HG_SUB)
    mid = (t // HG_SUB) * HG_SUB + (HG_SUB // 2 if rev else HG_SUB // 2 - 1)
    seen = (s >= t) if rev else (s <= t)
    seen_mid = (s >= mid) if rev else (s <= mid)
    one = lambda m: jnp.where(m, 1.0, 0.0).astype(F32)
    hp = lambda a, b: jnp.dot(a, b, precision=lax.Precision.HIGHEST, preferred_element_type=F32)
    cum = hp(one(same & seen), lf)
    ref = hp(one(same & seen_mid), lf)
    last = hp(one(same), lf)
    q = wide(q_refs)
    qe = (q * jnp.exp(cum)).astype(BF16)
    qm = (q * jnp.exp(cum - ref)).astype(BF16)
    km = (kk * jnp.exp(ref - cum)).astype(BF16)
    kd = (kk * jnp.exp(last - cum)).astype(BF16)
    dec = jnp.exp(last)
    vb = wide(v_refs).astype(BF16)
    steps = range(n // HG_SUB)
    heads = range(HG_HEADS)
    intra = []
    for h in heads:
        cols = slice(h * HG_DK, (h + 1) * HG_DK)
        att = lax.dot_general(qm[:, cols], km[:, cols], (((1,), (1,)), ((), ())), preferred_element_type=F32)
        att = jnp.where(same & seen, att, 0.0).astype(BF16)
        intra.append(jnp.dot(att, vb[:, cols], preferred_element_type=F32))
    for c in (reversed(steps) if rev else steps):
        rows = slice(c * HG_SUB, (c + 1) * HG_SUB)
        for h in heads:
            cols = slice(h * HG_DK, (h + 1) * HG_DK)
            st = st_ref[h]
            o_ref[rows, cols] = intra[h][rows] + lax.dot_general(
                qe[rows, cols], st.astype(BF16), (((1,), (1,)), ((), ())), preferred_element_type=F32)
            upd = lax.dot_general(vb[rows, cols], kd[rows, cols], (((0,), (0,)), ((), ())),
                                  preferred_element_type=F32)
            st_ref[h] = st * dec[c * HG_SUB:c * HG_SUB + 1, cols] + upd


def hgrn_scan(pa, lb, rev, batch):
    rows = pa.shape[0]
    per = rows // batch // HG_TILE
    nctx = CTX_LEN // HG_TILE
    col0 = GQA_IN // HG_PAIR
    blocks = HG_W // HG_PAIR

    def tok(b, k):
        if rev:
            k = jnp.where(k < nctx, nctx - 1 - k, per - 1 - (k - nctx))
        return b * per + k

    specs = lambda cb: [pl.BlockSpec((HG_TILE, HG_PAIR), lambda b, k, c=cb + p: (tok(b, k), c)) for p in range(blocks)]
    d = 1 if rev else 0
    return pl.pallas_call(
        functools.partial(_hgrn_kernel, rev=rev),
        grid=(batch, per),
        in_specs=specs(col0) + specs(col0 + (1 + d) * blocks) + specs(col0 + 3 * blocks)
        + [pl.BlockSpec((None, 1, HG_W), lambda b, k: (d, 0, 0))],
        out_specs=pl.BlockSpec((HG_TILE, HG_W), lambda b, k: (tok(b, k), 0)),
        out_shape=jax.ShapeDtypeStruct((rows, HG_W), F32),
        scratch_shapes=[pltpu.VMEM((HG_HEADS, HG_DV, HG_DK), F32)],
        compiler_params=_cparams("parallel", "arbitrary"),
        name="hgrn_scan_rev" if rev else "hgrn_scan_fwd",
    )(*([pa] * (3 * blocks)), lb)


def _rms_norm(x, g):
    xf = x.astype(F32)
    y = xf * lax.rsqrt(jnp.mean(xf * xf, axis=-1, keepdims=True) + EPS)
    return y * g.astype(F32)


def _hgrn2(pa2, batch, lb, onorm_g):
    lb = lb.reshape(2, 1, HG_W)
    o = hgrn_scan(pa2, lb, False, batch) + hgrn_scan(pa2, lb, True, batch)
    o = _rms_norm(o.reshape(-1, HG_HEADS, HG_DV), onorm_g).reshape(-1, HG_W)
    return o * jax.nn.silu(pa2[:, GQA_IN + 4 * HG_W:])


S5_CHUNK = 256
S5_LANES = S5_GROUPS * S5_STATE


def _s5_kernel(u_ref, bre_ref, bim_ref, cre_ref, cim_ref, are_ref, aim_ref, y_ref, xre, xim, hre, him):
    nb = V7X_SUBLANES // 2
    tiles = u_ref.shape[0] // V7X_SUBLANES
    direction = pl.program_id(0)
    rev = direction == 1

    @pl.when(pl.program_id(1) == 0)
    def _():
        hre[...] = jnp.zeros_like(hre)
        him[...] = jnp.zeros_like(him)

    ub = u_ref[...].astype(BF16)
    tile = V7X_MXU_DIM
    fan = (S5_LANES // tile) // (u_ref.shape[1] // tile)
    for nt in range(S5_LANES // tile):
        cols = slice(nt * tile, (nt + 1) * tile)
        src = ub[:, (nt // fan) * tile:(nt // fan + 1) * tile]
        xre[:, cols] = jnp.dot(src, bre_ref[nt], preferred_element_type=F32)
        xim[:, cols] = jnp.dot(src, bim_ref[nt], preferred_element_type=F32)
    shape = (V7X_SUBLANES, S5_LANES)
    second = (lax.broadcasted_iota(jnp.int32, shape, 0) // nb) != direction
    are = jnp.broadcast_to(are_ref[...], shape)
    aim = jnp.broadcast_to(aim_ref[...], shape)
    cre = jnp.where(second, are * are - aim * aim, are)
    cim = jnp.where(second, 2.0 * are * aim, aim)

    def step(k, carry):
        pr, pi = carry
        k = jnp.where(rev, tiles - 1 - k, k)
        r = pl.ds(pl.multiple_of(k * V7X_SUBLANES, V7X_SUBLANES), V7X_SUBLANES)
        xr, xi = xre[r, :], xim[r, :]
        sr = jnp.where(second, pltpu.roll(xr, nb, axis=0), 0.0)
        si = jnp.where(second, pltpu.roll(xi, nb, axis=0), 0.0)
        nr = xr + (are * sr - aim * si) + (cre * pr - cim * pi)
        ni = xi + (are * si + aim * sr) + (cre * pi + cim * pr)
        xre[r, :] = nr
        xim[r, :] = ni
        return (jnp.where(second, nr, pltpu.roll(nr, nb, axis=0)),
                jnp.where(second, ni, pltpu.roll(ni, nb, axis=0)))

    hr, hi = lax.fori_loop(0, tiles, step, (hre[...], him[...]))
    hre[...] = hr
    him[...] = hi
    for mt in range(u_ref.shape[1] // tile):
        acc = jnp.zeros((u_ref.shape[0], tile), F32)
        for kt in range(mt * fan, (mt + 1) * fan):
            cols = slice(kt * tile, (kt + 1) * tile)
            acc = acc + (jnp.dot(xre[:, cols].astype(BF16), cre_ref[kt], preferred_element_type=F32)
                         - jnp.dot(xim[:, cols].astype(BF16), cim_ref[kt], preferred_element_type=F32))
        y_ref[:, mt * tile:(mt + 1) * tile] = acc


def s5_scan(u, bre, bim, cre, cim, are, aim, nb):
    rows, w = u.shape
    blk = S5_CHUNK * nb
    assert rows % blk == 0 and 2 * nb == V7X_SUBLANES
    nblk, nctx = rows // blk, CTX_LEN // S5_CHUNK
    wspec = lambda shape: pl.BlockSpec((None,) + shape, lambda d, i: (d, 0, 0))
    tiles = pl.BlockSpec((None,) + bre.shape[1:], lambda d, i: (d, 0, 0, 0))

    def chunk(d, i):
        back = jnp.where(i < nctx, nctx - 1 - i, nblk - 1 - (i - nctx))
        return jnp.where(d == 1, back, i)
    return pl.pallas_call(
        _s5_kernel,
        grid=(2, rows // blk),
        in_specs=[
            pl.BlockSpec((blk, w), lambda d, i: (chunk(d, i), 0)),
            tiles, tiles, tiles, tiles,
            wspec((1, S5_LANES)), wspec((1, S5_LANES)),
        ],
        out_specs=pl.BlockSpec((None, blk, w), lambda d, i: (d, chunk(d, i), 0)),
        out_shape=jax.ShapeDtypeStruct((2, rows, w), F32),
        scratch_shapes=[pltpu.VMEM((blk, S5_LANES), F32), pltpu.VMEM((blk, S5_LANES), F32),
                        pltpu.VMEM((V7X_SUBLANES, S5_LANES), F32), pltpu.VMEM((V7X_SUBLANES, S5_LANES), F32)],
        compiler_params=_cparams("arbitrary", "arbitrary"),
        name="s5_scan",
    )(u, bre, bim, cre, cim, are, aim)


def _s5_core(u, a_re, a_im, log_dt, b_re, b_im, c_re, c_im, d_skip):
    b, s, w = u.shape
    dt = jnp.exp(log_dt)[..., None]
    mag = jnp.exp(a_re * dt)
    abar_re, abar_im = mag * jnp.cos(a_im * dt), mag * jnp.sin(a_im * dt)
    den = a_re * a_re + a_im * a_im
    k_re = ((abar_re - 1.0) * a_re + abar_im * a_im) / den
    k_im = (abar_im * a_re - (abar_re - 1.0) * a_im) / den
    bb_re = k_re[..., None] * b_re - k_im[..., None] * b_im
    bb_im = k_re[..., None] * b_im + k_im[..., None] * b_re
    eye = jnp.eye(S5_GROUPS, dtype=F32)
    tile, n_tiles = V7X_MXU_DIM, S5_LANES // V7X_MXU_DIM
    fan = n_tiles // (w // tile)
    cut = lambda m, r, c: m[:, r * tile:(r + 1) * tile, c * tile:(c + 1) * tile]

    def bd_in(m):
        full = jnp.einsum('dgpc,gh->dgchp', m, eye).reshape(2, w, S5_LANES).astype(BF16)
        return jnp.stack([cut(full, nt // fan, nt) for nt in range(n_tiles)], axis=1)

    def bd_out(m):
        full = jnp.einsum('dgcp,gh->dgphc', m, eye).reshape(2, S5_LANES, w).astype(BF16)
        return jnp.stack([cut(full, kt, kt // fan) for kt in range(n_tiles)], axis=1)
    y2 = s5_scan(u.transpose(1, 0, 2).reshape(s * b, w), bd_in(bb_re), bd_in(bb_im), bd_out(c_re), bd_out(c_im),
                 abar_re.reshape(2, 1, S5_LANES), abar_im.reshape(2, 1, S5_LANES), b)
    return d_skip * u + (y2[0] + y2[1]).reshape(s, b, w).transpose(1, 0, 2)


def _rope_tables(rows, rot_dim, lead, period):
    axis_dim = rot_dim // 2
    inv = ROPE_THETA ** (-jnp.arange(0, axis_dim, 2, dtype=F32) / axis_dim)
    t = jnp.arange(rows * GRID_W)
    r = (t // GRID_W).astype(F32)[:, None] * inv
    c = (t % GRID_W).astype(F32)[:, None] * inv
    n = t.shape[0]
    tail = period - lead - rot_dim
    cos = jnp.concatenate([jnp.ones((n, lead), F32), jnp.cos(r), jnp.cos(r), jnp.cos(c), jnp.cos(c),
                           jnp.ones((n, tail), F32)], axis=-1)
    sin = jnp.concatenate([jnp.zeros((n, lead), F32), -jnp.sin(r), jnp.sin(r), -jnp.sin(c), jnp.sin(c),
                           jnp.zeros((n, tail), F32)], axis=-1)
    reps = V7X_LANES // period
    cos = jnp.concatenate([jnp.ones((CTX_LEN, V7X_LANES), F32), jnp.tile(cos, (1, reps))], axis=0)
    sin = jnp.concatenate([jnp.zeros((CTX_LEN, V7X_LANES), F32), jnp.tile(sin, (1, reps))], axis=0)
    return cos, sin


def _attend_all(q, k, v, need_ctx):
    o_lat = attention(q[:, CTX_LEN:], k, v)
    if need_ctx:
        o_ctx = attention(q[:, :CTX_LEN], k[:, :CTX_LEN], v[:, :CTX_LEN])
    else:
        o_ctx = jnp.zeros((q.shape[0], CTX_LEN, q.shape[2]), F32)
    return jnp.concatenate([o_ctx, o_lat], axis=1)


def _gqa(pa2, b, qn_g, kn_g, cos, sin, need_ctx):
    s = pa2.shape[0] // b
    grp = GQA_Q_HEADS // GQA_KV_HEADS
    gains = jnp.concatenate([jnp.tile(qn_g, GQA_Q_HEADS), jnp.tile(kn_g, GQA_KV_HEADS)])[None, :]
    q, k, v = gqa_prep(pa2, gains, cos, sin, s)
    o = _attend_all(q.reshape(b, s, -1), k.reshape(b, s, -1), v.reshape(b, s, -1), need_ctx)
    o = o.reshape(b, s, GQA_Q_HEADS, V7X_LANES)
    first = (jnp.arange(GQA_Q_HEADS) < grp)[None, None, :, None]
    o = jnp.where(first, o[..., :HEAD_DIM], o[..., HEAD_DIM:])
    return o.reshape(b, s, GQA_Q_HEADS * HEAD_DIM)


def _mla(p, qa_g, w_qup, kva_g, w_kvup, cos, sin, need_ctx):
    b, s, _ = p.shape
    r0 = MLA_Q_RANK
    r1 = MLA_Q_RANK + MLA_KV_RANK
    blocks = lambda w, lo, hi: jnp.pad(w.reshape(w.shape[0], MLA_HEADS, -1)[..., lo:hi],
                                       ((0, 0), (0, 0), (0, V7X_LANES - (hi - lo)))).reshape(w.shape[0], -1)
    w_q = blocks(w_qup, 0, MLA_NOPE + MLA_ROPE).astype(BF16)
    w_kv = jnp.concatenate([blocks(w_kvup, 0, MLA_NOPE), blocks(w_kvup, MLA_NOPE, MLA_NOPE + MLA_V)],
                           axis=1).astype(BF16)
    zero = jnp.zeros((1, 1, r0), F32)
    q = norm_mod_matmul(p[..., :r0].reshape(b * s, r0), qa_g[None, :], zero, zero, w_q, lambda i: 0)
    zero = jnp.zeros((1, 1, MLA_KV_RANK), F32)
    kv = norm_mod_matmul(p[..., r0:r1].reshape(b * s, MLA_KV_RANK), kva_g[None, :], zero, zero, w_kv, lambda i: 0)
    kr = jnp.pad(p[..., r1:r1 + MLA_ROPE].reshape(b * s, MLA_ROPE),
                 ((0, 0), (MLA_NOPE, V7X_LANES - MLA_NOPE - MLA_ROPE)))
    q, k, v = mla_prep(q, kv, kr, cos, sin, s)
    o = _attend_all(q.reshape(b, s, -1), k.reshape(b, s, -1), v.reshape(b, s, -1), need_ctx)
    return o.reshape(b, s, MLA_HEADS, V7X_LANES)[..., :MLA_V].reshape(b, s, MLA_HEADS * MLA_V)


def kernel(x, c, ctx, c_ctx, ada_w, ada_b, norm1_g, norm2_g, ev_w_in, ev_w_out, gqa_qn_g, gqa_kn_g,
           hg_lb_logits, hg_onorm_g, od_w_in, od_w_out, mla_qa_g, mla_w_qup, mla_kva_g, mla_w_kvup,
           s5_a_re, s5_a_im, s5_log_dt, s5_b_re, s5_b_im, s5_c_re, s5_c_im, s5_d, s5_w_glu, s5_b_glu,
           peer_wq, peer_keys, peer_u, peer_v, final_g):
    b, t, d = x.shape
    s = CTX_LEN + t
    depth = ada_w.shape[0]
    rows = t // GRID_W
    cos_g, sin_g = _rope_tables(rows, HEAD_DIM, 0, HEAD_DIM)
    cos_m, sin_m = _rope_tables(rows, MLA_ROPE, MLA_NOPE, V7X_LANES)
    lb_all = jnp.cumsum(jax.nn.softmax(hg_lb_logits, axis=1), axis=1)

    def mod_index(tile):
        per, nctx = s // tile, CTX_LEN // tile
        return lambda i: jnp.where(i % per < nctx, b, i // per)

    mi_row = mod_index(ROW_TILE)
    mi_peer = mod_index(PEER_TILE)

    xa = jnp.concatenate([ctx, x], axis=1).reshape(b * s, d)
    s_all = jnp.concatenate([jax.nn.silu(c), jax.nn.silu(c_ctx)[None, :]], axis=0)
    s_pad = jnp.concatenate([s_all, jnp.zeros((V7X_SUBLANES - (b + 1) % V7X_SUBLANES, d), F32)], axis=0)

    for layer in range(depth):
        need_ctx = layer < depth - 1
        j = layer // 2
        mod = matmul(s_pad, ada_w[layer].astype(BF16), tm=s_pad.shape[0])[:b + 1] + ada_b[layer]
        mod = [m[:, None, :] for m in jnp.split(mod, 6, axis=-1)]
        if layer % 2 == 0:
            pa2 = norm_mod_matmul(xa, norm1_g[layer][None, :], mod[0], mod[1], ev_w_in[j].astype(BF16), mi_row)
            ya = _gqa(pa2, b, gqa_qn_g[j], gqa_kn_g[j], cos_g, sin_g, need_ctx)
            yb = _hgrn2(pa2, b, lb_all[:, j], hg_onorm_g[j]).reshape(b, s, HG_W)
            w_out = ev_w_out[j]
        else:
            pa = norm_mod_matmul(xa, norm1_g[layer][None, :], mod[0], mod[1], od_w_in[j].astype(BF16), mi_row)
            pa = pa.reshape(b, s, -1)
            ya = _mla(pa[..., :MLA_IN], mla_qa_g[j], mla_w_qup[j], mla_kva_g[j], mla_w_kvup[j], cos_m, sin_m, need_ctx)
            y5 = _s5_core(pa[..., MLA_IN:], s5_a_re[j], s5_a_im[j], s5_log_dt[j], s5_b_re[j], s5_b_im[j],
                          s5_c_re[j], s5_c_im[j], s5_d[j])
            z = jax.nn.gelu(y5, approximate=False).reshape(b * s, S5_WIDTH)
            gl = matmul(z, s5_w_glu[j].astype(BF16)) + s5_b_glu[j]
            yb = (z * jax.nn.sigmoid(gl)).reshape(b, s, S5_WIDTH)
            w_out = od_w_out[j]
        xa = matmul_residual(ya.reshape(b * s, -1), yb.reshape(b * s, -1), w_out.astype(BF16), xa, mod[2], mi_row)
        xa = peer(xa, norm2_g[layer][None, :], mod[3], mod[4], mod[5],
                  peer_wq[layer], peer_keys[layer], peer_u[layer], peer_v[layer], mi_peer)
    xl = xa.reshape(b, s, d)[:, CTX_LEN:].reshape(b * t, d)
    return rmsnorm_rows(xl, final_g[None, :]).reshape(b, t, d)
```

```python
import functools
import math

import jax
import jax.numpy as jnp
from jax import lax
from jax.experimental import pallas as pl
from jax.experimental.pallas import tpu as pltpu

F32 = jnp.float32
BF16 = jnp.bfloat16

D_MODEL = 1024
GRID_W = 64
CTX_LEN = 256
EPS = 1e-6
ROPE_THETA = 10000.0

MIX_HALF = D_MODEL // 2
HEAD_DIM = 64
GQA_Q_HEADS = MIX_HALF // HEAD_DIM
GQA_KV_HEADS = GQA_Q_HEADS // 4
GQA_IN = (GQA_Q_HEADS + 2 * GQA_KV_HEADS) * HEAD_DIM

HG_DK = 128
HG_DV = 128
HG_HEADS = MIX_HALF // HG_DV
HG_W = HG_HEADS * HG_DK

MLA_HEADS = MIX_HALF // HEAD_DIM
MLA_NOPE = 64
MLA_ROPE = 32
MLA_V = 64
MLA_Q_RANK = 384
MLA_KV_RANK = 256
MLA_IN = MLA_Q_RANK + MLA_KV_RANK + MLA_ROPE

S5_WIDTH = MIX_HALF
S5_GROUP = 16
S5_GROUPS = S5_WIDTH // S5_GROUP
S5_STATE = 64

PEER_HEADS = 8
PEER_KEYS = 128
N_EXPERTS = PEER_KEYS * PEER_KEYS
PEER_TOPK = 16
PEER_QDIM = 256
PEER_HALF = PEER_QDIM // 2
PEER_PAIRS = PEER_HEADS * PEER_TOPK

V7X_LANES = 128
V7X_SUBLANES = 8
V7X_VMEM_BYTES = 64 * 1024 * 1024
VMEM_LIMIT = V7X_VMEM_BYTES - 8 * 1024 * 1024

ROW_TILE = 256
PEER_TILE = 128
ATTN_Q_TILE = 1024
EXPERT_WORDS = D_MODEL // 2
EXPERT_ROWS = EXPERT_WORDS // V7X_LANES


def _cparams(*sem):
    return pltpu.CompilerParams(dimension_semantics=sem, vmem_limit_bytes=VMEM_LIMIT)


def _norm_mod(x, g, shift, scale):
    ms = jnp.mean(x * x, axis=-1, keepdims=True)
    h = (x * lax.rsqrt(ms + EPS)) * g
    return h * (1.0 + scale) + shift


def _nmm_kernel(x_ref, g_ref, sh_ref, sc_ref, w_ref, o_ref):
    h = _norm_mod(x_ref[...], g_ref[...], sh_ref[0], sc_ref[0])
    o_ref[...] = jnp.dot(h.astype(BF16), w_ref[...], preferred_element_type=F32)


def norm_mod_matmul(x, g, shift, scale, w, mod_index, tm=ROW_TILE):
    r, k = x.shape
    n = w.shape[1]
    assert r % tm == 0 and w.shape[0] == k
    return pl.pallas_call(
        _nmm_kernel,
        grid=(r // tm,),
        in_specs=[
            pl.BlockSpec((tm, k), lambda i: (i, 0)),
            pl.BlockSpec((1, k), lambda i: (0, 0)),
            pl.BlockSpec((1, 1, k), lambda i: (mod_index(i), 0, 0)),
            pl.BlockSpec((1, 1, k), lambda i: (mod_index(i), 0, 0)),
            pl.BlockSpec((k, n), lambda i: (0, 0)),
        ],
        out_specs=pl.BlockSpec((tm, n), lambda i: (i, 0)),
        out_shape=jax.ShapeDtypeStruct((r, n), F32),
        compiler_params=_cparams("parallel"),
        name="norm_mod_matmul",
    )(x, g, shift, scale, w)


def _mm_kernel(a_ref, w_ref, o_ref):
    o_ref[...] = jnp.dot(a_ref[...].astype(BF16), w_ref[...], preferred_element_type=F32)


def matmul(a, w, tm=ROW_TILE):
    r, k = a.shape
    n = w.shape[1]
    assert r % tm == 0
    return pl.pallas_call(
        _mm_kernel,
        grid=(r // tm,),
        in_specs=[pl.BlockSpec((tm, k), lambda i: (i, 0)), pl.BlockSpec((k, n), lambda i: (0, 0))],
        out_specs=pl.BlockSpec((tm, n), lambda i: (i, 0)),
        out_shape=jax.ShapeDtypeStruct((r, n), F32),
        compiler_params=_cparams("parallel"),
        name="matmul",
    )(a, w)


def _mmres_kernel(a_ref, b_ref, w_ref, x_ref, gate_ref, o_ref):
    ka = a_ref.shape[1]
    y = (jnp.dot(a_ref[...].astype(BF16), w_ref[:ka, :], preferred_element_type=F32)
         + jnp.dot(b_ref[...].astype(BF16), w_ref[ka:, :], preferred_element_type=F32))
    o_ref[...] = x_ref[...] + gate_ref[0] * y


def matmul_residual(a, b, w, x, gate, mod_index, tm=ROW_TILE):
    r, ka = a.shape
    kb = b.shape[1]
    k, n = w.shape
    assert r % tm == 0 and ka + kb == k
    return pl.pallas_call(
        _mmres_kernel,
        grid=(r // tm,),
        in_specs=[
            pl.BlockSpec((tm, ka), lambda i: (i, 0)),
            pl.BlockSpec((tm, kb), lambda i: (i, 0)),
            pl.BlockSpec((k, n), lambda i: (0, 0)),
            pl.BlockSpec((tm, n), lambda i: (i, 0)),
            pl.BlockSpec((1, 1, n), lambda i: (mod_index(i), 0, 0)),
        ],
        out_specs=pl.BlockSpec((tm, n), lambda i: (i, 0)),
        out_shape=jax.ShapeDtypeStruct((r, n), F32),
        compiler_params=_cparams("parallel"),
        name="matmul_residual",
    )(a, b, w, x, gate)


def _rms_kernel(x_ref, g_ref, o_ref):
    x = x_ref[...]
    ms = jnp.mean(x * x, axis=-1, keepdims=True)
    o_ref[...] = (x * lax.rsqrt(ms + EPS)) * g_ref[...]


def rmsnorm_rows(x, g, tm=ROW_TILE):
    r, k = x.shape
    return pl.pallas_call(
        _rms_kernel,
        grid=(r // tm,),
        in_specs=[pl.BlockSpec((tm, k), lambda i: (i, 0)), pl.BlockSpec((1, k), lambda i: (0, 0))],
        out_specs=pl.BlockSpec((tm, k), lambda i: (i, 0)),
        out_shape=jax.ShapeDtypeStruct((r, k), F32),
        compiler_params=_cparams("parallel"),
        name="rmsnorm_rows",
    )(x, g)


ATTN_KV_CHUNKS = 3
V7X_MXU_DIM = 256
LOG2E = math.log2(math.e)


def _attn_kernel(q_ref, k_ref, v_ref, o_ref):
    q = q_ref[...]
    sk = k_ref.shape[0]
    nch = ATTN_KV_CHUNKS if sk % (ATTN_KV_CHUNKS * V7X_MXU_DIM) == 0 else 1
    ck = sk // nch
    m = l = acc = None
    for c in range(nch):
        kc = k_ref[c * ck:(c + 1) * ck, :]
        vc = v_ref[c * ck:(c + 1) * ck, :]
        s = lax.dot_general(q, kc, (((1,), (1,)), ((), ())), preferred_element_type=F32)
        mc = jnp.max(s, axis=-1, keepdims=True)
        if c == 0:
            m = mc
            p = jnp.exp2(s - m)
            l = jnp.sum(p, axis=-1, keepdims=True)
            acc = jnp.dot(p.astype(BF16), vc, preferred_element_type=F32)
        else:
            m_new = jnp.maximum(m, mc)
            a = jnp.exp2(m - m_new)
            p = jnp.exp2(s - m_new)
            l = a * l + jnp.sum(p, axis=-1, keepdims=True)
            acc = a * acc + jnp.dot(p.astype(BF16), vc, preferred_element_type=F32)
            m = m_new
    o_ref[...] = acc / l


def attention(q, k, v, tq=ATTN_Q_TILE):
    b, sq, hl = q.shape
    sk, hkl = k.shape[1], k.shape[2]
    h, hk = hl // V7X_LANES, hkl // V7X_LANES
    grp = h // hk
    tq = min(tq, sq)
    assert sq % tq == 0
    return pl.pallas_call(
        _attn_kernel,
        grid=(b, h, sq // tq),
        in_specs=[
            pl.BlockSpec((None, tq, V7X_LANES), lambda bi, hi, qi: (bi, qi, hi)),
            pl.BlockSpec((None, sk, V7X_LANES), lambda bi, hi, qi: (bi, 0, hi // grp)),
            pl.BlockSpec((None, sk, V7X_LANES), lambda bi, hi, qi: (bi, 0, hi // grp)),
        ],
        out_specs=pl.BlockSpec((None, tq, V7X_LANES), lambda bi, hi, qi: (bi, qi, hi)),
        out_shape=jax.ShapeDtypeStruct((b, sq, hl), F32),
        compiler_params=_cparams("parallel", "parallel", "parallel"),
        name="attention",
    )(q, k, v)


def _rotate(x, cos, sin, quarter):
    n = x.shape[1]
    lane = lax.broadcasted_iota(jnp.int32, x.shape, 1)
    partner = jnp.where(lane % (2 * quarter) < quarter, pltpu.roll(x, n - quarter, axis=1),
                        pltpu.roll(x, quarter, axis=1))
    return x * cos + partner * sin


def _gqa_prep_kernel(p_ref, g_ref, cos_ref, sin_ref, avg_ref, q_ref, k_ref, v_ref):
    cos, sin = cos_ref[...], sin_ref[...]
    low = lax.broadcasted_iota(jnp.int32, cos.shape, 1) < HEAD_DIM
    qk_blocks = (GQA_Q_HEADS + GQA_KV_HEADS) * HEAD_DIM // V7X_LANES
    q_blocks = GQA_Q_HEADS * HEAD_DIM // V7X_LANES
    for c in range(qk_blocks):
        lanes = slice(c * V7X_LANES, (c + 1) * V7X_LANES)
        x = p_ref[:, lanes]
        ms = jnp.dot(x * x, avg_ref[...], precision=lax.Precision.HIGHEST, preferred_element_type=F32)
        y = _rotate(x * lax.rsqrt(ms + EPS) * g_ref[:, lanes], cos, sin, HEAD_DIM // 4)
        if c < q_blocks:
            y = y * (HEAD_DIM ** -0.5 * LOG2E)
            other = pltpu.roll(y, HEAD_DIM, axis=1)
            if c < q_blocks // 2:
                first, second = jnp.where(low, y, 0.0), jnp.where(low, other, 0.0)
            else:
                first, second = jnp.where(low, 0.0, other), jnp.where(low, 0.0, y)
            q_ref[:, 2 * c * V7X_LANES:(2 * c + 1) * V7X_LANES] = first.astype(BF16)
            q_ref[:, (2 * c + 1) * V7X_LANES:(2 * c + 2) * V7X_LANES] = second.astype(BF16)
        else:
            k_ref[...] = y.astype(BF16)
    v_ref[...] = p_ref[:, qk_blocks * V7X_LANES:(qk_blocks + 1) * V7X_LANES].astype(BF16)


def gqa_prep(pa, gains, cos, sin, seq, tm=ROW_TILE):
    r = pa.shape[0]
    per = seq // tm
    avg = jnp.kron(jnp.eye(V7X_LANES // HEAD_DIM, dtype=F32), jnp.full((HEAD_DIM, HEAD_DIM), 1.0 / HEAD_DIM, F32))
    tab = pl.BlockSpec((tm, V7X_LANES), lambda i: (i % per, 0))
    kv = pl.BlockSpec((tm, V7X_LANES), lambda i: (i, 0))
    return pl.pallas_call(
        _gqa_prep_kernel,
        grid=(r // tm,),
        in_specs=[pl.BlockSpec((tm, GQA_IN), lambda i: (i, 0)), pl.BlockSpec(gains.shape, lambda i: (0, 0)), tab, tab,
                  pl.BlockSpec(avg.shape, lambda i: (0, 0))],
        out_specs=[pl.BlockSpec((tm, GQA_Q_HEADS * V7X_LANES), lambda i: (i, 0)), kv, kv],
        out_shape=[jax.ShapeDtypeStruct((r, GQA_Q_HEADS * V7X_LANES), BF16),
                   jax.ShapeDtypeStruct((r, V7X_LANES), BF16), jax.ShapeDtypeStruct((r, V7X_LANES), BF16)],
        compiler_params=_cparams("parallel"),
        name="gqa_prep",
    )(pa, gains, cos, sin, avg)


def _mla_prep_kernel(q_ref, kv_ref, kr_ref, cos_ref, sin_ref, qo_ref, ko_ref, vo_ref):
    cos, sin = cos_ref[...], sin_ref[...]
    quarter = MLA_ROPE // 4
    kr = _rotate(kr_ref[...], cos, sin, quarter)
    scale = (MLA_NOPE + MLA_ROPE) ** -0.5 * LOG2E
    width = MLA_HEADS * V7X_LANES
    for h in range(MLA_HEADS):
        lanes = slice(h * V7X_LANES, (h + 1) * V7X_LANES)
        qo_ref[:, lanes] = (_rotate(q_ref[:, lanes], cos, sin, quarter) * scale).astype(BF16)
        ko_ref[:, lanes] = (kv_ref[:, lanes] + kr).astype(BF16)
        vo_ref[:, lanes] = kv_ref[:, width + h * V7X_LANES:width + (h + 1) * V7X_LANES].astype(BF16)


def mla_prep(q, kv, kr, cos, sin, seq, tm=ROW_TILE):
    r, width = q.shape
    per = seq // tm
    tab = pl.BlockSpec((tm, V7X_LANES), lambda i: (i % per, 0))
    out = pl.BlockSpec((tm, width), lambda i: (i, 0))
    return pl.pallas_call(
        _mla_prep_kernel,
        grid=(r // tm,),
        in_specs=[out, pl.BlockSpec((tm, 2 * width), lambda i: (i, 0)), pl.BlockSpec((tm, V7X_LANES), lambda i: (i, 0)),
                  tab, tab],
        out_specs=[out, out, out],
        out_shape=[jax.ShapeDtypeStruct((r, width), BF16)] * 3,
        compiler_params=_cparams("parallel"),
        name="mla_prep",
    )(q, kv, kr, cos, sin)


def _topk_rows(s, iota, k):
    n = s.shape[0]
    row = lax.broadcasted_iota(jnp.int32, (k, s.shape[1]), 0)
    vals = jnp.zeros((k, s.shape[1]), F32)
    ids = jnp.zeros((k, s.shape[1]), F32)
    for r in range(k):
        m = jnp.max(s, axis=0, keepdims=True)
        am = jnp.min(jnp.where(s == m, iota, float(n)), axis=0, keepdims=True)
        vals = jnp.where(row == r, m, vals)
        ids = jnp.where(row == r, am, ids)
        s = jnp.where(iota == am, -jnp.inf, s)
    return vals, ids


def pack_rows_bf16(tab):
    e, d = tab.shape
    t16 = lax.bitcast_convert_type(tab.astype(BF16), jnp.uint16).astype(jnp.uint32)
    word = t16[:, :d // 2] | (t16[:, d // 2:] << 16)
    return lax.bitcast_convert_type(word, jnp.int32).reshape(e * EXPERT_ROWS, V7X_LANES)


def _slab_value_row(r):
    return r // 2 + EXPERT_ROWS * (r % 2)


PEER_GROUP = V7X_SUBLANES
SLAB_ROWS_BF16 = 2 * EXPERT_ROWS
PAIR_LANES = PEER_PAIRS * SLAB_ROWS_BF16


def _slab(tab_ref, row):
    return tab_ref[pl.ds(pl.multiple_of(row, EXPERT_ROWS), EXPERT_ROWS), :]


def _token_rows_bf16(h8, i):
    row = lax.broadcasted_iota(jnp.int32, (SLAB_ROWS_BF16, V7X_LANES), 0)
    hq = jnp.zeros((SLAB_ROWS_BF16, V7X_LANES), F32)
    for r in range(SLAB_ROWS_BF16):
        v = _slab_value_row(r)
        hq = jnp.where(row == r, h8[i:i + 1, v * V7X_LANES:(v + 1) * V7X_LANES], hq)
    return hq.astype(BF16)


def _peer_v_kernel(idx_ref, w_ref, x_ref, gate_ref, tab_ref, spread_ref, o_ref, g_ref):
    groups = w_ref.shape[0] // PEER_GROUP
    last_slot = PEER_GROUP - 1
    shape = (SLAB_ROWS_BF16, PAIR_LANES)
    out_row = lax.broadcasted_iota(jnp.int32, shape, 0)
    diag = (lax.broadcasted_iota(jnp.int32, shape, 1) % SLAB_ROWS_BF16) == 2 * (out_row % EXPERT_ROWS) + out_row // EXPERT_ROWS
    slot_rows = PEER_PAIRS * EXPERT_ROWS
    gate = gate_ref[0]

    @pl.when(pl.program_id(0) == 0)
    def _():
        g_ref[pl.ds(last_slot * slot_rows, slot_rows), :] = jnp.zeros((slot_rows, V7X_LANES), jnp.int32)

    row8 = lax.broadcasted_iota(jnp.int32, (PEER_GROUP, V7X_LANES), 0)
    blocks = range(SLAB_ROWS_BF16)

    def contract(slot, wrow, tile):
        rows = pltpu.bitcast(g_ref[pl.ds(slot * slot_rows, slot_rows), :], BF16)
        wi = jnp.where(diag, jnp.broadcast_to(wrow, shape), 0.0).astype(BF16)
        out = jnp.dot(wi, rows, preferred_element_type=F32)
        return [jnp.where(row8 == slot, jnp.broadcast_to(out[v:v + 1, :], row8.shape), tile[v]) for v in blocks]

    def flush(g, tile):
        rows = pl.ds(pl.multiple_of(g * PEER_GROUP, PEER_GROUP), PEER_GROUP)
        for v in blocks:
            lanes = slice(v * V7X_LANES, (v + 1) * V7X_LANES)
            o_ref[rows, lanes] = x_ref[rows, lanes] + gate[:, lanes] * tile[v]

    def group(g, carry):
        w8 = w_ref[pl.ds(pl.multiple_of(g * PEER_GROUP, PEER_GROUP), PEER_GROUP), :]
        wide = jnp.dot(w8.astype(BF16), spread_ref[...], preferred_element_type=F32)
        tile = [jnp.zeros(row8.shape, F32) for _ in blocks]
        for i in range(PEER_GROUP):
            tok_idx = idx_ref.at[pl.ds((g * PEER_GROUP + i) * PEER_PAIRS, PEER_PAIRS)]
            rows = pltpu.bitcast(jnp.concatenate([_slab(tab_ref, tok_idx[j]) for j in range(PEER_PAIRS)], axis=0), BF16)
            wi = jnp.where(diag, jnp.broadcast_to(wide[i:i + 1], shape), 0.0).astype(BF16)
            out = jnp.dot(wi, rows, preferred_element_type=F32)
            tile = [jnp.where(row8 == i, jnp.broadcast_to(out[v:v + 1, :], row8.shape), tile[v]) for v in blocks]
        flush(g, tile)
        return carry

    lax.fori_loop(0, groups, group, 0)


def peer_expert_mix(idx4, w, x, gate, tab, mod_index, tb=PEER_TILE):
    r, d = x.shape
    spread = jnp.repeat(jnp.eye(PEER_PAIRS, dtype=BF16), SLAB_ROWS_BF16, axis=1)
    return pl.pallas_call(
        _peer_v_kernel,
        grid=(r // tb,),
        in_specs=[
            pl.BlockSpec((tb * PEER_PAIRS,), lambda i: (i,), memory_space=pltpu.SMEM),
            pl.BlockSpec((tb, PEER_PAIRS), lambda i: (i, 0)),
            pl.BlockSpec((tb, d), lambda i: (i, 0)),
            pl.BlockSpec((1, 1, d), lambda i: (mod_index(i), 0, 0)),
            pl.BlockSpec(tab.shape, lambda i: (0, 0), pipeline_mode=pl.Buffered(1)),
            pl.BlockSpec((PEER_PAIRS, PAIR_LANES), lambda i: (0, 0)),
        ],
        out_specs=pl.BlockSpec((tb, d), lambda i: (i, 0)),
        out_shape=jax.ShapeDtypeStruct((r, d), F32),
        scratch_shapes=[pltpu.VMEM((PEER_GROUP * PEER_PAIRS * EXPERT_ROWS, V7X_LANES), jnp.int32)],
        compiler_params=_cparams("arbitrary"),
        name="peer_expert_mix",
    )(idx4, w, x, gate, tab, spread)


TOPK_B_BEFORE_TOKEN = 2
TOPK_C_BEFORE_TOKEN = 5


def _peer_front_kernel(x_ref, g_ref, sh_ref, sc_ref, wq_ref, keys_ref, tab_ref, fold_ref, idx_out_ref, w_out_ref,
                       hb_ref, e_scr, w_scr, h_buf, gw_buf, idx_vmem, idx_smem, s_ref, sem):
    step = pl.program_id(0)
    wslot = step % 2
    rslot = 1 - wslot
    tb = x_ref.shape[0]

    @pl.when(step == 0)
    def _():
        h_buf[1] = jnp.zeros(h_buf.shape[1:], F32)
        gw_buf[1] = jnp.zeros(gw_buf.shape[1:], F32)
        idx_vmem[...] = jnp.zeros(idx_vmem.shape, jnp.int32)
        fill = pltpu.make_async_copy(idx_vmem, idx_smem.at[1], sem)
        fill.start()
        fill.wait()

    h = _norm_mod(x_ref[...], g_ref[...], sh_ref[0], sc_ref[0])
    h_buf[wslot] = h
    hb_ref[...] = h.astype(BF16)
    iota_n = lax.broadcasted_iota(jnp.int32, (PEER_KEYS, tb), 0).astype(F32)
    half = PEER_TOPK // 2
    sub = lambda n: lax.broadcasted_iota(jnp.int32, (n, tb), 0).astype(F32)
    pos_c = jnp.concatenate([sub(PEER_TOPK)] + [sub(half) + float(a * PEER_TOPK) for a in range(1, half)]
                            + [(sub(half) + float(half)) * float(PEER_TOPK)], axis=0)

    def pair_up(first, second, scale):
        return jnp.concatenate([first[0:1] * scale + second]
                               + [first[a:a + 1] * scale + second[0:half] for a in range(1, half)]
                               + [first[half:] * scale + second[0:1]], axis=0)

    def retrieve_stages(hd):
        q = jnp.dot(hb_ref[...], wq_ref[hd], preferred_element_type=F32)
        tops = []

        def first_stage(p):
            qp = q[:, p * PEER_HALF:(p + 1) * PEER_HALF].astype(BF16)
            s = lax.dot_general(keys_ref[hd, p], qp, (((1,), (1,)), ((), ())),
                                preferred_element_type=F32)
            tops.append(_topk_rows(s, iota_n, PEER_TOPK))

        return [functools.partial(first_stage, 0), functools.partial(first_stage, 1),
                functools.partial(second_stage, hd, tops)]

    def second_stage(hd, tops):
        (s1, i1), (s2, i2) = tops
        cand = pair_up(s1, s2, 1.0)
        cidx = pair_up(i1, i2, float(PEER_KEYS))
        row = lax.broadcasted_iota(jnp.int32, (PEER_TOPK, tb), 0)
        sc = jnp.zeros((PEER_TOPK, tb), F32)
        ex = jnp.zeros((PEER_TOPK, tb), F32)
        for r in range(PEER_TOPK):
            m = jnp.max(cand, axis=0, keepdims=True)
            am = jnp.min(jnp.where(cand == m, pos_c, float(PEER_TOPK * PEER_TOPK)), axis=0, keepdims=True)
            hit = pos_c == am
            e = jnp.max(jnp.where(hit, cidx, 0.0), axis=0, keepdims=True)
            sc = jnp.where(row == r, m, sc)
            ex = jnp.where(row == r, e, ex)
            cand = jnp.where(hit, -jnp.inf, cand)
        pexp = jnp.exp(sc - sc[0:1])
        rows = pl.ds(pl.multiple_of(hd * PEER_TOPK, PEER_TOPK), PEER_TOPK)
        w_scr[rows, :] = pexp / jnp.sum(pexp, axis=0, keepdims=True)
        e_scr[rows, :] = ex

    row8 = lax.broadcasted_iota(jnp.int32, (PEER_GROUP, V7X_LANES), 0)
    h_prev, gw_prev, idx_prev = h_buf.at[rslot], gw_buf.at[rslot], idx_smem.at[rslot]

    def finish(g, sums):
        hi = sums.astype(BF16)
        lo = (sums - hi.astype(F32)).astype(BF16)
        a = (jnp.dot(hi, fold_ref[...], preferred_element_type=F32)
             + jnp.dot(lo, fold_ref[...], preferred_element_type=F32))
        rows = pl.ds(pl.multiple_of(g * PEER_GROUP, PEER_GROUP), PEER_GROUP)
        gelu = 0.5 * a * (1.0 + lax.erf(a * (2.0 ** -0.5)))
        w_out_ref[rows, :] = gelu * gw_prev[rows, :]

    def gate_group(g, prev, before_token):
        finish(jnp.maximum(g - 1, 0), prev)
        sums = jnp.zeros((PEER_GROUP, PAIR_LANES), F32)
        h8 = h_prev[pl.ds(pl.multiple_of(g * PEER_GROUP, PEER_GROUP), PEER_GROUP), :]
        for i in range(PEER_GROUP):
            if i in before_token:
                before_token[i]()
            hb = _token_rows_bf16(h8, i)
            tok_idx = idx_prev.at[g * PEER_GROUP + i]
            for j in range(PEER_PAIRS):
                u = pltpu.bitcast(_slab(tab_ref, tok_idx[j]), BF16)
                s_ref[pl.ds((i * PEER_PAIRS + j) * EXPERT_ROWS, EXPERT_ROWS), :] = pltpu.bitcast(u * hb, jnp.int32)
            prod = pltpu.bitcast(s_ref[pl.ds(i * PEER_PAIRS * EXPERT_ROWS, PEER_PAIRS * EXPERT_ROWS), :], BF16)
            pick = jnp.where(row8 == i, 1.0, 0.0).astype(BF16)
            sums = sums + lax.dot_general(pick, prod, (((1,), (1,)), ((), ())), preferred_element_type=F32)
        return sums

    groups_per_head = tb // PEER_GROUP // PEER_HEADS

    def body(hd, sums):
        first_a, first_b, second = retrieve_stages(hd)
        placement = [{0: first_a, TOPK_B_BEFORE_TOKEN: first_b, TOPK_C_BEFORE_TOKEN: second}]
        for k in range(groups_per_head):
            sums = gate_group(hd * groups_per_head + k, sums, placement[k] if k < len(placement) else {})
        return sums

    last = lax.fori_loop(0, PEER_HEADS, body, jnp.zeros((PEER_GROUP, PAIR_LANES), F32))
    finish(tb // PEER_GROUP - 1, last)
    gw_buf[wslot] = w_scr[...].T
    idx = (e_scr[...].T * float(EXPERT_ROWS)).astype(jnp.int32)
    idx_out_ref[...] = idx
    idx_vmem[...] = idx
    handoff = pltpu.make_async_copy(idx_vmem, idx_smem.at[wslot], sem)
    handoff.start()
    handoff.wait()


def peer_front(x, g, shift, scale, wq_heads, keys, tab, mod_index, tb=PEER_TILE):
    r, d = x.shape
    nb = r // tb
    fold = jnp.repeat(jnp.eye(PEER_PAIRS, dtype=BF16), SLAB_ROWS_BF16, axis=0)
    cur = lambda i: jnp.minimum(i, nb - 1)
    const = lambda shape: pl.BlockSpec(shape, lambda i: (0,) * len(shape), pipeline_mode=pl.Buffered(1))
    return pl.pallas_call(
        _peer_front_kernel,
        grid=(nb + 1,),
        in_specs=[
            pl.BlockSpec((tb, d), lambda i: (cur(i), 0)),
            pl.BlockSpec((1, d), lambda i: (0, 0)),
            pl.BlockSpec((1, 1, d), lambda i: (mod_index(cur(i)), 0, 0)),
            pl.BlockSpec((1, 1, d), lambda i: (mod_index(cur(i)), 0, 0)),
            const((PEER_HEADS, d, PEER_QDIM)),
            const((PEER_HEADS, 2, PEER_KEYS, PEER_HALF)),
            const(tab.shape),
            const((PAIR_LANES, PEER_PAIRS)),
        ],
        out_specs=[
            pl.BlockSpec((tb, PEER_PAIRS), lambda i: (cur(i), 0)),
            pl.BlockSpec((tb, PEER_PAIRS), lambda i: (jnp.maximum(i - 1, 0), 0)),
        ],
        out_shape=[jax.ShapeDtypeStruct((r, PEER_PAIRS), jnp.int32), jax.ShapeDtypeStruct((r, PEER_PAIRS), F32)],
        scratch_shapes=[
            pltpu.VMEM((tb, d), BF16), pltpu.VMEM((PEER_PAIRS, tb), F32), pltpu.VMEM((PEER_PAIRS, tb), F32),
            pltpu.VMEM((2, tb, d), F32), pltpu.VMEM((2, tb, PEER_PAIRS), F32),
            pltpu.VMEM((tb, PEER_PAIRS), jnp.int32), pltpu.SMEM((2, tb, PEER_PAIRS), jnp.int32),
            pltpu.VMEM((PEER_GROUP * PEER_PAIRS * EXPERT_ROWS, V7X_LANES), jnp.int32),
            pltpu.SemaphoreType.DMA,
        ],
        compiler_params=_cparams("arbitrary"),
        name="peer_front",
    )(x, g, shift, scale, wq_heads, keys, tab, fold)


def peer(x, g, shift, scale, gate, wq, keys, u_tab, v_tab, mod_index_peer):
    r, d = x.shape
    wq_heads = wq.astype(BF16).reshape(d, PEER_HEADS, PEER_QDIM).transpose(1, 0, 2)
    idx4, w = peer_front(x, g, shift, scale, wq_heads, keys.astype(BF16), pack_rows_bf16(u_tab), mod_index_peer)
    return peer_expert_mix(idx4.reshape(r * PEER_PAIRS), w, x, gate, pack_rows_bf16(v_tab), mod_index_peer)


HG_TILE = 256
HG_SUB = 16
HG_PAIR = 2 * HG_DK


def _hgrn_kernel(*refs, rev):
    blocks = HG_W // HG_PAIR
    q_refs, f_refs, v_refs = refs[:blocks], refs[blocks:2 * blocks], refs[2 * blocks:3 * blocks]
    lb_ref, o_ref, st_ref = refs[3 * blocks:]
    wide = lambda rs: jnp.concatenate([r[...] for r in rs], axis=1)

    @pl.when(pl.program_id(1) == 0)
    def _():
        st_ref[...] = jnp.zeros_like(st_ref)

    n = o_ref.shape[0]
    lb = lb_ref[...]
    f = lb + (1.0 - lb) * jax.nn.sigmoid(wide(f_refs))
    kk = 1.0 - f
    lf = jnp.log(f)
    t = lax.broadcasted_iota(jnp.int32, (n, n), 0)
    s = lax.broadcasted_iota(jnp.int32, (n, n), 1)
    same = (t // HG_SUB) == (s // HG_SUB)
    mid = (t // HG_SUB) * HG_SUB + (HG_SUB // 2 if rev else HG_SUB // 2 - 1)
    seen = (s >= t) if rev else (s <= t)
    seen_mid = (s >= mid) if rev else (s <= mid)
    one = lambda m: jnp.where(m, 1.0, 0.0).astype(F32)
    hp = lambda a, b: jnp.dot(a, b, precision=lax.Precision.HIGHEST, preferred_element_type=F32)
    cum = hp(one(same & seen), lf)
    ref = hp(one(same & seen_mid), lf)
    last = hp(one(same), lf)
    q = wide(q_refs)
    qe = (q * jnp.exp(cum)).astype(BF16)
    qm = (q * jnp.exp(cum - ref)).astype(BF16)
    km = (kk * jnp.exp(ref - cum)).astype(BF16)
    kd = (kk * jnp.exp(last - cum)).astype(BF16)
    dec = jnp.exp(last)
    vb = wide(v_refs).astype(BF16)
    steps = range(n // HG_SUB)
    heads = range(HG_HEADS)
    intra = []
    for h in heads:
        cols = slice(h * HG_DK, (h + 1) * HG_DK)
        att = lax.dot_general(qm[:, cols], km[:, cols], (((1,), (1,)), ((), ())), preferred_element_type=F32)
        att = jnp.where(same & seen, att, 0.0).astype(BF16)
        intra.append(jnp.dot(att, vb[:, cols], preferred_element_type=F32))
    for c in (reversed(steps) if rev else steps):
        rows = slice(c * HG_SUB, (c + 1) * HG_SUB)
        for h in heads:
            cols = slice(h * HG_DK, (h + 1) * HG_DK)
            st = st_ref[h]
            o_ref[rows, cols] = intra[h][rows] + lax.dot_general(
                qe[rows, cols], st.astype(BF16), (((1,), (1,)), ((), ())), preferred_element_type=F32)
            upd = lax.dot_general(vb[rows, cols], kd[rows, cols], (((0,), (0,)), ((), ())),
                                  preferred_element_type=F32)
            st_ref[h] = st * dec[c * HG_SUB:c * HG_SUB + 1, cols] + upd


def hgrn_scan(pa, lb, rev, batch):
    rows = pa.shape[0]
    per = rows // batch // HG_TILE
    nctx = CTX_LEN // HG_TILE
    col0 = GQA_IN // HG_PAIR
    blocks = HG_W // HG_PAIR

    def tok(b, k):
        if rev:
            k = jnp.where(k < nctx, nctx - 1 - k, per - 1 - (k - nctx))
        return b * per + k

    specs = lambda cb: [pl.BlockSpec((HG_TILE, HG_PAIR), lambda b, k, c=cb + p: (tok(b, k), c)) for p in range(blocks)]
    d = 1 if rev else 0
    return pl.pallas_call(
        functools.partial(_hgrn_kernel, rev=rev),
        grid=(batch, per),
        in_specs=specs(col0) + specs(col0 + (1 + d) * blocks) + specs(col0 + 3 * blocks)
        + [pl.BlockSpec((None, 1, HG_W), lambda b, k: (d, 0, 0))],
        out_specs=pl.BlockSpec((HG_TILE, HG_W), lambda b, k: (tok(b, k), 0)),
        out_shape=jax.ShapeDtypeStruct((rows, HG_W), F32),
        scratch_shapes=[pltpu.VMEM((HG_HEADS, HG_DV, HG_DK), F32)],
        compiler_params=_cparams("parallel", "arbitrary"),
        name="hgrn_scan_rev" if rev else "hgrn_scan_fwd",
    )(*([pa] * (3 * blocks)), lb)


def _rms_norm(x, g):
    xf = x.astype(F32)
    y = xf * lax.rsqrt(jnp.mean(xf * xf, axis=-1, keepdims=True) + EPS)
    return y * g.astype(F32)


def _hgrn2(pa2, batch, lb, onorm_g):
    lb = lb.reshape(2, 1, HG_W)
    o = hgrn_scan(pa2, lb, False, batch) + hgrn_scan(pa2, lb, True, batch)
    o = _rms_norm(o.reshape(-1, HG_HEADS, HG_DV), onorm_g).reshape(-1, HG_W)
    return o * jax.nn.silu(pa2[:, GQA_IN + 4 * HG_W:])


S5_CHUNK = 256
S5_LANES = S5_GROUPS * S5_STATE


def _s5_kernel(u_ref, bre_ref, bim_ref, cre_ref, cim_ref, are_ref, aim_ref, y_ref, xre, xim, hre, him):
    nb = V7X_SUBLANES // 2
    tiles = u_ref.shape[0] // V7X_SUBLANES
    direction = pl.program_id(0)
    rev = direction == 1

    @pl.when(pl.program_id(1) == 0)
    def _():
        hre[...] = jnp.zeros_like(hre)
        him[...] = jnp.zeros_like(him)

    ub = u_ref[...].astype(BF16)
    tile = V7X_MXU_DIM
    fan = (S5_LANES // tile) // (u_ref.shape[1] // tile)
    for nt in range(S5_LANES // tile):
        cols = slice(nt * tile, (nt + 1) * tile)
        src = ub[:, (nt // fan) * tile:(nt // fan + 1) * tile]
        xre[:, cols] = jnp.dot(src, bre_ref[nt], preferred_element_type=F32)
        xim[:, cols] = jnp.dot(src, bim_ref[nt], preferred_element_type=F32)
    shape = (V7X_SUBLANES, S5_LANES)
    second = (lax.broadcasted_iota(jnp.int32, shape, 0) // nb) != direction
    are = jnp.broadcast_to(are_ref[...], shape)
    aim = jnp.broadcast_to(aim_ref[...], shape)
    cre = jnp.where(second, are * are - aim * aim, are)
    cim = jnp.where(second, 2.0 * are * aim, aim)

    def step(k, carry):
        pr, pi = carry
        k = jnp.where(rev, tiles - 1 - k, k)
        r = pl.ds(pl.multiple_of(k * V7X_SUBLANES, V7X_SUBLANES), V7X_SUBLANES)
        xr, xi = xre[r, :], xim[r, :]
        sr = jnp.where(second, pltpu.roll(xr, nb, axis=0), 0.0)
        si = jnp.where(second, pltpu.roll(xi, nb, axis=0), 0.0)
        nr = xr + (are * sr - aim * si) + (cre * pr - cim * pi)
        ni = xi + (are * si + aim * sr) + (cre * pi + cim * pr)
        xre[r, :] = nr
        xim[r, :] = ni
        return (jnp.where(second, nr, pltpu.roll(nr, nb, axis=0)),
                jnp.where(second, ni, pltpu.roll(ni, nb, axis=0)))

    hr, hi = lax.fori_loop(0, tiles, step, (hre[...], him[...]))
    hre[...] = hr
    him[...] = hi
    for mt in range(u_ref.shape[1] // tile):
        acc = jnp.zeros((u_ref.shape[0], tile), F32)
        for kt in range(mt * fan, (mt + 1) * fan):
            cols = slice(kt * tile, (kt + 1) * tile)
            acc = acc + (jnp.dot(xre[:, cols].astype(BF16), cre_ref[kt], preferred_element_type=F32)
                         - jnp.dot(xim[:, cols].astype(BF16), cim_ref[kt], preferred_element_type=F32))
        y_ref[:, mt * tile:(mt + 1) * tile] = acc


def s5_scan(u, bre, bim, cre, cim, are, aim, nb):
    rows, w = u.shape
    blk = S5_CHUNK * nb
    assert rows % blk == 0 and 2 * nb == V7X_SUBLANES
    nblk, nctx = rows // blk, CTX_LEN // S5_CHUNK
    wspec = lambda shape: pl.BlockSpec((None,) + shape, lambda d, i: (d, 0, 0))
    tiles = pl.BlockSpec((None,) + bre.shape[1:], lambda d, i: (d, 0, 0, 0))

    def chunk(d, i):
        back = jnp.where(i < nctx, nctx - 1 - i, nblk - 1 - (i - nctx))
        return jnp.where(d == 1, back, i)
    return pl.pallas_call(
        _s5_kernel,
        grid=(2, rows // blk),
        in_specs=[
            pl.BlockSpec((blk, w), lambda d, i: (chunk(d, i), 0)),
            tiles, tiles, tiles, tiles,
            wspec((1, S5_LANES)), wspec((1, S5_LANES)),
        ],
        out_specs=pl.BlockSpec((None, blk, w), lambda d, i: (d, chunk(d, i), 0)),
        out_shape=jax.ShapeDtypeStruct((2, rows, w), F32),
        scratch_shapes=[pltpu.VMEM((blk, S5_LANES), F32), pltpu.VMEM((blk, S5_LANES), F32),
                        pltpu.VMEM((V7X_SUBLANES, S5_LANES), F32), pltpu.VMEM((V7X_SUBLANES, S5_LANES), F32)],
        compiler_params=_cparams("arbitrary", "arbitrary"),
        name="s5_scan",
    )(u, bre, bim, cre, cim, are, aim)


def _s5_core(u, a_re, a_im, log_dt, b_re, b_im, c_re, c_im, d_skip):
    b, s, w = u.shape
    dt = jnp.exp(log_dt)[..., None]
    mag = jnp.exp(a_re * dt)
    abar_re, abar_im = mag * jnp.cos(a_im * dt), mag * jnp.sin(a_im * dt)
    den = a_re * a_re + a_im * a_im
    k_re = ((abar_re - 1.0) * a_re + abar_im * a_im) / den
    k_im = (abar_im * a_re - (abar_re - 1.0) * a_im) / den
    bb_re = k_re[..., None] * b_re - k_im[..., None] * b_im
    bb_im = k_re[..., None] * b_im + k_im[..., None] * b_re
    eye = jnp.eye(S5_GROUPS, dtype=F32)
    tile, n_tiles = V7X_MXU_DIM, S5_LANES // V7X_MXU_DIM
    fan = n_tiles // (w // tile)
    cut = lambda m, r, c: m[:, r * tile:(r + 1) * tile, c * tile:(c + 1) * tile]

    def bd_in(m):
        full = jnp.einsum('dgpc,gh->dgchp', m, eye).reshape(2, w, S5_LANES).astype(BF16)
        return jnp.stack([cut(full, nt // fan, nt) for nt in range(n_tiles)], axis=1)

    def bd_out(m):
        full = jnp.einsum('dgcp,gh->dgphc', m, eye).reshape(2, S5_LANES, w).astype(BF16)
        return jnp.stack([cut(full, kt, kt // fan) for kt in range(n_tiles)], axis=1)
    y2 = s5_scan(u.transpose(1, 0, 2).reshape(s * b, w), bd_in(bb_re), bd_in(bb_im), bd_out(c_re), bd_out(c_im),
                 abar_re.reshape(2, 1, S5_LANES), abar_im.reshape(2, 1, S5_LANES), b)
    return d_skip * u + (y2[0] + y2[1]).reshape(s, b, w).transpose(1, 0, 2)


def _rope_tables(rows, rot_dim, lead, period):
    axis_dim = rot_dim // 2
    inv = ROPE_THETA ** (-jnp.arange(0, axis_dim, 2, dtype=F32) / axis_dim)
    t = jnp.arange(rows * GRID_W)
    r = (t // GRID_W).astype(F32)[:, None] * inv
    c = (t % GRID_W).astype(F32)[:, None] * inv
    n = t.shape[0]
    tail = period - lead - rot_dim
    cos = jnp.concatenate([jnp.ones((n, lead), F32), jnp.cos(r), jnp.cos(r), jnp.cos(c), jnp.cos(c),
                           jnp.ones((n, tail), F32)], axis=-1)
    sin = jnp.concatenate([jnp.zeros((n, lead), F32), -jnp.sin(r), jnp.sin(r), -jnp.sin(c), jnp.sin(c),
                           jnp.zeros((n, tail), F32)], axis=-1)
    reps = V7X_LANES // period
    cos = jnp.concatenate([jnp.ones((CTX_LEN, V7X_LANES), F32), jnp.tile(cos, (1, reps))], axis=0)
    sin = jnp.concatenate([jnp.zeros((CTX_LEN, V7X_LANES), F32), jnp.tile(sin, (1, reps))], axis=0)
    return cos, sin


def _attend_all(q, k, v, need_ctx):
    o_lat = attention(q[:, CTX_LEN:], k, v)
    if need_ctx:
        o_ctx = attention(q[:, :CTX_LEN], k[:, :CTX_LEN], v[:, :CTX_LEN])
    else:
        o_ctx = jnp.zeros((q.shape[0], CTX_LEN, q.shape[2]), F32)
    return jnp.concatenate([o_ctx, o_lat], axis=1)


def _gqa(pa2, b, qn_g, kn_g, cos, sin, need_ctx):
    s = pa2.shape[0] // b
    grp = GQA_Q_HEADS // GQA_KV_HEADS
    gains = jnp.concatenate([jnp.tile(qn_g, GQA_Q_HEADS), jnp.tile(kn_g, GQA_KV_HEADS)])[None, :]
    q, k, v = gqa_prep(pa2, gains, cos, sin, s)
    o = _attend_all(q.reshape(b, s, -1), k.reshape(b, s, -1), v.reshape(b, s, -1), need_ctx)
    o = o.reshape(b, s, GQA_Q_HEADS, V7X_LANES)
    first = (jnp.arange(GQA_Q_HEADS) < grp)[None, None, :, None]
    o = jnp.where(first, o[..., :HEAD_DIM], o[..., HEAD_DIM:])
    return o.reshape(b, s, GQA_Q_HEADS * HEAD_DIM)


def _mla(p, qa_g, w_qup, kva_g, w_kvup, cos, sin, need_ctx):
    b, s, _ = p.shape
    r0 = MLA_Q_RANK
    r1 = MLA_Q_RANK + MLA_KV_RANK
    blocks = lambda w, lo, hi: jnp.pad(w.reshape(w.shape[0], MLA_HEADS, -1)[..., lo:hi],
                                       ((0, 0), (0, 0), (0, V7X_LANES - (hi - lo)))).reshape(w.shape[0], -1)
    w_q = blocks(w_qup, 0, MLA_NOPE + MLA_ROPE).astype(BF16)
    w_kv = jnp.concatenate([blocks(w_kvup, 0, MLA_NOPE), blocks(w_kvup, MLA_NOPE, MLA_NOPE + MLA_V)],
                           axis=1).astype(BF16)
    zero = jnp.zeros((1, 1, r0), F32)
    q = norm_mod_matmul(p[..., :r0].reshape(b * s, r0), qa_g[None, :], zero, zero, w_q, lambda i: 0)
    zero = jnp.zeros((1, 1, MLA_KV_RANK), F32)
    kv = norm_mod_matmul(p[..., r0:r1].reshape(b * s, MLA_KV_RANK), kva_g[None, :], zero, zero, w_kv, lambda i: 0)
    kr = jnp.pad(p[..., r1:r1 + MLA_ROPE].reshape(b * s, MLA_ROPE),
                 ((0, 0), (MLA_NOPE, V7X_LANES - MLA_NOPE - MLA_ROPE)))
    q, k, v = mla_prep(q, kv, kr, cos, sin, s)
    o = _attend_all(q.reshape(b, s, -1), k.reshape(b, s, -1), v.reshape(b, s, -1), need_ctx)
    return o.reshape(b, s, MLA_HEADS, V7X_LANES)[..., :MLA_V].reshape(b, s, MLA_HEADS * MLA_V)


def kernel(x, c, ctx, c_ctx, ada_w, ada_b, norm1_g, norm2_g, ev_w_in, ev_w_out, gqa_qn_g, gqa_kn_g,
           hg_lb_logits, hg_onorm_g, od_w_in, od_w_out, mla_qa_g, mla_w_qup, mla_kva_g, mla_w_kvup,
           s5_a_re, s5_a_im, s5_log_dt, s5_b_re, s5_b_im, s5_c_re, s5_c_im, s5_d, s5_w_glu, s5_b_glu,
           peer_wq, peer_keys, peer_u, peer_v, final_g):
    b, t, d = x.shape
    s = CTX_LEN + t
    depth = ada_w.shape[0]
    rows = t // GRID_W
    cos_g, sin_g = _rope_tables(rows, HEAD_DIM, 0, HEAD_DIM)
    cos_m, sin_m = _rope_tables(rows, MLA_ROPE, MLA_NOPE, V7X_LANES)
    lb_all = jnp.cumsum(jax.nn.softmax(hg_lb_logits, axis=1), axis=1)

    def mod_index(tile):
        per, nctx = s // tile, CTX_LEN // tile
        return lambda i: jnp.where(i % per < nctx, b, i // per)

    mi_row = mod_index(ROW_TILE)
    mi_peer = mod_index(PEER_TILE)

    xa = jnp.concatenate([ctx, x], axis=1).reshape(b * s, d)
    s_all = jnp.concatenate([jax.nn.silu(c), jax.nn.silu(c_ctx)[None, :]], axis=0)
    s_pad = jnp.concatenate([s_all, jnp.zeros((V7X_SUBLANES - (b + 1) % V7X_SUBLANES, d), F32)], axis=0)

    for layer in range(depth):
        need_ctx = layer < depth - 1
        j = layer // 2
        mod = matmul(s_pad, ada_w[layer].astype(BF16), tm=s_pad.shape[0])[:b + 1] + ada_b[layer]
        mod = [m[:, None, :] for m in jnp.split(mod, 6, axis=-1)]
        if layer % 2 == 0:
            pa2 = norm_mod_matmul(xa, norm1_g[layer][None, :], mod[0], mod[1], ev_w_in[j].astype(BF16), mi_row)
            ya = _gqa(pa2, b, gqa_qn_g[j], gqa_kn_g[j], cos_g, sin_g, need_ctx)
            yb = _hgrn2(pa2, b, lb_all[:, j], hg_onorm_g[j]).reshape(b, s, HG_W)
            w_out = ev_w_out[j]
        else:
            pa = norm_mod_matmul(xa, norm1_g[layer][None, :], mod[0], mod[1], od_w_in[j].astype(BF16), mi_row)
            pa = pa.reshape(b, s, -1)
            ya = _mla(pa[..., :MLA_IN], mla_qa_g[j], mla_w_qup[j], mla_kva_g[j], mla_w_kvup[j], cos_m, sin_m, need_ctx)
            y5 = _s5_core(pa[..., MLA_IN:], s5_a_re[j], s5_a_im[j], s5_log_dt[j], s5_b_re[j], s5_b_im[j],
                          s5_c_re[j], s5_c_im[j], s5_d[j])
            z = jax.nn.gelu(y5, approximate=False).reshape(b * s, S5_WIDTH)
            gl = matmul(z, s5_w_glu[j].astype(BF16)) + s5_b_glu[j]
            yb = (z * jax.nn.sigmoid(gl)).reshape(b, s, S5_WIDTH)
            w_out = od_w_out[j]
        xa = matmul_residual(ya.reshape(b * s, -1), yb.reshape(b * s, -1), w_out.astype(BF16), xa, mod[2], mi_row)
        xa = peer(xa, norm2_g[layer][None, :], mod[3], mod[4], mod[5],
                  peer_wq[layer], peer_keys[layer], peer_u[layer], peer_v[layer], mi_peer)
    xl = xa.reshape(b, s, d)[:, CTX_LEN:].reshape(b * t, d)
    return rmsnorm_rows(xl, final_g[None, :]).reshape(b, t, d)
```
